```python
import math
import jax, jax.numpy as jnp
from jax import lax
import numpy as np

D_MODEL = 1024
BATCH = 8
SEQ = 4096
DEPTH = 1

D_S5 = D_MODEL // 2
S5_GROUP = 16
S5_GROUPS = D_S5 // S5_GROUP
S5_STATE = 64
DT_MIN = 1e-3
DT_MAX = 1e-1
D_FOURIER = D_MODEL // 2
FOURIER_GROUPS = 4
FOURIER_WIDTH = D_FOURIER // FOURIER_GROUPS
N_BRANCHES = 2
W_IN = D_S5 + D_FOURIER + N_BRANCHES * D_MODEL
N_EXPERTS = 256
TOP_K = 8
N_EXPERT_GROUPS = 8
TOPK_GROUPS = 4
D_EXPERT = 256
D_SHARED = 256
ROUTED_SCALE = 2.5
EXPERT_BLOCK = 128
ALPHA = (2 * DEPTH) ** 0.25
BETA = (8 * DEPTH) ** -0.25
LN_EPS = 1e-5

kernel_name = 'hybrid_s5_fnet_moe_encoder_block'


def _standardize(x):
    xf = x.astype(jnp.float32)
    mu = jnp.mean(xf, axis=-1, keepdims=True)
    var = jnp.mean(jnp.square(xf - mu), axis=-1, keepdims=True)
    return (xf - mu) * lax.rsqrt(var + LN_EPS)


def _layer_norm(x, g, b):
    return (_standardize(x) * g.astype(jnp.float32) + b.astype(jnp.float32)).astype(x.dtype)


def _modulate(x, shift, scale):
    y = _standardize(x) * (1.0 + scale[:, None, :].astype(jnp.float32)) + shift[:, None, :].astype(jnp.float32)
    return y.astype(x.dtype)


def _diag_complex_combine(e1, e2):
    a1r, a1i, b1r, b1i = e1
    a2r, a2i, b2r, b2i = e2
    return (a2r * a1r - a2i * a1i,
            a2r * a1i + a2i * a1r,
            a2r * b1r - a2i * b1i + b2r,
            a2r * b1i + a2i * b1r + b2i)


def _s5_bidirectional(u, lam_re, lam_im, log_dt, b_re, b_im, c_re, c_im, d_skip):
    bsz, seq = u.shape[0], u.shape[1]
    f32 = jnp.float32
    u = u.astype(f32).reshape(bsz, seq, S5_GROUPS, S5_GROUP)
    y = d_skip.astype(f32).reshape(S5_GROUPS, S5_GROUP) * u
    for direction in range(2):
        lr = lam_re[direction].astype(f32)
        li = lam_im[direction].astype(f32)
        dt = jnp.exp(log_dt[direction].astype(f32))[:, None]
        mag = jnp.exp(lr * dt)
        ang = li * dt
        ab_re = mag * jnp.cos(ang)
        ab_im = mag * jnp.sin(ang)
        den = lr * lr + li * li
        nr = ab_re - 1.0
        coef_re = (nr * lr + ab_im * li) / den
        coef_im = (ab_im * lr - nr * li) / den
        br = b_re[direction].astype(f32)
        bi = b_im[direction].astype(f32)
        bb_re = coef_re[:, :, None] * br - coef_im[:, :, None] * bi
        bb_im = coef_re[:, :, None] * bi + coef_im[:, :, None] * br
        bu_re = jnp.einsum('bsgh,gph->sbgp', u, bb_re)
        bu_im = jnp.einsum('bsgh,gph->sbgp', u, bb_im)
        a_re = jnp.broadcast_to(ab_re, (seq, 1) + ab_re.shape)
        a_im = jnp.broadcast_to(ab_im, (seq, 1) + ab_im.shape)
        _, _, x_re, x_im = lax.associative_scan(
            _diag_complex_combine, (a_re, a_im, bu_re, bu_im), reverse=(direction == 1), axis=0)
        y = (y + jnp.einsum('ghp,sbgp->bsgh', c_re[direction].astype(f32), x_re)
               - jnp.einsum('ghp,sbgp->bsgh', c_im[direction].astype(f32), x_im))
    return y.reshape(bsz, seq, D_S5)


def _fourier_groups(u):
    bsz, seq = u.shape[0], u.shape[1]
    uf = u.astype(jnp.float32).reshape(bsz, seq, FOURIER_GROUPS, FOURIER_WIDTH)
    yf = jnp.real(jnp.fft.fft2(uf, axes=(1, 3), norm='ortho'))
    return yf.reshape(bsz, seq, D_FOURIER)


def _moe_ffn(h, w_router, router_bias, w_exp_gate, w_exp_up, w_exp_down, w_sh_gate, w_sh_up, w_sh_down):
    n_tok = h.shape[0]
    scores = jax.nn.sigmoid((h @ w_router).astype(jnp.float32))
    sel = scores + router_bias.astype(jnp.float32)
    grp = sel.reshape(n_tok, N_EXPERT_GROUPS, N_EXPERTS // N_EXPERT_GROUPS)
    grp_score = jnp.sum(lax.top_k(grp, 2)[0], axis=-1)
    _, gidx = lax.top_k(grp_score, TOPK_GROUPS)
    gmask = jnp.any(gidx[:, :, None] == jnp.arange(N_EXPERT_GROUPS)[None, None, :], axis=1)
    masked = jnp.where(gmask[:, :, None], grp, -jnp.inf).reshape(n_tok, N_EXPERTS)
    _, eidx = lax.top_k(masked, TOP_K)
    gate = jnp.take_along_axis(scores, eidx, axis=-1)
    gate = gate / jnp.sum(gate, axis=-1, keepdims=True) * ROUTED_SCALE

    n_assign = n_tok * TOP_K
    n_blocks = (n_assign + N_EXPERTS * (EXPERT_BLOCK - 1) + EXPERT_BLOCK - 1) // EXPERT_BLOCK
    n_rows = n_blocks * EXPERT_BLOCK
    e_flat = eidx.reshape(-1).astype(jnp.int32)
    w_flat = gate.reshape(-1)
    tok_flat = jnp.arange(n_assign, dtype=jnp.int32) // TOP_K
    order = jnp.argsort(e_flat)
    e_sorted = e_flat[order]
    counts = jnp.bincount(e_flat, length=N_EXPERTS).astype(jnp.int32)
    padded = ((counts + EXPERT_BLOCK - 1) // EXPERT_BLOCK) * EXPERT_BLOCK
    pend = jnp.cumsum(padded)
    pstart = pend - padded
    ustart = jnp.cumsum(counts) - counts
    dest = pstart[e_sorted] + (jnp.arange(n_assign, dtype=jnp.int32) - ustart[e_sorted])
    row_tok = jnp.zeros((n_rows,), jnp.int32).at[dest].set(tok_flat[order])
    row_w = jnp.zeros((n_rows,), jnp.float32).at[dest].set(w_flat[order])
    blk_start = jnp.arange(n_blocks, dtype=jnp.int32) * EXPERT_BLOCK
    blk_e = jnp.minimum(jnp.searchsorted(pend, blk_start, side='right'), N_EXPERTS - 1)

    def _expert_block(args):
        idx, e = args
        rows = h[idx]
        hid = jax.nn.silu(rows @ w_exp_gate[e]) * (rows @ w_exp_up[e])
        return hid @ w_exp_down[e]

    y_rows = lax.map(_expert_block, (row_tok.reshape(n_blocks, EXPERT_BLOCK), blk_e))
    y_rows = y_rows.reshape(n_rows, -1) * row_w.astype(h.dtype)[:, None]
    routed = jax.ops.segment_sum(y_rows, row_tok, num_segments=n_tok)
    shared = (jax.nn.silu(h @ w_sh_gate) * (h @ w_sh_up)) @ w_sh_down
    return shared + routed


def setup_inputs(seed: int = 0) -> dict:
    key = jax.random.key(seed)
    ks = jax.random.split(key, 32)
    L = DEPTH
    f32 = jnp.float32

    def nrm(k, shape, scale):
        return scale * jax.random.normal(k, shape, f32)

    lam_re = -0.5 + nrm(ks[6], (L, 2, S5_GROUPS, S5_STATE), 0.01)
    lam_im = jnp.pi * jnp.arange(S5_STATE, dtype=f32) + nrm(ks[7], (L, 2, S5_GROUPS, S5_STATE), 0.01)
    log_dt = jax.random.uniform(ks[8], (L, 2, S5_GROUPS), f32, math.log(DT_MIN), math.log(DT_MAX))
    return {
        'x': nrm(ks[0], (BATCH, SEQ, D_MODEL), 1.0),
        'c': nrm(ks[1], (BATCH, D_MODEL), 1.0),
        'w_ada': nrm(ks[2], (L, D_MODEL, 6 * D_MODEL), D_MODEL ** -0.5),
        'b_ada': nrm(ks[3], (L, 6 * D_MODEL), 0.01),
        'w_in': nrm(ks[4], (L, D_MODEL, W_IN), D_MODEL ** -0.5),
        'b_in': nrm(ks[5], (L, W_IN), 0.01),
        's5_lambda_re': lam_re,
        's5_lambda_im': lam_im,
        's5_log_dt': log_dt,
        's5_b_re': nrm(ks[9], (L, 2, S5_GROUPS, S5_STATE, S5_GROUP), (2 * S5_GROUP) ** -0.5),
        's5_b_im': nrm(ks[10], (L, 2, S5_GROUPS, S5_STATE, S5_GROUP), (2 * S5_GROUP) ** -0.5),
        's5_c_re': nrm(ks[11], (L, 2, S5_GROUPS, S5_GROUP, S5_STATE), S5_STATE ** -0.5),
        's5_c_im': nrm(ks[12], (L, 2, S5_GROUPS, S5_GROUP, S5_STATE), S5_STATE ** -0.5),
        's5_d': nrm(ks[13], (L, D_S5), 1.0),
        'w_s5_glu': nrm(ks[14], (L, D_S5, 2 * D_MODEL), D_S5 ** -0.5),
        'b_s5_glu': nrm(ks[15], (L, 2 * D_MODEL), 0.01),
        'w_fourier': nrm(ks[16], (L, D_FOURIER, D_MODEL), D_FOURIER ** -0.5),
        'b_fourier': nrm(ks[17], (L, D_MODEL), 0.01),
        'w_out': nrm(ks[18], (L, D_MODEL, D_MODEL), BETA * D_MODEL ** -0.5),
        'b_out': nrm(ks[19], (L, D_MODEL), 0.01),
        'ln1_g': 1.0 + nrm(ks[20], (L, D_MODEL), 0.01),
        'ln1_b': nrm(ks[21], (L, D_MODEL), 0.01),
        'w_router': nrm(ks[22], (L, D_MODEL, N_EXPERTS), D_MODEL ** -0.5),
        'router_bias': nrm(ks[23], (L, N_EXPERTS), 0.01),
        'w_exp_gate': nrm(ks[24], (L, N_EXPERTS, D_MODEL, D_EXPERT), D_MODEL ** -0.5),
        'w_exp_up': nrm(ks[25], (L, N_EXPERTS, D_MODEL, D_EXPERT), D_MODEL ** -0.5),
        'w_exp_down': nrm(ks[26], (L, N_EXPERTS, D_EXPERT, D_MODEL), BETA * D_EXPERT ** -0.5),
        'w_sh_gate': nrm(ks[27], (L, D_MODEL, D_SHARED), D_MODEL ** -0.5),
        'w_sh_up': nrm(ks[28], (L, D_MODEL, D_SHARED), D_MODEL ** -0.5),
        'w_sh_down': nrm(ks[29], (L, D_SHARED, D_MODEL), BETA * D_SHARED ** -0.5),
        'ln2_g': 1.0 + nrm(ks[30], (L, D_MODEL), 0.01),
        'ln2_b': nrm(ks[31], (L, D_MODEL), 0.01),
    }


def reference(x, c, w_ada, b_ada, w_in, b_in, s5_lambda_re, s5_lambda_im, s5_log_dt,
              s5_b_re, s5_b_im, s5_c_re, s5_c_im, s5_d, w_s5_glu, b_s5_glu,
              w_fourier, b_fourier, w_out, b_out, ln1_g, ln1_b,
              w_router, router_bias, w_exp_gate, w_exp_up, w_exp_down,
              w_sh_gate, w_sh_up, w_sh_down, ln2_g, ln2_b):
    bsz, seq, d = x.shape
    for l in range(DEPTH):
        mod = jax.nn.silu(c) @ w_ada[l] + b_ada[l]
        sh1, sc1, g1, sh2, sc2, g2 = jnp.split(mod, 6, axis=-1)

        u = _modulate(x, sh1, sc1)
        proj = u @ w_in[l] + b_in[l]
        u_s5 = proj[..., :D_S5]
        u_f = proj[..., D_S5:D_S5 + D_FOURIER]
        gate_pre = proj[..., D_S5 + D_FOURIER:]

        y_s5 = _s5_bidirectional(u_s5, s5_lambda_re[l], s5_lambda_im[l], s5_log_dt[l],
                                 s5_b_re[l], s5_b_im[l], s5_c_re[l], s5_c_im[l], s5_d[l]).astype(x.dtype)
        z = jax.nn.gelu(y_s5)
        glu = z @ w_s5_glu[l] + b_s5_glu[l]
        br_s5 = glu[..., :d] * jax.nn.sigmoid(glu[..., d:])

        br_f = _fourier_groups(u_f).astype(x.dtype) @ w_fourier[l] + b_fourier[l]

        gates = jax.nn.sigmoid(gate_pre)
        merged = gates[..., :d] * br_s5 + gates[..., d:] * br_f
        mix = merged @ w_out[l] + b_out[l]
        x = _layer_norm(ALPHA * x + g1[:, None, :] * mix, ln1_g[l], ln1_b[l])

        h = _modulate(x, sh2, sc2).reshape(bsz * seq, d)
        ffn = _moe_ffn(h, w_router[l], router_bias[l], w_exp_gate[l], w_exp_up[l], w_exp_down[l],
                       w_sh_gate[l], w_sh_up[l], w_sh_down[l]).reshape(bsz, seq, d)
        x = _layer_norm(ALPHA * x + g2[:, None, :] * ffn, ln2_g[l], ln2_b[l])
    return x
```

```python
import functools
import math

import jax
import jax.numpy as jnp
from jax import lax
from jax.experimental import pallas as pl
from jax.experimental.pallas import tpu as pltpu

F32 = jnp.float32
BF16 = jnp.bfloat16
I32 = jnp.int32

TOP_K = 8
N_EXPERT_GROUPS = 8
TOPK_GROUPS = 4
ROUTED_SCALE = 2.5
FOURIER_GROUPS = 4
LN_EPS = 1e-5

LANE = 128
VMEM_LIMIT = 56 * 1024 * 1024

HIGHEST = lax.Precision.HIGHEST
NEG_INF = float("-inf")


def _cparams(sem):
    return pltpu.CompilerParams(dimension_semantics=sem, vmem_limit_bytes=VMEM_LIMIT)


def _const_spec(shape):
    nd = len(shape)
    return pl.BlockSpec(shape, lambda *_: (0,) * nd, pipeline_mode=pl.Buffered(1))


def _standardize(x):
    mu = jnp.mean(x, axis=-1, keepdims=True)
    xc = x - mu
    var = jnp.mean(xc * xc, axis=-1, keepdims=True)
    return xc * lax.rsqrt(var + LN_EPS)


def _silu(x):
    return x * jax.nn.sigmoid(x)


def _gelu_tanh(x):
    return 0.5 * x * (1.0 + jnp.tanh(math.sqrt(2.0 / math.pi) * (x + 0.044715 * (x * x * x))))


def _adaln_kernel(c_ref, w_ref, b_ref, o_ref):
    a = _silu(c_ref[...])
    o_ref[...] = jnp.dot(a, w_ref[...], precision=HIGHEST, preferred_element_type=F32) + b_ref[...]


def _adaln(c, w, b):
    bsz, d = c.shape
    n = w.shape[1]
    tn = 512
    return pl.pallas_call(
        _adaln_kernel,
        grid=(n // tn,),
        in_specs=[pl.BlockSpec((bsz, d), lambda j: (0, 0)),
                  pl.BlockSpec((d, tn), lambda j: (0, j)),
                  pl.BlockSpec((1, tn), lambda j: (0, j))],
        out_specs=pl.BlockSpec((bsz, tn), lambda j: (0, j)),
        out_shape=jax.ShapeDtypeStruct((bsz, n), F32),
        compiler_params=_cparams(("parallel",)),
        name="adaln",
    )(c, w, b.reshape(1, n))


def _proj_kernel(x_ref, sc_ref, sh_ref, ws5t_ref, bs5_ref, wf_ref, bf_ref, cs_ref, wg_ref, bg_ref,
                 us5_ref, z_ref, gates_ref):
    tm = x_ref.shape[0]
    u = (_standardize(x_ref[...]) * (1.0 + sc_ref[0]) + sh_ref[0]).astype(BF16)
    p = lax.dot_general(ws5t_ref[...], u, (((1,), (1,)), ((), ())), preferred_element_type=F32)
    p = p + bs5_ref[:, 0:1]
    for j in range(tm // LANE):
        us5_ref[:, j, :] = p[:, j * LANE:(j + 1) * LANE]
    uf = (jnp.dot(u, wf_ref[...], preferred_element_type=F32) + bf_ref[...]).astype(BF16)
    d_f = uf.shape[1]
    fw = d_f // FOURIER_GROUPS
    for q in range(FOURIER_GROUPS):
        zq = jnp.dot(uf[:, q * fw:(q + 1) * fw], cs_ref[...], preferred_element_type=F32)
        z_ref[:, q * fw:(q + 1) * fw] = zq[:, :fw].astype(BF16)
        z_ref[:, d_f + q * fw:d_f + (q + 1) * fw] = zq[:, fw:].astype(BF16)
    n_g = wg_ref.shape[1]
    half = n_g // 2
    for q in range(2):
        gp = jnp.dot(u, wg_ref[:, q * half:(q + 1) * half], preferred_element_type=F32)
        gp = gp + bg_ref[:, q * half:(q + 1) * half]
        gates_ref[:, q * half:(q + 1) * half] = jax.nn.sigmoid(gp).astype(BF16)


def _proj(x2, sc, sh, ws5t, bs5, wf, bf, cs, wg, bg, seq):
    t, d = x2.shape
    d_s5 = ws5t.shape[0]
    d_f = wf.shape[1]
    n_g = wg.shape[1]
    tm = 1024
    tpb = seq // tm
    bsz = t // seq
    return pl.pallas_call(
        _proj_kernel,
        grid=(t // tm,),
        in_specs=[pl.BlockSpec((tm, d), lambda i: (i, 0)),
                  pl.BlockSpec((1, 1, d), lambda i: (i // tpb, 0, 0)),
                  pl.BlockSpec((1, 1, d), lambda i: (i // tpb, 0, 0)),
                  _const_spec((d_s5, d)), _const_spec((d_s5, LANE)),
                  _const_spec((d, d_f)), _const_spec((1, d_f)),
                  _const_spec(cs.shape),
                  _const_spec((d, n_g)), _const_spec((1, n_g))],
        out_specs=[pl.BlockSpec((d_s5, tm // LANE, LANE), lambda i: (0, i, 0)),
                   pl.BlockSpec((tm, 2 * d_f), lambda i: (i, 0)),
                   pl.BlockSpec((tm, n_g), lambda i: (i, 0))],
        out_shape=[jax.ShapeDtypeStruct((d_s5, t // LANE, LANE), F32),
                   jax.ShapeDtypeStruct((t, 2 * d_f), BF16),
                   jax.ShapeDtypeStruct((t, n_g), BF16)],
        compiler_params=_cparams(("parallel",)),
        name="proj",
    )(x2, sc.reshape(bsz, 1, d), sh.reshape(bsz, 1, d), ws5t, bs5, wf, bf, cs, wg, bg)


def _s5_tables(lam_re, lam_im, log_dt, b_re, b_im, c_re, c_im, d_skip):
    L = LANE
    hp = HIGHEST
    lr, li = lam_re.astype(F32), lam_im.astype(F32)
    dt = jnp.exp(log_dt.astype(F32))[:, :, None]
    mag = jnp.exp(lr * dt)
    ang = li * dt
    ab_re, ab_im = mag * jnp.cos(ang), mag * jnp.sin(ang)
    den = lr * lr + li * li
    nr = ab_re - 1.0
    coef_re = (nr * lr + ab_im * li) / den
    coef_im = (ab_im * lr - nr * li) / den
    br, bi = b_re.astype(F32), b_im.astype(F32)
    bb_re = coef_re[..., None] * br - coef_im[..., None] * bi
    bb_im = coef_re[..., None] * bi + coef_im[..., None] * br
    cr, ci = c_re.astype(F32), c_im.astype(F32)
    n_g, n_p, n_h = br.shape[1], br.shape[2], br.shape[3]

    k = jnp.arange(L + 1, dtype=F32)[None, None, :, None]
    pmag = jnp.exp(k * (lr * dt)[:, :, None, :])
    pang = k * (li * dt)[:, :, None, :]
    pw_re, pw_im = pmag * jnp.cos(pang), pmag * jnp.sin(pang)

    m_re = cr[:, :, :, None, :] * jnp.swapaxes(bb_re, 2, 3)[:, :, None, :, :] \
        - ci[:, :, :, None, :] * jnp.swapaxes(bb_im, 2, 3)[:, :, None, :, :]
    m_im = cr[:, :, :, None, :] * jnp.swapaxes(bb_im, 2, 3)[:, :, None, :, :] \
        + ci[:, :, :, None, :] * jnp.swapaxes(bb_re, 2, 3)[:, :, None, :, :]
    kap = jnp.einsum("dgohp,dgkp->dgohk", m_re, pw_re, precision=hp) \
        - jnp.einsum("dgohp,dgkp->dgohk", m_im, pw_im, precision=hp)
    kpos = kap[0, ..., :L].reshape(n_g, n_h * n_h, L)
    kb = kap[1]
    kneg = jnp.concatenate([kb[..., 0:1], kb[..., L - 1:0:-1]], axis=-1)
    kneg = kneg.reshape(n_g, n_h * n_h, L)

    pf_re, pf_im = pw_re[0, :, L - 1::-1][:, :L], pw_im[0, :, L - 1::-1][:, :L]
    pb_re, pb_im = pw_re[1, :, :L], pw_im[1, :, :L]

    def cmul_hs(p_re, p_im, q_re, q_im):
        a = p_re[:, None, :, :]
        b = p_im[:, None, :, :]
        c = jnp.swapaxes(q_re, 1, 2)[:, :, None, :]
        e = jnp.swapaxes(q_im, 1, 2)[:, :, None, :]
        return a * c - b * e, a * e + b * c

    wsf_re, wsf_im = cmul_hs(pf_re, pf_im, bb_re[0], bb_im[0])
    wsb_re, wsb_im = cmul_hs(pb_re, pb_im, bb_re[1], bb_im[1])
    ws = jnp.concatenate([wsf_re, wsf_im, wsb_re, wsb_im], axis=-1).reshape(n_g, n_h * L, 4 * n_p)

    qf_re, qf_im = pw_re[0, :, 1:L + 1], pw_im[0, :, 1:L + 1]
    qb_re, qb_im = pw_re[1, :, L:0:-1], pw_im[1, :, L:0:-1]

    def cmul_ot(c_r, c_i, q_r, q_i):
        a = jnp.swapaxes(c_r, 1, 2)[:, :, :, None]
        b = jnp.swapaxes(c_i, 1, 2)[:, :, :, None]
        c = jnp.swapaxes(q_r, 1, 2)[:, :, None, :]
        e = jnp.swapaxes(q_i, 1, 2)[:, :, None, :]
        return a * c - b * e, -(a * e + b * c)

    wof_re, wof_mi = cmul_ot(cr[0], ci[0], qf_re, qf_im)
    wob_re, wob_mi = cmul_ot(cr[1], ci[1], qb_re, qb_im)
    wo = jnp.concatenate([wof_re, wof_mi, wob_re, wob_mi], axis=1)
    wo = jnp.swapaxes(wo.reshape(n_g, 4 * n_p, n_h // 2, 2 * L), 1, 2)

    al_re, al_im = pw_re[:, :, L], pw_im[:, :, L]
    alx = jnp.concatenate([al_re[0], al_re[0], al_re[1], al_re[1]], axis=-1)
    aly = jnp.concatenate([-al_im[0], al_im[0], -al_im[1], al_im[1]], axis=-1)
    al = jnp.stack([alx, aly], axis=1)

    dsk = jnp.broadcast_to(d_skip.astype(F32).reshape(n_g, n_h, 1), (n_g, n_h, L))
    return kpos, kneg, ws.astype(BF16), wo.astype(BF16), al, dsk


def _s5_kernel(a_ref, kpos_ref, kneg_ref, ws_ref, wo_ref, al_ref, d_ref, y_ref,
               abf_ref, tp_ref, sf_ref, sb_ref, xf_ref, xb_ref, *, n_chunk):
    n_h, r, L = a_ref.shape
    bsz = r // n_chunk
    half = 2 * (ws_ref.shape[2] // 4)

    for h in range(n_h):
        abf_ref[:, h * L:(h + 1) * L] = a_ref[h].astype(BF16)
    abf = abf_ref[...]

    s_all = jnp.dot(abf, ws_ref[0], preferred_element_type=F32)
    sf_ref[...] = s_all[:, :half]
    sb_ref[...] = s_all[:, half:]

    alx = al_ref[0, 0:1, :]
    aly = al_ref[0, 1:2, :]

    def carry(e, lo):
        ex = e * alx[:, lo:lo + half]
        ey = pltpu.roll(e, half // 2, 1) * aly[:, lo:lo + half]
        return ex + ey

    e = jnp.zeros((bsz, half), F32)
    for c in range(n_chunk):
        xf_ref[pl.ds(c, bsz, stride=n_chunk), :] = e
        e = carry(e, 0) + sf_ref[pl.ds(c, bsz, stride=n_chunk), :]
    e = jnp.zeros((bsz, half), F32)
    for c in range(n_chunk - 1, -1, -1):
        xb_ref[pl.ds(c, bsz, stride=n_chunk), :] = e
        e = carry(e, half) + sb_ref[pl.ds(c, bsz, stride=n_chunk), :]
    xin = jnp.concatenate([xf_ref[...], xb_ref[...]], axis=1).astype(BF16)

    s_idx = lax.broadcasted_iota(I32, (L, L), 0)
    t_idx = lax.broadcasted_iota(I32, (L, L), 1)
    ge = t_idx >= s_idx
    le = t_idx <= s_idx

    def pair(op, _):
        for oo in range(2):
            o = 2 * op + oo
            for h in range(n_h):
                row = o * n_h + h
                kp = jnp.broadcast_to(kpos_ref[0, pl.ds(row, 1), :], (L, L))
                kn = jnp.broadcast_to(kneg_ref[0, pl.ds(row, 1), :], (L, L))
                tp = jnp.where(ge, pltpu.roll(kp, 0, 1, stride=1, stride_axis=0), 0.0)
                tn = jnp.where(le, pltpu.roll(kn, 0, 1, stride=1, stride_axis=0), 0.0)
                tp_ref[h * L:(h + 1) * L, oo * L:(oo + 1) * L] = (tp + tn).astype(BF16)
        yp = jnp.dot(abf, tp_ref[...], preferred_element_type=F32)
        yp = yp + jnp.dot(xin, wo_ref[0, op], preferred_element_type=F32)
        for oo in range(2):
            o = 2 * op + oo
            y_ref[o] = yp[:, oo * L:(oo + 1) * L] + a_ref[o] * d_ref[0, pl.ds(o, 1), :]
        return 0

    lax.fori_loop(0, n_h // 2, pair, 0)


def _s5(us5, tables, n_chunk):
    kpos, kneg, ws, wo, al, dsk = tables
    d_s5, r, L = us5.shape
    n_g = kpos.shape[0]
    n_h = d_s5 // n_g
    n_st = ws.shape[2]
    g3 = lambda g: (g, 0, 0)
    return pl.pallas_call(
        functools.partial(_s5_kernel, n_chunk=n_chunk),
        grid=(n_g,),
        in_specs=[pl.BlockSpec((n_h, r, L), g3),
                  pl.BlockSpec((1, n_h * n_h, L), g3), pl.BlockSpec((1, n_h * n_h, L), g3),
                  pl.BlockSpec((1, n_h * L, n_st), g3),
                  pl.BlockSpec((1, n_h // 2, n_st, 2 * L), lambda g: (g, 0, 0, 0)),
                  pl.BlockSpec((1, 2, n_st), g3), pl.BlockSpec((1, n_h, L), g3)],
        out_specs=pl.BlockSpec((n_h, r, L), g3),
        out_shape=jax.ShapeDtypeStruct((d_s5, r, L), F32),
        scratch_shapes=[pltpu.VMEM((r, n_h * L), BF16), pltpu.VMEM((n_h * L, 2 * L), BF16),
                        pltpu.VMEM((r, n_st // 2), F32), pltpu.VMEM((r, n_st // 2), F32),
                        pltpu.VMEM((r, n_st // 2), F32), pltpu.VMEM((r, n_st // 2), F32)],
        compiler_params=_cparams(("parallel",)),
        name="s5",
    )(us5, kpos, kneg, ws, wo, al, dsk)


def _dft_tables(seq, fw):
    def cs(n):
        i = jnp.arange(n, dtype=I32)
        m = (i[:, None] * i[None, :]) % n
        ang = (2.0 * math.pi / n) * m.astype(F32)
        return jnp.cos(ang), jnp.sin(ang)
    c_s, s_s = cs(seq)
    c_c, s_c = cs(fw)
    return (jnp.concatenate([c_s, -s_s], axis=1).astype(BF16),
            jnp.concatenate([c_c, s_c], axis=1).astype(BF16))


def _seqdft_kernel(f_ref, z_ref, o_ref, *, scale):
    seq = z_ref.shape[0]
    d_f = o_ref.shape[1]
    acc = jnp.dot(f_ref[:, :seq], z_ref[:, :d_f], preferred_element_type=F32)
    acc = acc + jnp.dot(f_ref[:, seq:], z_ref[:, d_f:], preferred_element_type=F32)
    o_ref[...] = (acc * scale).astype(o_ref.dtype)


def _seqdft(fmat, z, seq, fw):
    t, two_df = z.shape
    d_f = two_df // 2
    bsz = t // seq
    tk = 512
    nk = seq // tk
    scale = 1.0 / math.sqrt(seq * fw)
    return pl.pallas_call(
        functools.partial(_seqdft_kernel, scale=scale),
        grid=(nk, bsz),
        in_specs=[pl.BlockSpec((tk, 2 * seq), lambda k, b: (k, 0)),
                  pl.BlockSpec((seq, two_df), lambda k, b: (b, 0))],
        out_specs=pl.BlockSpec((tk, d_f), lambda k, b: (b * nk + k, 0)),
        out_shape=jax.ShapeDtypeStruct((t, d_f), BF16),
        compiler_params=_cparams(("parallel", "parallel")),
        name="seqdft",
    )(fmat, z)


def _glu_kernel(y_ref, w_ref, b_ref, o_ref, zt_ref):
    n_j = y_ref.shape[1]
    d = o_ref.shape[1]
    for j in range(n_j):
        zt_ref[j * LANE:(j + 1) * LANE, :] = _gelu_tanh(y_ref[:, j, :]).T.astype(BF16)
    zt = zt_ref[...]
    a = jnp.dot(zt, w_ref[:, :d], preferred_element_type=F32) + b_ref[:, :d]
    g = jnp.dot(zt, w_ref[:, d:], preferred_element_type=F32) + b_ref[:, d:]
    o_ref[...] = (a * jax.nn.sigmoid(g)).astype(BF16)


def _glu(ys5, w, b):
    d_s5, r, L = ys5.shape
    t = r * L
    n = w.shape[1]
    tm = 1024
    return pl.pallas_call(
        _glu_kernel,
        grid=(t // tm,),
        in_specs=[pl.BlockSpec((d_s5, tm // L, L), lambda i: (0, i, 0)),
                  _const_spec((d_s5, n)), _const_spec((1, n))],
        out_specs=pl.BlockSpec((tm, n // 2), lambda i: (i, 0)),
        out_shape=jax.ShapeDtypeStruct((t, n // 2), BF16),
        scratch_shapes=[pltpu.VMEM((tm, d_s5), BF16)],
        compiler_params=_cparams(("parallel",)),
        name="glu",
    )(ys5, w, b)


def _merge_kernel(x_ref, brs_ref, yf_ref, gates_ref, g1_ref, sc_ref, sh_ref, wfo_ref, bfo_ref,
                  wo_ref, bo_ref, lng_ref, lnb_ref, x1_ref, hhi_ref, hlo_ref, *, alpha):
    d = x_ref.shape[1]
    br_f = jnp.dot(yf_ref[...], wfo_ref[...], preferred_element_type=F32) + bfo_ref[...]
    merged = gates_ref[:, :d].astype(F32) * brs_ref[...].astype(F32) + gates_ref[:, d:].astype(F32) * br_f
    mix = jnp.dot(merged.astype(BF16), wo_ref[...], preferred_element_type=F32) + bo_ref[...]
    v = alpha * x_ref[...] + g1_ref[0] * mix
    x1 = _standardize(v) * lng_ref[...] + lnb_ref[...]
    x1_ref[...] = x1
    h = _standardize(x1) * (1.0 + sc_ref[0]) + sh_ref[0]
    hhi = h.astype(BF16)
    hhi_ref[...] = hhi
    hlo_ref[...] = (h - hhi.astype(F32)).astype(BF16)


def _merge(x2, brs, yf, gates, g1, sc2, sh2, wfo, bfo, wo, bo, lng, lnb, seq, alpha):
    t, d = x2.shape
    d_f = yf.shape[1]
    tm = 512
    tpb = seq // tm
    bsz = t // seq
    row = lambda i: (i, 0)
    bat = lambda i: (i // tpb, 0, 0)
    return pl.pallas_call(
        functools.partial(_merge_kernel, alpha=alpha),
        grid=(t // tm,),
        in_specs=[pl.BlockSpec((tm, d), row), pl.BlockSpec((tm, d), row), pl.BlockSpec((tm, d_f), row),
                  pl.BlockSpec((tm, 2 * d), row),
                  pl.BlockSpec((1, 1, d), bat), pl.BlockSpec((1, 1, d), bat), pl.BlockSpec((1, 1, d), bat),
                  _const_spec((d_f, d)), _const_spec((1, d)), _const_spec((d, d)), _const_spec((1, d)),
                  _const_spec((1, d)), _const_spec((1, d))],
        out_specs=[pl.BlockSpec((tm, d), row), pl.BlockSpec((tm, d), row), pl.BlockSpec((tm, d), row)],
        out_shape=[jax.ShapeDtypeStruct((t, d), F32), jax.ShapeDtypeStruct((t, d), BF16),
                   jax.ShapeDtypeStruct((t, d), BF16)],
        compiler_params=_cparams(("parallel",)),
        name="merge",
    )(x2, brs, yf, gates, g1.reshape(bsz, 1, d), sc2.reshape(bsz, 1, d), sh2.reshape(bsz, 1, d),
      wfo, bfo, wo, bo, lng, lnb)


def _router_kernel(hhi_ref, hlo_ref, whi_ref, wlo_ref, bias_ref, tri_ref,
                   eidx_ref, gate_ref, rank_ref, cnt_ref, base_ref):
    n_e = whi_ref.shape[0]
    tm = hhi_ref.shape[0]
    gsz = n_e // N_EXPERT_GROUPS
    nt = (((1,), (1,)), ((), ()))

    @pl.when(pl.program_id(0) == 0)
    def _():
        base_ref[...] = jnp.zeros_like(base_ref)

    hhi = hhi_ref[...]
    logits = lax.dot_general(whi_ref[...], hhi, nt, preferred_element_type=F32)
    logits = logits + lax.dot_general(wlo_ref[...], hhi, nt, preferred_element_type=F32)
    logits = logits + lax.dot_general(whi_ref[...], hlo_ref[...], nt, preferred_element_type=F32)
    scores = jax.nn.sigmoid(logits)
    sel = scores + bias_ref[:, 0:1]

    g3 = sel.reshape(N_EXPERT_GROUPS, gsz, tm)
    i3 = lax.broadcasted_iota(I32, g3.shape, 1).astype(F32)
    m1 = jnp.max(g3, axis=1, keepdims=True)
    first = jnp.min(jnp.where(g3 == m1, i3, float(gsz)), axis=1, keepdims=True)
    m2 = jnp.max(jnp.where(i3 == first, NEG_INF, g3), axis=1, keepdims=True)
    gs = (m1 + m2).reshape(N_EXPERT_GROUPS, tm)

    gi = lax.broadcasted_iota(I32, gs.shape, 0).astype(F32)
    gsel = jnp.zeros(gs.shape, F32)
    cur = gs
    for _ in range(TOPK_GROUPS):
        m = jnp.max(cur, axis=0, keepdims=True)
        f = jnp.min(jnp.where(cur == m, gi, float(N_EXPERT_GROUPS)), axis=0, keepdims=True)
        pick = gi == f
        gsel = jnp.where(pick, 1.0, gsel)
        cur = jnp.where(pick, NEG_INF, cur)
    gmask = jnp.broadcast_to(gsel.reshape(N_EXPERT_GROUPS, 1, tm), g3.shape).reshape(n_e, tm)
    masked = jnp.where(gmask > 0.5, sel, NEG_INF)

    ri = lax.broadcasted_iota(I32, (n_e, tm), 0).astype(F32)
    picks = []
    gates = []
    multihot = jnp.zeros((n_e, tm), F32)
    for _ in range(TOP_K):
        m = jnp.max(masked, axis=0, keepdims=True)
        f = jnp.min(jnp.where(masked == m, ri, float(n_e)), axis=0, keepdims=True)
        pick = ri == f
        picks.append(f)
        gates.append(jnp.sum(jnp.where(pick, scores, 0.0), axis=0, keepdims=True))
        multihot = jnp.where(pick, 1.0, multihot)
        masked = jnp.where(pick, NEG_INF, masked)
    gsum = gates[0]
    for g in gates[1:]:
        gsum = gsum + g

    rankmat = jnp.dot(multihot.astype(BF16), tri_ref[...], preferred_element_type=F32) + base_ref[:, 0:1]
    for k in range(TOP_K):
        pick = ri == picks[k]
        eidx_ref[k:k + 1, :] = picks[k].astype(I32)
        gate_ref[k:k + 1, :] = gates[k] / gsum * ROUTED_SCALE
        rank_ref[k:k + 1, :] = jnp.sum(jnp.where(pick, rankmat, 0.0), axis=0, keepdims=True).astype(I32)
    base_ref[...] = base_ref[...] + jnp.sum(multihot, axis=1, keepdims=True)
    cnt_ref[...] = base_ref[...].astype(I32)


def _router(hhi, hlo, wrt_hi, wrt_lo, bias):
    t, d = hhi.shape
    n_e = wrt_hi.shape[0]
    tm = 512
    tri = (jnp.arange(tm)[:, None] < jnp.arange(tm)[None, :]).astype(BF16)
    kt = lambda i: (0, i)
    return pl.pallas_call(
        _router_kernel,
        grid=(t // tm,),
        in_specs=[pl.BlockSpec((tm, d), lambda i: (i, 0)), pl.BlockSpec((tm, d), lambda i: (i, 0)),
                  _const_spec((n_e, d)), _const_spec((n_e, d)), _const_spec((n_e, LANE)),
                  _const_spec((tm, tm))],
        out_specs=[pl.BlockSpec((TOP_K, tm), kt), pl.BlockSpec((TOP_K, tm), kt), pl.BlockSpec((TOP_K, tm), kt),
                   pl.BlockSpec((n_e, LANE), lambda i: (0, 0))],
        out_shape=[jax.ShapeDtypeStruct((TOP_K, t), I32), jax.ShapeDtypeStruct((TOP_K, t), F32),
                   jax.ShapeDtypeStruct((TOP_K, t), I32), jax.ShapeDtypeStruct((n_e, LANE), I32)],
        scratch_shapes=[pltpu.VMEM((n_e, LANE), F32)],
        compiler_params=_cparams(("arbitrary",)),
        name="router",
    )(hhi, hlo, wrt_hi, wrt_lo, jnp.broadcast_to(bias.astype(F32).reshape(n_e, 1), (n_e, LANE)), tri)


FFN_BLOCK = 256


def _ffn_kernel(blk_e_ref, n_used_ref, rows_ref, wg_ref, wu_ref, wd_ref, y_ref):
    @pl.when(pl.program_id(0) < n_used_ref[0])
    def _():
        x = rows_ref[...]
        a = jnp.dot(x, wg_ref[0].astype(BF16), preferred_element_type=F32)
        u = jnp.dot(x, wu_ref[0].astype(BF16), preferred_element_type=F32)
        hid = (_silu(a) * u).astype(BF16)
        y_ref[...] = jnp.dot(hid, wd_ref[0].astype(BF16), preferred_element_type=F32).astype(y_ref.dtype)


def _ffn(rows, blk_e, n_used, wg, wu, wd):
    n_rows, d = rows.shape
    n_e, _, d_e = wg.shape
    n_blk = n_rows // FFN_BLOCK
    return pl.pallas_call(
        _ffn_kernel,
        grid_spec=pltpu.PrefetchScalarGridSpec(
            num_scalar_prefetch=2,
            grid=(n_blk,),
            in_specs=[pl.BlockSpec((FFN_BLOCK, d), lambda b, e, n: (b, 0)),
                      pl.BlockSpec((1, d, d_e), lambda b, e, n: (e[b], 0, 0)),
                      pl.BlockSpec((1, d, d_e), lambda b, e, n: (e[b], 0, 0)),
                      pl.BlockSpec((1, d_e, d), lambda b, e, n: (e[b], 0, 0))],
            out_specs=pl.BlockSpec((FFN_BLOCK, d), lambda b, e, n: (b, 0)),
        ),
        out_shape=jax.ShapeDtypeStruct((n_rows, d), F32),
        compiler_params=_cparams(("arbitrary",)),
        name="ffn",
    )(blk_e, n_used, rows, wg, wu, wd)


def _final_kernel(x1_ref, h_ref, routed_ref, g2_ref, wsg_ref, wsu_ref, wsd_ref, lng_ref, lnb_ref, o_ref, *, alpha):
    h = h_ref[...]
    a = jnp.dot(h, wsg_ref[...], preferred_element_type=F32)
    u = jnp.dot(h, wsu_ref[...], preferred_element_type=F32)
    shared = jnp.dot((_silu(a) * u).astype(BF16), wsd_ref[...], preferred_element_type=F32)
    v = alpha * x1_ref[...] + g2_ref[0] * (shared + routed_ref[...])
    o_ref[...] = _standardize(v) * lng_ref[...] + lnb_ref[...]


def _final(x1, hhi, routed, g2, wsg, wsu, wsd, lng, lnb, seq, alpha):
    t, d = x1.shape
    d_sh = wsg.shape[1]
    tm = 512
    tpb = seq // tm
    bsz = t // seq
    row = lambda i: (i, 0)
    return pl.pallas_call(
        functools.partial(_final_kernel, alpha=alpha),
        grid=(t // tm,),
        in_specs=[pl.BlockSpec((tm, d), row), pl.BlockSpec((tm, d), row), pl.BlockSpec((tm, d), row),
                  pl.BlockSpec((1, 1, d), lambda i: (i // tpb, 0, 0)),
                  _const_spec((d, d_sh)), _const_spec((d, d_sh)), _const_spec((d_sh, d)),
                  _const_spec((1, d)), _const_spec((1, d))],
        out_specs=pl.BlockSpec((tm, d), row),
        out_shape=jax.ShapeDtypeStruct((t, d), F32),
        compiler_params=_cparams(("parallel",)),
        name="final",
    )(x1, hhi, routed, g2.reshape(bsz, 1, d), wsg, wsu, wsd, lng, lnb)


def _split_hi_lo(w):
    hi = w.astype(BF16)
    return hi, (w - hi.astype(F32)).astype(BF16)


def kernel(x, c, w_ada, b_ada, w_in, b_in, s5_lambda_re, s5_lambda_im, s5_log_dt, s5_b_re, s5_b_im, s5_c_re, s5_c_im, s5_d, w_s5_glu, b_s5_glu, w_fourier, b_fourier, w_out, b_out, ln1_g, ln1_b, w_router, router_bias, w_exp_gate, w_exp_up, w_exp_down, w_sh_gate, w_sh_up, w_sh_down, ln2_g, ln2_b):
    bsz, seq, d = x.shape
    depth = w_ada.shape[0]
    alpha = (2 * depth) ** 0.25
    t = bsz * seq
    d_s5 = s5_d.shape[1]
    d_f = w_fourier.shape[1]
    fw = d_f // FOURIER_GROUPS
    n_e = w_router.shape[2]
    n_chunk = seq // LANE
    fmat, cmat = _dft_tables(seq, fw)
    row = lambda v: v.astype(F32).reshape(1, -1)

    x2 = x.reshape(t, d)
    for l in range(depth):
        mod = _adaln(c, w_ada[l], b_ada[l])
        sh1, sc1, g1, sh2, sc2, g2 = jnp.split(mod, 6, axis=-1)

        wi = w_in[l]
        bi = b_in[l].astype(F32)
        ws5t = wi[:, :d_s5].T.astype(BF16)
        bs5 = jnp.broadcast_to(bi[:d_s5].reshape(d_s5, 1), (d_s5, LANE))
        us5, z, gates = _proj(x2, sc1, sh1, ws5t, bs5,
                              wi[:, d_s5:d_s5 + d_f].astype(BF16), row(bi[d_s5:d_s5 + d_f]), cmat,
                              wi[:, d_s5 + d_f:].astype(BF16), row(bi[d_s5 + d_f:]), seq)

        tables = _s5_tables(s5_lambda_re[l], s5_lambda_im[l], s5_log_dt[l], s5_b_re[l], s5_b_im[l],
                            s5_c_re[l], s5_c_im[l], s5_d[l])
        ys5 = _s5(us5, tables, n_chunk)
        brs = _glu(ys5, w_s5_glu[l].astype(BF16), row(b_s5_glu[l]))
        yf = _seqdft(fmat, z, seq, fw)

        x1, hhi, hlo = _merge(x2, brs, yf, gates, g1, sc2, sh2,
                              w_fourier[l].astype(BF16), row(b_fourier[l]),
                              w_out[l].astype(BF16), row(b_out[l]), row(ln1_g[l]), row(ln1_b[l]), seq, alpha)

        wrt_hi, wrt_lo = _split_hi_lo(w_router[l].astype(F32).T)
        eidx, gate, rank, cnt = _router(hhi, hlo, wrt_hi, wrt_lo, router_bias[l])

        counts = cnt[:, 0]
        padded = ((counts + FFN_BLOCK - 1) // FFN_BLOCK) * FFN_BLOCK
        pend = jnp.cumsum(padded)
        pstart = pend - padded
        dest = pstart[eidx] + rank
        n_blk = (t * TOP_K + n_e * (FFN_BLOCK - 1) + FFN_BLOCK - 1) // FFN_BLOCK
        n_rows = n_blk * FFN_BLOCK
        blk_e = jnp.minimum(jnp.searchsorted(pend, jnp.arange(n_blk, dtype=I32) * FFN_BLOCK, side="right"),
                            n_e - 1).astype(I32)
        n_used = (pend[-1] // FFN_BLOCK).astype(I32).reshape(1)
        tok = jnp.broadcast_to(jnp.arange(t, dtype=I32)[None, :], (TOP_K, t))
        row_tok = jnp.zeros((n_rows,), I32).at[dest.reshape(-1)].set(tok.reshape(-1))
        rows = hhi[row_tok]
        y_rows = _ffn(rows, blk_e, n_used, w_exp_gate[l], w_exp_up[l], w_exp_down[l])
        routed = jnp.sum(y_rows[dest] * gate[:, :, None], axis=0)

        x2 = _final(x1, hhi, routed, g2, w_sh_gate[l].astype(BF16), w_sh_up[l].astype(BF16),
                    w_sh_down[l].astype(BF16), row(ln2_g[l]), row(ln2_b[l]), seq, alpha)
    return x2.reshape(bsz, seq, d)
```

```python
import functools
import math

import jax
import jax.numpy as jnp
from jax import lax
from jax.experimental import pallas as pl
from jax.experimental.pallas import tpu as pltpu

F32 = jnp.float32
BF16 = jnp.bfloat16
I32 = jnp.int32

TOP_K = 8
N_EXPERT_GROUPS = 8
TOPK_GROUPS = 4
ROUTED_SCALE = 2.5
FOURIER_GROUPS = 4
LN_EPS = 1e-5

LANE = 128
VMEM_LIMIT = 56 * 1024 * 1024

HIGHEST = lax.Precision.HIGHEST
NEG_INF = float("-inf")


def _cparams(sem):
    return pltpu.CompilerParams(dimension_semantics=sem, vmem_limit_bytes=VMEM_LIMIT)


def _const_spec(shape):
    nd = len(shape)
    return pl.BlockSpec(shape, lambda *_: (0,) * nd, pipeline_mode=pl.Buffered(1))


def _standardize(x):
    mu = jnp.mean(x, axis=-1, keepdims=True)
    xc = x - mu
    var = jnp.mean(xc * xc, axis=-1, keepdims=True)
    return xc * lax.rsqrt(var + LN_EPS)


def _silu(x):
    return x * jax.nn.sigmoid(x)


def _gelu_tanh(x):
    return 0.5 * x * (1.0 + jnp.tanh(math.sqrt(2.0 / math.pi) * (x + 0.044715 * (x * x * x))))


def _adaln_kernel(c_ref, w_ref, b_ref, o_ref):
    a = _silu(c_ref[...])
    o_ref[...] = jnp.dot(a, w_ref[...], precision=HIGHEST, preferred_element_type=F32) + b_ref[...]


def _adaln(c, w, b):
    bsz, d = c.shape
    n = w.shape[1]
    tn = 512
    return pl.pallas_call(
        _adaln_kernel,
        grid=(n // tn,),
        in_specs=[pl.BlockSpec((bsz, d), lambda j: (0, 0)),
                  pl.BlockSpec((d, tn), lambda j: (0, j)),
                  pl.BlockSpec((1, tn), lambda j: (0, j))],
        out_specs=pl.BlockSpec((bsz, tn), lambda j: (0, j)),
        out_shape=jax.ShapeDtypeStruct((bsz, n), F32),
        compiler_params=_cparams(("parallel",)),
        name="adaln",
    )(c, w, b.reshape(1, n))


def _proj_kernel(x_ref, sc_ref, sh_ref, ws5t_ref, bs5_ref, wf_ref, bf_ref, cs_ref, wg_ref, bg_ref,
                 us5_ref, z_ref, gates_ref):
    tm = x_ref.shape[0]
    u = (_standardize(x_ref[...]) * (1.0 + sc_ref[0]) + sh_ref[0]).astype(BF16)
    p = lax.dot_general(ws5t_ref[...], u, (((1,), (1,)), ((), ())), preferred_element_type=F32)
    p = p + bs5_ref[:, 0:1]
    for j in range(tm // LANE):
        us5_ref[:, j, :] = p[:, j * LANE:(j + 1) * LANE]
    uf = (jnp.dot(u, wf_ref[...], preferred_element_type=F32) + bf_ref[...]).astype(BF16)
    d_f = uf.shape[1]
    fw = d_f // FOURIER_GROUPS
    for q in range(FOURIER_GROUPS):
        zq = jnp.dot(uf[:, q * fw:(q + 1) * fw], cs_ref[...], preferred_element_type=F32)
        z_ref[:, q * fw:(q + 1) * fw] = zq[:, :fw].astype(BF16)
        z_ref[:, d_f + q * fw:d_f + (q + 1) * fw] = zq[:, fw:].astype(BF16)
    n_g = wg_ref.shape[1]
    half = n_g // 2
    for q in range(2):
        gp = jnp.dot(u, wg_ref[:, q * half:(q + 1) * half], preferred_element_type=F32)
        gp = gp + bg_ref[:, q * half:(q + 1) * half]
        gates_ref[:, q * half:(q + 1) * half] = jax.nn.sigmoid(gp).astype(BF16)


def _proj(x2, sc, sh, ws5t, bs5, wf, bf, cs, wg, bg, seq):
    t, d = x2.shape
    d_s5 = ws5t.shape[0]
    d_f = wf.shape[1]
    n_g = wg.shape[1]
    tm = 1024
    tpb = seq // tm
    bsz = t // seq
    return pl.pallas_call(
        _proj_kernel,
        grid=(t // tm,),
        in_specs=[pl.BlockSpec((tm, d), lambda i: (i, 0)),
                  pl.BlockSpec((1, 1, d), lambda i: (i // tpb, 0, 0)),
                  pl.BlockSpec((1, 1, d), lambda i: (i // tpb, 0, 0)),
                  _const_spec((d_s5, d)), _const_spec((d_s5, LANE)),
                  _const_spec((d, d_f)), _const_spec((1, d_f)),
                  _const_spec(cs.shape),
                  _const_spec((d, n_g)), _const_spec((1, n_g))],
        out_specs=[pl.BlockSpec((d_s5, tm // LANE, LANE), lambda i: (0, i, 0)),
                   pl.BlockSpec((tm, 2 * d_f), lambda i: (i, 0)),
                   pl.BlockSpec((tm, n_g), lambda i: (i, 0))],
        out_shape=[jax.ShapeDtypeStruct((d_s5, t // LANE, LANE), F32),
                   jax.ShapeDtypeStruct((t, 2 * d_f), BF16),
                   jax.ShapeDtypeStruct((t, n_g), BF16)],
        compiler_params=_cparams(("parallel",)),
        name="proj",
    )(x2, sc.reshape(bsz, 1, d), sh.reshape(bsz, 1, d), ws5t, bs5, wf, bf, cs, wg, bg)


def _s5_tables(lam_re, lam_im, log_dt, b_re, b_im, c_re, c_im, d_skip):
    L = LANE
    hp = HIGHEST
    lr, li = lam_re.astype(F32), lam_im.astype(F32)
    dt = jnp.exp(log_dt.astype(F32))[:, :, None]
    mag = jnp.exp(lr * dt)
    ang = li * dt
    ab_re, ab_im = mag * jnp.cos(ang), mag * jnp.sin(ang)
    den = lr * lr + li * li
    nr = ab_re - 1.0
    coef_re = (nr * lr + ab_im * li) / den
    coef_im = (ab_im * lr - nr * li) / den
    br, bi = b_re.astype(F32), b_im.astype(F32)
    bb_re = coef_re[..., None] * br - coef_im[..., None] * bi
    bb_im = coef_re[..., None] * bi + coef_im[..., None] * br
    cr, ci = c_re.astype(F32), c_im.astype(F32)
    n_g, n_p, n_h = br.shape[1], br.shape[2], br.shape[3]

    k = jnp.arange(L + 1, dtype=F32)[None, None, :, None]
    pmag = jnp.exp(k * (lr * dt)[:, :, None, :])
    pang = k * (li * dt)[:, :, None, :]
    pw_re, pw_im = pmag * jnp.cos(pang), pmag * jnp.sin(pang)

    m_re = cr[:, :, :, None, :] * jnp.swapaxes(bb_re, 2, 3)[:, :, None, :, :] \
        - ci[:, :, :, None, :] * jnp.swapaxes(bb_im, 2, 3)[:, :, None, :, :]
    m_im = cr[:, :, :, None, :] * jnp.swapaxes(bb_im, 2, 3)[:, :, None, :, :] \
        + ci[:, :, :, None, :] * jnp.swapaxes(bb_re, 2, 3)[:, :, None, :, :]
    kap = jnp.einsum("dgohp,dgkp->dgohk", m_re, pw_re, precision=hp) \
        - jnp.einsum("dgohp,dgkp->dgohk", m_im, pw_im, precision=hp)
    kpos = kap[0, ..., :L].reshape(n_g, n_h * n_h, L)
    kb = kap[1]
    kneg = jnp.concatenate([kb[..., 0:1], kb[..., L - 1:0:-1]], axis=-1)
    kneg = kneg.reshape(n_g, n_h * n_h, L)

    pf_re, pf_im = pw_re[0, :, L - 1::-1][:, :L], pw_im[0, :, L - 1::-1][:, :L]
    pb_re, pb_im = pw_re[1, :, :L], pw_im[1, :, :L]

    def cmul_hs(p_re, p_im, q_re, q_im):
        a = p_re[:, None, :, :]
        b = p_im[:, None, :, :]
        c = jnp.swapaxes(q_re, 1, 2)[:, :, None, :]
        e = jnp.swapaxes(q_im, 1, 2)[:, :, None, :]
        return a * c - b * e, a * e + b * c

    wsf_re, wsf_im = cmul_hs(pf_re, pf_im, bb_re[0], bb_im[0])
    wsb_re, wsb_im = cmul_hs(pb_re, pb_im, bb_re[1], bb_im[1])
    ws = jnp.concatenate([wsf_re, wsf_im, wsb_re, wsb_im], axis=-1).reshape(n_g, n_h * L, 4 * n_p)

    qf_re, qf_im = pw_re[0, :, 1:L + 1], pw_im[0, :, 1:L + 1]
    qb_re, qb_im = pw_re[1, :, L:0:-1], pw_im[1, :, L:0:-1]

    def cmul_ot(c_r, c_i, q_r, q_i):
        a = jnp.swapaxes(c_r, 1, 2)[:, :, :, None]
        b = jnp.swapaxes(c_i, 1, 2)[:, :, :, None]
        c = jnp.swapaxes(q_r, 1, 2)[:, :, None, :]
        e = jnp.swapaxes(q_i, 1, 2)[:, :, None, :]
        return a * c - b * e, -(a * e + b * c)

    wof_re, wof_mi = cmul_ot(cr[0], ci[0], qf_re, qf_im)
    wob_re, wob_mi = cmul_ot(cr[1], ci[1], qb_re, qb_im)
    wo = jnp.concatenate([wof_re, wof_mi, wob_re, wob_mi], axis=1)
    wo = jnp.swapaxes(wo.reshape(n_g, 4 * n_p, n_h // 2, 2 * L), 1, 2)

    al_re, al_im = pw_re[:, :, L], pw_im[:, :, L]
    alx = jnp.concatenate([al_re[0], al_re[0], al_re[1], al_re[1]], axis=-1)
    aly = jnp.concatenate([-al_im[0], al_im[0], -al_im[1], al_im[1]], axis=-1)
    al = jnp.stack([alx, aly], axis=1)

    dsk = jnp.broadcast_to(d_skip.astype(F32).reshape(n_g, n_h, 1), (n_g, n_h, L))
    return kpos, kneg, ws.astype(BF16), wo.astype(BF16), al, dsk


def _s5_kernel(a_ref, kpos_ref, kneg_ref, ws_ref, wo_ref, al_ref, d_ref, y_ref,
               abf_ref, tp_ref, sf_ref, sb_ref, xf_ref, xb_ref, *, n_chunk):
    n_h, r, L = a_ref.shape
    bsz = r // n_chunk
    half = 2 * (ws_ref.shape[2] // 4)

    for h in range(n_h):
        abf_ref[:, h * L:(h + 1) * L] = a_ref[h].astype(BF16)
    abf = abf_ref[...]

    s_all = jnp.dot(abf, ws_ref[0], preferred_element_type=F32)
    sf_ref[...] = s_all[:, :half]
    sb_ref[...] = s_all[:, half:]

    alx = al_ref[0, 0:1, :]
    aly = al_ref[0, 1:2, :]

    def carry(e, lo):
        ex = e * alx[:, lo:lo + half]
        ey = pltpu.roll(e, half // 2, 1) * aly[:, lo:lo + half]
        return ex + ey

    e = jnp.zeros((bsz, half), F32)
    for c in range(n_chunk):
        xf_ref[pl.ds(c, bsz, stride=n_chunk), :] = e
        e = carry(e, 0) + sf_ref[pl.ds(c, bsz, stride=n_chunk), :]
    e = jnp.zeros((bsz, half), F32)
    for c in range(n_chunk - 1, -1, -1):
        xb_ref[pl.ds(c, bsz, stride=n_chunk), :] = e
        e = carry(e, half) + sb_ref[pl.ds(c, bsz, stride=n_chunk), :]
    xin = jnp.concatenate([xf_ref[...], xb_ref[...]], axis=1).astype(BF16)

    s_idx = lax.broadcasted_iota(I32, (L, L), 0)
    t_idx = lax.broadcasted_iota(I32, (L, L), 1)
    ge = t_idx >= s_idx
    le = t_idx <= s_idx

    def pair(op, _):
        for oo in range(2):
            o = 2 * op + oo
            for h in range(n_h):
                row = o * n_h + h
                kp = jnp.broadcast_to(kpos_ref[0, pl.ds(row, 1), :], (L, L))
                kn = jnp.broadcast_to(kneg_ref[0, pl.ds(row, 1), :], (L, L))
                tp = jnp.where(ge, pltpu.roll(kp, 0, 1, stride=1, stride_axis=0), 0.0)
                tn = jnp.where(le, pltpu.roll(kn, 0, 1, stride=1, stride_axis=0), 0.0)
                tp_ref[h * L:(h + 1) * L, oo * L:(oo + 1) * L] = (tp + tn).astype(BF16)
        yp = jnp.dot(abf, tp_ref[...], preferred_element_type=F32)
        yp = yp + jnp.dot(xin, wo_ref[0, op], preferred_element_type=F32)
        for oo in range(2):
            o = 2 * op + oo
            y_ref[o] = yp[:, oo * L:(oo + 1) * L] + a_ref[o] * d_ref[0, pl.ds(o, 1), :]
        return 0

    lax.fori_loop(0, n_h // 2, pair, 0)


def _s5(us5, tables, n_chunk):
    kpos, kneg, ws, wo, al, dsk = tables
    d_s5, r, L = us5.shape
    n_g = kpos.shape[0]
    n_h = d_s5 // n_g
    n_st = ws.shape[2]
    g3 = lambda g: (g, 0, 0)
    return pl.pallas_call(
        functools.partial(_s5_kernel, n_chunk=n_chunk),
        grid=(n_g,),
        in_specs=[pl.BlockSpec((n_h, r, L), g3),
                  pl.BlockSpec((1, n_h * n_h, L), g3), pl.BlockSpec((1, n_h * n_h, L), g3),
                  pl.BlockSpec((1, n_h * L, n_st), g3),
                  pl.BlockSpec((1, n_h // 2, n_st, 2 * L), lambda g: (g, 0, 0, 0)),
                  pl.BlockSpec((1, 2, n_st), g3), pl.BlockSpec((1, n_h, L), g3)],
        out_specs=pl.BlockSpec((n_h, r, L), g3),
        out_shape=jax.ShapeDtypeStruct((d_s5, r, L), F32),
        scratch_shapes=[pltpu.VMEM((r, n_h * L), BF16), pltpu.VMEM((n_h * L, 2 * L), BF16),
                        pltpu.VMEM((r, n_st // 2), F32), pltpu.VMEM((r, n_st // 2), F32),
                        pltpu.VMEM((r, n_st // 2), F32), pltpu.VMEM((r, n_st // 2), F32)],
        compiler_params=_cparams(("parallel",)),
        name="s5",
    )(us5, kpos, kneg, ws, wo, al, dsk)


def _dft_tables(seq, fw):
    def cs(n):
        i = jnp.arange(n, dtype=I32)
        m = (i[:, None] * i[None, :]) % n
        ang = (2.0 * math.pi / n) * m.astype(F32)
        return jnp.cos(ang), jnp.sin(ang)
    c_s, s_s = cs(seq)
    c_c, s_c = cs(fw)
    return (jnp.concatenate([c_s, -s_s], axis=1).astype(BF16),
            jnp.concatenate([c_c, s_c], axis=1).astype(BF16))


def _seqdft_kernel(f_ref, z_ref, o_ref, *, scale):
    seq = z_ref.shape[0]
    d_f = o_ref.shape[1]
    acc = jnp.dot(f_ref[:, :seq], z_ref[:, :d_f], preferred_element_type=F32)
    acc = acc + jnp.dot(f_ref[:, seq:], z_ref[:, d_f:], preferred_element_type=F32)
    o_ref[...] = (acc * scale).astype(o_ref.dtype)


def _seqdft(fmat, z, seq, fw):
    t, two_df = z.shape
    d_f = two_df // 2
    bsz = t // seq
    tk = 512
    nk = seq // tk
    scale = 1.0 / math.sqrt(seq * fw)
    return pl.pallas_call(
        functools.partial(_seqdft_kernel, scale=scale),
        grid=(nk, bsz),
        in_specs=[pl.BlockSpec((tk, 2 * seq), lambda k, b: (k, 0)),
                  pl.BlockSpec((seq, two_df), lambda k, b: (b, 0))],
        out_specs=pl.BlockSpec((tk, d_f), lambda k, b: (b * nk + k, 0)),
        out_shape=jax.ShapeDtypeStruct((t, d_f), BF16),
        compiler_params=_cparams(("parallel", "parallel")),
        name="seqdft",
    )(fmat, z)


def _glu_kernel(y_ref, w_ref, b_ref, o_ref, zt_ref):
    n_j = y_ref.shape[1]
    d = o_ref.shape[1]
    for j in range(n_j):
        zt_ref[j * LANE:(j + 1) * LANE, :] = _gelu_tanh(y_ref[:, j, :]).T.astype(BF16)
    zt = zt_ref[...]
    a = jnp.dot(zt, w_ref[:, :d], preferred_element_type=F32) + b_ref[:, :d]
    g = jnp.dot(zt, w_ref[:, d:], preferred_element_type=F32) + b_ref[:, d:]
    o_ref[...] = (a * jax.nn.sigmoid(g)).astype(BF16)


def _glu(ys5, w, b):
    d_s5, r, L = ys5.shape
    t = r * L
    n = w.shape[1]
    tm = 1024
    return pl.pallas_call(
        _glu_kernel,
        grid=(t // tm,),
        in_specs=[pl.BlockSpec((d_s5, tm // L, L), lambda i: (0, i, 0)),
                  _const_spec((d_s5, n)), _const_spec((1, n))],
        out_specs=pl.BlockSpec((tm, n // 2), lambda i: (i, 0)),
        out_shape=jax.ShapeDtypeStruct((t, n // 2), BF16),
        scratch_shapes=[pltpu.VMEM((tm, d_s5), BF16)],
        compiler_params=_cparams(("parallel",)),
        name="glu",
    )(ys5, w, b)


U32 = jnp.uint32
PACK_SUB = 4


def _row_slab(ref, r):
    return ref.at[pl.ds(pl.multiple_of(r * PACK_SUB, PACK_SUB), PACK_SUB), :]


def _pack_rows(ref, v):
    rows, d = v.shape
    half = d // 2
    bits = lax.bitcast_convert_type(v.astype(BF16).astype(F32), U32)
    for c in range(PACK_SUB):
        lo = bits[:, c * LANE:(c + 1) * LANE] >> 16
        hi = bits[:, half + c * LANE:half + (c + 1) * LANE] & jnp.uint32(0xFFFF0000)
        ref[pl.ds(c, rows, stride=PACK_SUB), :] = hi | lo


def _unpack_rows(ref, rows, first=0):
    los, his = [], []
    for c in range(PACK_SUB):
        w = ref[pl.ds(first * PACK_SUB + c, rows, stride=PACK_SUB), :]
        los.append(lax.bitcast_convert_type(w << 16, F32))
        his.append(lax.bitcast_convert_type(w & jnp.uint32(0xFFFF0000), F32))
    return jnp.concatenate(los + his, axis=1)


def _merge_kernel(x_ref, brs_ref, yf_ref, gates_ref, g1_ref, sc_ref, sh_ref, wfo_ref, bfo_ref,
                  wo_ref, bo_ref, lng_ref, lnb_ref, x1_ref, hhi_ref, hlo_ref, hp_ref, *, alpha):
    d = x_ref.shape[1]
    br_f = jnp.dot(yf_ref[...], wfo_ref[...], preferred_element_type=F32) + bfo_ref[...]
    merged = gates_ref[:, :d].astype(F32) * brs_ref[...].astype(F32) + gates_ref[:, d:].astype(F32) * br_f
    mix = jnp.dot(merged.astype(BF16), wo_ref[...], preferred_element_type=F32) + bo_ref[...]
    v = alpha * x_ref[...] + g1_ref[0] * mix
    x1 = _standardize(v) * lng_ref[...] + lnb_ref[...]
    x1_ref[...] = x1
    h = _standardize(x1) * (1.0 + sc_ref[0]) + sh_ref[0]
    hhi = h.astype(BF16)
    hhi_ref[...] = hhi
    hlo_ref[...] = (h - hhi.astype(F32)).astype(BF16)
    _pack_rows(hp_ref, h)


def _merge(x2, brs, yf, gates, g1, sc2, sh2, wfo, bfo, wo, bo, lng, lnb, seq, alpha):
    t, d = x2.shape
    d_f = yf.shape[1]
    tm = 512
    tpb = seq // tm
    bsz = t // seq
    row = lambda i: (i, 0)
    bat = lambda i: (i // tpb, 0, 0)
    return pl.pallas_call(
        functools.partial(_merge_kernel, alpha=alpha),
        grid=(t // tm,),
        in_specs=[pl.BlockSpec((tm, d), row), pl.BlockSpec((tm, d), row), pl.BlockSpec((tm, d_f), row),
                  pl.BlockSpec((tm, 2 * d), row),
                  pl.BlockSpec((1, 1, d), bat), pl.BlockSpec((1, 1, d), bat), pl.BlockSpec((1, 1, d), bat),
                  _const_spec((d_f, d)), _const_spec((1, d)), _const_spec((d, d)), _const_spec((1, d)),
                  _const_spec((1, d)), _const_spec((1, d))],
        out_specs=[pl.BlockSpec((tm, d), row), pl.BlockSpec((tm, d), row), pl.BlockSpec((tm, d), row),
                   pl.BlockSpec((tm * PACK_SUB, LANE), row)],
        out_shape=[jax.ShapeDtypeStruct((t, d), F32), jax.ShapeDtypeStruct((t, d), BF16),
                   jax.ShapeDtypeStruct((t, d), BF16), jax.ShapeDtypeStruct((t * PACK_SUB, LANE), U32)],
        compiler_params=_cparams(("parallel",)),
        name="merge",
    )(x2, brs, yf, gates, g1.reshape(bsz, 1, d), sc2.reshape(bsz, 1, d), sh2.reshape(bsz, 1, d),
      wfo, bfo, wo, bo, lng, lnb)


def _router_kernel(hhi_ref, hlo_ref, whi_ref, wlo_ref, bias_ref, tri_ref,
                   eidx_ref, gate_ref, rank_ref, cnt_ref, base_ref):
    n_e = whi_ref.shape[0]
    tm = hhi_ref.shape[0]
    gsz = n_e // N_EXPERT_GROUPS
    nt = (((1,), (1,)), ((), ()))

    @pl.when(pl.program_id(0) == 0)
    def _():
        base_ref[...] = jnp.zeros_like(base_ref)

    hhi = hhi_ref[...]
    logits = lax.dot_general(whi_ref[...], hhi, nt, preferred_element_type=F32)
    logits = logits + lax.dot_general(wlo_ref[...], hhi, nt, preferred_element_type=F32)
    logits = logits + lax.dot_general(whi_ref[...], hlo_ref[...], nt, preferred_element_type=F32)
    scores = jax.nn.sigmoid(logits)
    sel = scores + bias_ref[:, 0:1]

    g3 = sel.reshape(N_EXPERT_GROUPS, gsz, tm)
    i3 = lax.broadcasted_iota(I32, g3.shape, 1).astype(F32)
    m1 = jnp.max(g3, axis=1, keepdims=True)
    first = jnp.min(jnp.where(g3 == m1, i3, float(gsz)), axis=1, keepdims=True)
    m2 = jnp.max(jnp.where(i3 == first, NEG_INF, g3), axis=1, keepdims=True)
    gs = (m1 + m2).reshape(N_EXPERT_GROUPS, tm)

    gi = lax.broadcasted_iota(I32, gs.shape, 0).astype(F32)
    gsel = jnp.zeros(gs.shape, F32)
    cur = gs
    for _ in range(TOPK_GROUPS):
        m = jnp.max(cur, axis=0, keepdims=True)
        f = jnp.min(jnp.where(cur == m, gi, float(N_EXPERT_GROUPS)), axis=0, keepdims=True)
        pick = gi == f
        gsel = jnp.where(pick, 1.0, gsel)
        cur = jnp.where(pick, NEG_INF, cur)
    gmask = jnp.broadcast_to(gsel.reshape(N_EXPERT_GROUPS, 1, tm), g3.shape).reshape(n_e, tm)
    masked = jnp.where(gmask > 0.5, sel, NEG_INF)

    ri = lax.broadcasted_iota(I32, (n_e, tm), 0).astype(F32)
    picks = []
    gates = []
    multihot = jnp.zeros((n_e, tm), F32)
    for _ in range(TOP_K):
        m = jnp.max(masked, axis=0, keepdims=True)
        f = jnp.min(jnp.where(masked == m, ri, float(n_e)), axis=0, keepdims=True)
        pick = ri == f
        picks.append(f)
        gates.append(jnp.sum(jnp.where(pick, scores, 0.0), axis=0, keepdims=True))
        multihot = jnp.where(pick, 1.0, multihot)
        masked = jnp.where(pick, NEG_INF, masked)
    gsum = gates[0]
    for g in gates[1:]:
        gsum = gsum + g

    rankmat = jnp.dot(multihot.astype(BF16), tri_ref[...], preferred_element_type=F32) + base_ref[:, 0:1]
    for k in range(TOP_K):
        pick = ri == picks[k]
        eidx_ref[k:k + 1, :] = picks[k].astype(I32)
        gate_ref[k:k + 1, :] = gates[k] / gsum * ROUTED_SCALE
        rank_ref[k:k + 1, :] = jnp.sum(jnp.where(pick, rankmat, 0.0), axis=0, keepdims=True).astype(I32)
    base_ref[...] = base_ref[...] + jnp.sum(multihot, axis=1, keepdims=True)
    cnt_ref[...] = base_ref[...].astype(I32)


def _router(hhi, hlo, wrt_hi, wrt_lo, bias):
    t, d = hhi.shape
    n_e = wrt_hi.shape[0]
    tm = 512
    tri = (jnp.arange(tm)[:, None] < jnp.arange(tm)[None, :]).astype(BF16)
    kt = lambda i: (0, i)
    return pl.pallas_call(
        _router_kernel,
        grid=(t // tm,),
        in_specs=[pl.BlockSpec((tm, d), lambda i: (i, 0)), pl.BlockSpec((tm, d), lambda i: (i, 0)),
                  _const_spec((n_e, d)), _const_spec((n_e, d)), _const_spec((n_e, LANE)),
                  _const_spec((tm, tm))],
        out_specs=[pl.BlockSpec((TOP_K, tm), kt), pl.BlockSpec((TOP_K, tm), kt), pl.BlockSpec((TOP_K, tm), kt),
                   pl.BlockSpec((n_e, LANE), lambda i: (0, 0))],
        out_shape=[jax.ShapeDtypeStruct((TOP_K, t), I32), jax.ShapeDtypeStruct((TOP_K, t), F32),
                   jax.ShapeDtypeStruct((TOP_K, t), I32), jax.ShapeDtypeStruct((n_e, LANE), I32)],
        scratch_shapes=[pltpu.VMEM((n_e, LANE), F32)],
        compiler_params=_cparams(("arbitrary",)),
        name="router",
    )(hhi, hlo, wrt_hi, wrt_lo, jnp.broadcast_to(bias.astype(F32).reshape(n_e, 1), (n_e, LANE)), tri)


FFN_BLOCK = 256
DISPATCH_TILE = 512
COMBINE_TILE = 256


def _dest_kernel(eidx_ref, rank_ref, pstart_ref, dest_ref):
    n_e = pstart_ref.shape[0]
    tm = eidx_ref.shape[1]
    ri = lax.broadcasted_iota(I32, (n_e, tm), 0)
    ps = pstart_ref[:, 0:1].astype(F32)
    for k in range(TOP_K):
        hit = ri == eidx_ref[k:k + 1, :]
        base = jnp.sum(jnp.where(hit, ps, 0.0), axis=0, keepdims=True)
        dest_ref[k:k + 1, :] = base.astype(I32) + rank_ref[k:k + 1, :]


def _dest(eidx, rank, pstart):
    k, t = eidx.shape
    n_e = pstart.shape[0]
    tm = 2048
    kt = lambda i: (0, i)
    return pl.pallas_call(
        _dest_kernel,
        grid=(t // tm,),
        in_specs=[pl.BlockSpec((k, tm), kt), pl.BlockSpec((k, tm), kt), _const_spec((n_e, LANE))],
        out_specs=pl.BlockSpec((k, tm), kt),
        out_shape=jax.ShapeDtypeStruct((k, t), I32),
        compiler_params=_cparams(("parallel",)),
        name="dest",
    )(eidx, rank, jnp.broadcast_to(pstart.astype(I32).reshape(n_e, 1), (n_e, LANE)))


def _zero_fill(pad_lo_ref, pad_n_ref, zeros_ref, rows_ref, sem):
    def each_copy(act):
        def per_entry(e, _):
            lo = pad_lo_ref[e]
            n = pad_n_ref[e]
            n_full = n // FFN_BLOCK

            def full(j, _):
                act(pltpu.make_async_copy(
                    zeros_ref, rows_ref.at[pl.ds(pl.multiple_of((lo + j * FFN_BLOCK) * PACK_SUB, PACK_SUB),
                                                 FFN_BLOCK * PACK_SUB), :], sem))
                return 0

            lax.fori_loop(0, n_full, full, 0)
            off = lo + n_full * FFN_BLOCK
            rem = n - n_full * FFN_BLOCK
            bit = FFN_BLOCK // 2
            while bit >= 1:
                take = rem & bit

                @pl.when(take != 0)
                def _(off=off, bit=bit):
                    act(pltpu.make_async_copy(
                        zeros_ref.at[pl.ds(0, bit * PACK_SUB), :],
                        rows_ref.at[pl.ds(pl.multiple_of(off * PACK_SUB, PACK_SUB), bit * PACK_SUB), :], sem))

                off = off + take
                bit //= 2
            return 0

        lax.fori_loop(0, pad_lo_ref.shape[0], per_entry, 0)

    each_copy(lambda cp: cp.start())
    each_copy(lambda cp: cp.wait())


def _dispatch_kernel(pad_lo_ref, pad_n_ref, dest_ref, hp_ref, rows_ref, zeros_ref, sem, zsem):
    tm = hp_ref.shape[0] // PACK_SUB

    @pl.when(pl.program_id(0) == 0)
    def _():
        zeros_ref[...] = jnp.zeros_like(zeros_ref)
        _zero_fill(pad_lo_ref, pad_n_ref, zeros_ref, rows_ref, zsem)

    def body(tt, _):
        for k in range(TOP_K):
            pltpu.make_async_copy(_row_slab(hp_ref, tt), _row_slab(rows_ref, dest_ref[0, 0, k * tm + tt]),
                                  sem).start()
        return 0

    lax.fori_loop(0, tm, body, 0)
    all_rows = rows_ref.at[pl.ds(0, TOP_K * tm * PACK_SUB), :]
    pltpu.make_async_copy(all_rows, all_rows, sem).wait()


def _dispatch(hp, dest_tiles, pad_lo, pad_n, n_rows):
    n_tile, _, per_tile = dest_tiles.shape
    tm = per_tile // TOP_K
    return pl.pallas_call(
        _dispatch_kernel,
        grid_spec=pltpu.PrefetchScalarGridSpec(
            num_scalar_prefetch=2,
            grid=(n_tile,),
            in_specs=[pl.BlockSpec((1, 1, per_tile), lambda i, lo, n: (i, 0, 0), memory_space=pltpu.SMEM),
                      pl.BlockSpec((tm * PACK_SUB, LANE), lambda i, lo, n: (i, 0))],
            out_specs=pl.BlockSpec(memory_space=pl.ANY),
            scratch_shapes=[pltpu.VMEM((FFN_BLOCK * PACK_SUB, LANE), U32),
                            pltpu.SemaphoreType.DMA, pltpu.SemaphoreType.DMA],
        ),
        out_shape=jax.ShapeDtypeStruct((n_rows * PACK_SUB, LANE), U32),
        compiler_params=_cparams(("arbitrary",)),
        name="dispatch",
    )(pad_lo, pad_n, dest_tiles, hp)


def _ffn_kernel(blk_e_ref, n_used_ref, rows_ref, wg_ref, wu_ref, wd_ref, y_ref):
    live = pl.program_id(0) < n_used_ref[0]

    @pl.when(live)
    def _():
        x = _unpack_rows(rows_ref, FFN_BLOCK).astype(BF16)
        a = jnp.dot(x, wg_ref[0].astype(BF16), preferred_element_type=F32)
        u = jnp.dot(x, wu_ref[0].astype(BF16), preferred_element_type=F32)
        hid = (_silu(a) * u).astype(BF16)
        _pack_rows(y_ref, jnp.dot(hid, wd_ref[0].astype(BF16), preferred_element_type=F32))

    @pl.when(jnp.logical_not(live))
    def _():
        y_ref[...] = jnp.zeros_like(y_ref)


def _ffn(rows, blk_e, blk_valid, wg, wu, wd):
    n_rows = rows.shape[0] // PACK_SUB
    n_e, d, d_e = wg.shape
    n_blk = n_rows // FFN_BLOCK
    blk = lambda b, e, n: (b, 0)
    exp = lambda b, e, n: (e[b], 0, 0)
    return pl.pallas_call(
        _ffn_kernel,
        grid_spec=pltpu.PrefetchScalarGridSpec(
            num_scalar_prefetch=2,
            grid=(n_blk,),
            in_specs=[pl.BlockSpec((FFN_BLOCK * PACK_SUB, LANE), blk),
                      pl.BlockSpec((1, d, d_e), exp), pl.BlockSpec((1, d, d_e), exp),
                      pl.BlockSpec((1, d_e, d), exp)],
            out_specs=pl.BlockSpec((FFN_BLOCK * PACK_SUB, LANE), blk),
        ),
        out_shape=jax.ShapeDtypeStruct((n_rows * PACK_SUB, LANE), U32),
        compiler_params=_cparams(("arbitrary",)),
        name="ffn",
    )(blk_e, blk_valid, rows, wg, wu, wd)


def _final_kernel(dest_ref, x1_ref, h_ref, gate_ref, g2_ref, wsg_ref, wsu_ref, wsd_ref, lng_ref, lnb_ref,
                  y_ref, o_ref, ybuf_ref, sem, *, alpha):
    tm = x1_ref.shape[0]

    def body(tt, _):
        for k in range(TOP_K):
            pltpu.make_async_copy(_row_slab(y_ref, dest_ref[0, 0, k * tm + tt]),
                                  _row_slab(ybuf_ref, k * tm + tt), sem).start()
        return 0

    lax.fori_loop(0, tm, body, 0)

    h = h_ref[...]
    a = jnp.dot(h, wsg_ref[...], preferred_element_type=F32)
    u = jnp.dot(h, wsu_ref[...], preferred_element_type=F32)
    shared = jnp.dot((_silu(a) * u).astype(BF16), wsd_ref[...], preferred_element_type=F32)

    pltpu.make_async_copy(ybuf_ref, ybuf_ref, sem).wait()
    routed = gate_ref[:, 0:1] * _unpack_rows(ybuf_ref, tm)
    for k in range(1, TOP_K):
        routed = routed + gate_ref[:, k:k + 1] * _unpack_rows(ybuf_ref, tm, first=k * tm)

    v = alpha * x1_ref[...] + g2_ref[0] * (shared + routed)
    o_ref[...] = _standardize(v) * lng_ref[...] + lnb_ref[...]


def _final(x1, hhi, y_rows, dest_tiles, gate_t, g2, wsg, wsu, wsd, lng, lnb, seq, alpha):
    t, d = x1.shape
    d_sh = wsg.shape[1]
    n_tile, _, per_tile = dest_tiles.shape
    tm = per_tile // TOP_K
    tpb = seq // tm
    bsz = t // seq
    row = lambda i: (i, 0)
    return pl.pallas_call(
        functools.partial(_final_kernel, alpha=alpha),
        grid=(n_tile,),
        in_specs=[pl.BlockSpec((1, 1, per_tile), lambda i: (i, 0, 0), memory_space=pltpu.SMEM),
                  pl.BlockSpec((tm, d), row), pl.BlockSpec((tm, d), row), pl.BlockSpec((tm, TOP_K), row),
                  pl.BlockSpec((1, 1, d), lambda i: (i // tpb, 0, 0)),
                  _const_spec((d, d_sh)), _const_spec((d, d_sh)), _const_spec((d_sh, d)),
                  _const_spec((1, d)), _const_spec((1, d)),
                  pl.BlockSpec(memory_space=pl.ANY)],
        out_specs=pl.BlockSpec((tm, d), row),
        out_shape=jax.ShapeDtypeStruct((t, d), F32),
        scratch_shapes=[pltpu.VMEM((TOP_K * tm * PACK_SUB, LANE), U32), pltpu.SemaphoreType.DMA],
        compiler_params=_cparams(("arbitrary",)),
        name="final",
    )(dest_tiles, x1, hhi, gate_t, g2.reshape(bsz, 1, d), wsg, wsu, wsd, lng, lnb, y_rows)


def _split_hi_lo(w):
    hi = w.astype(BF16)
    return hi, (w - hi.astype(F32)).astype(BF16)


def kernel(x, c, w_ada, b_ada, w_in, b_in, s5_lambda_re, s5_lambda_im, s5_log_dt, s5_b_re, s5_b_im, s5_c_re, s5_c_im, s5_d, w_s5_glu, b_s5_glu, w_fourier, b_fourier, w_out, b_out, ln1_g, ln1_b, w_router, router_bias, w_exp_gate, w_exp_up, w_exp_down, w_sh_gate, w_sh_up, w_sh_down, ln2_g, ln2_b):
    bsz, seq, d = x.shape
    depth = w_ada.shape[0]
    alpha = (2 * depth) ** 0.25
    t = bsz * seq
    d_s5 = s5_d.shape[1]
    d_f = w_fourier.shape[1]
    fw = d_f // FOURIER_GROUPS
    n_e = w_router.shape[2]
    n_chunk = seq // LANE
    fmat, cmat = _dft_tables(seq, fw)
    row = lambda v: v.astype(F32).reshape(1, -1)

    x2 = x.reshape(t, d)
    for l in range(depth):
        mod = _adaln(c, w_ada[l], b_ada[l])
        sh1, sc1, g1, sh2, sc2, g2 = jnp.split(mod, 6, axis=-1)

        wi = w_in[l]
        bi = b_in[l].astype(F32)
        ws5t = wi[:, :d_s5].T.astype(BF16)
        bs5 = jnp.broadcast_to(bi[:d_s5].reshape(d_s5, 1), (d_s5, LANE))
        us5, z, gates = _proj(x2, sc1, sh1, ws5t, bs5,
                              wi[:, d_s5:d_s5 + d_f].astype(BF16), row(bi[d_s5:d_s5 + d_f]), cmat,
                              wi[:, d_s5 + d_f:].astype(BF16), row(bi[d_s5 + d_f:]), seq)

        tables = _s5_tables(s5_lambda_re[l], s5_lambda_im[l], s5_log_dt[l], s5_b_re[l], s5_b_im[l],
                            s5_c_re[l], s5_c_im[l], s5_d[l])
        ys5 = _s5(us5, tables, n_chunk)
        brs = _glu(ys5, w_s5_glu[l].astype(BF16), row(b_s5_glu[l]))
        yf = _seqdft(fmat, z, seq, fw)

        x1, hhi, hlo, hp = _merge(x2, brs, yf, gates, g1, sc2, sh2,
                              w_fourier[l].astype(BF16), row(b_fourier[l]),
                              w_out[l].astype(BF16), row(b_out[l]), row(ln1_g[l]), row(ln1_b[l]), seq, alpha)

        wrt_hi, wrt_lo = _split_hi_lo(w_router[l].astype(F32).T)
        eidx, gate, rank, cnt = _router(hhi, hlo, wrt_hi, wrt_lo, router_bias[l])

        counts = cnt[:, 0]
        padded = ((counts + FFN_BLOCK - 1) // FFN_BLOCK) * FFN_BLOCK
        pend = jnp.cumsum(padded)
        pstart = pend - padded
        dest = _dest(eidx, rank, pstart)
        n_blk = (t * TOP_K + n_e * (FFN_BLOCK - 1) + FFN_BLOCK - 1) // FFN_BLOCK
        n_rows = n_blk * FFN_BLOCK
        blk_start = jnp.arange(n_blk, dtype=I32) * FFN_BLOCK
        blk_e = jnp.minimum(jnp.searchsorted(pend, blk_start, side="right"), n_e - 1).astype(I32)
        n_used = (pend[-1:] // FFN_BLOCK).astype(I32)
        pad_lo = jnp.concatenate([pstart + counts, pend[-1:]]).astype(I32)
        pad_n = jnp.concatenate([padded - counts, n_rows - pend[-1:]]).astype(I32)

        def tiles(tm):
            return dest.reshape(TOP_K, t // tm, tm).transpose(1, 0, 2).reshape(t // tm, 1, TOP_K * tm)

        rows = _dispatch(hp, tiles(DISPATCH_TILE), pad_lo, pad_n, n_rows)
        y_rows = _ffn(rows, blk_e, n_used, w_exp_gate[l], w_exp_up[l], w_exp_down[l])
        x2 = _final(x1, hhi, y_rows, tiles(COMBINE_TILE), gate.T, g2,
                    w_sh_gate[l].astype(BF16), w_sh_up[l].astype(BF16), w_sh_down[l].astype(BF16),
                    row(ln2_g[l]), row(ln2_b[l]), seq, alpha)
    return x2.reshape(bsz, seq, d)
```

```python
import functools
import math

import jax
import jax.numpy as jnp
from jax import lax
from jax.experimental import pallas as pl
from jax.experimental.pallas import tpu as pltpu

F32 = jnp.float32
BF16 = jnp.bfloat16
I32 = jnp.int32

TOP_K = 8
N_EXPERT_GROUPS = 8
TOPK_GROUPS = 4
ROUTED_SCALE = 2.5
FOURIER_GROUPS = 4
LN_EPS = 1e-5

LANE = 128
VMEM_LIMIT = 56 * 1024 * 1024

HIGHEST = lax.Precision.HIGHEST
NEG_INF = float("-inf")

MOE_TILE = 256
FFN_BLOCK = 256
SEG_ROWS = 16
N_OVF = 4


def _cparams(sem):
    return pltpu.CompilerParams(dimension_semantics=sem, vmem_limit_bytes=VMEM_LIMIT)


def _const_spec(shape):
    nd = len(shape)
    return pl.BlockSpec(shape, lambda *_: (0,) * nd, pipeline_mode=pl.Buffered(1))


def _standardize(x):
    mu = jnp.mean(x, axis=-1, keepdims=True)
    xc = x - mu
    var = jnp.mean(xc * xc, axis=-1, keepdims=True)
    return xc * lax.rsqrt(var + LN_EPS)


def _silu(x):
    return x * jax.nn.sigmoid(x)


def _gelu_tanh(x):
    return 0.5 * x * (1.0 + jnp.tanh(math.sqrt(2.0 / math.pi) * (x + 0.044715 * (x * x * x))))


def _adaln_kernel(c_ref, w_ref, b_ref, o_ref):
    a = _silu(c_ref[...])
    o_ref[...] = jnp.dot(a, w_ref[...], precision=HIGHEST, preferred_element_type=F32) + b_ref[...]


def _adaln(c, w, b):
    bsz, d = c.shape
    n = w.shape[1]
    tn = 512
    return pl.pallas_call(
        _adaln_kernel,
        grid=(n // tn,),
        in_specs=[pl.BlockSpec((bsz, d), lambda j: (0, 0)),
                  pl.BlockSpec((d, tn), lambda j: (0, j)),
                  pl.BlockSpec((1, tn), lambda j: (0, j))],
        out_specs=pl.BlockSpec((bsz, tn), lambda j: (0, j)),
        out_shape=jax.ShapeDtypeStruct((bsz, n), F32),
        compiler_params=_cparams(("parallel",)),
        name="adaln",
    )(c, w, b.reshape(1, n))


def _proj_kernel(x_ref, sc_ref, sh_ref, ws5t_ref, bs5_ref, wf_ref, bf_ref, cs_ref, wg_ref, bg_ref,
                 us5_ref, z_ref, gates_ref):
    tm = x_ref.shape[0]
    u = (_standardize(x_ref[...]) * (1.0 + sc_ref[0]) + sh_ref[0]).astype(BF16)
    p = lax.dot_general(ws5t_ref[...], u, (((1,), (1,)), ((), ())), preferred_element_type=F32)
    p = p + bs5_ref[:, 0:1]
    for j in range(tm // LANE):
        us5_ref[:, j, :] = p[:, j * LANE:(j + 1) * LANE]
    uf = (jnp.dot(u, wf_ref[...], preferred_element_type=F32) + bf_ref[...]).astype(BF16)
    d_f = uf.shape[1]
    fw = d_f // FOURIER_GROUPS
    for q in range(FOURIER_GROUPS):
        zq = jnp.dot(uf[:, q * fw:(q + 1) * fw], cs_ref[...], preferred_element_type=F32)
        z_ref[:, q * fw:(q + 1) * fw] = zq[:, :fw].astype(BF16)
        z_ref[:, d_f + q * fw:d_f + (q + 1) * fw] = zq[:, fw:].astype(BF16)
    n_g = wg_ref.shape[1]
    half = n_g // 2
    for q in range(2):
        gp = jnp.dot(u, wg_ref[:, q * half:(q + 1) * half], preferred_element_type=F32)
        gp = gp + bg_ref[:, q * half:(q + 1) * half]
        gates_ref[:, q * half:(q + 1) * half] = jax.nn.sigmoid(gp).astype(BF16)


def _proj(x2, sc, sh, ws5t, bs5, wf, bf, cs, wg, bg, seq):
    t, d = x2.shape
    d_s5 = ws5t.shape[0]
    d_f = wf.shape[1]
    n_g = wg.shape[1]
    tm = 1024
    tpb = seq // tm
    bsz = t // seq
    return pl.pallas_call(
        _proj_kernel,
        grid=(t // tm,),
        in_specs=[pl.BlockSpec((tm, d), lambda i: (i, 0)),
                  pl.BlockSpec((1, 1, d), lambda i: (i // tpb, 0, 0)),
                  pl.BlockSpec((1, 1, d), lambda i: (i // tpb, 0, 0)),
                  _const_spec((d_s5, d)), _const_spec((d_s5, LANE)),
                  _const_spec((d, d_f)), _const_spec((1, d_f)),
                  _const_spec(cs.shape),
                  _const_spec((d, n_g)), _const_spec((1, n_g))],
        out_specs=[pl.BlockSpec((d_s5, tm // LANE, LANE), lambda i: (0, i, 0)),
                   pl.BlockSpec((tm, 2 * d_f), lambda i: (i, 0)),
                   pl.BlockSpec((tm, n_g), lambda i: (i, 0))],
        out_shape=[jax.ShapeDtypeStruct((d_s5, t // LANE, LANE), F32),
                   jax.ShapeDtypeStruct((t, 2 * d_f), BF16),
                   jax.ShapeDtypeStruct((t, n_g), BF16)],
        compiler_params=_cparams(("parallel",)),
        name="proj",
    )(x2, sc.reshape(bsz, 1, d), sh.reshape(bsz, 1, d), ws5t, bs5, wf, bf, cs, wg, bg)


def _s5_tables(lam_re, lam_im, log_dt, b_re, b_im, c_re, c_im, d_skip):
    L = LANE
    hp = HIGHEST
    lr, li = lam_re.astype(F32), lam_im.astype(F32)
    dt = jnp.exp(log_dt.astype(F32))[:, :, None]
    mag = jnp.exp(lr * dt)
    ang = li * dt
    ab_re, ab_im = mag * jnp.cos(ang), mag * jnp.sin(ang)
    den = lr * lr + li * li
    nr = ab_re - 1.0
    coef_re = (nr * lr + ab_im * li) / den
    coef_im = (ab_im * lr - nr * li) / den
    br, bi = b_re.astype(F32), b_im.astype(F32)
    bb_re = coef_re[..., None] * br - coef_im[..., None] * bi
    bb_im = coef_re[..., None] * bi + coef_im[..., None] * br
    cr, ci = c_re.astype(F32), c_im.astype(F32)
    n_g, n_p, n_h = br.shape[1], br.shape[2], br.shape[3]

    k = jnp.arange(L + 1, dtype=F32)[None, None, :, None]
    pmag = jnp.exp(k * (lr * dt)[:, :, None, :])
    pang = k * (li * dt)[:, :, None, :]
    pw_re, pw_im = pmag * jnp.cos(pang), pmag * jnp.sin(pang)

    m_re = cr[:, :, :, None, :] * jnp.swapaxes(bb_re, 2, 3)[:, :, None, :, :] \
        - ci[:, :, :, None, :] * jnp.swapaxes(bb_im, 2, 3)[:, :, None, :, :]
    m_im = cr[:, :, :, None, :] * jnp.swapaxes(bb_im, 2, 3)[:, :, None, :, :] \
        + ci[:, :, :, None, :] * jnp.swapaxes(bb_re, 2, 3)[:, :, None, :, :]
    kap = jnp.einsum("dgohp,dgkp->dgohk", m_re, pw_re, precision=hp) \
        - jnp.einsum("dgohp,dgkp->dgohk", m_im, pw_im, precision=hp)
    kpos = kap[0, ..., :L].reshape(n_g, n_h * n_h, L)
    kb = kap[1]
    kneg = jnp.concatenate([kb[..., 0:1], kb[..., L - 1:0:-1]], axis=-1)
    kneg = kneg.reshape(n_g, n_h * n_h, L)

    pf_re, pf_im = pw_re[0, :, L - 1::-1][:, :L], pw_im[0, :, L - 1::-1][:, :L]
    pb_re, pb_im = pw_re[1, :, :L], pw_im[1, :, :L]

    def cmul_hs(p_re, p_im, q_re, q_im):
        a = p_re[:, None, :, :]
        b = p_im[:, None, :, :]
        c = jnp.swapaxes(q_re, 1, 2)[:, :, None, :]
        e = jnp.swapaxes(q_im, 1, 2)[:, :, None, :]
        return a * c - b * e, a * e + b * c

    wsf_re, wsf_im = cmul_hs(pf_re, pf_im, bb_re[0], bb_im[0])
    wsb_re, wsb_im = cmul_hs(pb_re, pb_im, bb_re[1], bb_im[1])
    ws = jnp.concatenate([wsf_re, wsf_im, wsb_re, wsb_im], axis=-1).reshape(n_g, n_h * L, 4 * n_p)

    qf_re, qf_im = pw_re[0, :, 1:L + 1], pw_im[0, :, 1:L + 1]
    qb_re, qb_im = pw_re[1, :, L:0:-1], pw_im[1, :, L:0:-1]

    def cmul_ot(c_r, c_i, q_r, q_i):
        a = jnp.swapaxes(c_r, 1, 2)[:, :, :, None]
        b = jnp.swapaxes(c_i, 1, 2)[:, :, :, None]
        c = jnp.swapaxes(q_r, 1, 2)[:, :, None, :]
        e = jnp.swapaxes(q_i, 1, 2)[:, :, None, :]
        return a * c - b * e, -(a * e + b * c)

    wof_re, wof_mi = cmul_ot(cr[0], ci[0], qf_re, qf_im)
    wob_re, wob_mi = cmul_ot(cr[1], ci[1], qb_re, qb_im)
    wo = jnp.concatenate([wof_re, wof_mi, wob_re, wob_mi], axis=1)
    wo = jnp.swapaxes(wo.reshape(n_g, 4 * n_p, n_h // 2, 2 * L), 1, 2)

    al_re, al_im = pw_re[:, :, L], pw_im[:, :, L]
    alx = jnp.concatenate([al_re[0], al_re[0], al_re[1], al_re[1]], axis=-1)
    aly = jnp.concatenate([-al_im[0], al_im[0], -al_im[1], al_im[1]], axis=-1)
    al = jnp.stack([alx, aly], axis=1)

    dsk = jnp.broadcast_to(d_skip.astype(F32).reshape(n_g, n_h, 1), (n_g, n_h, L))
    return kpos, kneg, ws.astype(BF16), wo.astype(BF16), al, dsk


def _s5_kernel(a_ref, kpos_ref, kneg_ref, ws_ref, wo_ref, al_ref, d_ref, y_ref,
               abf_ref, tp_ref, sf_ref, sb_ref, xf_ref, xb_ref, *, n_chunk):
    n_h, r, L = a_ref.shape
    bsz = r // n_chunk
    half = 2 * (ws_ref.shape[2] // 4)

    for h in range(n_h):
        abf_ref[:, h * L:(h + 1) * L] = a_ref[h].astype(BF16)
    abf = abf_ref[...]

    s_all = jnp.dot(abf, ws_ref[0], preferred_element_type=F32)
    sf_ref[...] = s_all[:, :half]
    sb_ref[...] = s_all[:, half:]

    alx = al_ref[0, 0:1, :]
    aly = al_ref[0, 1:2, :]

    def carry(e, lo):
        ex = e * alx[:, lo:lo + half]
        ey = pltpu.roll(e, half // 2, 1) * aly[:, lo:lo + half]
        return ex + ey

    e = jnp.zeros((bsz, half), F32)
    for c in range(n_chunk):
        xf_ref[pl.ds(c, bsz, stride=n_chunk), :] = e
        e = carry(e, 0) + sf_ref[pl.ds(c, bsz, stride=n_chunk), :]
    e = jnp.zeros((bsz, half), F32)
    for c in range(n_chunk - 1, -1, -1):
        xb_ref[pl.ds(c, bsz, stride=n_chunk), :] = e
        e = carry(e, half) + sb_ref[pl.ds(c, bsz, stride=n_chunk), :]
    xin = jnp.concatenate([xf_ref[...], xb_ref[...]], axis=1).astype(BF16)

    s_idx = lax.broadcasted_iota(I32, (L, L), 0)
    t_idx = lax.broadcasted_iota(I32, (L, L), 1)
    ge = t_idx >= s_idx
    le = t_idx <= s_idx

    def pair(op, _):
        for oo in range(2):
            o = 2 * op + oo
            for h in range(n_h):
                row = o * n_h + h
                kp = jnp.broadcast_to(kpos_ref[0, pl.ds(row, 1), :], (L, L))
                kn = jnp.broadcast_to(kneg_ref[0, pl.ds(row, 1), :], (L, L))
                tp = jnp.where(ge, pltpu.roll(kp, 0, 1, stride=1, stride_axis=0), 0.0)
                tn = jnp.where(le, pltpu.roll(kn, 0, 1, stride=1, stride_axis=0), 0.0)
                tp_ref[h * L:(h + 1) * L, oo * L:(oo + 1) * L] = (tp + tn).astype(BF16)
        yp = jnp.dot(abf, tp_ref[...], preferred_element_type=F32)
        yp = yp + jnp.dot(xin, wo_ref[0, op], preferred_element_type=F32)
        for oo in range(2):
            o = 2 * op + oo
            y_ref[o] = yp[:, oo * L:(oo + 1) * L] + a_ref[o] * d_ref[0, pl.ds(o, 1), :]
        return 0

    lax.fori_loop(0, n_h // 2, pair, 0)


def _s5(us5, tables, n_chunk):
    kpos, kneg, ws, wo, al, dsk = tables
    d_s5, r, L = us5.shape
    n_g = kpos.shape[0]
    n_h = d_s5 // n_g
    n_st = ws.shape[2]
    g3 = lambda g: (g, 0, 0)
    return pl.pallas_call(
        functools.partial(_s5_kernel, n_chunk=n_chunk),
        grid=(n_g,),
        in_specs=[pl.BlockSpec((n_h, r, L), g3),
                  pl.BlockSpec((1, n_h * n_h, L), g3), pl.BlockSpec((1, n_h * n_h, L), g3),
                  pl.BlockSpec((1, n_h * L, n_st), g3),
                  pl.BlockSpec((1, n_h // 2, n_st, 2 * L), lambda g: (g, 0, 0, 0)),
                  pl.BlockSpec((1, 2, n_st), g3), pl.BlockSpec((1, n_h, L), g3)],
        out_specs=pl.BlockSpec((n_h, r, L), g3),
        out_shape=jax.ShapeDtypeStruct((d_s5, r, L), F32),
        scratch_shapes=[pltpu.VMEM((r, n_h * L), BF16), pltpu.VMEM((n_h * L, 2 * L), BF16),
                        pltpu.VMEM((r, n_st // 2), F32), pltpu.VMEM((r, n_st // 2), F32),
                        pltpu.VMEM((r, n_st // 2), F32), pltpu.VMEM((r, n_st // 2), F32)],
        compiler_params=_cparams(("parallel",)),
        name="s5",
    )(us5, kpos, kneg, ws, wo, al, dsk)


def _dft_tables(seq, fw):
    def cs(n):
        i = jnp.arange(n, dtype=I32)
        m = (i[:, None] * i[None, :]) % n
        ang = (2.0 * math.pi / n) * m.astype(F32)
        return jnp.cos(ang), jnp.sin(ang)

    def cs_split(n, r):
        s = jnp.arange(n, dtype=I32)[None, :]
        q = jnp.arange(r, dtype=I32)[:, None]
        ang_a = (2.0 * math.pi / n) * ((r * q * s) % n).astype(F32)
        ang_b = (2.0 * math.pi / n) * ((q * s) % n).astype(F32)
        ca, sa = jnp.cos(ang_a)[:, None, :], jnp.sin(ang_a)[:, None, :]
        cb, sb = jnp.cos(ang_b)[None, :, :], jnp.sin(ang_b)[None, :, :]
        return (ca * cb - sa * sb).reshape(n, n), (sa * cb + ca * sb).reshape(n, n)

    r = int(math.isqrt(seq))
    c_s, s_s = cs_split(seq, r) if r * r == seq else cs(seq)
    c_c, s_c = cs(fw)
    return (jnp.concatenate([c_s, -s_s], axis=1).astype(BF16),
            jnp.concatenate([c_c, s_c], axis=1).astype(BF16))


def _seqdft_kernel(f_ref, z_ref, o_ref, *, scale):
    seq = z_ref.shape[0]
    d_f = o_ref.shape[1]
    acc = jnp.dot(f_ref[:, :seq], z_ref[:, :d_f], preferred_element_type=F32)
    acc = acc + jnp.dot(f_ref[:, seq:], z_ref[:, d_f:], preferred_element_type=F32)
    o_ref[...] = (acc * scale).astype(o_ref.dtype)


def _seqdft(fmat, z, seq, fw):
    t, two_df = z.shape
    d_f = two_df // 2
    bsz = t // seq
    tk = 512
    nk = seq // tk
    scale = 1.0 / math.sqrt(seq * fw)
    return pl.pallas_call(
        functools.partial(_seqdft_kernel, scale=scale),
        grid=(nk, bsz),
        in_specs=[pl.BlockSpec((tk, 2 * seq), lambda k, b: (k, 0)),
                  pl.BlockSpec((seq, two_df), lambda k, b: (b, 0))],
        out_specs=pl.BlockSpec((tk, d_f), lambda k, b: (b * nk + k, 0)),
        out_shape=jax.ShapeDtypeStruct((t, d_f), BF16),
        compiler_params=_cparams(("parallel", "parallel")),
        name="seqdft",
    )(fmat, z)


def _glu_kernel(y_ref, w_ref, b_ref, o_ref, zt_ref):
    n_j = y_ref.shape[1]
    d = o_ref.shape[1]
    for j in range(n_j):
        zt_ref[j * LANE:(j + 1) * LANE, :] = _gelu_tanh(y_ref[:, j, :]).T.astype(BF16)
    zt = zt_ref[...]
    a = jnp.dot(zt, w_ref[:, :d], preferred_element_type=F32) + b_ref[:, :d]
    g = jnp.dot(zt, w_ref[:, d:], preferred_element_type=F32) + b_ref[:, d:]
    o_ref[...] = (a * jax.nn.sigmoid(g)).astype(BF16)


def _glu(ys5, w, b):
    d_s5, r, L = ys5.shape
    t = r * L
    n = w.shape[1]
    tm = 1024
    return pl.pallas_call(
        _glu_kernel,
        grid=(t // tm,),
        in_specs=[pl.BlockSpec((d_s5, tm // L, L), lambda i: (0, i, 0)),
                  _const_spec((d_s5, n)), _const_spec((1, n))],
        out_specs=pl.BlockSpec((tm, n // 2), lambda i: (i, 0)),
        out_shape=jax.ShapeDtypeStruct((t, n // 2), BF16),
        scratch_shapes=[pltpu.VMEM((tm, d_s5), BF16)],
        compiler_params=_cparams(("parallel",)),
        name="glu",
    )(ys5, w, b)


U32 = jnp.uint32
PACK_SUB = 4


def _row_slab(ref, r):
    return ref.at[pl.ds(pl.multiple_of(r * PACK_SUB, PACK_SUB), PACK_SUB), :]


def _pack_rows(ref, v, first=0, is_bf16_exact=False):
    rows, d = v.shape
    half = d // 2
    bits = lax.bitcast_convert_type(v if is_bf16_exact else v.astype(BF16).astype(F32), U32)
    for c in range(PACK_SUB):
        lo = bits[:, c * LANE:(c + 1) * LANE] >> 16
        hi = bits[:, half + c * LANE:half + (c + 1) * LANE] & jnp.uint32(0xFFFF0000)
        ref[pl.ds(first * PACK_SUB + c, rows, stride=PACK_SUB), :] = hi | lo


def _unpack_rows(ref, rows, first=0):
    los, his = [], []
    for c in range(PACK_SUB):
        w = ref[pl.ds(first * PACK_SUB + c, rows, stride=PACK_SUB), :]
        los.append(lax.bitcast_convert_type(w << 16, F32))
        his.append(lax.bitcast_convert_type(w & jnp.uint32(0xFFFF0000), F32))
    return jnp.concatenate(los + his, axis=1)


def _merge_kernel(x_ref, brs_ref, yf_ref, gates_ref, g1_ref, sc_ref, sh_ref, wfo_ref, bfo_ref,
                  wo_ref, bo_ref, lng_ref, lnb_ref, x1_ref, hhi_ref, hlo_ref, *, alpha):
    d = x_ref.shape[1]
    br_f = jnp.dot(yf_ref[...], wfo_ref[...], preferred_element_type=F32) + bfo_ref[...]
    merged = gates_ref[:, :d].astype(F32) * brs_ref[...].astype(F32) + gates_ref[:, d:].astype(F32) * br_f
    mix = jnp.dot(merged.astype(BF16), wo_ref[...], preferred_element_type=F32) + bo_ref[...]
    v = alpha * x_ref[...] + g1_ref[0] * mix
    x1 = _standardize(v) * lng_ref[...] + lnb_ref[...]
    x1_ref[...] = x1
    h = _standardize(x1) * (1.0 + sc_ref[0]) + sh_ref[0]
    hhi = h.astype(BF16)
    hhi_ref[...] = hhi
    hlo_ref[...] = (h - hhi.astype(F32)).astype(BF16)


def _merge(x2, brs, yf, gates, g1, sc2, sh2, wfo, bfo, wo, bo, lng, lnb, seq, alpha):
    t, d = x2.shape
    d_f = yf.shape[1]
    tm = 512
    tpb = seq // tm
    bsz = t // seq
    row = lambda i: (i, 0)
    bat = lambda i: (i // tpb, 0, 0)
    return pl.pallas_call(
        functools.partial(_merge_kernel, alpha=alpha),
        grid=(t // tm,),
        in_specs=[pl.BlockSpec((tm, d), row), pl.BlockSpec((tm, d), row), pl.BlockSpec((tm, d_f), row),
                  pl.BlockSpec((tm, 2 * d), row),
                  pl.BlockSpec((1, 1, d), bat), pl.BlockSpec((1, 1, d), bat), pl.BlockSpec((1, 1, d), bat),
                  _const_spec((d_f, d)), _const_spec((1, d)), _const_spec((d, d)), _const_spec((1, d)),
                  _const_spec((1, d)), _const_spec((1, d))],
        out_specs=[pl.BlockSpec((tm, d), row), pl.BlockSpec((tm, d), row), pl.BlockSpec((tm, d), row)],
        out_shape=[jax.ShapeDtypeStruct((t, d), F32), jax.ShapeDtypeStruct((t, d), BF16),
                   jax.ShapeDtypeStruct((t, d), BF16)],
        compiler_params=_cparams(("parallel",)),
        name="merge",
    )(x2, brs, yf, gates, g1.reshape(bsz, 1, d), sc2.reshape(bsz, 1, d), sh2.reshape(bsz, 1, d),
      wfo, bfo, wo, bo, lng, lnb)


def _router_kernel(hhi_ref, hlo_ref, whi_ref, wlo_ref, bias_ref, tri_ref, ltri_ref, utri_ref,
                   eidx_ref, gate_ref, rank_ref, pos_ref, tcnt_ref, tcum_ref, tbase_ref, tmax_ref, cnt_ref,
                   base_ref, brow_ref):
    n_e = whi_ref.shape[0]
    tm = hhi_ref.shape[0]
    gsz = n_e // N_EXPERT_GROUPS
    nt = (((1,), (1,)), ((), ()))

    @pl.when(pl.program_id(0) == 0)
    def _():
        base_ref[...] = jnp.zeros_like(base_ref)
        brow_ref[...] = jnp.zeros_like(brow_ref)

    hhi = hhi_ref[...]
    logits = lax.dot_general(whi_ref[...], hhi, nt, preferred_element_type=F32)
    logits = logits + lax.dot_general(wlo_ref[...], hhi, nt, preferred_element_type=F32)
    logits = logits + lax.dot_general(whi_ref[...], hlo_ref[...], nt, preferred_element_type=F32)
    scores = jax.nn.sigmoid(logits)
    sel = scores + bias_ref[:, 0:1]

    g3 = sel.reshape(N_EXPERT_GROUPS, gsz, tm)
    i3 = lax.broadcasted_iota(I32, g3.shape, 1).astype(F32)
    m1 = jnp.max(g3, axis=1, keepdims=True)
    first = jnp.min(jnp.where(g3 == m1, i3, float(gsz)), axis=1, keepdims=True)
    m2 = jnp.max(jnp.where(i3 == first, NEG_INF, g3), axis=1, keepdims=True)
    gs = (m1 + m2).reshape(N_EXPERT_GROUPS, tm)

    gi = lax.broadcasted_iota(I32, gs.shape, 0).astype(F32)
    gsel = jnp.zeros(gs.shape, F32)
    cur = gs
    for _ in range(TOPK_GROUPS):
        m = jnp.max(cur, axis=0, keepdims=True)
        f = jnp.min(jnp.where(cur == m, gi, float(N_EXPERT_GROUPS)), axis=0, keepdims=True)
        pick = gi == f
        gsel = jnp.where(pick, 1.0, gsel)
        cur = jnp.where(pick, NEG_INF, cur)
    gmask = jnp.broadcast_to(gsel.reshape(N_EXPERT_GROUPS, 1, tm), g3.shape).reshape(n_e, tm)
    masked = jnp.where(gmask > 0.5, sel, NEG_INF)

    ri = lax.broadcasted_iota(I32, (n_e, tm), 0).astype(F32)
    picks = []
    gates = []
    multihot = jnp.zeros((n_e, tm), F32)
    for _ in range(TOP_K):
        m = jnp.max(masked, axis=0, keepdims=True)
        f = jnp.min(jnp.where(masked == m, ri, float(n_e)), axis=0, keepdims=True)
        pick = ri == f
        picks.append(f)
        gates.append(jnp.sum(jnp.where(pick, scores, 0.0), axis=0, keepdims=True))
        multihot = jnp.where(pick, 1.0, multihot)
        masked = jnp.where(pick, NEG_INF, masked)
    gsum = gates[0]
    for g in gates[1:]:
        gsum = gsum + g

    mh = multihot.astype(BF16)
    rank_local = jnp.dot(mh, tri_ref[...], preferred_element_type=F32)
    n_col = jnp.sum(multihot, axis=1, keepdims=True)
    cum_col = jnp.dot(ltri_ref[...], jnp.broadcast_to(n_col, (n_e, LANE)).astype(BF16),
                      preferred_element_type=F32)[:, 0:1]
    n_row = lax.dot_general(jnp.ones((8, tm), BF16), mh, nt, preferred_element_type=F32)
    cum_row = jnp.dot(n_row.astype(BF16), utri_ref[...], preferred_element_type=F32)
    base_col = base_ref[:, 0:1]
    for k in range(TOP_K):
        pick = ri == picks[k]
        lookup = lambda m: jnp.sum(jnp.where(pick, m, 0.0), axis=0, keepdims=True)
        rl = lookup(rank_local)
        eidx_ref[k:k + 1, :] = picks[k].astype(I32)
        gate_ref[k:k + 1, :] = gates[k] / gsum * ROUTED_SCALE
        rank_ref[k:k + 1, :] = (rl + lookup(base_col)).astype(I32)
        pos_ref[k:k + 1, :] = (rl + lookup(cum_col)).astype(I32)
    tcnt_ref[0] = n_row[0:1, :].astype(I32)
    tcum_ref[0] = cum_row[0:1, :].astype(I32)
    tbase_ref[0] = brow_ref[0:1, :].astype(I32)
    cur = n_row[0:1, :]
    e_lane = lax.broadcasted_iota(I32, cur.shape, 1).astype(F32)
    o_lane = lax.broadcasted_iota(I32, (1, LANE), 1)
    ovf = jnp.where(o_lane == N_OVF, jnp.sum(jnp.where(cur > SEG_ROWS, 1.0, 0.0), axis=1, keepdims=True), -1.0)
    for j in range(N_OVF):
        m = jnp.max(cur, axis=1, keepdims=True)
        idx = jnp.min(jnp.where(cur == m, e_lane, float(n_e)), axis=1, keepdims=True)
        ovf = jnp.where(jnp.logical_and(o_lane == j, m > SEG_ROWS), idx, ovf)
        cur = jnp.where(e_lane == idx, -1.0, cur)
    tmax_ref[0] = ovf.astype(I32)
    brow_ref[...] = brow_ref[...] + n_row
    base_ref[...] = base_ref[...] + n_col
    cnt_ref[...] = base_ref[...].astype(I32)


def _router(hhi, hlo, wrt_hi, wrt_lo, bias):
    t, d = hhi.shape
    n_e = wrt_hi.shape[0]
    tm = MOE_TILE
    n_tile = t // tm
    tri = (jnp.arange(tm)[:, None] < jnp.arange(tm)[None, :]).astype(BF16)
    utri = (jnp.arange(n_e)[:, None] < jnp.arange(n_e)[None, :]).astype(BF16)
    kt = lambda i: (0, i)
    tab = lambda i: (i, 0, 0)
    return pl.pallas_call(
        _router_kernel,
        grid=(n_tile,),
        in_specs=[pl.BlockSpec((tm, d), lambda i: (i, 0)), pl.BlockSpec((tm, d), lambda i: (i, 0)),
                  _const_spec((n_e, d)), _const_spec((n_e, d)), _const_spec((n_e, LANE)),
                  _const_spec((tm, tm)), _const_spec((n_e, n_e)), _const_spec((n_e, n_e))],
        out_specs=[pl.BlockSpec((TOP_K, tm), kt), pl.BlockSpec((TOP_K, tm), kt), pl.BlockSpec((TOP_K, tm), kt),
                   pl.BlockSpec((TOP_K, tm), kt),
                   pl.BlockSpec((1, 1, n_e), tab), pl.BlockSpec((1, 1, n_e), tab), pl.BlockSpec((1, 1, n_e), tab),
                   pl.BlockSpec((1, 1, LANE), tab),
                   pl.BlockSpec((n_e, LANE), lambda i: (0, 0))],
        out_shape=[jax.ShapeDtypeStruct((TOP_K, t), I32), jax.ShapeDtypeStruct((TOP_K, t), F32),
                   jax.ShapeDtypeStruct((TOP_K, t), I32), jax.ShapeDtypeStruct((TOP_K, t), I32),
                   jax.ShapeDtypeStruct((n_tile, 1, n_e), I32), jax.ShapeDtypeStruct((n_tile, 1, n_e), I32),
                   jax.ShapeDtypeStruct((n_tile, 1, n_e), I32), jax.ShapeDtypeStruct((n_tile, 1, LANE), I32),
                   jax.ShapeDtypeStruct((n_e, LANE), I32)],
        scratch_shapes=[pltpu.VMEM((n_e, LANE), F32), pltpu.VMEM((8, n_e), F32)],
        compiler_params=_cparams(("arbitrary",)),
        name="router",
    )(hhi, hlo, wrt_hi, wrt_lo, jnp.broadcast_to(bias.astype(F32).reshape(n_e, 1), (n_e, LANE)), tri,
      utri.T, utri)


def _dest_kernel(eidx_ref, rank_ref, pstart_ref, dest_ref):
    n_e = pstart_ref.shape[0]
    tm = eidx_ref.shape[1]
    ri = lax.broadcasted_iota(I32, (n_e, tm), 0)
    ps = pstart_ref[:, 0:1].astype(F32)
    for k in range(TOP_K):
        hit = ri == eidx_ref[k:k + 1, :]
        base = jnp.sum(jnp.where(hit, ps, 0.0), axis=0, keepdims=True)
        dest_ref[k:k + 1, :] = base.astype(I32) + rank_ref[k:k + 1, :]


def _dest(eidx, rank, pstart):
    k, t = eidx.shape
    n_e = pstart.shape[0]
    tm = 2048
    kt = lambda i: (0, i)
    return pl.pallas_call(
        _dest_kernel,
        grid=(t // tm,),
        in_specs=[pl.BlockSpec((k, tm), kt), pl.BlockSpec((k, tm), kt), _const_spec((n_e, LANE))],
        out_specs=pl.BlockSpec((k, tm), kt),
        out_shape=jax.ShapeDtypeStruct((k, t), I32),
        compiler_params=_cparams(("parallel",)),
        name="dest",
    )(eidx, rank, jnp.broadcast_to(pstart.astype(I32).reshape(n_e, 1), (n_e, LANE)))


def _zero_fill(pad_lo_ref, pad_n_ref, zeros_ref, rows_ref, sem):
    def each_copy(act):
        def per_entry(e, _):
            lo = pad_lo_ref[e]
            n = pad_n_ref[e]
            n_full = n // FFN_BLOCK

            def full(j, _):
                act(pltpu.make_async_copy(
                    zeros_ref, rows_ref.at[pl.ds(pl.multiple_of((lo + j * FFN_BLOCK) * PACK_SUB, PACK_SUB),
                                                 FFN_BLOCK * PACK_SUB), :], sem))
                return 0

            lax.fori_loop(0, n_full, full, 0)
            off = lo + n_full * FFN_BLOCK
            rem = n - n_full * FFN_BLOCK
            bit = FFN_BLOCK // 2
            while bit >= 1:
                take = rem & bit

                @pl.when(take != 0)
                def _(off=off, bit=bit):
                    act(pltpu.make_async_copy(
                        zeros_ref.at[pl.ds(0, bit * PACK_SUB), :],
                        rows_ref.at[pl.ds(pl.multiple_of(off * PACK_SUB, PACK_SUB), bit * PACK_SUB), :], sem))

                off = off + take
                bit //= 2
            return 0

        lax.fori_loop(0, pad_lo_ref.shape[0], per_entry, 0)

    each_copy(lambda cp: cp.start())
    each_copy(lambda cp: cp.wait())


def _seg_slab(ref, row, n_rows=SEG_ROWS):
    return ref.at[pl.ds(pl.multiple_of(row * PACK_SUB, PACK_SUB), n_rows * PACK_SUB), :]


def _dispatch_kernel(pad_lo_ref, pad_n_ref, pstart_ref, tcnt_ref, tcum_ref, tbase_ref, tmax_ref, pos_ref, h_ref,
                     rows_ref, seg_ref, zeros_ref, sem, xsem, zsem):
    i = pl.program_id(0)
    tm = h_ref.shape[0]
    n_e = tcnt_ref.shape[2]
    n_loc = TOP_K * tm
    seg0 = (i % 2) * (n_loc + SEG_ROWS)

    @pl.when(i == 0)
    def _():
        zeros_ref[...] = jnp.zeros_like(zeros_ref)
        _zero_fill(pad_lo_ref, pad_n_ref, zeros_ref, rows_ref, zsem)
        for s in range(2):
            seg_ref[pl.ds((s * (n_loc + SEG_ROWS) + n_loc) * PACK_SUB, SEG_ROWS * PACK_SUB), :] = jnp.zeros(
                (SEG_ROWS * PACK_SUB, LANE), U32)

    p_idx = lax.broadcasted_iota(I32, (n_loc, tm), 0)
    onehot = jnp.zeros((n_loc, tm), F32)
    for k in range(TOP_K):
        onehot = jnp.where(p_idx == pos_ref[k:k + 1, :], 1.0, onehot)
    _pack_rows(seg_ref, jnp.dot(onehot.astype(BF16), h_ref[...], preferred_element_type=F32), first=seg0,
               is_bf16_exact=True)

    def wait_tile():
        full = rows_ref.at[pl.ds(0, n_e * SEG_ROWS * PACK_SUB), :]
        pltpu.make_async_copy(full, full, sem).wait()

    @pl.when(i > 0)
    def _():
        wait_tile()

    unroll = 8

    def first_chunks(g, _):
        for u in range(unroll):
            e = g * unroll + u
            pltpu.make_async_copy(_seg_slab(seg_ref, seg0 + tcum_ref[0, 0, e]),
                                  _seg_slab(rows_ref, pstart_ref[e] + tbase_ref[0, 0, e]), sem).start()
        return 0

    lax.fori_loop(0, n_e // unroll, first_chunks, 0)

    def rest_of_segment(e):
        src = seg0 + tcum_ref[0, 0, e]
        dst = pstart_ref[e] + tbase_ref[0, 0, e]

        def extra(c, _):
            cp = pltpu.make_async_copy(_seg_slab(seg_ref, src + c * SEG_ROWS),
                                       _seg_slab(rows_ref, dst + c * SEG_ROWS), xsem)
            cp.start()
            cp.wait()
            return 0

        lax.fori_loop(1, (tcnt_ref[0, 0, e] + SEG_ROWS - 1) // SEG_ROWS, extra, 0)

    n_ovf = tmax_ref[0, 0, N_OVF]
    for j in range(N_OVF):
        @pl.when(jnp.logical_and(n_ovf <= N_OVF, tmax_ref[0, 0, j] >= 0))
        def _(j=j):
            rest_of_segment(tmax_ref[0, 0, j])

    @pl.when(n_ovf > N_OVF)
    def _():
        def per_expert(e, _):
            rest_of_segment(e)
            return 0

        lax.fori_loop(0, n_e, per_expert, 0)

    @pl.when(i == pl.num_programs(0) - 1)
    def _():
        wait_tile()


def _dispatch(hhi, pos, tcnt, tcum, tbase, tmax, pstart, pad_lo, pad_n, n_rows):
    t, d = hhi.shape
    n_tile, _, n_e = tcnt.shape
    tm = t // n_tile
    n_loc = TOP_K * tm
    tab = pl.BlockSpec((1, 1, n_e), lambda i, *_: (i, 0, 0), memory_space=pltpu.SMEM)
    return pl.pallas_call(
        _dispatch_kernel,
        grid_spec=pltpu.PrefetchScalarGridSpec(
            num_scalar_prefetch=3,
            grid=(n_tile,),
            in_specs=[tab, tab, tab,
                      pl.BlockSpec((1, 1, LANE), lambda i, *_: (i, 0, 0), memory_space=pltpu.SMEM),
                      pl.BlockSpec((TOP_K, tm), lambda i, *_: (0, i)),
                      pl.BlockSpec((tm, d), lambda i, *_: (i, 0))],
            out_specs=pl.BlockSpec(memory_space=pl.ANY),
            scratch_shapes=[pltpu.VMEM((2 * (n_loc + SEG_ROWS) * PACK_SUB, LANE), U32),
                            pltpu.VMEM((FFN_BLOCK * PACK_SUB, LANE), U32),
                            pltpu.SemaphoreType.DMA, pltpu.SemaphoreType.DMA, pltpu.SemaphoreType.DMA],
        ),
        out_shape=jax.ShapeDtypeStruct((n_rows * PACK_SUB, LANE), U32),
        compiler_params=_cparams(("arbitrary",)),
        name="dispatch",
    )(pad_lo, pad_n, pstart, tcnt, tcum, tbase, tmax, pos, hhi)


def _ffn_kernel(blk_e_ref, n_used_ref, rows_ref, wg_ref, wu_ref, wd_ref, y_ref):
    live = pl.program_id(0) < n_used_ref[0]

    @pl.when(live)
    def _():
        x = _unpack_rows(rows_ref, FFN_BLOCK).astype(BF16)
        a = jnp.dot(x, wg_ref[0].astype(BF16), preferred_element_type=F32)
        u = jnp.dot(x, wu_ref[0].astype(BF16), preferred_element_type=F32)
        hid = (_silu(a) * u).astype(BF16)
        _pack_rows(y_ref, jnp.dot(hid, wd_ref[0].astype(BF16), preferred_element_type=F32))

    @pl.when(jnp.logical_not(live))
    def _():
        y_ref[...] = jnp.zeros_like(y_ref)


def _ffn(rows, blk_e, blk_valid, wg, wu, wd):
    n_rows = rows.shape[0] // PACK_SUB
    n_e, d, d_e = wg.shape
    n_blk = n_rows // FFN_BLOCK
    blk = lambda b, e, n: (b, 0)
    exp = lambda b, e, n: (e[b], 0, 0)
    return pl.pallas_call(
        _ffn_kernel,
        grid_spec=pltpu.PrefetchScalarGridSpec(
            num_scalar_prefetch=2,
            grid=(n_blk,),
            in_specs=[pl.BlockSpec((FFN_BLOCK * PACK_SUB, LANE), blk),
                      pl.BlockSpec((1, d, d_e), exp), pl.BlockSpec((1, d, d_e), exp),
                      pl.BlockSpec((1, d_e, d), exp)],
            out_specs=pl.BlockSpec((FFN_BLOCK * PACK_SUB, LANE), blk),
        ),
        out_shape=jax.ShapeDtypeStruct((n_rows * PACK_SUB, LANE), U32),
        compiler_params=_cparams(("arbitrary",)),
        name="ffn",
    )(blk_e, blk_valid, rows, wg, wu, wd)


def _final_kernel(dest_ref, x1_ref, h_ref, gate_ref, g2_ref, wsg_ref, wsu_ref, wsd_ref, lng_ref, lnb_ref,
                  y_ref, o_ref, ybuf_ref, sem, *, alpha):
    tm = x1_ref.shape[0]

    def body(tt, _):
        for k in range(TOP_K):
            pltpu.make_async_copy(_row_slab(y_ref, dest_ref[0, 0, k * tm + tt]),
                                  _row_slab(ybuf_ref, k * tm + tt), sem).start()
        return 0

    lax.fori_loop(0, tm, body, 0)

    h = h_ref[...]
    a = jnp.dot(h, wsg_ref[...], preferred_element_type=F32)
    u = jnp.dot(h, wsu_ref[...], preferred_element_type=F32)
    shared = jnp.dot((_silu(a) * u).astype(BF16), wsd_ref[...], preferred_element_type=F32)

    pltpu.make_async_copy(ybuf_ref, ybuf_ref, sem).wait()
    routed = gate_ref[:, 0:1] * _unpack_rows(ybuf_ref, tm)
    for k in range(1, TOP_K):
        routed = routed + gate_ref[:, k:k + 1] * _unpack_rows(ybuf_ref, tm, first=k * tm)

    v = alpha * x1_ref[...] + g2_ref[0] * (shared + routed)
    o_ref[...] = _standardize(v) * lng_ref[...] + lnb_ref[...]


def _final(x1, hhi, y_rows, dest_tiles, gate_t, g2, wsg, wsu, wsd, lng, lnb, seq, alpha):
    t, d = x1.shape
    d_sh = wsg.shape[1]
    n_tile, _, per_tile = dest_tiles.shape
    tm = per_tile // TOP_K
    tpb = seq // tm
    bsz = t // seq
    row = lambda i: (i, 0)
    return pl.pallas_call(
        functools.partial(_final_kernel, alpha=alpha),
        grid=(n_tile,),
        in_specs=[pl.BlockSpec((1, 1, per_tile), lambda i: (i, 0, 0), memory_space=pltpu.SMEM),
                  pl.BlockSpec((tm, d), row), pl.BlockSpec((tm, d), row), pl.BlockSpec((tm, TOP_K), row),
                  pl.BlockSpec((1, 1, d), lambda i: (i // tpb, 0, 0)),
                  _const_spec((d, d_sh)), _const_spec((d, d_sh)), _const_spec((d_sh, d)),
                  _const_spec((1, d)), _const_spec((1, d)),
                  pl.BlockSpec(memory_space=pl.ANY)],
        out_specs=pl.BlockSpec((tm, d), row),
        out_shape=jax.ShapeDtypeStruct((t, d), F32),
        scratch_shapes=[pltpu.VMEM((TOP_K * tm * PACK_SUB, LANE), U32), pltpu.SemaphoreType.DMA],
        compiler_params=_cparams(("arbitrary",)),
        name="final",
    )(dest_tiles, x1, hhi, gate_t, g2.reshape(bsz, 1, d), wsg, wsu, wsd, lng, lnb, y_rows)


def _split_hi_lo(w):
    hi = w.astype(BF16)
    return hi, (w - hi.astype(F32)).astype(BF16)


def kernel(x, c, w_ada, b_ada, w_in, b_in, s5_lambda_re, s5_lambda_im, s5_log_dt, s5_b_re, s5_b_im, s5_c_re, s5_c_im, s5_d, w_s5_glu, b_s5_glu, w_fourier, b_fourier, w_out, b_out, ln1_g, ln1_b, w_router, router_bias, w_exp_gate, w_exp_up, w_exp_down, w_sh_gate, w_sh_up, w_sh_down, ln2_g, ln2_b):
    bsz, seq, d = x.shape
    depth = w_ada.shape[0]
    alpha = (2 * depth) ** 0.25
    t = bsz * seq
    d_s5 = s5_d.shape[1]
    d_f = w_fourier.shape[1]
    fw = d_f // FOURIER_GROUPS
    n_e = w_router.shape[2]
    n_chunk = seq // LANE
    fmat, cmat = _dft_tables(seq, fw)
    row = lambda v: v.astype(F32).reshape(1, -1)

    x2 = x.reshape(t, d)
    for l in range(depth):
        mod = _adaln(c, w_ada[l], b_ada[l])
        sh1, sc1, g1, sh2, sc2, g2 = jnp.split(mod, 6, axis=-1)

        wi = w_in[l]
        bi = b_in[l].astype(F32)
        ws5t = wi[:, :d_s5].T.astype(BF16)
        bs5 = jnp.broadcast_to(bi[:d_s5].reshape(d_s5, 1), (d_s5, LANE))
        us5, z, gates = _proj(x2, sc1, sh1, ws5t, bs5,
                              wi[:, d_s5:d_s5 + d_f].astype(BF16), row(bi[d_s5:d_s5 + d_f]), cmat,
                              wi[:, d_s5 + d_f:].astype(BF16), row(bi[d_s5 + d_f:]), seq)

        tables = _s5_tables(s5_lambda_re[l], s5_lambda_im[l], s5_log_dt[l], s5_b_re[l], s5_b_im[l],
                            s5_c_re[l], s5_c_im[l], s5_d[l])
        ys5 = _s5(us5, tables, n_chunk)
        brs = _glu(ys5, w_s5_glu[l].astype(BF16), row(b_s5_glu[l]))
        yf = _seqdft(fmat, z, seq, fw)

        x1, hhi, hlo = _merge(x2, brs, yf, gates, g1, sc2, sh2,
                              w_fourier[l].astype(BF16), row(b_fourier[l]),
                              w_out[l].astype(BF16), row(b_out[l]), row(ln1_g[l]), row(ln1_b[l]), seq, alpha)

        wrt_hi, wrt_lo = _split_hi_lo(w_router[l].astype(F32).T)
        eidx, gate, rank, pos, tcnt, tcum, tbase, tmax, cnt = _router(hhi, hlo, wrt_hi, wrt_lo, router_bias[l])

        counts = cnt[:, 0]
        padded = ((counts + SEG_ROWS + FFN_BLOCK - 1) // FFN_BLOCK) * FFN_BLOCK
        pend = jnp.cumsum(padded)
        pstart = (pend - padded).astype(I32)
        dest = _dest(eidx, rank, pstart)
        n_blk = (t * TOP_K + n_e * (SEG_ROWS + FFN_BLOCK - 1) + FFN_BLOCK - 1) // FFN_BLOCK
        n_rows = n_blk * FFN_BLOCK
        blk_start = jnp.arange(n_blk, dtype=I32) * FFN_BLOCK
        blk_e = jnp.minimum(jnp.sum((pend[None, :] <= blk_start[:, None]).astype(I32), axis=1), n_e - 1)
        n_used = (pend[-1:] // FFN_BLOCK).astype(I32)
        pad_lo = jnp.concatenate([pstart + counts, pend[-1:]]).astype(I32)
        pad_n = jnp.concatenate([padded - counts, n_rows - pend[-1:]]).astype(I32)

        def tiles(tm):
            return dest.reshape(TOP_K, t // tm, tm).transpose(1, 0, 2).reshape(t // tm, 1, TOP_K * tm)

        rows = _dispatch(hhi, pos, tcnt, tcum, tbase, tmax, pstart, pad_lo, pad_n, n_rows)
        y_rows = _ffn(rows, blk_e, n_used, w_exp_gate[l], w_exp_up[l], w_exp_down[l])
        x2 = _final(x1, hhi, y_rows, tiles(MOE_TILE), gate.T, g2,
                    w_sh_gate[l].astype(BF16), w_sh_up[l].astype(BF16), w_sh_down[l].astype(BF16),
                    row(ln2_g[l]), row(ln2_b[l]), seq, alpha)
    return x2.reshape(bsz, seq, d)
```

```python
import functools
import math

import jax
import jax.numpy as jnp
from jax import lax
from jax.experimental import pallas as pl
from jax.experimental.pallas import tpu as pltpu

F32 = jnp.float32
BF16 = jnp.bfloat16
I32 = jnp.int32

TOP_K = 8
N_EXPERT_GROUPS = 8
TOPK_GROUPS = 4
ROUTED_SCALE = 2.5
FOURIER_GROUPS = 4
LN_EPS = 1e-5

LANE = 128
VMEM_LIMIT = 56 * 1024 * 1024

HIGHEST = lax.Precision.HIGHEST
NEG_INF = float("-inf")

MOE_TILE = 256
FFN_BLOCK = 256
SEG_ROWS = 16
N_OVF = 4


def _cparams(sem):
    return pltpu.CompilerParams(dimension_semantics=sem, vmem_limit_bytes=VMEM_LIMIT)


def _const_spec(shape):
    nd = len(shape)
    return pl.BlockSpec(shape, lambda *_: (0,) * nd, pipeline_mode=pl.Buffered(1))


def _standardize(x):
    mu = jnp.mean(x, axis=-1, keepdims=True)
    xc = x - mu
    var = jnp.mean(xc * xc, axis=-1, keepdims=True)
    return xc * lax.rsqrt(var + LN_EPS)


def _silu(x):
    return x * jax.nn.sigmoid(x)


def _gelu_tanh(x):
    return 0.5 * x * (1.0 + jnp.tanh(math.sqrt(2.0 / math.pi) * (x + 0.044715 * (x * x * x))))


def _adaln_kernel(c_ref, w_ref, b_ref, o_ref):
    a = _silu(c_ref[...])
    o_ref[...] = jnp.dot(a, w_ref[...], precision=HIGHEST, preferred_element_type=F32) + b_ref[...]


def _adaln(c, w, b):
    bsz, d = c.shape
    n = w.shape[1]
    tn = 512
    return pl.pallas_call(
        _adaln_kernel,
        grid=(n // tn,),
        in_specs=[pl.BlockSpec((bsz, d), lambda j: (0, 0)),
                  pl.BlockSpec((d, tn), lambda j: (0, j)),
                  pl.BlockSpec((1, tn), lambda j: (0, j))],
        out_specs=pl.BlockSpec((bsz, tn), lambda j: (0, j)),
        out_shape=jax.ShapeDtypeStruct((bsz, n), F32),
        compiler_params=_cparams(("parallel",)),
        name="adaln",
    )(c, w, b.reshape(1, n))


def _proj_kernel(x_ref, sc_ref, sh_ref, ws5t_ref, bs5_ref, wf_ref, bf_ref, cs_ref, wg_ref, bg_ref,
                 us5_ref, z_ref, gates_ref):
    tm = x_ref.shape[0]
    u = (_standardize(x_ref[...]) * (1.0 + sc_ref[0]) + sh_ref[0]).astype(BF16)
    p = lax.dot_general(ws5t_ref[...], u, (((1,), (1,)), ((), ())), preferred_element_type=F32)
    p = p + bs5_ref[:, 0:1]
    for j in range(tm // LANE):
        us5_ref[:, j, :] = p[:, j * LANE:(j + 1) * LANE]
    uf = (jnp.dot(u, wf_ref[...], preferred_element_type=F32) + bf_ref[...]).astype(BF16)
    d_f = uf.shape[1]
    fw = d_f // FOURIER_GROUPS
    for q in range(FOURIER_GROUPS):
        zq = jnp.dot(uf[:, q * fw:(q + 1) * fw], cs_ref[...], preferred_element_type=F32)
        z_ref[:, q * fw:(q + 1) * fw] = zq[:, :fw].astype(BF16)
        z_ref[:, d_f + q * fw:d_f + (q + 1) * fw] = zq[:, fw:].astype(BF16)
    n_g = wg_ref.shape[1]
    half = n_g // 2
    for q in range(2):
        gp = jnp.dot(u, wg_ref[:, q * half:(q + 1) * half], preferred_element_type=F32)
        gp = gp + bg_ref[:, q * half:(q + 1) * half]
        gates_ref[:, q * half:(q + 1) * half] = jax.nn.sigmoid(gp).astype(BF16)


def _proj(x2, sc, sh, ws5t, bs5, wf, bf, cs, wg, bg, seq):
    t, d = x2.shape
    d_s5 = ws5t.shape[0]
    d_f = wf.shape[1]
    n_g = wg.shape[1]
    tm = 1024
    tpb = seq // tm
    bsz = t // seq
    return pl.pallas_call(
        _proj_kernel,
        grid=(t // tm,),
        in_specs=[pl.BlockSpec((tm, d), lambda i: (i, 0)),
                  pl.BlockSpec((1, 1, d), lambda i: (i // tpb, 0, 0)),
                  pl.BlockSpec((1, 1, d), lambda i: (i // tpb, 0, 0)),
                  _const_spec((d_s5, d)), _const_spec((d_s5, LANE)),
                  _const_spec((d, d_f)), _const_spec((1, d_f)),
                  _const_spec(cs.shape),
                  _const_spec((d, n_g)), _const_spec((1, n_g))],
        out_specs=[pl.BlockSpec((d_s5, tm // LANE, LANE), lambda i: (0, i, 0)),
                   pl.BlockSpec((tm, 2 * d_f), lambda i: (i, 0)),
                   pl.BlockSpec((tm, n_g), lambda i: (i, 0))],
        out_shape=[jax.ShapeDtypeStruct((d_s5, t // LANE, LANE), F32),
                   jax.ShapeDtypeStruct((t, 2 * d_f), BF16),
                   jax.ShapeDtypeStruct((t, n_g), BF16)],
        compiler_params=_cparams(("parallel",)),
        name="proj",
    )(x2, sc.reshape(bsz, 1, d), sh.reshape(bsz, 1, d), ws5t, bs5, wf, bf, cs, wg, bg)


def _s5_tables(lam_re, lam_im, log_dt, b_re, b_im, c_re, c_im, d_skip):
    L = LANE
    hp = HIGHEST
    lr, li = lam_re.astype(F32), lam_im.astype(F32)
    dt = jnp.exp(log_dt.astype(F32))[:, :, None]
    mag = jnp.exp(lr * dt)
    ang = li * dt
    ab_re, ab_im = mag * jnp.cos(ang), mag * jnp.sin(ang)
    den = lr * lr + li * li
    nr = ab_re - 1.0
    coef_re = (nr * lr + ab_im * li) / den
    coef_im = (ab_im * lr - nr * li) / den
    br, bi = b_re.astype(F32), b_im.astype(F32)
    bb_re = coef_re[..., None] * br - coef_im[..., None] * bi
    bb_im = coef_re[..., None] * bi + coef_im[..., None] * br
    cr, ci = c_re.astype(F32), c_im.astype(F32)
    n_g, n_p, n_h = br.shape[1], br.shape[2], br.shape[3]

    k = jnp.arange(L + 1, dtype=F32)[None, None, :, None]
    pmag = jnp.exp(k * (lr * dt)[:, :, None, :])
    pang = k * (li * dt)[:, :, None, :]
    pw_re, pw_im = pmag * jnp.cos(pang), pmag * jnp.sin(pang)

    m_re = cr[:, :, :, None, :] * jnp.swapaxes(bb_re, 2, 3)[:, :, None, :, :] \
        - ci[:, :, :, None, :] * jnp.swapaxes(bb_im, 2, 3)[:, :, None, :, :]
    m_im = cr[:, :, :, None, :] * jnp.swapaxes(bb_im, 2, 3)[:, :, None, :, :] \
        + ci[:, :, :, None, :] * jnp.swapaxes(bb_re, 2, 3)[:, :, None, :, :]
    kap = jnp.einsum("dgohp,dgkp->dgohk", m_re, pw_re, precision=hp) \
        - jnp.einsum("dgohp,dgkp->dgohk", m_im, pw_im, precision=hp)
    kpos = kap[0, ..., :L].reshape(n_g, n_h * n_h, L)
    kb = kap[1]
    kneg = jnp.concatenate([kb[..., 0:1], kb[..., L - 1:0:-1]], axis=-1)
    kneg = kneg.reshape(n_g, n_h * n_h, L)

    pf_re, pf_im = pw_re[0, :, L - 1::-1][:, :L], pw_im[0, :, L - 1::-1][:, :L]
    pb_re, pb_im = pw_re[1, :, :L], pw_im[1, :, :L]

    def cmul_hs(p_re, p_im, q_re, q_im):
        a = p_re[:, None, :, :]
        b = p_im[:, None, :, :]
        c = jnp.swapaxes(q_re, 1, 2)[:, :, None, :]
        e = jnp.swapaxes(q_im, 1, 2)[:, :, None, :]
        return a * c - b * e, a * e + b * c

    wsf_re, wsf_im = cmul_hs(pf_re, pf_im, bb_re[0], bb_im[0])
    wsb_re, wsb_im = cmul_hs(pb_re, pb_im, bb_re[1], bb_im[1])
    ws = jnp.concatenate([wsf_re, wsf_im, wsb_re, wsb_im], axis=-1).reshape(n_g, n_h * L, 4 * n_p)

    qf_re, qf_im = pw_re[0, :, 1:L + 1], pw_im[0, :, 1:L + 1]
    qb_re, qb_im = pw_re[1, :, L:0:-1], pw_im[1, :, L:0:-1]

    def cmul_ot(c_r, c_i, q_r, q_i):
        a = jnp.swapaxes(c_r, 1, 2)[:, :, :, None]
        b = jnp.swapaxes(c_i, 1, 2)[:, :, :, None]
        c = jnp.swapaxes(q_r, 1, 2)[:, :, None, :]
        e = jnp.swapaxes(q_i, 1, 2)[:, :, None, :]
        return a * c - b * e, -(a * e + b * c)

    wof_re, wof_mi = cmul_ot(cr[0], ci[0], qf_re, qf_im)
    wob_re, wob_mi = cmul_ot(cr[1], ci[1], qb_re, qb_im)
    wo = jnp.concatenate([wof_re, wof_mi, wob_re, wob_mi], axis=1)
    wo = jnp.swapaxes(wo.reshape(n_g, 4 * n_p, n_h // 2, 2 * L), 1, 2)

    al_re, al_im = pw_re[:, :, L], pw_im[:, :, L]
    alx = jnp.concatenate([al_re[0], al_re[0], al_re[1], al_re[1]], axis=-1)
    aly = jnp.concatenate([-al_im[0], al_im[0], -al_im[1], al_im[1]], axis=-1)
    al = jnp.stack([alx, aly], axis=1)

    dsk = jnp.broadcast_to(d_skip.astype(F32).reshape(n_g, n_h, 1), (n_g, n_h, L))
    return kpos, kneg, ws.astype(BF16), wo.astype(BF16), al, dsk


def _s5_kernel(a_ref, kpos_ref, kneg_ref, ws_ref, wo_ref, al_ref, d_ref, y_ref,
               abf_ref, tp_ref, sf_ref, sb_ref, xf_ref, xb_ref, *, n_chunk):
    n_h, r, L = a_ref.shape
    bsz = r // n_chunk
    half = 2 * (ws_ref.shape[2] // 4)

    for h in range(n_h):
        abf_ref[:, h * L:(h + 1) * L] = a_ref[h].astype(BF16)
    abf = abf_ref[...]

    s_all = jnp.dot(abf, ws_ref[0], preferred_element_type=F32)
    sf_ref[...] = s_all[:, :half]
    sb_ref[...] = s_all[:, half:]

    alx = al_ref[0, 0:1, :]
    aly = al_ref[0, 1:2, :]

    def carry(e, lo):
        ex = e * alx[:, lo:lo + half]
        ey = pltpu.roll(e, half // 2, 1) * aly[:, lo:lo + half]
        return ex + ey

    e = jnp.zeros((bsz, half), F32)
    for c in range(n_chunk):
        xf_ref[pl.ds(c, bsz, stride=n_chunk), :] = e
        e = carry(e, 0) + sf_ref[pl.ds(c, bsz, stride=n_chunk), :]
    e = jnp.zeros((bsz, half), F32)
    for c in range(n_chunk - 1, -1, -1):
        xb_ref[pl.ds(c, bsz, stride=n_chunk), :] = e
        e = carry(e, half) + sb_ref[pl.ds(c, bsz, stride=n_chunk), :]
    xin = jnp.concatenate([xf_ref[...], xb_ref[...]], axis=1).astype(BF16)

    s_idx = lax.broadcasted_iota(I32, (L, L), 0)
    t_idx = lax.broadcasted_iota(I32, (L, L), 1)
    ge = t_idx >= s_idx
    le = t_idx <= s_idx

    def pair(op, _):
        for oo in range(2):
            o = 2 * op + oo
            for h in range(n_h):
                row = o * n_h + h
                kp = jnp.broadcast_to(kpos_ref[0, pl.ds(row, 1), :], (L, L))
                kn = jnp.broadcast_to(kneg_ref[0, pl.ds(row, 1), :], (L, L))
                tp = jnp.where(ge, pltpu.roll(kp, 0, 1, stride=1, stride_axis=0), 0.0)
                tn = jnp.where(le, pltpu.roll(kn, 0, 1, stride=1, stride_axis=0), 0.0)
                tp_ref[h * L:(h + 1) * L, oo * L:(oo + 1) * L] = (tp + tn).astype(BF16)
        yp = jnp.dot(abf, tp_ref[...], preferred_element_type=F32)
        yp = yp + jnp.dot(xin, wo_ref[0, op], preferred_element_type=F32)
        for oo in range(2):
            o = 2 * op + oo
            y_ref[o] = yp[:, oo * L:(oo + 1) * L] + a_ref[o] * d_ref[0, pl.ds(o, 1), :]
        return 0

    lax.fori_loop(0, n_h // 2, pair, 0)


def _s5(us5, tables, n_chunk):
    kpos, kneg, ws, wo, al, dsk = tables
    d_s5, r, L = us5.shape
    n_g = kpos.shape[0]
    n_h = d_s5 // n_g
    n_st = ws.shape[2]
    g3 = lambda g: (g, 0, 0)
    return pl.pallas_call(
        functools.partial(_s5_kernel, n_chunk=n_chunk),
        grid=(n_g,),
        in_specs=[pl.BlockSpec((n_h, r, L), g3),
                  pl.BlockSpec((1, n_h * n_h, L), g3), pl.BlockSpec((1, n_h * n_h, L), g3),
                  pl.BlockSpec((1, n_h * L, n_st), g3),
                  pl.BlockSpec((1, n_h // 2, n_st, 2 * L), lambda g: (g, 0, 0, 0)),
                  pl.BlockSpec((1, 2, n_st), g3), pl.BlockSpec((1, n_h, L), g3)],
        out_specs=pl.BlockSpec((n_h, r, L), g3),
        out_shape=jax.ShapeDtypeStruct((d_s5, r, L), F32),
        scratch_shapes=[pltpu.VMEM((r, n_h * L), BF16), pltpu.VMEM((n_h * L, 2 * L), BF16),
                        pltpu.VMEM((r, n_st // 2), F32), pltpu.VMEM((r, n_st // 2), F32),
                        pltpu.VMEM((r, n_st // 2), F32), pltpu.VMEM((r, n_st // 2), F32)],
        compiler_params=_cparams(("parallel",)),
        name="s5",
    )(us5, kpos, kneg, ws, wo, al, dsk)


def _dft_tables(seq, fw):
    def cs(n):
        i = jnp.arange(n, dtype=I32)
        m = (i[:, None] * i[None, :]) % n
        ang = (2.0 * math.pi / n) * m.astype(F32)
        return jnp.cos(ang), jnp.sin(ang)

    def cs_split(n, r):
        s = jnp.arange(n, dtype=I32)[None, :]
        q = jnp.arange(r, dtype=I32)[:, None]
        ang_a = (2.0 * math.pi / n) * ((r * q * s) % n).astype(F32)
        ang_b = (2.0 * math.pi / n) * ((q * s) % n).astype(F32)
        ca, sa = jnp.cos(ang_a)[:, None, :], jnp.sin(ang_a)[:, None, :]
        cb, sb = jnp.cos(ang_b)[None, :, :], jnp.sin(ang_b)[None, :, :]
        return (ca * cb - sa * sb).reshape(n, n), (sa * cb + ca * sb).reshape(n, n)

    r = int(math.isqrt(seq))
    c_s, s_s = cs_split(seq, r) if r * r == seq else cs(seq)
    c_c, s_c = cs(fw)
    return (jnp.concatenate([c_s, -s_s], axis=1).astype(BF16),
            jnp.concatenate([c_c, s_c], axis=1).astype(BF16))


def _seqdft_kernel(f_ref, z_ref, o_ref, *, scale):
    seq = z_ref.shape[0]
    d_f = o_ref.shape[1]
    acc = jnp.dot(f_ref[:, :seq], z_ref[:, :d_f], preferred_element_type=F32)
    acc = acc + jnp.dot(f_ref[:, seq:], z_ref[:, d_f:], preferred_element_type=F32)
    o_ref[...] = (acc * scale).astype(o_ref.dtype)


def _seqdft(fmat, z, seq, fw):
    t, two_df = z.shape
    d_f = two_df // 2
    bsz = t // seq
    tk = 512
    nk = seq // tk
    scale = 1.0 / math.sqrt(seq * fw)
    return pl.pallas_call(
        functools.partial(_seqdft_kernel, scale=scale),
        grid=(nk, bsz),
        in_specs=[pl.BlockSpec((tk, 2 * seq), lambda k, b: (k, 0)),
                  pl.BlockSpec((seq, two_df), lambda k, b: (b, 0))],
        out_specs=pl.BlockSpec((tk, d_f), lambda k, b: (b * nk + k, 0)),
        out_shape=jax.ShapeDtypeStruct((t, d_f), BF16),
        compiler_params=_cparams(("parallel", "parallel")),
        name="seqdft",
    )(fmat, z)


def _glu_kernel(y_ref, w_ref, b_ref, o_ref, zt_ref):
    n_j = y_ref.shape[1]
    d = o_ref.shape[1]
    for j in range(n_j):
        zt_ref[j * LANE:(j + 1) * LANE, :] = _gelu_tanh(y_ref[:, j, :]).T.astype(BF16)
    zt = zt_ref[...]
    a = jnp.dot(zt, w_ref[:, :d], preferred_element_type=F32) + b_ref[:, :d]
    g = jnp.dot(zt, w_ref[:, d:], preferred_element_type=F32) + b_ref[:, d:]
    o_ref[...] = (a * jax.nn.sigmoid(g)).astype(BF16)


def _glu(ys5, w, b):
    d_s5, r, L = ys5.shape
    t = r * L
    n = w.shape[1]
    tm = 1024
    return pl.pallas_call(
        _glu_kernel,
        grid=(t // tm,),
        in_specs=[pl.BlockSpec((d_s5, tm // L, L), lambda i: (0, i, 0)),
                  _const_spec((d_s5, n)), _const_spec((1, n))],
        out_specs=pl.BlockSpec((tm, n // 2), lambda i: (i, 0)),
        out_shape=jax.ShapeDtypeStruct((t, n // 2), BF16),
        scratch_shapes=[pltpu.VMEM((tm, d_s5), BF16)],
        compiler_params=_cparams(("parallel",)),
        name="glu",
    )(ys5, w, b)


U32 = jnp.uint32
PACK_SUB = 4


def _row_slab(ref, r):
    return ref.at[pl.ds(pl.multiple_of(r * PACK_SUB, PACK_SUB), PACK_SUB), :]


def _pack_rows(ref, v, first=0, is_bf16_exact=False):
    rows, d = v.shape
    half = d // 2
    bits = lax.bitcast_convert_type(v if is_bf16_exact else v.astype(BF16).astype(F32), U32)
    for c in range(PACK_SUB):
        lo = bits[:, c * LANE:(c + 1) * LANE] >> 16
        hi = bits[:, half + c * LANE:half + (c + 1) * LANE] & jnp.uint32(0xFFFF0000)
        ref[pl.ds(first * PACK_SUB + c, rows, stride=PACK_SUB), :] = hi | lo


def _unpack_rows(ref, rows, first=0):
    los, his = [], []
    for c in range(PACK_SUB):
        w = ref[pl.ds(first * PACK_SUB + c, rows, stride=PACK_SUB), :]
        los.append(lax.bitcast_convert_type(w << 16, F32))
        his.append(lax.bitcast_convert_type(w & jnp.uint32(0xFFFF0000), F32))
    return jnp.concatenate(los + his, axis=1)


def _merge_kernel(x_ref, brs_ref, yf_ref, gates_ref, g1_ref, sc_ref, sh_ref, wfo_ref, bfo_ref,
                  wo_ref, bo_ref, lng_ref, lnb_ref, x1_ref, hhi_ref, hlo_ref, *, alpha):
    d = x_ref.shape[1]
    br_f = jnp.dot(yf_ref[...], wfo_ref[...], preferred_element_type=F32) + bfo_ref[...]
    merged = gates_ref[:, :d].astype(F32) * brs_ref[...].astype(F32) + gates_ref[:, d:].astype(F32) * br_f
    mix = jnp.dot(merged.astype(BF16), wo_ref[...], preferred_element_type=F32) + bo_ref[...]
    v = alpha * x_ref[...] + g1_ref[0] * mix
    x1 = _standardize(v) * lng_ref[...] + lnb_ref[...]
    x1_ref[...] = x1
    h = _standardize(x1) * (1.0 + sc_ref[0]) + sh_ref[0]
    hhi = h.astype(BF16)
    hhi_ref[...] = hhi
    hlo_ref[...] = (h - hhi.astype(F32)).astype(BF16)


def _merge(x2, brs, yf, gates, g1, sc2, sh2, wfo, bfo, wo, bo, lng, lnb, seq, alpha):
    t, d = x2.shape
    d_f = yf.shape[1]
    tm = 512
    tpb = seq // tm
    bsz = t // seq
    row = lambda i: (i, 0)
    bat = lambda i: (i // tpb, 0, 0)
    return pl.pallas_call(
        functools.partial(_merge_kernel, alpha=alpha),
        grid=(t // tm,),
        in_specs=[pl.BlockSpec((tm, d), row), pl.BlockSpec((tm, d), row), pl.BlockSpec((tm, d_f), row),
                  pl.BlockSpec((tm, 2 * d), row),
                  pl.BlockSpec((1, 1, d), bat), pl.BlockSpec((1, 1, d), bat), pl.BlockSpec((1, 1, d), bat),
                  _const_spec((d_f, d)), _const_spec((1, d)), _const_spec((d, d)), _const_spec((1, d)),
                  _const_spec((1, d)), _const_spec((1, d))],
        out_specs=[pl.BlockSpec((tm, d), row), pl.BlockSpec((tm, d), row), pl.BlockSpec((tm, d), row)],
        out_shape=[jax.ShapeDtypeStruct((t, d), F32), jax.ShapeDtypeStruct((t, d), BF16),
                   jax.ShapeDtypeStruct((t, d), BF16)],
        compiler_params=_cparams(("parallel",)),
        name="merge",
    )(x2, brs, yf, gates, g1.reshape(bsz, 1, d), sc2.reshape(bsz, 1, d), sh2.reshape(bsz, 1, d),
      wfo, bfo, wo, bo, lng, lnb)


def _router_kernel(hhi_ref, hlo_ref, whi_ref, wlo_ref, bias_ref, tri_ref, ltri_ref, utri_ref,
                   eidx_ref, gate_ref, rank_ref, pos_ref, tcnt_ref, tcum_ref, tbase_ref, tmax_ref, cnt_ref,
                   base_ref, brow_ref):
    n_e = whi_ref.shape[0]
    tm = hhi_ref.shape[0]
    gsz = n_e // N_EXPERT_GROUPS
    nt = (((1,), (1,)), ((), ()))

    @pl.when(pl.program_id(0) == 0)
    def _():
        base_ref[...] = jnp.zeros_like(base_ref)
        brow_ref[...] = jnp.zeros_like(brow_ref)

    hhi = hhi_ref[...]
    logits = lax.dot_general(whi_ref[...], hhi, nt, preferred_element_type=F32)
    logits = logits + lax.dot_general(wlo_ref[...], hhi, nt, preferred_element_type=F32)
    logits = logits + lax.dot_general(whi_ref[...], hlo_ref[...], nt, preferred_element_type=F32)
    scores = jax.nn.sigmoid(logits)
    sel = scores + bias_ref[:, 0:1]

    g3 = sel.reshape(N_EXPERT_GROUPS, gsz, tm)
    i3 = lax.broadcasted_iota(I32, g3.shape, 1).astype(F32)
    m1 = jnp.max(g3, axis=1, keepdims=True)
    first = jnp.min(jnp.where(g3 == m1, i3, float(gsz)), axis=1, keepdims=True)
    m2 = jnp.max(jnp.where(i3 == first, NEG_INF, g3), axis=1, keepdims=True)
    gs = (m1 + m2).reshape(N_EXPERT_GROUPS, tm)

    gi = lax.broadcasted_iota(I32, gs.shape, 0).astype(F32)
    gsel = jnp.zeros(gs.shape, F32)
    cur = gs
    for _ in range(TOPK_GROUPS):
        m = jnp.max(cur, axis=0, keepdims=True)
        f = jnp.min(jnp.where(cur == m, gi, float(N_EXPERT_GROUPS)), axis=0, keepdims=True)
        pick = gi == f
        gsel = jnp.where(pick, 1.0, gsel)
        cur = jnp.where(pick, NEG_INF, cur)
    gmask = jnp.broadcast_to(gsel.reshape(N_EXPERT_GROUPS, 1, tm), g3.shape).reshape(n_e, tm)
    masked = jnp.where(gmask > 0.5, sel, NEG_INF)

    ri = lax.broadcasted_iota(I32, (n_e, tm), 0).astype(F32)
    picks = []
    gates = []
    multihot = jnp.zeros((n_e, tm), F32)
    for _ in range(TOP_K):
        m = jnp.max(masked, axis=0, keepdims=True)
        f = jnp.min(jnp.where(masked == m, ri, float(n_e)), axis=0, keepdims=True)
        pick = ri == f
        picks.append(f)
        gates.append(jnp.sum(jnp.where(pick, scores, 0.0), axis=0, keepdims=True))
        multihot = jnp.where(pick, 1.0, multihot)
        masked = jnp.where(pick, NEG_INF, masked)
    gsum = gates[0]
    for g in gates[1:]:
        gsum = gsum + g

    even = lambda n: n + (n - 2.0 * jnp.floor(n * 0.5))
    mh = multihot.astype(BF16)
    rank_local = jnp.dot(mh, tri_ref[...], preferred_element_type=F32)
    n_true_row = lax.dot_general(jnp.ones((8, tm), BF16), mh, nt, preferred_element_type=F32)
    n_col = even(jnp.sum(multihot, axis=1, keepdims=True))
    n_row = even(n_true_row)
    cum_col = jnp.dot(ltri_ref[...], jnp.broadcast_to(n_col, (n_e, LANE)).astype(BF16),
                      preferred_element_type=F32)[:, 0:1]
    cum_row = jnp.dot(n_row.astype(BF16), utri_ref[...], preferred_element_type=F32)
    rank_global = rank_local + base_ref[:, 0:1]
    pos_tile = rank_local + cum_col
    for k in range(TOP_K):
        pick = ri == picks[k]
        lookup = lambda m: jnp.sum(jnp.where(pick, m, 0.0), axis=0, keepdims=True)
        eidx_ref[k:k + 1, :] = picks[k].astype(I32)
        gate_ref[k:k + 1, :] = gates[k] / gsum * ROUTED_SCALE
        rank_ref[k:k + 1, :] = lookup(rank_global).astype(I32)
        pos_ref[k:k + 1, :] = lookup(pos_tile).astype(I32)
    tcnt_ref[0] = n_true_row[0:1, :].astype(I32)
    tcum_ref[0] = cum_row[0:1, :].astype(I32)
    tbase_ref[0] = brow_ref[0:1, :].astype(I32)
    cur = n_row[0:1, :]
    e_lane = lax.broadcasted_iota(I32, cur.shape, 1).astype(F32)
    o_lane = lax.broadcasted_iota(I32, (1, LANE), 1)
    ovf = jnp.where(o_lane == N_OVF, jnp.sum(jnp.where(cur > SEG_ROWS, 1.0, 0.0), axis=1, keepdims=True), -1.0)
    for j in range(N_OVF):
        m = jnp.max(cur, axis=1, keepdims=True)
        idx = jnp.min(jnp.where(cur == m, e_lane, float(n_e)), axis=1, keepdims=True)
        ovf = jnp.where(jnp.logical_and(o_lane == j, m > SEG_ROWS), idx, ovf)
        cur = jnp.where(e_lane == idx, -1.0, cur)
    tmax_ref[0] = ovf.astype(I32)
    brow_ref[...] = brow_ref[...] + n_row
    base_ref[...] = base_ref[...] + n_col
    cnt_ref[...] = base_ref[...].astype(I32)


def _router(hhi, hlo, wrt_hi, wrt_lo, bias):
    t, d = hhi.shape
    n_e = wrt_hi.shape[0]
    tm = MOE_TILE
    n_tile = t // tm
    tri = (jnp.arange(tm)[:, None] < jnp.arange(tm)[None, :]).astype(BF16)
    utri = (jnp.arange(n_e)[:, None] < jnp.arange(n_e)[None, :]).astype(BF16)
    kt = lambda i: (0, i)
    tab = lambda i: (i, 0, 0)
    return pl.pallas_call(
        _router_kernel,
        grid=(n_tile,),
        in_specs=[pl.BlockSpec((tm, d), lambda i: (i, 0)), pl.BlockSpec((tm, d), lambda i: (i, 0)),
                  _const_spec((n_e, d)), _const_spec((n_e, d)), _const_spec((n_e, LANE)),
                  _const_spec((tm, tm)), _const_spec((n_e, n_e)), _const_spec((n_e, n_e))],
        out_specs=[pl.BlockSpec((TOP_K, tm), kt), pl.BlockSpec((TOP_K, tm), kt), pl.BlockSpec((TOP_K, tm), kt),
                   pl.BlockSpec((TOP_K, tm), kt),
                   pl.BlockSpec((1, 1, n_e), tab), pl.BlockSpec((1, 1, n_e), tab), pl.BlockSpec((1, 1, n_e), tab),
                   pl.BlockSpec((1, 1, LANE), tab),
                   pl.BlockSpec((n_e, LANE), lambda i: (0, 0))],
        out_shape=[jax.ShapeDtypeStruct((TOP_K, t), I32), jax.ShapeDtypeStruct((TOP_K, t), F32),
                   jax.ShapeDtypeStruct((TOP_K, t), I32), jax.ShapeDtypeStruct((TOP_K, t), I32),
                   jax.ShapeDtypeStruct((n_tile, 1, n_e), I32), jax.ShapeDtypeStruct((n_tile, 1, n_e), I32),
                   jax.ShapeDtypeStruct((n_tile, 1, n_e), I32), jax.ShapeDtypeStruct((n_tile, 1, LANE), I32),
                   jax.ShapeDtypeStruct((n_e, LANE), I32)],
        scratch_shapes=[pltpu.VMEM((n_e, LANE), F32), pltpu.VMEM((8, n_e), F32)],
        compiler_params=_cparams(("arbitrary",)),
        name="router",
    )(hhi, hlo, wrt_hi, wrt_lo, jnp.broadcast_to(bias.astype(F32).reshape(n_e, 1), (n_e, LANE)), tri,
      utri.T, utri)


def _dest_kernel(eidx_ref, rank_ref, pstart_ref, dest_ref):
    n_e = pstart_ref.shape[0]
    tm = eidx_ref.shape[1]
    ri = lax.broadcasted_iota(I32, (n_e, tm), 0)
    ps = pstart_ref[:, 0:1].astype(F32)
    for k in range(TOP_K):
        hit = ri == eidx_ref[k:k + 1, :]
        base = jnp.sum(jnp.where(hit, ps, 0.0), axis=0, keepdims=True)
        dest_ref[k:k + 1, :] = base.astype(I32) + rank_ref[k:k + 1, :]


def _dest(eidx, rank, pstart):
    k, t = eidx.shape
    n_e = pstart.shape[0]
    tm = 2048
    kt = lambda i: (0, i)
    return pl.pallas_call(
        _dest_kernel,
        grid=(t // tm,),
        in_specs=[pl.BlockSpec((k, tm), kt), pl.BlockSpec((k, tm), kt), _const_spec((n_e, LANE))],
        out_specs=pl.BlockSpec((k, tm), kt),
        out_shape=jax.ShapeDtypeStruct((k, t), I32),
        compiler_params=_cparams(("parallel",)),
        name="dest",
    )(eidx, rank, jnp.broadcast_to(pstart.astype(I32).reshape(n_e, 1), (n_e, LANE)))


def _zero_fill(pad_lo_ref, pad_n_ref, zeros_ref, rows_ref, sem):
    def each_copy(act):
        def per_entry(e, _):
            lo = pad_lo_ref[e]
            n = pad_n_ref[e]
            n_full = n // FFN_BLOCK

            def full(j, _):
                act(pltpu.make_async_copy(
                    zeros_ref, rows_ref.at[pl.ds(pl.multiple_of((lo + j * FFN_BLOCK) * PACK_SUB, PACK_SUB),
                                                 FFN_BLOCK * PACK_SUB), :], sem))
                return 0

            lax.fori_loop(0, n_full, full, 0)
            off = lo + n_full * FFN_BLOCK
            rem = n - n_full * FFN_BLOCK
            bit = FFN_BLOCK // 2
            while bit >= 1:
                take = rem & bit

                @pl.when(take != 0)
                def _(off=off, bit=bit):
                    act(pltpu.make_async_copy(
                        zeros_ref.at[pl.ds(0, bit * PACK_SUB), :],
                        rows_ref.at[pl.ds(pl.multiple_of(off * PACK_SUB, PACK_SUB), bit * PACK_SUB), :], sem))

                off = off + take
                bit //= 2
            return 0

        lax.fori_loop(0, pad_lo_ref.shape[0], per_entry, 0)

    each_copy(lambda cp: cp.start())
    each_copy(lambda cp: cp.wait())


def _seg_slab(ref, row, n_rows=SEG_ROWS):
    return ref.at[pl.ds(pl.multiple_of(row * PACK_SUB, 2 * PACK_SUB), n_rows * PACK_SUB), :]


def _dispatch_kernel(pad_lo_ref, pad_n_ref, pstart_ref, tcnt_ref, tcum_ref, tbase_ref, tmax_ref, pos_ref, h_ref,
                     rows_ref, seg_ref, zeros_ref, sem, xsem, zsem):
    i = pl.program_id(0)
    tm = h_ref.shape[0]
    n_e = tcnt_ref.shape[2]
    n_loc = TOP_K * tm + n_e
    seg0 = (i % 2) * (n_loc + SEG_ROWS)

    @pl.when(i == 0)
    def _():
        zeros_ref[...] = jnp.zeros_like(zeros_ref)
        _zero_fill(pad_lo_ref, pad_n_ref, zeros_ref, rows_ref, zsem)
        for s in range(2):
            seg_ref[pl.ds((s * (n_loc + SEG_ROWS) + n_loc) * PACK_SUB, SEG_ROWS * PACK_SUB), :] = jnp.zeros(
                (SEG_ROWS * PACK_SUB, LANE), U32)

    p_idx = lax.broadcasted_iota(I32, (n_loc, tm), 0)
    onehot = jnp.zeros((n_loc, tm), F32)
    for k in range(TOP_K):
        onehot = jnp.where(p_idx == pos_ref[k:k + 1, :], 1.0, onehot)
    _pack_rows(seg_ref, jnp.dot(onehot.astype(BF16), h_ref[...], preferred_element_type=F32), first=seg0,
               is_bf16_exact=True)

    def wait_tile():
        full = rows_ref.at[pl.ds(0, n_e * SEG_ROWS * PACK_SUB), :]
        pltpu.make_async_copy(full, full, sem).wait()

    @pl.when(i > 0)
    def _():
        wait_tile()

    unroll = 8

    def first_chunks(g, _):
        for u in range(unroll):
            e = g * unroll + u
            pltpu.make_async_copy(_seg_slab(seg_ref, seg0 + tcum_ref[0, 0, e]),
                                  _seg_slab(rows_ref, pstart_ref[e] + tbase_ref[0, 0, e]), sem).start(priority=u % 2)
        return 0

    lax.fori_loop(0, n_e // unroll, first_chunks, 0)

    def rest_of_segment(e):
        src = seg0 + tcum_ref[0, 0, e]
        dst = pstart_ref[e] + tbase_ref[0, 0, e]

        def extra(c, _):
            cp = pltpu.make_async_copy(_seg_slab(seg_ref, src + c * SEG_ROWS),
                                       _seg_slab(rows_ref, dst + c * SEG_ROWS), xsem)
            cp.start()
            cp.wait()
            return 0

        lax.fori_loop(1, (tcnt_ref[0, 0, e] + SEG_ROWS - 1) // SEG_ROWS, extra, 0)

    n_ovf = tmax_ref[0, 0, N_OVF]
    for j in range(N_OVF):
        @pl.when(jnp.logical_and(n_ovf <= N_OVF, tmax_ref[0, 0, j] >= 0))
        def _(j=j):
            rest_of_segment(tmax_ref[0, 0, j])

    @pl.when(n_ovf > N_OVF)
    def _():
        def per_expert(e, _):
            rest_of_segment(e)
            return 0

        lax.fori_loop(0, n_e, per_expert, 0)

    @pl.when(i == pl.num_programs(0) - 1)
    def _():
        wait_tile()


def _dispatch(hhi, pos, tcnt, tcum, tbase, tmax, pstart, pad_lo, pad_n, n_rows):
    t, d = hhi.shape
    n_tile, _, n_e = tcnt.shape
    tm = t // n_tile
    n_loc = TOP_K * tm + n_e
    tab = pl.BlockSpec((1, 1, n_e), lambda i, *_: (i, 0, 0), memory_space=pltpu.SMEM)
    return pl.pallas_call(
        _dispatch_kernel,
        grid_spec=pltpu.PrefetchScalarGridSpec(
            num_scalar_prefetch=3,
            grid=(n_tile,),
            in_specs=[tab, tab, tab,
                      pl.BlockSpec((1, 1, LANE), lambda i, *_: (i, 0, 0), memory_space=pltpu.SMEM),
                      pl.BlockSpec((TOP_K, tm), lambda i, *_: (0, i)),
                      pl.BlockSpec((tm, d), lambda i, *_: (i, 0))],
            out_specs=pl.BlockSpec(memory_space=pl.ANY),
            scratch_shapes=[pltpu.VMEM((2 * (n_loc + SEG_ROWS) * PACK_SUB, LANE), U32),
                            pltpu.VMEM((FFN_BLOCK * PACK_SUB, LANE), U32),
                            pltpu.SemaphoreType.DMA, pltpu.SemaphoreType.DMA, pltpu.SemaphoreType.DMA],
        ),
        out_shape=jax.ShapeDtypeStruct((n_rows * PACK_SUB, LANE), U32),
        compiler_params=_cparams(("arbitrary",)),
        name="dispatch",
    )(pad_lo, pad_n, pstart, tcnt, tcum, tbase, tmax, pos, hhi)


def _ffn_kernel(blk_e_ref, n_used_ref, rows_ref, wg_ref, wu_ref, wd_ref, y_ref):
    live = pl.program_id(0) < n_used_ref[0]

    @pl.when(live)
    def _():
        x = _unpack_rows(rows_ref, FFN_BLOCK).astype(BF16)
        a = jnp.dot(x, wg_ref[0].astype(BF16), preferred_element_type=F32)
        u = jnp.dot(x, wu_ref[0].astype(BF16), preferred_element_type=F32)
        hid = (_silu(a) * u).astype(BF16)
        _pack_rows(y_ref, jnp.dot(hid, wd_ref[0].astype(BF16), preferred_element_type=F32))

    @pl.when(jnp.logical_not(live))
    def _():
        y_ref[...] = jnp.zeros_like(y_ref)


def _ffn(rows, blk_e, blk_valid, wg, wu, wd):
    n_rows = rows.shape[0] // PACK_SUB
    n_e, d, d_e = wg.shape
    n_blk = n_rows // FFN_BLOCK
    blk = lambda b, e, n: (b, 0)
    exp = lambda b, e, n: (e[b], 0, 0)
    return pl.pallas_call(
        _ffn_kernel,
        grid_spec=pltpu.PrefetchScalarGridSpec(
            num_scalar_prefetch=2,
            grid=(n_blk,),
            in_specs=[pl.BlockSpec((FFN_BLOCK * PACK_SUB, LANE), blk),
                      pl.BlockSpec((1, d, d_e), exp), pl.BlockSpec((1, d, d_e), exp),
                      pl.BlockSpec((1, d_e, d), exp)],
            out_specs=pl.BlockSpec((FFN_BLOCK * PACK_SUB, LANE), blk),
        ),
        out_shape=jax.ShapeDtypeStruct((n_rows * PACK_SUB, LANE), U32),
        compiler_params=_cparams(("arbitrary",)),
        name="ffn",
    )(blk_e, blk_valid, rows, wg, wu, wd)


def _final_kernel(dest_ref, x1_ref, h_ref, gate_ref, g2_ref, wsg_ref, wsu_ref, wsd_ref, lng_ref, lnb_ref,
                  y_ref, o_ref, ybuf_ref, sem, *, alpha):
    tm = x1_ref.shape[0]

    def body(tt, _):
        for k in range(TOP_K):
            pltpu.make_async_copy(_row_slab(y_ref, dest_ref[0, 0, k * tm + tt]),
                                  _row_slab(ybuf_ref, k * tm + tt), sem).start()
        return 0

    lax.fori_loop(0, tm, body, 0)

    h = h_ref[...]
    a = jnp.dot(h, wsg_ref[...], preferred_element_type=F32)
    u = jnp.dot(h, wsu_ref[...], preferred_element_type=F32)
    shared = jnp.dot((_silu(a) * u).astype(BF16), wsd_ref[...], preferred_element_type=F32)

    pltpu.make_async_copy(ybuf_ref, ybuf_ref, sem).wait()
    routed = gate_ref[:, 0:1] * _unpack_rows(ybuf_ref, tm)
    for k in range(1, TOP_K):
        routed = routed + gate_ref[:, k:k + 1] * _unpack_rows(ybuf_ref, tm, first=k * tm)

    v = alpha * x1_ref[...] + g2_ref[0] * (shared + routed)
    o_ref[...] = _standardize(v) * lng_ref[...] + lnb_ref[...]


def _final(x1, hhi, y_rows, dest_tiles, gate_t, g2, wsg, wsu, wsd, lng, lnb, seq, alpha):
    t, d = x1.shape
    d_sh = wsg.shape[1]
    n_tile, _, per_tile = dest_tiles.shape
    tm = per_tile // TOP_K
    tpb = seq // tm
    bsz = t // seq
    row = lambda i: (i, 0)
    return pl.pallas_call(
        functools.partial(_final_kernel, alpha=alpha),
        grid=(n_tile,),
        in_specs=[pl.BlockSpec((1, 1, per_tile), lambda i: (i, 0, 0), memory_space=pltpu.SMEM),
                  pl.BlockSpec((tm, d), row), pl.BlockSpec((tm, d), row), pl.BlockSpec((tm, TOP_K), row),
                  pl.BlockSpec((1, 1, d), lambda i: (i // tpb, 0, 0)),
                  _const_spec((d, d_sh)), _const_spec((d, d_sh)), _const_spec((d_sh, d)),
                  _const_spec((1, d)), _const_spec((1, d)),
                  pl.BlockSpec(memory_space=pl.ANY)],
        out_specs=pl.BlockSpec((tm, d), row),
        out_shape=jax.ShapeDtypeStruct((t, d), F32),
        scratch_shapes=[pltpu.VMEM((TOP_K * tm * PACK_SUB, LANE), U32), pltpu.SemaphoreType.DMA],
        compiler_params=_cparams(("arbitrary",)),
        name="final",
    )(dest_tiles, x1, hhi, gate_t, g2.reshape(bsz, 1, d), wsg, wsu, wsd, lng, lnb, y_rows)


def _split_hi_lo(w):
    hi = w.astype(BF16)
    return hi, (w - hi.astype(F32)).astype(BF16)


def kernel(x, c, w_ada, b_ada, w_in, b_in, s5_lambda_re, s5_lambda_im, s5_log_dt, s5_b_re, s5_b_im, s5_c_re, s5_c_im, s5_d, w_s5_glu, b_s5_glu, w_fourier, b_fourier, w_out, b_out, ln1_g, ln1_b, w_router, router_bias, w_exp_gate, w_exp_up, w_exp_down, w_sh_gate, w_sh_up, w_sh_down, ln2_g, ln2_b):
    bsz, seq, d = x.shape
    depth = w_ada.shape[0]
    alpha = (2 * depth) ** 0.25
    t = bsz * seq
    d_s5 = s5_d.shape[1]
    d_f = w_fourier.shape[1]
    fw = d_f // FOURIER_GROUPS
    n_e = w_router.shape[2]
    n_chunk = seq // LANE
    fmat, cmat = _dft_tables(seq, fw)
    row = lambda v: v.astype(F32).reshape(1, -1)

    x2 = x.reshape(t, d)
    for l in range(depth):
        mod = _adaln(c, w_ada[l], b_ada[l])
        sh1, sc1, g1, sh2, sc2, g2 = jnp.split(mod, 6, axis=-1)

        wi = w_in[l]
        bi = b_in[l].astype(F32)
        ws5t = wi[:, :d_s5].T.astype(BF16)
        bs5 = jnp.broadcast_to(bi[:d_s5].reshape(d_s5, 1), (d_s5, LANE))
        us5, z, gates = _proj(x2, sc1, sh1, ws5t, bs5,
                              wi[:, d_s5:d_s5 + d_f].astype(BF16), row(bi[d_s5:d_s5 + d_f]), cmat,
                              wi[:, d_s5 + d_f:].astype(BF16), row(bi[d_s5 + d_f:]), seq)

        tables = _s5_tables(s5_lambda_re[l], s5_lambda_im[l], s5_log_dt[l], s5_b_re[l], s5_b_im[l],
                            s5_c_re[l], s5_c_im[l], s5_d[l])
        ys5 = _s5(us5, tables, n_chunk)
        brs = _glu(ys5, w_s5_glu[l].astype(BF16), row(b_s5_glu[l]))
        yf = _seqdft(fmat, z, seq, fw)

        x1, hhi, hlo = _merge(x2, brs, yf, gates, g1, sc2, sh2,
                              w_fourier[l].astype(BF16), row(b_fourier[l]),
                              w_out[l].astype(BF16), row(b_out[l]), row(ln1_g[l]), row(ln1_b[l]), seq, alpha)

        wrt_hi, wrt_lo = _split_hi_lo(w_router[l].astype(F32).T)
        eidx, gate, rank, pos, tcnt, tcum, tbase, tmax, cnt = _router(hhi, hlo, wrt_hi, wrt_lo, router_bias[l])

        counts = cnt[:, 0]
        padded = ((counts + SEG_ROWS + FFN_BLOCK - 1) // FFN_BLOCK) * FFN_BLOCK
        pend = jnp.cumsum(padded)
        pstart = (pend - padded).astype(I32)
        dest = _dest(eidx, rank, pstart)
        n_blk = (t * TOP_K + (t // MOE_TILE) * n_e + n_e * (SEG_ROWS + FFN_BLOCK - 1) + FFN_BLOCK - 1) // FFN_BLOCK
        n_rows = n_blk * FFN_BLOCK
        blk_start = jnp.arange(n_blk, dtype=I32) * FFN_BLOCK
        blk_e = jnp.minimum(jnp.sum((pend[None, :] <= blk_start[:, None]).astype(I32), axis=1), n_e - 1)
        n_used = (pend[-1:] // FFN_BLOCK).astype(I32)
        pad_lo = jnp.concatenate([pstart + counts, pend[-1:]]).astype(I32)
        pad_n = jnp.concatenate([padded - counts, n_rows - pend[-1:]]).astype(I32)

        def tiles(tm):
            return dest.reshape(TOP_K, t // tm, tm).transpose(1, 0, 2).reshape(t // tm, 1, TOP_K * tm)

        rows = _dispatch(hhi, pos, tcnt, tcum, tbase, tmax, pstart, pad_lo, pad_n, n_rows)
        y_rows = _ffn(rows, blk_e, n_used, w_exp_gate[l], w_exp_up[l], w_exp_down[l])
        x2 = _final(x1, hhi, y_rows, tiles(MOE_TILE), gate.T, g2,
                    w_sh_gate[l].astype(BF16), w_sh_up[l].astype(BF16), w_sh_down[l].astype(BF16),
                    row(ln2_g[l]), row(ln2_b[l]), seq, alpha)
    return x2.reshape(bsz, seq, d)
```

```python
import functools
import math

import jax
import jax.numpy as jnp
from jax import lax
from jax.experimental import pallas as pl
from jax.experimental.pallas import tpu as pltpu

F32 = jnp.float32
BF16 = jnp.bfloat16
I32 = jnp.int32

TOP_K = 8
N_EXPERT_GROUPS = 8
TOPK_GROUPS = 4
ROUTED_SCALE = 2.5
FOURIER_GROUPS = 4
LN_EPS = 1e-5

LANE = 128
VMEM_LIMIT = 56 * 1024 * 1024

HIGHEST = lax.Precision.HIGHEST
NEG_INF = float("-inf")

FFN_BLOCK = 512
DISPATCH_TILE = 512
COMBINE_TILE = 256


def _cparams(sem):
    return pltpu.CompilerParams(dimension_semantics=sem, vmem_limit_bytes=VMEM_LIMIT)


def _const_spec(shape):
    nd = len(shape)
    return pl.BlockSpec(shape, lambda *_: (0,) * nd, pipeline_mode=pl.Buffered(1))


def _standardize(x):
    mu = jnp.mean(x, axis=-1, keepdims=True)
    xc = x - mu
    var = jnp.mean(xc * xc, axis=-1, keepdims=True)
    return xc * lax.rsqrt(var + LN_EPS)


def _silu(x):
    return x * jax.nn.sigmoid(x)


def _gelu_tanh(x):
    return 0.5 * x * (1.0 + jnp.tanh(math.sqrt(2.0 / math.pi) * (x + 0.044715 * (x * x * x))))


def _adaln_kernel(c_ref, w_ref, b_ref, o_ref):
    a = _silu(c_ref[...])
    o_ref[...] = jnp.dot(a, w_ref[...], precision=HIGHEST, preferred_element_type=F32) + b_ref[...]


def _adaln(c, w, b):
    bsz, d = c.shape
    n = w.shape[1]
    tn = 512
    return pl.pallas_call(
        _adaln_kernel,
        grid=(n // tn,),
        in_specs=[pl.BlockSpec((bsz, d), lambda j: (0, 0)),
                  pl.BlockSpec((d, tn), lambda j: (0, j)),
                  pl.BlockSpec((1, tn), lambda j: (0, j))],
        out_specs=pl.BlockSpec((bsz, tn), lambda j: (0, j)),
        out_shape=jax.ShapeDtypeStruct((bsz, n), F32),
        compiler_params=_cparams(("parallel",)),
        name="adaln",
    )(c, w, b.reshape(1, n))


def _proj_kernel(x_ref, sc_ref, sh_ref, ws5t_ref, bs5_ref, wf_ref, bf_ref, cs_ref, wg_ref, bg_ref,
                 us5_ref, z_ref, gates_ref):
    tm = x_ref.shape[0]
    u = (_standardize(x_ref[...]) * (1.0 + sc_ref[0]) + sh_ref[0]).astype(BF16)
    p = lax.dot_general(ws5t_ref[...], u, (((1,), (1,)), ((), ())), preferred_element_type=F32)
    p = p + bs5_ref[:, 0:1]
    for j in range(tm // LANE):
        us5_ref[:, j, :] = p[:, j * LANE:(j + 1) * LANE]
    uf = (jnp.dot(u, wf_ref[...], preferred_element_type=F32) + bf_ref[...]).astype(BF16)
    d_f = uf.shape[1]
    fw = d_f // FOURIER_GROUPS
    for q in range(FOURIER_GROUPS):
        zq = jnp.dot(uf[:, q * fw:(q + 1) * fw], cs_ref[...], preferred_element_type=F32)
        z_ref[:, q * fw:(q + 1) * fw] = zq[:, :fw].astype(BF16)
        z_ref[:, d_f + q * fw:d_f + (q + 1) * fw] = zq[:, fw:].astype(BF16)
    n_g = wg_ref.shape[1]
    half = n_g // 2
    for q in range(2):
        gp = jnp.dot(u, wg_ref[:, q * half:(q + 1) * half], preferred_element_type=F32)
        gp = gp + bg_ref[:, q * half:(q + 1) * half]
        gates_ref[:, q * half:(q + 1) * half] = jax.nn.sigmoid(gp).astype(BF16)


def _proj(x2, sc, sh, ws5t, bs5, wf, bf, cs, wg, bg, seq):
    t, d = x2.shape
    d_s5 = ws5t.shape[0]
    d_f = wf.shape[1]
    n_g = wg.shape[1]
    tm = 1024
    tpb = seq // tm
    bsz = t // seq
    return pl.pallas_call(
        _proj_kernel,
        grid=(t // tm,),
        in_specs=[pl.BlockSpec((tm, d), lambda i: (i, 0)),
                  pl.BlockSpec((1, 1, d), lambda i: (i // tpb, 0, 0)),
                  pl.BlockSpec((1, 1, d), lambda i: (i // tpb, 0, 0)),
                  _const_spec((d_s5, d)), _const_spec((d_s5, LANE)),
                  _const_spec((d, d_f)), _const_spec((1, d_f)),
                  _const_spec(cs.shape),
                  _const_spec((d, n_g)), _const_spec((1, n_g))],
        out_specs=[pl.BlockSpec((d_s5, tm // LANE, LANE), lambda i: (0, i, 0)),
                   pl.BlockSpec((tm, 2 * d_f), lambda i: (i, 0)),
                   pl.BlockSpec((tm, n_g), lambda i: (i, 0))],
        out_shape=[jax.ShapeDtypeStruct((d_s5, t // LANE, LANE), F32),
                   jax.ShapeDtypeStruct((t, 2 * d_f), BF16),
                   jax.ShapeDtypeStruct((t, n_g), BF16)],
        compiler_params=_cparams(("parallel",)),
        name="proj",
    )(x2, sc.reshape(bsz, 1, d), sh.reshape(bsz, 1, d), ws5t, bs5, wf, bf, cs, wg, bg)


def _s5_tables(lam_re, lam_im, log_dt, b_re, b_im, c_re, c_im, d_skip):
    L = LANE
    hp = HIGHEST
    lr, li = lam_re.astype(F32), lam_im.astype(F32)
    dt = jnp.exp(log_dt.astype(F32))[:, :, None]
    mag = jnp.exp(lr * dt)
    ang = li * dt
    ab_re, ab_im = mag * jnp.cos(ang), mag * jnp.sin(ang)
    den = lr * lr + li * li
    nr = ab_re - 1.0
    coef_re = (nr * lr + ab_im * li) / den
    coef_im = (ab_im * lr - nr * li) / den
    br, bi = b_re.astype(F32), b_im.astype(F32)
    bb_re = coef_re[..., None] * br - coef_im[..., None] * bi
    bb_im = coef_re[..., None] * bi + coef_im[..., None] * br
    cr, ci = c_re.astype(F32), c_im.astype(F32)
    n_g, n_p, n_h = br.shape[1], br.shape[2], br.shape[3]

    k = jnp.arange(L + 1, dtype=F32)[None, None, :, None]
    pmag = jnp.exp(k * (lr * dt)[:, :, None, :])
    pang = k * (li * dt)[:, :, None, :]
    pw_re, pw_im = pmag * jnp.cos(pang), pmag * jnp.sin(pang)

    m_re = cr[:, :, :, None, :] * jnp.swapaxes(bb_re, 2, 3)[:, :, None, :, :] \
        - ci[:, :, :, None, :] * jnp.swapaxes(bb_im, 2, 3)[:, :, None, :, :]
    m_im = cr[:, :, :, None, :] * jnp.swapaxes(bb_im, 2, 3)[:, :, None, :, :] \
        + ci[:, :, :, None, :] * jnp.swapaxes(bb_re, 2, 3)[:, :, None, :, :]
    kap = jnp.einsum("dgohp,dgkp->dgohk", m_re, pw_re, precision=hp) \
        - jnp.einsum("dgohp,dgkp->dgohk", m_im, pw_im, precision=hp)
    kpos = kap[0, ..., :L].reshape(n_g, n_h * n_h, L)
    kb = kap[1]
    kneg = jnp.concatenate([kb[..., 0:1], kb[..., L - 1:0:-1]], axis=-1)
    kneg = kneg.reshape(n_g, n_h * n_h, L)

    pf_re, pf_im = pw_re[0, :, L - 1::-1][:, :L], pw_im[0, :, L - 1::-1][:, :L]
    pb_re, pb_im = pw_re[1, :, :L], pw_im[1, :, :L]

    def cmul_hs(p_re, p_im, q_re, q_im):
        a = p_re[:, None, :, :]
        b = p_im[:, None, :, :]
        c = jnp.swapaxes(q_re, 1, 2)[:, :, None, :]
        e = jnp.swapaxes(q_im, 1, 2)[:, :, None, :]
        return a * c - b * e, a * e + b * c

    wsf_re, wsf_im = cmul_hs(pf_re, pf_im, bb_re[0], bb_im[0])
    wsb_re, wsb_im = cmul_hs(pb_re, pb_im, bb_re[1], bb_im[1])
    ws = jnp.concatenate([wsf_re, wsf_im, wsb_re, wsb_im], axis=-1).reshape(n_g, n_h * L, 4 * n_p)

    qf_re, qf_im = pw_re[0, :, 1:L + 1], pw_im[0, :, 1:L + 1]
    qb_re, qb_im = pw_re[1, :, L:0:-1], pw_im[1, :, L:0:-1]

    def cmul_ot(c_r, c_i, q_r, q_i):
        a = jnp.swapaxes(c_r, 1, 2)[:, :, :, None]
        b = jnp.swapaxes(c_i, 1, 2)[:, :, :, None]
        c = jnp.swapaxes(q_r, 1, 2)[:, :, None, :]
        e = jnp.swapaxes(q_i, 1, 2)[:, :, None, :]
        return a * c - b * e, -(a * e + b * c)

    wof_re, wof_mi = cmul_ot(cr[0], ci[0], qf_re, qf_im)
    wob_re, wob_mi = cmul_ot(cr[1], ci[1], qb_re, qb_im)
    wo = jnp.concatenate([wof_re, wof_mi, wob_re, wob_mi], axis=1)
    wo = jnp.swapaxes(wo.reshape(n_g, 4 * n_p, n_h // 2, 2 * L), 1, 2)

    al_re, al_im = pw_re[:, :, L], pw_im[:, :, L]
    alx = jnp.concatenate([al_re[0], al_re[0], al_re[1], al_re[1]], axis=-1)
    aly = jnp.concatenate([-al_im[0], al_im[0], -al_im[1], al_im[1]], axis=-1)
    al = jnp.stack([alx, aly], axis=1)

    dsk = jnp.broadcast_to(d_skip.astype(F32).reshape(n_g, n_h, 1), (n_g, n_h, L))
    return kpos, kneg, ws.astype(BF16), wo.astype(BF16), al, dsk


def _s5_kernel(a_ref, kpos_ref, kneg_ref, ws_ref, wo_ref, al_ref, d_ref, y_ref,
               abf_ref, tp_ref, sf_ref, sb_ref, xf_ref, xb_ref, *, n_chunk):
    n_h, r, L = a_ref.shape
    bsz = r // n_chunk
    half = 2 * (ws_ref.shape[2] // 4)

    for h in range(n_h):
        abf_ref[:, h * L:(h + 1) * L] = a_ref[h].astype(BF16)
    abf = abf_ref[...]

    s_all = jnp.dot(abf, ws_ref[0], preferred_element_type=F32)
    sf_ref[...] = s_all[:, :half]
    sb_ref[...] = s_all[:, half:]

    alx = al_ref[0, 0:1, :]
    aly = al_ref[0, 1:2, :]

    def carry(e, lo):
        ex = e * alx[:, lo:lo + half]
        ey = pltpu.roll(e, half // 2, 1) * aly[:, lo:lo + half]
        return ex + ey

    e = jnp.zeros((bsz, half), F32)
    for c in range(n_chunk):
        xf_ref[pl.ds(c, bsz, stride=n_chunk), :] = e
        e = carry(e, 0) + sf_ref[pl.ds(c, bsz, stride=n_chunk), :]
    e = jnp.zeros((bsz, half), F32)
    for c in range(n_chunk - 1, -1, -1):
        xb_ref[pl.ds(c, bsz, stride=n_chunk), :] = e
        e = carry(e, half) + sb_ref[pl.ds(c, bsz, stride=n_chunk), :]
    xin = jnp.concatenate([xf_ref[...], xb_ref[...]], axis=1).astype(BF16)

    s_idx = lax.broadcasted_iota(I32, (L, L), 0)
    t_idx = lax.broadcasted_iota(I32, (L, L), 1)
    ge = t_idx >= s_idx
    le = t_idx <= s_idx

    def pair(op, _):
        for oo in range(2):
            o = 2 * op + oo
            for h in range(n_h):
                row = o * n_h + h
                kp = jnp.broadcast_to(kpos_ref[0, pl.ds(row, 1), :], (L, L))
                kn = jnp.broadcast_to(kneg_ref[0, pl.ds(row, 1), :], (L, L))
                tp = jnp.where(ge, pltpu.roll(kp, 0, 1, stride=1, stride_axis=0), 0.0)
                tn = jnp.where(le, pltpu.roll(kn, 0, 1, stride=1, stride_axis=0), 0.0)
                tp_ref[h * L:(h + 1) * L, oo * L:(oo + 1) * L] = (tp + tn).astype(BF16)
        yp = jnp.dot(abf, tp_ref[...], preferred_element_type=F32)
        yp = yp + jnp.dot(xin, wo_ref[0, op], preferred_element_type=F32)
        for oo in range(2):
            o = 2 * op + oo
            y_ref[o] = yp[:, oo * L:(oo + 1) * L] + a_ref[o] * d_ref[0, pl.ds(o, 1), :]
        return 0

    lax.fori_loop(0, n_h // 2, pair, 0)


def _s5(us5, tables, n_chunk):
    kpos, kneg, ws, wo, al, dsk = tables
    d_s5, r, L = us5.shape
    n_g = kpos.shape[0]
    n_h = d_s5 // n_g
    n_st = ws.shape[2]
    g3 = lambda g: (g, 0, 0)
    return pl.pallas_call(
        functools.partial(_s5_kernel, n_chunk=n_chunk),
        grid=(n_g,),
        in_specs=[pl.BlockSpec((n_h, r, L), g3),
                  pl.BlockSpec((1, n_h * n_h, L), g3), pl.BlockSpec((1, n_h * n_h, L), g3),
                  pl.BlockSpec((1, n_h * L, n_st), g3),
                  pl.BlockSpec((1, n_h // 2, n_st, 2 * L), lambda g: (g, 0, 0, 0)),
                  pl.BlockSpec((1, 2, n_st), g3), pl.BlockSpec((1, n_h, L), g3)],
        out_specs=pl.BlockSpec((n_h, r, L), g3),
        out_shape=jax.ShapeDtypeStruct((d_s5, r, L), F32),
        scratch_shapes=[pltpu.VMEM((r, n_h * L), BF16), pltpu.VMEM((n_h * L, 2 * L), BF16),
                        pltpu.VMEM((r, n_st // 2), F32), pltpu.VMEM((r, n_st // 2), F32),
                        pltpu.VMEM((r, n_st // 2), F32), pltpu.VMEM((r, n_st // 2), F32)],
        compiler_params=_cparams(("parallel",)),
        name="s5",
    )(us5, kpos, kneg, ws, wo, al, dsk)


def _dft_tables(seq, fw):
    def cs(n):
        i = jnp.arange(n, dtype=I32)
        m = (i[:, None] * i[None, :]) % n
        ang = (2.0 * math.pi / n) * m.astype(F32)
        return jnp.cos(ang), jnp.sin(ang)

    def cs_split(n, r):
        s = jnp.arange(n, dtype=I32)[None, :]
        q = jnp.arange(r, dtype=I32)[:, None]
        ang_a = (2.0 * math.pi / n) * ((r * q * s) % n).astype(F32)
        ang_b = (2.0 * math.pi / n) * ((q * s) % n).astype(F32)
        ca, sa = jnp.cos(ang_a)[:, None, :], jnp.sin(ang_a)[:, None, :]
        cb, sb = jnp.cos(ang_b)[None, :, :], jnp.sin(ang_b)[None, :, :]
        return (ca * cb - sa * sb).reshape(n, n), (sa * cb + ca * sb).reshape(n, n)

    r = int(math.isqrt(seq))
    c_s, s_s = cs_split(seq, r) if r * r == seq else cs(seq)
    c_c, s_c = cs(fw)
    return (jnp.concatenate([c_s, -s_s], axis=1).astype(BF16),
            jnp.concatenate([c_c, s_c], axis=1).astype(BF16))


def _seqdft_kernel(f_ref, z_ref, o_ref, *, scale):
    seq = z_ref.shape[0]
    d_f = o_ref.shape[1]
    acc = jnp.dot(f_ref[:, :seq], z_ref[:, :d_f], preferred_element_type=F32)
    acc = acc + jnp.dot(f_ref[:, seq:], z_ref[:, d_f:], preferred_element_type=F32)
    o_ref[...] = (acc * scale).astype(o_ref.dtype)


def _seqdft(fmat, z, seq, fw):
    t, two_df = z.shape
    d_f = two_df // 2
    bsz = t // seq
    tk = 512
    nk = seq // tk
    scale = 1.0 / math.sqrt(seq * fw)
    return pl.pallas_call(
        functools.partial(_seqdft_kernel, scale=scale),
        grid=(nk, bsz),
        in_specs=[pl.BlockSpec((tk, 2 * seq), lambda k, b: (k, 0)),
                  pl.BlockSpec((seq, two_df), lambda k, b: (b, 0))],
        out_specs=pl.BlockSpec((tk, d_f), lambda k, b: (b * nk + k, 0)),
        out_shape=jax.ShapeDtypeStruct((t, d_f), BF16),
        compiler_params=_cparams(("parallel", "parallel")),
        name="seqdft",
    )(fmat, z)


def _glu_kernel(y_ref, w_ref, b_ref, o_ref, zt_ref):
    n_j = y_ref.shape[1]
    d = o_ref.shape[1]
    for j in range(n_j):
        zt_ref[j * LANE:(j + 1) * LANE, :] = _gelu_tanh(y_ref[:, j, :]).T.astype(BF16)
    zt = zt_ref[...]
    a = jnp.dot(zt, w_ref[:, :d], preferred_element_type=F32) + b_ref[:, :d]
    g = jnp.dot(zt, w_ref[:, d:], preferred_element_type=F32) + b_ref[:, d:]
    o_ref[...] = (a * jax.nn.sigmoid(g)).astype(BF16)


def _glu(ys5, w, b):
    d_s5, r, L = ys5.shape
    t = r * L
    n = w.shape[1]
    tm = 1024
    return pl.pallas_call(
        _glu_kernel,
        grid=(t // tm,),
        in_specs=[pl.BlockSpec((d_s5, tm // L, L), lambda i: (0, i, 0)),
                  _const_spec((d_s5, n)), _const_spec((1, n))],
        out_specs=pl.BlockSpec((tm, n // 2), lambda i: (i, 0)),
        out_shape=jax.ShapeDtypeStruct((t, n // 2), BF16),
        scratch_shapes=[pltpu.VMEM((tm, d_s5), BF16)],
        compiler_params=_cparams(("parallel",)),
        name="glu",
    )(ys5, w, b)


U32 = jnp.uint32
PACK_SUB = 4


def _row_slab(ref, r):
    return ref.at[pl.ds(pl.multiple_of(r * PACK_SUB, PACK_SUB), PACK_SUB), :]


def _pack_rows(ref, v, first=0):
    rows, d = v.shape
    half = d // 2
    bits = lax.bitcast_convert_type(v.astype(BF16).astype(F32), U32)
    for c in range(PACK_SUB):
        lo = bits[:, c * LANE:(c + 1) * LANE] >> 16
        hi = bits[:, half + c * LANE:half + (c + 1) * LANE] & jnp.uint32(0xFFFF0000)
        ref[pl.ds(first * PACK_SUB + c, rows, stride=PACK_SUB), :] = hi | lo


def _unpack_rows(ref, rows, first=0):
    los, his = [], []
    for c in range(PACK_SUB):
        w = ref[pl.ds(first * PACK_SUB + c, rows, stride=PACK_SUB), :]
        los.append(lax.bitcast_convert_type(w << 16, F32))
        his.append(lax.bitcast_convert_type(w & jnp.uint32(0xFFFF0000), F32))
    return jnp.concatenate(los + his, axis=1)


def _merge_kernel(x_ref, brs_ref, yf_ref, gates_ref, g1_ref, sc_ref, sh_ref, wfo_ref, bfo_ref,
                  wo_ref, bo_ref, lng_ref, lnb_ref, x1_ref, hhi_ref, hlo_ref, hp_ref, *, alpha):
    d = x_ref.shape[1]
    br_f = jnp.dot(yf_ref[...], wfo_ref[...], preferred_element_type=F32) + bfo_ref[...]
    merged = gates_ref[:, :d].astype(F32) * brs_ref[...].astype(F32) + gates_ref[:, d:].astype(F32) * br_f
    mix = jnp.dot(merged.astype(BF16), wo_ref[...], preferred_element_type=F32) + bo_ref[...]
    v = alpha * x_ref[...] + g1_ref[0] * mix
    x1 = _standardize(v) * lng_ref[...] + lnb_ref[...]
    x1_ref[...] = x1
    h = _standardize(x1) * (1.0 + sc_ref[0]) + sh_ref[0]
    hhi = h.astype(BF16)
    hhi_ref[...] = hhi
    hlo_ref[...] = (h - hhi.astype(F32)).astype(BF16)
    _pack_rows(hp_ref, h)


def _merge(x2, brs, yf, gates, g1, sc2, sh2, wfo, bfo, wo, bo, lng, lnb, seq, alpha):
    t, d = x2.shape
    d_f = yf.shape[1]
    tm = 512
    tpb = seq // tm
    bsz = t // seq
    row = lambda i: (i, 0)
    bat = lambda i: (i // tpb, 0, 0)
    return pl.pallas_call(
        functools.partial(_merge_kernel, alpha=alpha),
        grid=(t // tm,),
        in_specs=[pl.BlockSpec((tm, d), row), pl.BlockSpec((tm, d), row), pl.BlockSpec((tm, d_f), row),
                  pl.BlockSpec((tm, 2 * d), row),
                  pl.BlockSpec((1, 1, d), bat), pl.BlockSpec((1, 1, d), bat), pl.BlockSpec((1, 1, d), bat),
                  _const_spec((d_f, d)), _const_spec((1, d)), _const_spec((d, d)), _const_spec((1, d)),
                  _const_spec((1, d)), _const_spec((1, d))],
        out_specs=[pl.BlockSpec((tm, d), row), pl.BlockSpec((tm, d), row), pl.BlockSpec((tm, d), row),
                   pl.BlockSpec((tm * PACK_SUB, LANE), row)],
        out_shape=[jax.ShapeDtypeStruct((t, d), F32), jax.ShapeDtypeStruct((t, d), BF16),
                   jax.ShapeDtypeStruct((t, d), BF16), jax.ShapeDtypeStruct((t * PACK_SUB, LANE), U32)],
        compiler_params=_cparams(("parallel",)),
        name="merge",
    )(x2, brs, yf, gates, g1.reshape(bsz, 1, d), sc2.reshape(bsz, 1, d), sh2.reshape(bsz, 1, d),
      wfo, bfo, wo, bo, lng, lnb)


def _router_kernel(hhi_ref, hlo_ref, whi_ref, wlo_ref, bias_ref, tri_ref,
                   eidx_ref, gate_ref, rank_ref, cnt_ref, base_ref):
    n_e = whi_ref.shape[0]
    tm = hhi_ref.shape[0]
    gsz = n_e // N_EXPERT_GROUPS
    nt = (((1,), (1,)), ((), ()))

    @pl.when(pl.program_id(0) == 0)
    def _():
        base_ref[...] = jnp.zeros_like(base_ref)

    hhi = hhi_ref[...]
    logits = lax.dot_general(whi_ref[...], hhi, nt, preferred_element_type=F32)
    logits = logits + lax.dot_general(wlo_ref[...], hhi, nt, preferred_element_type=F32)
    logits = logits + lax.dot_general(whi_ref[...], hlo_ref[...], nt, preferred_element_type=F32)
    scores = jax.nn.sigmoid(logits)
    sel = scores + bias_ref[:, 0:1]

    g3 = sel.reshape(N_EXPERT_GROUPS, gsz, tm)
    i3 = lax.broadcasted_iota(I32, g3.shape, 1).astype(F32)
    m1 = jnp.max(g3, axis=1, keepdims=True)
    first = jnp.min(jnp.where(g3 == m1, i3, float(gsz)), axis=1, keepdims=True)
    m2 = jnp.max(jnp.where(i3 == first, NEG_INF, g3), axis=1, keepdims=True)
    gs = (m1 + m2).reshape(N_EXPERT_GROUPS, tm)

    gi = lax.broadcasted_iota(I32, gs.shape, 0).astype(F32)
    gsel = jnp.zeros(gs.shape, F32)
    cur = gs
    for _ in range(TOPK_GROUPS):
        m = jnp.max(cur, axis=0, keepdims=True)
        f = jnp.min(jnp.where(cur == m, gi, float(N_EXPERT_GROUPS)), axis=0, keepdims=True)
        pick = gi == f
        gsel = jnp.where(pick, 1.0, gsel)
        cur = jnp.where(pick, NEG_INF, cur)
    gmask = jnp.broadcast_to(gsel.reshape(N_EXPERT_GROUPS, 1, tm), g3.shape).reshape(n_e, tm)
    masked = jnp.where(gmask > 0.5, sel, NEG_INF)

    ri = lax.broadcasted_iota(I32, (n_e, tm), 0).astype(F32)
    picks = []
    gates = []
    multihot = jnp.zeros((n_e, tm), F32)
    for _ in range(TOP_K):
        m = jnp.max(masked, axis=0, keepdims=True)
        f = jnp.min(jnp.where(masked == m, ri, float(n_e)), axis=0, keepdims=True)
        pick = ri == f
        picks.append(f)
        gates.append(jnp.sum(jnp.where(pick, scores, 0.0), axis=0, keepdims=True))
        multihot = jnp.where(pick, 1.0, multihot)
        masked = jnp.where(pick, NEG_INF, masked)
    gsum = gates[0]
    for g in gates[1:]:
        gsum = gsum + g

    rankmat = jnp.dot(multihot.astype(BF16), tri_ref[...], preferred_element_type=F32) + base_ref[:, 0:1]
    for k in range(TOP_K):
        pick = ri == picks[k]
        eidx_ref[k:k + 1, :] = picks[k].astype(I32)
        gate_ref[k:k + 1, :] = gates[k] / gsum * ROUTED_SCALE
        rank_ref[k:k + 1, :] = jnp.sum(jnp.where(pick, rankmat, 0.0), axis=0, keepdims=True).astype(I32)
    base_ref[...] = base_ref[...] + jnp.sum(multihot, axis=1, keepdims=True)
    cnt_ref[...] = base_ref[...].astype(I32)


def _router(hhi, hlo, wrt_hi, wrt_lo, bias):
    t, d = hhi.shape
    n_e = wrt_hi.shape[0]
    tm = 512
    tri = (jnp.arange(tm)[:, None] < jnp.arange(tm)[None, :]).astype(BF16)
    kt = lambda i: (0, i)
    return pl.pallas_call(
        _router_kernel,
        grid=(t // tm,),
        in_specs=[pl.BlockSpec((tm, d), lambda i: (i, 0)), pl.BlockSpec((tm, d), lambda i: (i, 0)),
                  _const_spec((n_e, d)), _const_spec((n_e, d)), _const_spec((n_e, LANE)),
                  _const_spec((tm, tm))],
        out_specs=[pl.BlockSpec((TOP_K, tm), kt), pl.BlockSpec((TOP_K, tm), kt), pl.BlockSpec((TOP_K, tm), kt),
                   pl.BlockSpec((n_e, LANE), lambda i: (0, 0))],
        out_shape=[jax.ShapeDtypeStruct((TOP_K, t), I32), jax.ShapeDtypeStruct((TOP_K, t), F32),
                   jax.ShapeDtypeStruct((TOP_K, t), I32), jax.ShapeDtypeStruct((n_e, LANE), I32)],
        scratch_shapes=[pltpu.VMEM((n_e, LANE), F32)],
        compiler_params=_cparams(("arbitrary",)),
        name="router",
    )(hhi, hlo, wrt_hi, wrt_lo, jnp.broadcast_to(bias.astype(F32).reshape(n_e, 1), (n_e, LANE)), tri)


def _dest_kernel(eidx_ref, rank_ref, pstart_ref, dest_ref):
    n_e = pstart_ref.shape[0]
    tm = eidx_ref.shape[1]
    ri = lax.broadcasted_iota(I32, (n_e, tm), 0)
    ps = pstart_ref[:, 0:1].astype(F32)
    for k in range(TOP_K):
        hit = ri == eidx_ref[k:k + 1, :]
        base = jnp.sum(jnp.where(hit, ps, 0.0), axis=0, keepdims=True)
        dest_ref[k:k + 1, :] = base.astype(I32) + rank_ref[k:k + 1, :]


def _dest(eidx, rank, pstart):
    k, t = eidx.shape
    n_e = pstart.shape[0]
    tm = 2048
    kt = lambda i: (0, i)
    return pl.pallas_call(
        _dest_kernel,
        grid=(t // tm,),
        in_specs=[pl.BlockSpec((k, tm), kt), pl.BlockSpec((k, tm), kt), _const_spec((n_e, LANE))],
        out_specs=pl.BlockSpec((k, tm), kt),
        out_shape=jax.ShapeDtypeStruct((k, t), I32),
        compiler_params=_cparams(("parallel",)),
        name="dest",
    )(eidx, rank, jnp.broadcast_to(pstart.astype(I32).reshape(n_e, 1), (n_e, LANE)))


def _zero_fill(pad_lo_ref, pad_n_ref, zeros_ref, rows_ref, sem):
    def each_copy(act):
        def per_entry(e, _):
            lo = pad_lo_ref[e]
            n = pad_n_ref[e]
            n_full = n // FFN_BLOCK

            def full(j, _):
                act(pltpu.make_async_copy(
                    zeros_ref, rows_ref.at[pl.ds(pl.multiple_of((lo + j * FFN_BLOCK) * PACK_SUB, PACK_SUB),
                                                 FFN_BLOCK * PACK_SUB), :], sem))
                return 0

            lax.fori_loop(0, n_full, full, 0)
            off = lo + n_full * FFN_BLOCK
            rem = n - n_full * FFN_BLOCK
            bit = FFN_BLOCK // 2
            while bit >= 1:
                take = rem & bit

                @pl.when(take != 0)
                def _(off=off, bit=bit):
                    act(pltpu.make_async_copy(
                        zeros_ref.at[pl.ds(0, bit * PACK_SUB), :],
                        rows_ref.at[pl.ds(pl.multiple_of(off * PACK_SUB, PACK_SUB), bit * PACK_SUB), :], sem))

                off = off + take
                bit //= 2
            return 0

        lax.fori_loop(0, pad_lo_ref.shape[0], per_entry, 0)

    each_copy(lambda cp: cp.start())
    each_copy(lambda cp: cp.wait())


def _dispatch_kernel(pad_lo_ref, pad_n_ref, dest_ref, hp_ref, rows_ref, zeros_ref, sem, zsem):
    tm = hp_ref.shape[0] // PACK_SUB

    @pl.when(pl.program_id(0) == 0)
    def _():
        zeros_ref[...] = jnp.zeros_like(zeros_ref)
        _zero_fill(pad_lo_ref, pad_n_ref, zeros_ref, rows_ref, zsem)

    def body(tt, _):
        for k in range(TOP_K):
            pltpu.make_async_copy(_row_slab(hp_ref, tt), _row_slab(rows_ref, dest_ref[0, 0, k * tm + tt]),
                                  sem).start(priority=k % 2)
        return 0

    lax.fori_loop(0, tm, body, 0)
    all_rows = rows_ref.at[pl.ds(0, TOP_K * tm * PACK_SUB), :]
    pltpu.make_async_copy(all_rows, all_rows, sem).wait()


def _dispatch(hp, dest_tiles, pad_lo, pad_n, n_rows):
    n_tile, _, per_tile = dest_tiles.shape
    tm = per_tile // TOP_K
    return pl.pallas_call(
        _dispatch_kernel,
        grid_spec=pltpu.PrefetchScalarGridSpec(
            num_scalar_prefetch=2,
            grid=(n_tile,),
            in_specs=[pl.BlockSpec((1, 1, per_tile), lambda i, lo, n: (i, 0, 0), memory_space=pltpu.SMEM),
                      pl.BlockSpec((tm * PACK_SUB, LANE), lambda i, lo, n: (i, 0))],
            out_specs=pl.BlockSpec(memory_space=pl.ANY),
            scratch_shapes=[pltpu.VMEM((FFN_BLOCK * PACK_SUB, LANE), U32),
                            pltpu.SemaphoreType.DMA, pltpu.SemaphoreType.DMA],
        ),
        out_shape=jax.ShapeDtypeStruct((n_rows * PACK_SUB, LANE), U32),
        compiler_params=_cparams(("arbitrary",)),
        name="dispatch",
    )(pad_lo, pad_n, dest_tiles, hp)


def _ffn_kernel(blk_e_ref, n_used_ref, rows_ref, wg_ref, wu_ref, wd_ref, y_ref, wgu_bf_ref, wd_bf_ref):
    b = pl.program_id(0)
    live = b < n_used_ref[0]
    d_e = wd_ref.shape[1]

    @pl.when(jnp.logical_or(b == 0, blk_e_ref[b] != blk_e_ref[jnp.maximum(b - 1, 0)]))
    def _():
        wgu_bf_ref[:, :d_e] = wg_ref[0].astype(BF16)
        wgu_bf_ref[:, d_e:] = wu_ref[0].astype(BF16)
        wd_bf_ref[...] = wd_ref[0].astype(BF16)

    @pl.when(live)
    def _():
        x = _unpack_rows(rows_ref, FFN_BLOCK).astype(BF16)
        au = jnp.dot(x, wgu_bf_ref[...], preferred_element_type=F32)
        hid = (_silu(au[:, :d_e]) * au[:, d_e:]).astype(BF16)
        _pack_rows(y_ref, jnp.dot(hid, wd_bf_ref[...], preferred_element_type=F32))

    @pl.when(jnp.logical_not(live))
    def _():
        y_ref[...] = jnp.zeros_like(y_ref)


def _ffn(rows, blk_e, n_used, wg, wu, wd):
    n_rows = rows.shape[0] // PACK_SUB
    n_e, d, d_e = wg.shape
    n_blk = n_rows // FFN_BLOCK
    blk = lambda b, e, n: (b, 0)
    exp = lambda b, e, n: (e[b], 0, 0)
    return pl.pallas_call(
        _ffn_kernel,
        grid_spec=pltpu.PrefetchScalarGridSpec(
            num_scalar_prefetch=2,
            grid=(n_blk,),
            in_specs=[pl.BlockSpec((FFN_BLOCK * PACK_SUB, LANE), blk),
                      pl.BlockSpec((1, d, d_e), exp), pl.BlockSpec((1, d, d_e), exp),
                      pl.BlockSpec((1, d_e, d), exp)],
            out_specs=pl.BlockSpec((FFN_BLOCK * PACK_SUB, LANE), blk),
            scratch_shapes=[pltpu.VMEM((d, 2 * d_e), BF16), pltpu.VMEM((d_e, d), BF16)],
        ),
        out_shape=jax.ShapeDtypeStruct((n_rows * PACK_SUB, LANE), U32),
        compiler_params=_cparams(("arbitrary",)),
        name="ffn",
    )(blk_e, n_used, rows, wg, wu, wd)


def _final_kernel(dest_ref, dnext_ref, x1_ref, h_ref, gate_ref, g2_ref, wsg_ref, wsu_ref, wsd_ref, lng_ref, lnb_ref,
                  y_ref, o_ref, ybuf_ref, sems, *, alpha):
    i = pl.program_id(0)
    tm = x1_ref.shape[0]
    per_tile = TOP_K * tm

    def gather(d_ref, slot):
        def body(tt, _):
            for k in range(TOP_K):
                pltpu.make_async_copy(_row_slab(y_ref, d_ref[0, 0, k * tm + tt]),
                                      _row_slab(ybuf_ref, slot * per_tile + k * tm + tt),
                                      sems.at[slot]).start(priority=k % 2)
            return 0

        lax.fori_loop(0, tm, body, 0)

    @pl.when(i == 0)
    def _():
        gather(dest_ref, 0)

    @pl.when(i + 1 < pl.num_programs(0))
    def _():
        gather(dnext_ref, (i + 1) % 2)

    h = h_ref[...]
    a = jnp.dot(h, wsg_ref[...], preferred_element_type=F32)
    u = jnp.dot(h, wsu_ref[...], preferred_element_type=F32)
    shared = jnp.dot((_silu(a) * u).astype(BF16), wsd_ref[...], preferred_element_type=F32)

    slot = i % 2
    half = ybuf_ref.at[pl.ds(pl.multiple_of(slot * per_tile * PACK_SUB, per_tile * PACK_SUB), per_tile * PACK_SUB), :]
    pltpu.make_async_copy(half, half, sems.at[slot]).wait()
    first = slot * per_tile
    routed = gate_ref[:, 0:1] * _unpack_rows(ybuf_ref, tm, first=first)
    for k in range(1, TOP_K):
        routed = routed + gate_ref[:, k:k + 1] * _unpack_rows(ybuf_ref, tm, first=first + k * tm)

    v = alpha * x1_ref[...] + g2_ref[0] * (shared + routed)
    o_ref[...] = _standardize(v) * lng_ref[...] + lnb_ref[...]


def _final(x1, hhi, y_rows, dest_tiles, gate_t, g2, wsg, wsu, wsd, lng, lnb, seq, alpha):
    t, d = x1.shape
    d_sh = wsg.shape[1]
    n_tile, _, per_tile = dest_tiles.shape
    tm = per_tile // TOP_K
    tpb = seq // tm
    bsz = t // seq
    row = lambda i: (i, 0)
    return pl.pallas_call(
        functools.partial(_final_kernel, alpha=alpha),
        grid=(n_tile,),
        in_specs=[pl.BlockSpec((1, 1, per_tile), lambda i: (i, 0, 0), memory_space=pltpu.SMEM),
                  pl.BlockSpec((1, 1, per_tile), lambda i: (jnp.minimum(i + 1, n_tile - 1), 0, 0),
                               memory_space=pltpu.SMEM),
                  pl.BlockSpec((tm, d), row), pl.BlockSpec((tm, d), row), pl.BlockSpec((tm, TOP_K), row),
                  pl.BlockSpec((1, 1, d), lambda i: (i // tpb, 0, 0)),
                  _const_spec((d, d_sh)), _const_spec((d, d_sh)), _const_spec((d_sh, d)),
                  _const_spec((1, d)), _const_spec((1, d)),
                  pl.BlockSpec(memory_space=pl.ANY)],
        out_specs=pl.BlockSpec((tm, d), row),
        out_shape=jax.ShapeDtypeStruct((t, d), F32),
        scratch_shapes=[pltpu.VMEM((2 * TOP_K * tm * PACK_SUB, LANE), U32), pltpu.SemaphoreType.DMA((2,))],
        compiler_params=_cparams(("arbitrary",)),
        name="final",
    )(dest_tiles, dest_tiles, x1, hhi, gate_t, g2.reshape(bsz, 1, d), wsg, wsu, wsd, lng, lnb, y_rows)


def _split_hi_lo(w):
    hi = w.astype(BF16)
    return hi, (w - hi.astype(F32)).astype(BF16)


def kernel(x, c, w_ada, b_ada, w_in, b_in, s5_lambda_re, s5_lambda_im, s5_log_dt, s5_b_re, s5_b_im, s5_c_re, s5_c_im, s5_d, w_s5_glu, b_s5_glu, w_fourier, b_fourier, w_out, b_out, ln1_g, ln1_b, w_router, router_bias, w_exp_gate, w_exp_up, w_exp_down, w_sh_gate, w_sh_up, w_sh_down, ln2_g, ln2_b):
    bsz, seq, d = x.shape
    depth = w_ada.shape[0]
    alpha = (2 * depth) ** 0.25
    t = bsz * seq
    d_s5 = s5_d.shape[1]
    d_f = w_fourier.shape[1]
    fw = d_f // FOURIER_GROUPS
    n_e = w_router.shape[2]
    n_chunk = seq // LANE
    fmat, cmat = _dft_tables(seq, fw)
    row = lambda v: v.astype(F32).reshape(1, -1)

    x2 = x.reshape(t, d)
    for l in range(depth):
        mod = _adaln(c, w_ada[l], b_ada[l])
        sh1, sc1, g1, sh2, sc2, g2 = jnp.split(mod, 6, axis=-1)

        wi = w_in[l]
        bi = b_in[l].astype(F32)
        ws5t = wi[:, :d_s5].T.astype(BF16)
        bs5 = jnp.broadcast_to(bi[:d_s5].reshape(d_s5, 1), (d_s5, LANE))
        us5, z, gates = _proj(x2, sc1, sh1, ws5t, bs5,
                              wi[:, d_s5:d_s5 + d_f].astype(BF16), row(bi[d_s5:d_s5 + d_f]), cmat,
                              wi[:, d_s5 + d_f:].astype(BF16), row(bi[d_s5 + d_f:]), seq)

        tables = _s5_tables(s5_lambda_re[l], s5_lambda_im[l], s5_log_dt[l], s5_b_re[l], s5_b_im[l],
                            s5_c_re[l], s5_c_im[l], s5_d[l])
        ys5 = _s5(us5, tables, n_chunk)
        brs = _glu(ys5, w_s5_glu[l].astype(BF16), row(b_s5_glu[l]))
        yf = _seqdft(fmat, z, seq, fw)

        x1, hhi, hlo, hp = _merge(x2, brs, yf, gates, g1, sc2, sh2,
                              w_fourier[l].astype(BF16), row(b_fourier[l]),
                              w_out[l].astype(BF16), row(b_out[l]), row(ln1_g[l]), row(ln1_b[l]), seq, alpha)

        wrt_hi, wrt_lo = _split_hi_lo(w_router[l].astype(F32).T)
        eidx, gate, rank, cnt = _router(hhi, hlo, wrt_hi, wrt_lo, router_bias[l])

        counts = cnt[:, 0]
        padded = ((counts + FFN_BLOCK - 1) // FFN_BLOCK) * FFN_BLOCK
        pend = jnp.cumsum(padded)
        pstart = (pend - padded).astype(I32)
        dest = _dest(eidx, rank, pstart)
        n_blk = (t * TOP_K + n_e * (FFN_BLOCK - 1) + FFN_BLOCK - 1) // FFN_BLOCK
        n_rows = n_blk * FFN_BLOCK
        blk_start = jnp.arange(n_blk, dtype=I32) * FFN_BLOCK
        blk_e = jnp.minimum(jnp.sum((pend[None, :] <= blk_start[:, None]).astype(I32), axis=1), n_e - 1)
        n_used = (pend[-1:] // FFN_BLOCK).astype(I32)
        pad_lo = jnp.concatenate([pstart + counts, pend[-1:]]).astype(I32)
        pad_n = jnp.concatenate([padded - counts, n_rows - pend[-1:]]).astype(I32)

        def tiles(tm):
            return dest.reshape(TOP_K, t // tm, tm).transpose(1, 0, 2).reshape(t // tm, 1, TOP_K * tm)

        rows = _dispatch(hp, tiles(DISPATCH_TILE), pad_lo, pad_n, n_rows)
        y_rows = _ffn(rows, blk_e, n_used, w_exp_gate[l], w_exp_up[l], w_exp_down[l])
        x2 = _final(x1, hhi, y_rows, tiles(COMBINE_TILE), gate.T, g2,
                    w_sh_gate[l].astype(BF16), w_sh_up[l].astype(BF16), w_sh_down[l].astype(BF16),
                    row(ln2_g[l]), row(ln2_b[l]), seq, alpha)
    return x2.reshape(bsz, seq, d)
```

```python
import functools
import math

import jax
import jax.numpy as jnp
from jax import lax
from jax.experimental import pallas as pl
from jax.experimental.pallas import tpu as pltpu

F32 = jnp.float32
BF16 = jnp.bfloat16
I32 = jnp.int32

TOP_K = 8
N_EXPERT_GROUPS = 8
TOPK_GROUPS = 4
ROUTED_SCALE = 2.5
FOURIER_GROUPS = 4
LN_EPS = 1e-5

LANE = 128
VMEM_LIMIT = 56 * 1024 * 1024

HIGHEST = lax.Precision.HIGHEST
NEG_INF = float("-inf")

FFN_BLOCK = 512
DISPATCH_TILE = 512
COMBINE_TILE = 256


def _cparams(sem):
    return pltpu.CompilerParams(dimension_semantics=sem, vmem_limit_bytes=VMEM_LIMIT)


def _const_spec(shape):
    nd = len(shape)
    return pl.BlockSpec(shape, lambda *_: (0,) * nd, pipeline_mode=pl.Buffered(1))


def _standardize(x):
    mu = jnp.mean(x, axis=-1, keepdims=True)
    xc = x - mu
    var = jnp.mean(xc * xc, axis=-1, keepdims=True)
    return xc * lax.rsqrt(var + LN_EPS)


def _silu(x):
    return x * jax.nn.sigmoid(x)


def _gelu_tanh(x):
    return 0.5 * x * (1.0 + jnp.tanh(math.sqrt(2.0 / math.pi) * (x + 0.044715 * (x * x * x))))


def _adaln_kernel(c_ref, w_ref, b_ref, o_ref):
    a = _silu(c_ref[...])
    o_ref[...] = jnp.dot(a, w_ref[...], precision=HIGHEST, preferred_element_type=F32) + b_ref[...]


def _adaln(c, w, b):
    bsz, d = c.shape
    n = w.shape[1]
    tn = 512
    return pl.pallas_call(
        _adaln_kernel,
        grid=(n // tn,),
        in_specs=[pl.BlockSpec((bsz, d), lambda j: (0, 0)),
                  pl.BlockSpec((d, tn), lambda j: (0, j)),
                  pl.BlockSpec((1, tn), lambda j: (0, j))],
        out_specs=pl.BlockSpec((bsz, tn), lambda j: (0, j)),
        out_shape=jax.ShapeDtypeStruct((bsz, n), F32),
        compiler_params=_cparams(("parallel",)),
        name="adaln",
    )(c, w, b.reshape(1, n))


def _proj_kernel(x_ref, sc_ref, sh_ref, ws5t_ref, bs5_ref, wf_ref, bf_ref, cs_ref, wg_ref, bg_ref,
                 us5_ref, z_ref, gates_ref):
    tm = x_ref.shape[0]
    u = (_standardize(x_ref[...]) * (1.0 + sc_ref[0]) + sh_ref[0]).astype(BF16)
    p = lax.dot_general(ws5t_ref[...], u, (((1,), (1,)), ((), ())), preferred_element_type=F32)
    p = p + bs5_ref[:, 0:1]
    d_s5, n_j, _ = us5_ref.shape
    us5_2d = us5_ref.reshape(d_s5 * n_j, LANE)
    for j in range(n_j):
        us5_2d[pl.ds(j, d_s5, stride=n_j), :] = p[:, j * LANE:(j + 1) * LANE]
    uf = (jnp.dot(u, wf_ref[...], preferred_element_type=F32) + bf_ref[...]).astype(BF16)
    d_f = uf.shape[1]
    fw = d_f // FOURIER_GROUPS
    for q in range(FOURIER_GROUPS):
        zq = jnp.dot(uf[:, q * fw:(q + 1) * fw], cs_ref[...], preferred_element_type=F32)
        z_ref[:, q * fw:(q + 1) * fw] = zq[:, :fw].astype(BF16)
        z_ref[:, d_f + q * fw:d_f + (q + 1) * fw] = zq[:, fw:].astype(BF16)
    n_g = wg_ref.shape[1]
    half = n_g // 2
    for q in range(2):
        gp = jnp.dot(u, wg_ref[:, q * half:(q + 1) * half], preferred_element_type=F32)
        gp = gp + bg_ref[:, q * half:(q + 1) * half]
        gates_ref[:, q * half:(q + 1) * half] = jax.nn.sigmoid(gp).astype(BF16)


def _proj(x2, sc, sh, ws5t, bs5, wf, bf, cs, wg, bg, seq):
    t, d = x2.shape
    d_s5 = ws5t.shape[0]
    d_f = wf.shape[1]
    n_g = wg.shape[1]
    tm = 1024
    tpb = seq // tm
    bsz = t // seq
    return pl.pallas_call(
        _proj_kernel,
        grid=(t // tm,),
        in_specs=[pl.BlockSpec((tm, d), lambda i: (i, 0)),
                  pl.BlockSpec((1, 1, d), lambda i: (i // tpb, 0, 0)),
                  pl.BlockSpec((1, 1, d), lambda i: (i // tpb, 0, 0)),
                  _const_spec((d_s5, d)), _const_spec((d_s5, LANE)),
                  _const_spec((d, d_f)), _const_spec((1, d_f)),
                  _const_spec(cs.shape),
                  _const_spec((d, n_g)), _const_spec((1, n_g))],
        out_specs=[pl.BlockSpec((d_s5, tm // LANE, LANE), lambda i: (0, i, 0)),
                   pl.BlockSpec((tm, 2 * d_f), lambda i: (i, 0)),
                   pl.BlockSpec((tm, n_g), lambda i: (i, 0))],
        out_shape=[jax.ShapeDtypeStruct((d_s5, t // LANE, LANE), F32),
                   jax.ShapeDtypeStruct((t, 2 * d_f), BF16),
                   jax.ShapeDtypeStruct((t, n_g), BF16)],
        compiler_params=_cparams(("parallel",)),
        name="proj",
    )(x2, sc.reshape(bsz, 1, d), sh.reshape(bsz, 1, d), ws5t, bs5, wf, bf, cs, wg, bg)


def _s5_tables(lam_re, lam_im, log_dt, b_re, b_im, c_re, c_im, d_skip):
    L = LANE
    hp = HIGHEST
    lr, li = lam_re.astype(F32), lam_im.astype(F32)
    dt = jnp.exp(log_dt.astype(F32))[:, :, None]
    mag = jnp.exp(lr * dt)
    ang = li * dt
    ab_re, ab_im = mag * jnp.cos(ang), mag * jnp.sin(ang)
    den = lr * lr + li * li
    nr = ab_re - 1.0
    coef_re = (nr * lr + ab_im * li) / den
    coef_im = (ab_im * lr - nr * li) / den
    br, bi = b_re.astype(F32), b_im.astype(F32)
    bb_re = coef_re[..., None] * br - coef_im[..., None] * bi
    bb_im = coef_re[..., None] * bi + coef_im[..., None] * br
    cr, ci = c_re.astype(F32), c_im.astype(F32)
    n_g, n_p, n_h = br.shape[1], br.shape[2], br.shape[3]

    k = jnp.arange(L + 1, dtype=F32)[None, None, :, None]
    pmag = jnp.exp(k * (lr * dt)[:, :, None, :])
    pang = k * (li * dt)[:, :, None, :]
    pw_re, pw_im = pmag * jnp.cos(pang), pmag * jnp.sin(pang)

    m_re = cr[:, :, :, None, :] * jnp.swapaxes(bb_re, 2, 3)[:, :, None, :, :] \
        - ci[:, :, :, None, :] * jnp.swapaxes(bb_im, 2, 3)[:, :, None, :, :]
    m_im = cr[:, :, :, None, :] * jnp.swapaxes(bb_im, 2, 3)[:, :, None, :, :] \
        + ci[:, :, :, None, :] * jnp.swapaxes(bb_re, 2, 3)[:, :, None, :, :]
    kap = jnp.einsum("dgohp,dgkp->dgohk", m_re, pw_re, precision=hp) \
        - jnp.einsum("dgohp,dgkp->dgohk", m_im, pw_im, precision=hp)
    kb = kap[1]
    kpos = kap[0, ..., :L].at[..., 0].add(kb[..., 0]).reshape(n_g, n_h * n_h, L)
    kneg = jnp.concatenate([jnp.zeros_like(kb[..., 0:1]), kb[..., L - 1:0:-1]], axis=-1)
    kneg = kneg.reshape(n_g, n_h * n_h, L)

    pf_re, pf_im = pw_re[0, :, L - 1::-1][:, :L], pw_im[0, :, L - 1::-1][:, :L]
    pb_re, pb_im = pw_re[1, :, :L], pw_im[1, :, :L]

    def cmul_hs(p_re, p_im, q_re, q_im):
        a = p_re[:, None, :, :]
        b = p_im[:, None, :, :]
        c = jnp.swapaxes(q_re, 1, 2)[:, :, None, :]
        e = jnp.swapaxes(q_im, 1, 2)[:, :, None, :]
        return a * c - b * e, a * e + b * c

    wsf_re, wsf_im = cmul_hs(pf_re, pf_im, bb_re[0], bb_im[0])
    wsb_re, wsb_im = cmul_hs(pb_re, pb_im, bb_re[1], bb_im[1])
    ws = jnp.concatenate([wsf_re, wsf_im, wsb_re, wsb_im], axis=-1).reshape(n_g, n_h * L, 4 * n_p)

    qf_re, qf_im = pw_re[0, :, 1:L + 1], pw_im[0, :, 1:L + 1]
    qb_re, qb_im = pw_re[1, :, L:0:-1], pw_im[1, :, L:0:-1]

    def cmul_ot(c_r, c_i, q_r, q_i):
        a = jnp.swapaxes(c_r, 1, 2)[:, :, :, None]
        b = jnp.swapaxes(c_i, 1, 2)[:, :, :, None]
        c = jnp.swapaxes(q_r, 1, 2)[:, :, None, :]
        e = jnp.swapaxes(q_i, 1, 2)[:, :, None, :]
        return a * c - b * e, -(a * e + b * c)

    wof_re, wof_mi = cmul_ot(cr[0], ci[0], qf_re, qf_im)
    wob_re, wob_mi = cmul_ot(cr[1], ci[1], qb_re, qb_im)
    wo = jnp.concatenate([wof_re, wof_mi, wob_re, wob_mi], axis=1)
    wo = jnp.swapaxes(wo.reshape(n_g, 4 * n_p, n_h // 2, 2 * L), 1, 2)

    al_re, al_im = pw_re[:, :, L], pw_im[:, :, L]
    alx = jnp.concatenate([al_re[0], al_re[0], al_re[1], al_re[1]], axis=-1)
    aly = jnp.concatenate([-al_im[0], al_im[0], -al_im[1], al_im[1]], axis=-1)
    al = jnp.stack([alx, aly], axis=1)

    dsk = jnp.broadcast_to(d_skip.astype(F32).reshape(n_g, n_h, 1), (n_g, n_h, L))
    return kpos, kneg, ws.astype(BF16), wo.astype(BF16), al, dsk


def _s5_kernel(a_ref, kpos_ref, kneg_ref, ws_ref, wo_ref, al_ref, d_ref, y_ref,
               abf_ref, tp_ref, sf_ref, sb_ref, xf_ref, xb_ref, *, n_chunk):
    n_h, r, L = a_ref.shape
    bsz = r // n_chunk
    half = 2 * (ws_ref.shape[2] // 4)

    for h in range(n_h):
        abf_ref[:, h * L:(h + 1) * L] = a_ref[h].astype(BF16)
    abf = abf_ref[...]

    s_all = jnp.dot(abf, ws_ref[0], preferred_element_type=F32)
    sf_ref[...] = s_all[:, :half]
    sb_ref[...] = s_all[:, half:]

    alx = al_ref[0, 0:1, :]
    aly = al_ref[0, 1:2, :]

    def carry(e, lo):
        ex = e * alx[:, lo:lo + half]
        ey = pltpu.roll(e, half // 2, 1) * aly[:, lo:lo + half]
        return ex + ey

    e = jnp.zeros((bsz, half), F32)
    for c in range(n_chunk):
        xf_ref[pl.ds(c, bsz, stride=n_chunk), :] = e
        e = carry(e, 0) + sf_ref[pl.ds(c, bsz, stride=n_chunk), :]
    e = jnp.zeros((bsz, half), F32)
    for c in range(n_chunk - 1, -1, -1):
        xb_ref[pl.ds(c, bsz, stride=n_chunk), :] = e
        e = carry(e, half) + sb_ref[pl.ds(c, bsz, stride=n_chunk), :]
    xin = jnp.concatenate([xf_ref[...], xb_ref[...]], axis=1).astype(BF16)

    s_idx = lax.broadcasted_iota(I32, (L, L), 0)
    j_idx = lax.broadcasted_iota(I32, (L, L), 1)
    fwd_part = j_idx + s_idx < L

    def pair(op, _):
        for oo in range(2):
            o = 2 * op + oo
            for h in range(n_h):
                row = o * n_h + h
                kp = jnp.broadcast_to(kpos_ref[0, pl.ds(row, 1), :], (L, L))
                kn = jnp.broadcast_to(kneg_ref[0, pl.ds(row, 1), :], (L, L))
                tile = pltpu.roll(jnp.where(fwd_part, kp, kn), 0, 1, stride=1, stride_axis=0)
                tp_ref[h * L:(h + 1) * L, oo * L:(oo + 1) * L] = tile.astype(BF16)
        yp = jnp.dot(abf, tp_ref[...], preferred_element_type=F32)
        yp = yp + jnp.dot(xin, wo_ref[0, op], preferred_element_type=F32)
        for oo in range(2):
            o = 2 * op + oo
            y_ref[o] = yp[:, oo * L:(oo + 1) * L] + a_ref[o] * d_ref[0, pl.ds(o, 1), :]
        return 0

    lax.fori_loop(0, n_h // 2, pair, 0)


def _s5(us5, tables, n_chunk):
    kpos, kneg, ws, wo, al, dsk = tables
    d_s5, r, L = us5.shape
    n_g = kpos.shape[0]
    n_h = d_s5 // n_g
    n_st = ws.shape[2]
    g3 = lambda g: (g, 0, 0)
    return pl.pallas_call(
        functools.partial(_s5_kernel, n_chunk=n_chunk),
        grid=(n_g,),
        in_specs=[pl.BlockSpec((n_h, r, L), g3),
                  pl.BlockSpec((1, n_h * n_h, L), g3), pl.BlockSpec((1, n_h * n_h, L), g3),
                  pl.BlockSpec((1, n_h * L, n_st), g3),
                  pl.BlockSpec((1, n_h // 2, n_st, 2 * L), lambda g: (g, 0, 0, 0)),
                  pl.BlockSpec((1, 2, n_st), g3), pl.BlockSpec((1, n_h, L), g3)],
        out_specs=pl.BlockSpec((n_h, r, L), g3),
        out_shape=jax.ShapeDtypeStruct((d_s5, r, L), F32),
        scratch_shapes=[pltpu.VMEM((r, n_h * L), BF16), pltpu.VMEM((n_h * L, 2 * L), BF16),
                        pltpu.VMEM((r, n_st // 2), F32), pltpu.VMEM((r, n_st // 2), F32),
                        pltpu.VMEM((r, n_st // 2), F32), pltpu.VMEM((r, n_st // 2), F32)],
        compiler_params=_cparams(("parallel",)),
        name="s5",
    )(us5, kpos, kneg, ws, wo, al, dsk)


def _dft_tables(seq, fw):
    def cs(n):
        i = jnp.arange(n, dtype=I32)
        m = (i[:, None] * i[None, :]) % n
        ang = (2.0 * math.pi / n) * m.astype(F32)
        return jnp.cos(ang), jnp.sin(ang)

    def cs_split(n, r):
        s = jnp.arange(n, dtype=I32)[None, :]
        q = jnp.arange(r, dtype=I32)[:, None]
        ang_a = (2.0 * math.pi / n) * ((r * q * s) % n).astype(F32)
        ang_b = (2.0 * math.pi / n) * ((q * s) % n).astype(F32)
        ca, sa = jnp.cos(ang_a)[:, None, :], jnp.sin(ang_a)[:, None, :]
        cb, sb = jnp.cos(ang_b)[None, :, :], jnp.sin(ang_b)[None, :, :]
        return (ca * cb - sa * sb).reshape(n, n), (sa * cb + ca * sb).reshape(n, n)

    r = int(math.isqrt(seq))
    c_s, s_s = cs_split(seq, r) if r * r == seq else cs(seq)
    c_c, s_c = cs(fw)
    return (jnp.concatenate([c_s, -s_s], axis=1).astype(BF16),
            jnp.concatenate([c_c, s_c], axis=1).astype(BF16))


def _seqdft_kernel(f_ref, z_ref, o_ref, *, scale):
    seq = z_ref.shape[0]
    d_f = o_ref.shape[1]
    acc = jnp.dot(f_ref[:, :seq], z_ref[:, :d_f], preferred_element_type=F32)
    acc = acc + jnp.dot(f_ref[:, seq:], z_ref[:, d_f:], preferred_element_type=F32)
    o_ref[...] = (acc * scale).astype(o_ref.dtype)


def _seqdft(fmat, z, seq, fw):
    t, two_df = z.shape
    d_f = two_df // 2
    bsz = t // seq
    tk = 512
    nk = seq // tk
    scale = 1.0 / math.sqrt(seq * fw)
    return pl.pallas_call(
        functools.partial(_seqdft_kernel, scale=scale),
        grid=(nk, bsz),
        in_specs=[pl.BlockSpec((tk, 2 * seq), lambda k, b: (k, 0)),
                  pl.BlockSpec((seq, two_df), lambda k, b: (b, 0))],
        out_specs=pl.BlockSpec((tk, d_f), lambda k, b: (b * nk + k, 0)),
        out_shape=jax.ShapeDtypeStruct((t, d_f), BF16),
        compiler_params=_cparams(("parallel", "parallel")),
        name="seqdft",
    )(fmat, z)


def _glu_kernel(y_ref, w_ref, b_ref, o_ref, zt_ref):
    d_s5, n_j, _ = y_ref.shape
    d = o_ref.shape[1]
    y2 = y_ref.reshape(d_s5 * n_j, LANE)
    for j in range(n_j):
        zt_ref[j * LANE:(j + 1) * LANE, :] = _gelu_tanh(y2[pl.ds(j, d_s5, stride=n_j), :]).T.astype(BF16)
    zt = zt_ref[...]
    a = jnp.dot(zt, w_ref[:, :d], preferred_element_type=F32) + b_ref[:, :d]
    g = jnp.dot(zt, w_ref[:, d:], preferred_element_type=F32) + b_ref[:, d:]
    o_ref[...] = (a * jax.nn.sigmoid(g)).astype(BF16)


def _glu(ys5, w, b):
    d_s5, r, L = ys5.shape
    t = r * L
    n = w.shape[1]
    tm = 1024
    return pl.pallas_call(
        _glu_kernel,
        grid=(t // tm,),
        in_specs=[pl.BlockSpec((d_s5, tm // L, L), lambda i: (0, i, 0)),
                  _const_spec((d_s5, n)), _const_spec((1, n))],
        out_specs=pl.BlockSpec((tm, n // 2), lambda i: (i, 0)),
        out_shape=jax.ShapeDtypeStruct((t, n // 2), BF16),
        scratch_shapes=[pltpu.VMEM((tm, d_s5), BF16)],
        compiler_params=_cparams(("parallel",)),
        name="glu",
    )(ys5, w, b)


U32 = jnp.uint32
PACK_SUB = 4


def _row_slab(ref, r):
    return ref.at[pl.ds(pl.multiple_of(r * PACK_SUB, PACK_SUB), PACK_SUB), :]


def _pack_rows(ref, v, first=0):
    rows, d = v.shape
    half = d // 2
    bits = lax.bitcast_convert_type(v.astype(BF16).astype(F32), U32)
    for c in range(PACK_SUB):
        lo = bits[:, c * LANE:(c + 1) * LANE] >> 16
        hi = bits[:, half + c * LANE:half + (c + 1) * LANE] & jnp.uint32(0xFFFF0000)
        ref[pl.ds(first * PACK_SUB + c, rows, stride=PACK_SUB), :] = hi | lo


def _unpack_rows(ref, rows, first=0):
    los, his = [], []
    for c in range(PACK_SUB):
        w = ref[pl.ds(first * PACK_SUB + c, rows, stride=PACK_SUB), :]
        los.append(lax.bitcast_convert_type(w << 16, F32))
        his.append(lax.bitcast_convert_type(w & jnp.uint32(0xFFFF0000), F32))
    return jnp.concatenate(los + his, axis=1)


def _merge_kernel(x_ref, brs_ref, yf_ref, gates_ref, g1_ref, sc_ref, sh_ref, wfo_ref, bfo_ref,
                  wo_ref, bo_ref, lng_ref, lnb_ref, x1_ref, hhi_ref, hlo_ref, hp_ref, *, alpha):
    d = x_ref.shape[1]
    br_f = jnp.dot(yf_ref[...], wfo_ref[...], preferred_element_type=F32) + bfo_ref[...]
    merged = gates_ref[:, :d].astype(F32) * brs_ref[...].astype(F32) + gates_ref[:, d:].astype(F32) * br_f
    mix = jnp.dot(merged.astype(BF16), wo_ref[...], preferred_element_type=F32) + bo_ref[...]
    v = alpha * x_ref[...] + g1_ref[0] * mix
    x1 = _standardize(v) * lng_ref[...] + lnb_ref[...]
    x1_ref[...] = x1
    h = _standardize(x1) * (1.0 + sc_ref[0]) + sh_ref[0]
    hhi = h.astype(BF16)
    hhi_ref[...] = hhi
    hlo_ref[...] = (h - hhi.astype(F32)).astype(BF16)
    _pack_rows(hp_ref, h)


def _merge(x2, brs, yf, gates, g1, sc2, sh2, wfo, bfo, wo, bo, lng, lnb, seq, alpha):
    t, d = x2.shape
    d_f = yf.shape[1]
    tm = 512
    tpb = seq // tm
    bsz = t // seq
    row = lambda i: (i, 0)
    bat = lambda i: (i // tpb, 0, 0)
    return pl.pallas_call(
        functools.partial(_merge_kernel, alpha=alpha),
        grid=(t // tm,),
        in_specs=[pl.BlockSpec((tm, d), row), pl.BlockSpec((tm, d), row), pl.BlockSpec((tm, d_f), row),
                  pl.BlockSpec((tm, 2 * d), row),
                  pl.BlockSpec((1, 1, d), bat), pl.BlockSpec((1, 1, d), bat), pl.BlockSpec((1, 1, d), bat),
                  _const_spec((d_f, d)), _const_spec((1, d)), _const_spec((d, d)), _const_spec((1, d)),
                  _const_spec((1, d)), _const_spec((1, d))],
        out_specs=[pl.BlockSpec((tm, d), row), pl.BlockSpec((tm, d), row), pl.BlockSpec((tm, d), row),
                   pl.BlockSpec((tm * PACK_SUB, LANE), row)],
        out_shape=[jax.ShapeDtypeStruct((t, d), F32), jax.ShapeDtypeStruct((t, d), BF16),
                   jax.ShapeDtypeStruct((t, d), BF16), jax.ShapeDtypeStruct((t * PACK_SUB, LANE), U32)],
        compiler_params=_cparams(("parallel",)),
        name="merge",
    )(x2, brs, yf, gates, g1.reshape(bsz, 1, d), sc2.reshape(bsz, 1, d), sh2.reshape(bsz, 1, d),
      wfo, bfo, wo, bo, lng, lnb)


def _router_kernel(hhi_ref, hlo_ref, whi_ref, wlo_ref, bias_ref, tri_ref,
                   eidx_ref, gate_ref, rank_ref, cnt_ref, base_ref):
    n_e = whi_ref.shape[0]
    tm = hhi_ref.shape[0]
    gsz = n_e // N_EXPERT_GROUPS
    nt = (((1,), (1,)), ((), ()))

    @pl.when(pl.program_id(0) == 0)
    def _():
        base_ref[...] = jnp.zeros_like(base_ref)

    hhi = hhi_ref[...]
    logits = lax.dot_general(whi_ref[...], hhi, nt, preferred_element_type=F32)
    logits = logits + lax.dot_general(wlo_ref[...], hhi, nt, preferred_element_type=F32)
    logits = logits + lax.dot_general(whi_ref[...], hlo_ref[...], nt, preferred_element_type=F32)
    scores = jax.nn.sigmoid(logits)
    sel = scores + bias_ref[:, 0:1]

    g3 = sel.reshape(N_EXPERT_GROUPS, gsz, tm)
    i3 = lax.broadcasted_iota(I32, g3.shape, 1).astype(F32)
    m1 = jnp.max(g3, axis=1, keepdims=True)
    first = jnp.min(jnp.where(g3 == m1, i3, float(gsz)), axis=1, keepdims=True)
    m2 = jnp.max(jnp.where(i3 == first, NEG_INF, g3), axis=1, keepdims=True)
    gs = (m1 + m2).reshape(N_EXPERT_GROUPS, tm)

    gi = lax.broadcasted_iota(I32, gs.shape, 0).astype(F32)
    gsel = jnp.zeros(gs.shape, F32)
    cur = gs
    for _ in range(TOPK_GROUPS):
        m = jnp.max(cur, axis=0, keepdims=True)
        f = jnp.min(jnp.where(cur == m, gi, float(N_EXPERT_GROUPS)), axis=0, keepdims=True)
        pick = gi == f
        gsel = jnp.where(pick, 1.0, gsel)
        cur = jnp.where(pick, NEG_INF, cur)
    gmask = jnp.broadcast_to(gsel.reshape(N_EXPERT_GROUPS, 1, tm), g3.shape).reshape(n_e, tm)
    masked = jnp.where(gmask > 0.5, sel, NEG_INF)

    ri = lax.broadcasted_iota(I32, (n_e, tm), 0).astype(F32)
    picks = []
    gates = []
    multihot = jnp.zeros((n_e, tm), F32)
    for _ in range(TOP_K):
        m = jnp.max(masked, axis=0, keepdims=True)
        f = jnp.min(jnp.where(masked == m, ri, float(n_e)), axis=0, keepdims=True)
        pick = ri == f
        picks.append(f)
        gates.append(jnp.sum(jnp.where(pick, scores, 0.0), axis=0, keepdims=True))
        multihot = jnp.where(pick, 1.0, multihot)
        masked = jnp.where(pick, NEG_INF, masked)
    gsum = gates[0]
    for g in gates[1:]:
        gsum = gsum + g

    rankmat = jnp.dot(multihot.astype(BF16), tri_ref[...], preferred_element_type=F32) + base_ref[:, 0:1]
    for k in range(TOP_K):
        pick = ri == picks[k]
        eidx_ref[k:k + 1, :] = picks[k].astype(I32)
        gate_ref[k:k + 1, :] = gates[k] / gsum * ROUTED_SCALE
        rank_ref[k:k + 1, :] = jnp.sum(jnp.where(pick, rankmat, 0.0), axis=0, keepdims=True).astype(I32)
    base_ref[...] = base_ref[...] + jnp.sum(multihot, axis=1, keepdims=True)
    cnt_ref[...] = base_ref[...].astype(I32)


def _router(hhi, hlo, wrt_hi, wrt_lo, bias):
    t, d = hhi.shape
    n_e = wrt_hi.shape[0]
    tm = 512
    tri = (jnp.arange(tm)[:, None] < jnp.arange(tm)[None, :]).astype(BF16)
    kt = lambda i: (0, i)
    return pl.pallas_call(
        _router_kernel,
        grid=(t // tm,),
        in_specs=[pl.BlockSpec((tm, d), lambda i: (i, 0)), pl.BlockSpec((tm, d), lambda i: (i, 0)),
                  _const_spec((n_e, d)), _const_spec((n_e, d)), _const_spec((n_e, LANE)),
                  _const_spec((tm, tm))],
        out_specs=[pl.BlockSpec((TOP_K, tm), kt), pl.BlockSpec((TOP_K, tm), kt), pl.BlockSpec((TOP_K, tm), kt),
                   pl.BlockSpec((n_e, LANE), lambda i: (0, 0))],
        out_shape=[jax.ShapeDtypeStruct((TOP_K, t), I32), jax.ShapeDtypeStruct((TOP_K, t), F32),
                   jax.ShapeDtypeStruct((TOP_K, t), I32), jax.ShapeDtypeStruct((n_e, LANE), I32)],
        scratch_shapes=[pltpu.VMEM((n_e, LANE), F32)],
        compiler_params=_cparams(("arbitrary",)),
        name="router",
    )(hhi, hlo, wrt_hi, wrt_lo, jnp.broadcast_to(bias.astype(F32).reshape(n_e, 1), (n_e, LANE)), tri)


def _dest_kernel(eidx_ref, rank_ref, pstart_ref, dest_ref):
    n_e = pstart_ref.shape[0]
    tm = eidx_ref.shape[1]
    ri = lax.broadcasted_iota(I32, (n_e, tm), 0)
    ps = pstart_ref[:, 0:1].astype(F32)
    for k in range(TOP_K):
        hit = ri == eidx_ref[k:k + 1, :]
        base = jnp.sum(jnp.where(hit, ps, 0.0), axis=0, keepdims=True)
        dest_ref[k:k + 1, :] = base.astype(I32) + rank_ref[k:k + 1, :]


def _dest(eidx, rank, pstart):
    k, t = eidx.shape
    n_e = pstart.shape[0]
    tm = 2048
    kt = lambda i: (0, i)
    return pl.pallas_call(
        _dest_kernel,
        grid=(t // tm,),
        in_specs=[pl.BlockSpec((k, tm), kt), pl.BlockSpec((k, tm), kt), _const_spec((n_e, LANE))],
        out_specs=pl.BlockSpec((k, tm), kt),
        out_shape=jax.ShapeDtypeStruct((k, t), I32),
        compiler_params=_cparams(("parallel",)),
        name="dest",
    )(eidx, rank, jnp.broadcast_to(pstart.astype(I32).reshape(n_e, 1), (n_e, LANE)))


def _zero_fill(pad_lo_ref, pad_n_ref, zeros_ref, rows_ref, sem):
    def each_copy(act):
        def per_entry(e, _):
            lo = pad_lo_ref[e]
            n = pad_n_ref[e]
            n_full = n // FFN_BLOCK

            def full(j, _):
                act(pltpu.make_async_copy(
                    zeros_ref, rows_ref.at[pl.ds(pl.multiple_of((lo + j * FFN_BLOCK) * PACK_SUB, PACK_SUB),
                                                 FFN_BLOCK * PACK_SUB), :], sem))
                return 0

            lax.fori_loop(0, n_full, full, 0)
            off = lo + n_full * FFN_BLOCK
            rem = n - n_full * FFN_BLOCK
            bit = FFN_BLOCK // 2
            while bit >= 1:
                take = rem & bit

                @pl.when(take != 0)
                def _(off=off, bit=bit):
                    act(pltpu.make_async_copy(
                        zeros_ref.at[pl.ds(0, bit * PACK_SUB), :],
                        rows_ref.at[pl.ds(pl.multiple_of(off * PACK_SUB, PACK_SUB), bit * PACK_SUB), :], sem))

                off = off + take
                bit //= 2
            return 0

        lax.fori_loop(0, pad_lo_ref.shape[0], per_entry, 0)

    each_copy(lambda cp: cp.start())
    each_copy(lambda cp: cp.wait())


def _dispatch_kernel(pad_lo_ref, pad_n_ref, dest_ref, hp_ref, rows_ref, zeros_ref, sem, zsem):
    tm = hp_ref.shape[0] // PACK_SUB

    @pl.when(pl.program_id(0) == 0)
    def _():
        zeros_ref[...] = jnp.zeros_like(zeros_ref)
        _zero_fill(pad_lo_ref, pad_n_ref, zeros_ref, rows_ref, zsem)

    def body(tt, _):
        for k in range(TOP_K):
            pltpu.make_async_copy(_row_slab(hp_ref, tt), _row_slab(rows_ref, dest_ref[0, 0, k * tm + tt]),
                                  sem).start(priority=k % 2)
        return 0

    lax.fori_loop(0, tm, body, 0)
    all_rows = rows_ref.at[pl.ds(0, TOP_K * tm * PACK_SUB), :]
    pltpu.make_async_copy(all_rows, all_rows, sem).wait()


def _dispatch(hp, dest_tiles, pad_lo, pad_n, n_rows):
    n_tile, _, per_tile = dest_tiles.shape
    tm = per_tile // TOP_K
    return pl.pallas_call(
        _dispatch_kernel,
        grid_spec=pltpu.PrefetchScalarGridSpec(
            num_scalar_prefetch=2,
            grid=(n_tile,),
            in_specs=[pl.BlockSpec((1, 1, per_tile), lambda i, lo, n: (i, 0, 0), memory_space=pltpu.SMEM),
                      pl.BlockSpec((tm * PACK_SUB, LANE), lambda i, lo, n: (i, 0))],
            out_specs=pl.BlockSpec(memory_space=pl.ANY),
            scratch_shapes=[pltpu.VMEM((FFN_BLOCK * PACK_SUB, LANE), U32),
                            pltpu.SemaphoreType.DMA, pltpu.SemaphoreType.DMA],
        ),
        out_shape=jax.ShapeDtypeStruct((n_rows * PACK_SUB, LANE), U32),
        compiler_params=_cparams(("arbitrary",)),
        name="dispatch",
    )(pad_lo, pad_n, dest_tiles, hp)


def _ffn_kernel(blk_e_ref, n_used_ref, rows_ref, wg_ref, wu_ref, wd_ref, y_ref, wgu_bf_ref, wd_bf_ref):
    b = pl.program_id(0)
    live = b < n_used_ref[0]
    d_e = wd_ref.shape[1]

    @pl.when(jnp.logical_or(b == 0, blk_e_ref[b] != blk_e_ref[jnp.maximum(b - 1, 0)]))
    def _():
        wgu_bf_ref[:, :d_e] = wg_ref[0].astype(BF16)
        wgu_bf_ref[:, d_e:] = wu_ref[0].astype(BF16)
        wd_bf_ref[...] = wd_ref[0].astype(BF16)

    @pl.when(live)
    def _():
        x = _unpack_rows(rows_ref, FFN_BLOCK).astype(BF16)
        au = jnp.dot(x, wgu_bf_ref[...], preferred_element_type=F32)
        hid = (_silu(au[:, :d_e]) * au[:, d_e:]).astype(BF16)
        _pack_rows(y_ref, jnp.dot(hid, wd_bf_ref[...], preferred_element_type=F32))

    @pl.when(jnp.logical_not(live))
    def _():
        y_ref[...] = jnp.zeros_like(y_ref)


def _ffn(rows, blk_e, n_used, wg, wu, wd):
    n_rows = rows.shape[0] // PACK_SUB
    n_e, d, d_e = wg.shape
    n_blk = n_rows // FFN_BLOCK
    blk = lambda b, e, n: (b, 0)
    exp = lambda b, e, n: (e[b], 0, 0)
    return pl.pallas_call(
        _ffn_kernel,
        grid_spec=pltpu.PrefetchScalarGridSpec(
            num_scalar_prefetch=2,
            grid=(n_blk,),
            in_specs=[pl.BlockSpec((FFN_BLOCK * PACK_SUB, LANE), blk),
                      pl.BlockSpec((1, d, d_e), exp), pl.BlockSpec((1, d, d_e), exp),
                      pl.BlockSpec((1, d_e, d), exp)],
            out_specs=pl.BlockSpec((FFN_BLOCK * PACK_SUB, LANE), blk),
            scratch_shapes=[pltpu.VMEM((d, 2 * d_e), BF16), pltpu.VMEM((d_e, d), BF16)],
        ),
        out_shape=jax.ShapeDtypeStruct((n_rows * PACK_SUB, LANE), U32),
        compiler_params=_cparams(("arbitrary",)),
        name="ffn",
    )(blk_e, n_used, rows, wg, wu, wd)


def _final_kernel(dest_ref, dnext_ref, x1_ref, h_ref, gate_ref, g2_ref, wsg_ref, wsu_ref, wsd_ref, lng_ref, lnb_ref,
                  y_ref, o_ref, ybuf_ref, sems, *, alpha):
    i = pl.program_id(0)
    tm = x1_ref.shape[0]
    per_tile = TOP_K * tm

    def gather(d_ref, slot):
        def body(tt, _):
            for k in range(TOP_K):
                pltpu.make_async_copy(_row_slab(y_ref, d_ref[0, 0, k * tm + tt]),
                                      _row_slab(ybuf_ref, slot * per_tile + k * tm + tt),
                                      sems.at[slot]).start(priority=k % 2)
            return 0

        lax.fori_loop(0, tm, body, 0)

    @pl.when(i == 0)
    def _():
        gather(dest_ref, 0)

    @pl.when(i + 1 < pl.num_programs(0))
    def _():
        gather(dnext_ref, (i + 1) % 2)

    h = h_ref[...]
    a = jnp.dot(h, wsg_ref[...], preferred_element_type=F32)
    u = jnp.dot(h, wsu_ref[...], preferred_element_type=F32)
    shared = jnp.dot((_silu(a) * u).astype(BF16), wsd_ref[...], preferred_element_type=F32)

    slot = i % 2
    half = ybuf_ref.at[pl.ds(pl.multiple_of(slot * per_tile * PACK_SUB, per_tile * PACK_SUB), per_tile * PACK_SUB), :]
    pltpu.make_async_copy(half, half, sems.at[slot]).wait()
    first = slot * per_tile
    routed = gate_ref[:, 0:1] * _unpack_rows(ybuf_ref, tm, first=first)
    for k in range(1, TOP_K):
        routed = routed + gate_ref[:, k:k + 1] * _unpack_rows(ybuf_ref, tm, first=first + k * tm)

    v = alpha * x1_ref[...] + g2_ref[0] * (shared + routed)
    o_ref[...] = _standardize(v) * lng_ref[...] + lnb_ref[...]


def _final(x1, hhi, y_rows, dest_tiles, gate_t, g2, wsg, wsu, wsd, lng, lnb, seq, alpha):
    t, d = x1.shape
    d_sh = wsg.shape[1]
    n_tile, _, per_tile = dest_tiles.shape
    tm = per_tile // TOP_K
    tpb = seq // tm
    bsz = t // seq
    row = lambda i: (i, 0)
    return pl.pallas_call(
        functools.partial(_final_kernel, alpha=alpha),
        grid=(n_tile,),
        in_specs=[pl.BlockSpec((1, 1, per_tile), lambda i: (i, 0, 0), memory_space=pltpu.SMEM),
                  pl.BlockSpec((1, 1, per_tile), lambda i: (jnp.minimum(i + 1, n_tile - 1), 0, 0),
                               memory_space=pltpu.SMEM),
                  pl.BlockSpec((tm, d), row), pl.BlockSpec((tm, d), row), pl.BlockSpec((tm, TOP_K), row),
                  pl.BlockSpec((1, 1, d), lambda i: (i // tpb, 0, 0)),
                  _const_spec((d, d_sh)), _const_spec((d, d_sh)), _const_spec((d_sh, d)),
                  _const_spec((1, d)), _const_spec((1, d)),
                  pl.BlockSpec(memory_space=pl.ANY)],
        out_specs=pl.BlockSpec((tm, d), row),
        out_shape=jax.ShapeDtypeStruct((t, d), F32),
        scratch_shapes=[pltpu.VMEM((2 * TOP_K * tm * PACK_SUB, LANE), U32), pltpu.SemaphoreType.DMA((2,))],
        compiler_params=_cparams(("arbitrary",)),
        name="final",
    )(dest_tiles, dest_tiles, x1, hhi, gate_t, g2.reshape(bsz, 1, d), wsg, wsu, wsd, lng, lnb, y_rows)


def _split_hi_lo(w):
    hi = w.astype(BF16)
    return hi, (w - hi.astype(F32)).astype(BF16)


def kernel(x, c, w_ada, b_ada, w_in, b_in, s5_lambda_re, s5_lambda_im, s5_log_dt, s5_b_re, s5_b_im, s5_c_re, s5_c_im, s5_d, w_s5_glu, b_s5_glu, w_fourier, b_fourier, w_out, b_out, ln1_g, ln1_b, w_router, router_bias, w_exp_gate, w_exp_up, w_exp_down, w_sh_gate, w_sh_up, w_sh_down, ln2_g, ln2_b):
    bsz, seq, d = x.shape
    depth = w_ada.shape[0]
    alpha = (2 * depth) ** 0.25
    t = bsz * seq
    d_s5 = s5_d.shape[1]
    d_f = w_fourier.shape[1]
    fw = d_f // FOURIER_GROUPS
    n_e = w_router.shape[2]
    n_chunk = seq // LANE
    fmat, cmat = _dft_tables(seq, fw)
    row = lambda v: v.astype(F32).reshape(1, -1)

    x2 = x.reshape(t, d)
    for l in range(depth):
        mod = _adaln(c, w_ada[l], b_ada[l])
        sh1, sc1, g1, sh2, sc2, g2 = jnp.split(mod, 6, axis=-1)

        wi = w_in[l]
        bi = b_in[l].astype(F32)
        ws5t = wi[:, :d_s5].T.astype(BF16)
        bs5 = jnp.broadcast_to(bi[:d_s5].reshape(d_s5, 1), (d_s5, LANE))
        us5, z, gates = _proj(x2, sc1, sh1, ws5t, bs5,
                              wi[:, d_s5:d_s5 + d_f].astype(BF16), row(bi[d_s5:d_s5 + d_f]), cmat,
                              wi[:, d_s5 + d_f:].astype(BF16), row(bi[d_s5 + d_f:]), seq)

        tables = _s5_tables(s5_lambda_re[l], s5_lambda_im[l], s5_log_dt[l], s5_b_re[l], s5_b_im[l],
                            s5_c_re[l], s5_c_im[l], s5_d[l])
        ys5 = _s5(us5, tables, n_chunk)
        brs = _glu(ys5, w_s5_glu[l].astype(BF16), row(b_s5_glu[l]))
        yf = _seqdft(fmat, z, seq, fw)

        x1, hhi, hlo, hp = _merge(x2, brs, yf, gates, g1, sc2, sh2,
                              w_fourier[l].astype(BF16), row(b_fourier[l]),
                              w_out[l].astype(BF16), row(b_out[l]), row(ln1_g[l]), row(ln1_b[l]), seq, alpha)

        wrt_hi, wrt_lo = _split_hi_lo(w_router[l].astype(F32).T)
        eidx, gate, rank, cnt = _router(hhi, hlo, wrt_hi, wrt_lo, router_bias[l])

        counts = cnt[:, 0]
        padded = ((counts + FFN_BLOCK - 1) // FFN_BLOCK) * FFN_BLOCK
        pend = jnp.cumsum(padded)
        pstart = (pend - padded).astype(I32)
        dest = _dest(eidx, rank, pstart)
        n_blk = (t * TOP_K + n_e * (FFN_BLOCK - 1) + FFN_BLOCK - 1) // FFN_BLOCK
        n_rows = n_blk * FFN_BLOCK
        blk_start = jnp.arange(n_blk, dtype=I32) * FFN_BLOCK
        blk_e = jnp.minimum(jnp.sum((pend[None, :] <= blk_start[:, None]).astype(I32), axis=1), n_e - 1)
        n_used = (pend[-1:] // FFN_BLOCK).astype(I32)
        pad_lo = jnp.concatenate([pstart + counts, pend[-1:]]).astype(I32)
        pad_n = jnp.concatenate([padded - counts, n_rows - pend[-1:]]).astype(I32)

        def tiles(tm):
            return dest.reshape(TOP_K, t // tm, tm).transpose(1, 0, 2).reshape(t // tm, 1, TOP_K * tm)

        rows = _dispatch(hp, tiles(DISPATCH_TILE), pad_lo, pad_n, n_rows)
        y_rows = _ffn(rows, blk_e, n_used, w_exp_gate[l], w_exp_up[l], w_exp_down[l])
        x2 = _final(x1, hhi, y_rows, tiles(COMBINE_TILE), gate.T, g2,
                    w_sh_gate[l].astype(BF16), w_sh_up[l].astype(BF16), w_sh_down[l].astype(BF16),
                    row(ln2_g[l]), row(ln2_b[l]), seq, alpha)
    return x2.reshape(bsz, seq, d)
```

```python
import functools
import math

import jax
import jax.numpy as jnp
from jax import lax
from jax.experimental import pallas as pl
from jax.experimental.pallas import tpu as pltpu

F32 = jnp.float32
BF16 = jnp.bfloat16
I32 = jnp.int32

TOP_K = 8
N_EXPERT_GROUPS = 8
TOPK_GROUPS = 4
ROUTED_SCALE = 2.5
FOURIER_GROUPS = 4
LN_EPS = 1e-5

LANE = 128
VMEM_LIMIT = 56 * 1024 * 1024

HIGHEST = lax.Precision.HIGHEST
NEG_INF = float("-inf")

FFN_BLOCK = 512
DISPATCH_TILE = 512
COMBINE_TILE = 256
ISSUE_UNROLL = 4


def _cparams(sem):
    return pltpu.CompilerParams(dimension_semantics=sem, vmem_limit_bytes=VMEM_LIMIT)


def _const_spec(shape):
    nd = len(shape)
    return pl.BlockSpec(shape, lambda *_: (0,) * nd, pipeline_mode=pl.Buffered(1))


def _standardize(x):
    mu = jnp.mean(x, axis=-1, keepdims=True)
    xc = x - mu
    var = jnp.mean(xc * xc, axis=-1, keepdims=True)
    return xc * lax.rsqrt(var + LN_EPS)


def _silu(x):
    return x * jax.nn.sigmoid(x)


def _gelu_tanh(x):
    return 0.5 * x * (1.0 + jnp.tanh(math.sqrt(2.0 / math.pi) * (x + 0.044715 * (x * x * x))))


def _adaln_kernel(c_ref, w_ref, b_ref, o_ref):
    a = _silu(c_ref[...])
    o_ref[...] = jnp.dot(a, w_ref[...], precision=HIGHEST, preferred_element_type=F32) + b_ref[...]


def _adaln(c, w, b):
    bsz, d = c.shape
    n = w.shape[1]
    tn = 512
    return pl.pallas_call(
        _adaln_kernel,
        grid=(n // tn,),
        in_specs=[pl.BlockSpec((bsz, d), lambda j: (0, 0)),
                  pl.BlockSpec((d, tn), lambda j: (0, j)),
                  pl.BlockSpec((1, tn), lambda j: (0, j))],
        out_specs=pl.BlockSpec((bsz, tn), lambda j: (0, j)),
        out_shape=jax.ShapeDtypeStruct((bsz, n), F32),
        compiler_params=_cparams(("parallel",)),
        name="adaln",
    )(c, w, b.reshape(1, n))


def _proj_kernel(x_ref, sc_ref, sh_ref, ws5t_ref, bs5_ref, wf_ref, bf_ref, cs_ref, wg_ref, bg_ref,
                 us5_ref, z_ref, gates_ref):
    tm = x_ref.shape[0]
    u = (_standardize(x_ref[...]) * (1.0 + sc_ref[0]) + sh_ref[0]).astype(BF16)
    p = lax.dot_general(ws5t_ref[...], u, (((1,), (1,)), ((), ())), preferred_element_type=F32)
    p = p + bs5_ref[:, 0:1]
    d_s5, n_j, _ = us5_ref.shape
    us5_2d = us5_ref.reshape(d_s5 * n_j, LANE)
    for j in range(n_j):
        us5_2d[pl.ds(j, d_s5, stride=n_j), :] = p[:, j * LANE:(j + 1) * LANE]
    uf = (jnp.dot(u, wf_ref[...], preferred_element_type=F32) + bf_ref[...]).astype(BF16)
    d_f = uf.shape[1]
    fw = d_f // FOURIER_GROUPS
    for q in range(FOURIER_GROUPS):
        zq = jnp.dot(uf[:, q * fw:(q + 1) * fw], cs_ref[...], preferred_element_type=F32)
        z_ref[:, q * fw:(q + 1) * fw] = zq[:, :fw].astype(BF16)
        z_ref[:, d_f + q * fw:d_f + (q + 1) * fw] = zq[:, fw:].astype(BF16)
    n_g = wg_ref.shape[1]
    half = n_g // 2
    for q in range(2):
        gp = jnp.dot(u, wg_ref[:, q * half:(q + 1) * half], preferred_element_type=F32)
        gp = gp + bg_ref[:, q * half:(q + 1) * half]
        gates_ref[:, q * half:(q + 1) * half] = jax.nn.sigmoid(gp).astype(BF16)


def _proj(x2, sc, sh, ws5t, bs5, wf, bf, cs, wg, bg, seq):
    t, d = x2.shape
    d_s5 = ws5t.shape[0]
    d_f = wf.shape[1]
    n_g = wg.shape[1]
    tm = 1024
    tpb = seq // tm
    bsz = t // seq
    return pl.pallas_call(
        _proj_kernel,
        grid=(t // tm,),
        in_specs=[pl.BlockSpec((tm, d), lambda i: (i, 0)),
                  pl.BlockSpec((1, 1, d), lambda i: (i // tpb, 0, 0)),
                  pl.BlockSpec((1, 1, d), lambda i: (i // tpb, 0, 0)),
                  _const_spec((d_s5, d)), _const_spec((d_s5, LANE)),
                  _const_spec((d, d_f)), _const_spec((1, d_f)),
                  _const_spec(cs.shape),
                  _const_spec((d, n_g)), _const_spec((1, n_g))],
        out_specs=[pl.BlockSpec((d_s5, tm // LANE, LANE), lambda i: (0, i, 0)),
                   pl.BlockSpec((tm, 2 * d_f), lambda i: (i, 0)),
                   pl.BlockSpec((tm, n_g), lambda i: (i, 0))],
        out_shape=[jax.ShapeDtypeStruct((d_s5, t // LANE, LANE), F32),
                   jax.ShapeDtypeStruct((t, 2 * d_f), BF16),
                   jax.ShapeDtypeStruct((t, n_g), BF16)],
        compiler_params=_cparams(("parallel",)),
        name="proj",
    )(x2, sc.reshape(bsz, 1, d), sh.reshape(bsz, 1, d), ws5t, bs5, wf, bf, cs, wg, bg)


def _s5_tables(lam_re, lam_im, log_dt, b_re, b_im, c_re, c_im, d_skip):
    L = LANE
    hp = HIGHEST
    lr, li = lam_re.astype(F32), lam_im.astype(F32)
    dt = jnp.exp(log_dt.astype(F32))[:, :, None]
    mag = jnp.exp(lr * dt)
    ang = li * dt
    ab_re, ab_im = mag * jnp.cos(ang), mag * jnp.sin(ang)
    den = lr * lr + li * li
    nr = ab_re - 1.0
    coef_re = (nr * lr + ab_im * li) / den
    coef_im = (ab_im * lr - nr * li) / den
    br, bi = b_re.astype(F32), b_im.astype(F32)
    bb_re = coef_re[..., None] * br - coef_im[..., None] * bi
    bb_im = coef_re[..., None] * bi + coef_im[..., None] * br
    cr, ci = c_re.astype(F32), c_im.astype(F32)
    n_g, n_p, n_h = br.shape[1], br.shape[2], br.shape[3]

    k = jnp.arange(L + 1, dtype=F32)[None, None, :, None]
    pmag = jnp.exp(k * (lr * dt)[:, :, None, :])
    pang = k * (li * dt)[:, :, None, :]
    pw_re, pw_im = pmag * jnp.cos(pang), pmag * jnp.sin(pang)

    m_re = cr[:, :, :, None, :] * jnp.swapaxes(bb_re, 2, 3)[:, :, None, :, :] \
        - ci[:, :, :, None, :] * jnp.swapaxes(bb_im, 2, 3)[:, :, None, :, :]
    m_im = cr[:, :, :, None, :] * jnp.swapaxes(bb_im, 2, 3)[:, :, None, :, :] \
        + ci[:, :, :, None, :] * jnp.swapaxes(bb_re, 2, 3)[:, :, None, :, :]
    kap = jnp.einsum("dgohp,dgkp->dgohk", m_re, pw_re, precision=hp) \
        - jnp.einsum("dgohp,dgkp->dgohk", m_im, pw_im, precision=hp)
    kb = kap[1]
    kpos = kap[0, ..., :L].at[..., 0].add(kb[..., 0]).reshape(n_g, n_h * n_h, L)
    kneg = jnp.concatenate([jnp.zeros_like(kb[..., 0:1]), kb[..., L - 1:0:-1]], axis=-1)
    kneg = kneg.reshape(n_g, n_h * n_h, L)

    pf_re, pf_im = pw_re[0, :, L - 1::-1][:, :L], pw_im[0, :, L - 1::-1][:, :L]
    pb_re, pb_im = pw_re[1, :, :L], pw_im[1, :, :L]

    def cmul_hs(p_re, p_im, q_re, q_im):
        a = p_re[:, None, :, :]
        b = p_im[:, None, :, :]
        c = jnp.swapaxes(q_re, 1, 2)[:, :, None, :]
        e = jnp.swapaxes(q_im, 1, 2)[:, :, None, :]
        return a * c - b * e, a * e + b * c

    wsf_re, wsf_im = cmul_hs(pf_re, pf_im, bb_re[0], bb_im[0])
    wsb_re, wsb_im = cmul_hs(pb_re, pb_im, bb_re[1], bb_im[1])
    ws = jnp.concatenate([wsf_re, wsf_im, wsb_re, wsb_im], axis=-1).reshape(n_g, n_h * L, 4 * n_p)

    qf_re, qf_im = pw_re[0, :, 1:L + 1], pw_im[0, :, 1:L + 1]
    qb_re, qb_im = pw_re[1, :, L:0:-1], pw_im[1, :, L:0:-1]

    def cmul_ot(c_r, c_i, q_r, q_i):
        a = jnp.swapaxes(c_r, 1, 2)[:, :, :, None]
        b = jnp.swapaxes(c_i, 1, 2)[:, :, :, None]
        c = jnp.swapaxes(q_r, 1, 2)[:, :, None, :]
        e = jnp.swapaxes(q_i, 1, 2)[:, :, None, :]
        return a * c - b * e, -(a * e + b * c)

    wof_re, wof_mi = cmul_ot(cr[0], ci[0], qf_re, qf_im)
    wob_re, wob_mi = cmul_ot(cr[1], ci[1], qb_re, qb_im)
    wo = jnp.concatenate([wof_re, wof_mi, wob_re, wob_mi], axis=1)
    wo = jnp.swapaxes(wo.reshape(n_g, 4 * n_p, n_h // 2, 2 * L), 1, 2)

    al_re, al_im = pw_re[:, :, L], pw_im[:, :, L]
    alx = jnp.concatenate([al_re[0], al_re[0], al_re[1], al_re[1]], axis=-1)
    aly = jnp.concatenate([-al_im[0], al_im[0], -al_im[1], al_im[1]], axis=-1)
    al = jnp.stack([alx, aly], axis=1)

    dsk = jnp.broadcast_to(d_skip.astype(F32).reshape(n_g, n_h, 1), (n_g, n_h, L))
    return kpos, kneg, ws.astype(BF16), wo.astype(BF16), al, dsk


def _s5_kernel(a_ref, kpos_ref, kneg_ref, ws_ref, wo_ref, al_ref, d_ref, y_ref,
               abf_ref, tp_ref, sf_ref, sb_ref, xf_ref, xb_ref, *, n_chunk):
    n_h, r, L = a_ref.shape
    bsz = r // n_chunk
    half = 2 * (ws_ref.shape[2] // 4)

    for h in range(n_h):
        abf_ref[:, h * L:(h + 1) * L] = a_ref[h].astype(BF16)
    abf = abf_ref[...]

    s_all = jnp.dot(abf, ws_ref[0], preferred_element_type=F32)
    sf_ref[...] = s_all[:, :half]
    sb_ref[...] = s_all[:, half:]

    alx = al_ref[0, 0:1, :]
    aly = al_ref[0, 1:2, :]

    def carry(e, lo):
        ex = e * alx[:, lo:lo + half]
        ey = pltpu.roll(e, half // 2, 1) * aly[:, lo:lo + half]
        return ex + ey

    e = jnp.zeros((bsz, half), F32)
    for c in range(n_chunk):
        xf_ref[pl.ds(c, bsz, stride=n_chunk), :] = e
        e = carry(e, 0) + sf_ref[pl.ds(c, bsz, stride=n_chunk), :]
    e = jnp.zeros((bsz, half), F32)
    for c in range(n_chunk - 1, -1, -1):
        xb_ref[pl.ds(c, bsz, stride=n_chunk), :] = e
        e = carry(e, half) + sb_ref[pl.ds(c, bsz, stride=n_chunk), :]
    xin = jnp.concatenate([xf_ref[...], xb_ref[...]], axis=1).astype(BF16)

    s_idx = lax.broadcasted_iota(I32, (L, L), 0)
    j_idx = lax.broadcasted_iota(I32, (L, L), 1)
    fwd_part = j_idx + s_idx < L

    def pair(op, _):
        for oo in range(2):
            o = 2 * op + oo
            for h in range(n_h):
                row = o * n_h + h
                kp = jnp.broadcast_to(kpos_ref[0, pl.ds(row, 1), :], (L, L))
                kn = jnp.broadcast_to(kneg_ref[0, pl.ds(row, 1), :], (L, L))
                tile = pltpu.roll(jnp.where(fwd_part, kp, kn), 0, 1, stride=1, stride_axis=0)
                tp_ref[h * L:(h + 1) * L, oo * L:(oo + 1) * L] = tile.astype(BF16)
        yp = jnp.dot(abf, tp_ref[...], preferred_element_type=F32)
        yp = yp + jnp.dot(xin, wo_ref[0, op], preferred_element_type=F32)
        for oo in range(2):
            o = 2 * op + oo
            y_ref[o] = yp[:, oo * L:(oo + 1) * L] + a_ref[o] * d_ref[0, pl.ds(o, 1), :]
        return 0

    lax.fori_loop(0, n_h // 2, pair, 0)


def _s5(us5, tables, n_chunk):
    kpos, kneg, ws, wo, al, dsk = tables
    d_s5, r, L = us5.shape
    n_g = kpos.shape[0]
    n_h = d_s5 // n_g
    n_st = ws.shape[2]
    g3 = lambda g: (g, 0, 0)
    return pl.pallas_call(
        functools.partial(_s5_kernel, n_chunk=n_chunk),
        grid=(n_g,),
        in_specs=[pl.BlockSpec((n_h, r, L), g3),
                  pl.BlockSpec((1, n_h * n_h, L), g3), pl.BlockSpec((1, n_h * n_h, L), g3),
                  pl.BlockSpec((1, n_h * L, n_st), g3),
                  pl.BlockSpec((1, n_h // 2, n_st, 2 * L), lambda g: (g, 0, 0, 0)),
                  pl.BlockSpec((1, 2, n_st), g3), pl.BlockSpec((1, n_h, L), g3)],
        out_specs=pl.BlockSpec((n_h, r, L), g3),
        out_shape=jax.ShapeDtypeStruct((d_s5, r, L), F32),
        scratch_shapes=[pltpu.VMEM((r, n_h * L), BF16), pltpu.VMEM((n_h * L, 2 * L), BF16),
                        pltpu.VMEM((r, n_st // 2), F32), pltpu.VMEM((r, n_st // 2), F32),
                        pltpu.VMEM((r, n_st // 2), F32), pltpu.VMEM((r, n_st // 2), F32)],
        compiler_params=_cparams(("parallel",)),
        name="s5",
    )(us5, kpos, kneg, ws, wo, al, dsk)


DFT_SPLIT = 64


def _dft_tables(seq, fw):
    def angles(mult, n_rows, n):
        s = jnp.arange(n, dtype=I32)[None, :]
        q = jnp.arange(n_rows, dtype=I32)[:, None]
        return (2.0 * math.pi / n) * ((mult * q * s) % n).astype(F32)

    ang_a = angles(DFT_SPLIT, seq // DFT_SPLIT, seq)
    ang_b = angles(1, DFT_SPLIT, seq)
    seq_tabs = (jnp.cos(ang_a), jnp.sin(ang_a), jnp.cos(ang_b), jnp.sin(ang_b))
    ang_c = angles(1, fw, fw)
    return seq_tabs, jnp.concatenate([jnp.cos(ang_c), jnp.sin(ang_c)], axis=1).astype(BF16)


def _seqdft_kernel(ca_ref, sa_ref, cb_ref, sb_ref, z_ref, o_ref, f_ref, *, scale):
    seq = z_ref.shape[0]
    d_f = o_ref.shape[1]
    tk = o_ref.shape[0]

    @pl.when(pl.program_id(1) == 0)
    def _():
        a0 = pl.program_id(0) * (tk // DFT_SPLIT)
        cb, sb = cb_ref[...], sb_ref[...]
        for j in range(tk // DFT_SPLIT):
            ca = ca_ref[pl.ds(a0 + j, 1), :]
            sa = sa_ref[pl.ds(a0 + j, 1), :]
            rows = slice(j * DFT_SPLIT, (j + 1) * DFT_SPLIT)
            f_ref[rows, :seq] = (ca * cb - sa * sb).astype(BF16)
            f_ref[rows, seq:] = (-(sa * cb + ca * sb)).astype(BF16)

    acc = jnp.dot(f_ref[:, :seq], z_ref[:, :d_f], preferred_element_type=F32)
    acc = acc + jnp.dot(f_ref[:, seq:], z_ref[:, d_f:], preferred_element_type=F32)
    o_ref[...] = (acc * scale).astype(o_ref.dtype)


def _seqdft(seq_tabs, z, seq, fw):
    t, two_df = z.shape
    d_f = two_df // 2
    bsz = t // seq
    tk = 512
    nk = seq // tk
    scale = 1.0 / math.sqrt(seq * fw)
    return pl.pallas_call(
        functools.partial(_seqdft_kernel, scale=scale),
        grid=(nk, bsz),
        in_specs=[_const_spec(tab.shape) for tab in seq_tabs]
        + [pl.BlockSpec((seq, two_df), lambda k, b: (b, 0))],
        out_specs=pl.BlockSpec((tk, d_f), lambda k, b: (b * nk + k, 0)),
        out_shape=jax.ShapeDtypeStruct((t, d_f), BF16),
        scratch_shapes=[pltpu.VMEM((tk, 2 * seq), BF16)],
        compiler_params=_cparams(("parallel", "arbitrary")),
        name="seqdft",
    )(*seq_tabs, z)


def _glu_kernel(y_ref, w_ref, b_ref, o_ref, zt_ref):
    d_s5, n_j, _ = y_ref.shape
    d = o_ref.shape[1]
    y2 = y_ref.reshape(d_s5 * n_j, LANE)
    for j in range(n_j):
        zt_ref[j * LANE:(j + 1) * LANE, :] = _gelu_tanh(y2[pl.ds(j, d_s5, stride=n_j), :]).T.astype(BF16)
    zt = zt_ref[...]
    a = jnp.dot(zt, w_ref[:, :d], preferred_element_type=F32) + b_ref[:, :d]
    g = jnp.dot(zt, w_ref[:, d:], preferred_element_type=F32) + b_ref[:, d:]
    o_ref[...] = (a * jax.nn.sigmoid(g)).astype(BF16)


def _glu(ys5, w, b):
    d_s5, r, L = ys5.shape
    t = r * L
    n = w.shape[1]
    tm = 1024
    return pl.pallas_call(
        _glu_kernel,
        grid=(t // tm,),
        in_specs=[pl.BlockSpec((d_s5, tm // L, L), lambda i: (0, i, 0)),
                  _const_spec((d_s5, n)), _const_spec((1, n))],
        out_specs=pl.BlockSpec((tm, n // 2), lambda i: (i, 0)),
        out_shape=jax.ShapeDtypeStruct((t, n // 2), BF16),
        scratch_shapes=[pltpu.VMEM((tm, d_s5), BF16)],
        compiler_params=_cparams(("parallel",)),
        name="glu",
    )(ys5, w, b)


U32 = jnp.uint32
PACK_SUB = 4


def _row_slab(ref, r):
    return ref.at[pl.ds(pl.multiple_of(r * PACK_SUB, PACK_SUB), PACK_SUB), :]


def _pack_rows(ref, v, first=0):
    rows, d = v.shape
    half = d // 2
    bits = lax.bitcast_convert_type(v.astype(BF16).astype(F32), U32)
    for c in range(PACK_SUB):
        lo = bits[:, c * LANE:(c + 1) * LANE] >> 16
        hi = bits[:, half + c * LANE:half + (c + 1) * LANE] & jnp.uint32(0xFFFF0000)
        ref[pl.ds(first * PACK_SUB + c, rows, stride=PACK_SUB), :] = hi | lo


def _unpack_rows(ref, rows, first=0):
    los, his = [], []
    for c in range(PACK_SUB):
        w = ref[pl.ds(first * PACK_SUB + c, rows, stride=PACK_SUB), :]
        los.append(lax.bitcast_convert_type(w << 16, F32))
        his.append(lax.bitcast_convert_type(w & jnp.uint32(0xFFFF0000), F32))
    return jnp.concatenate(los + his, axis=1)


def _merge_kernel(x_ref, brs_ref, yf_ref, gates_ref, g1_ref, sc_ref, sh_ref, wfo_ref, bfo_ref,
                  wo_ref, bo_ref, lng_ref, lnb_ref, x1_ref, hhi_ref, hlo_ref, hp_ref, *, alpha):
    d = x_ref.shape[1]
    br_f = jnp.dot(yf_ref[...], wfo_ref[...], preferred_element_type=F32) + bfo_ref[...]
    merged = gates_ref[:, :d].astype(F32) * brs_ref[...].astype(F32) + gates_ref[:, d:].astype(F32) * br_f
    mix = jnp.dot(merged.astype(BF16), wo_ref[...], preferred_element_type=F32) + bo_ref[...]
    v = alpha * x_ref[...] + g1_ref[0] * mix
    x1 = _standardize(v) * lng_ref[...] + lnb_ref[...]
    x1_ref[...] = x1
    h = _standardize(x1) * (1.0 + sc_ref[0]) + sh_ref[0]
    hhi = h.astype(BF16)
    hhi_ref[...] = hhi
    hlo_ref[...] = (h - hhi.astype(F32)).astype(BF16)
    _pack_rows(hp_ref, h)


def _merge(x2, brs, yf, gates, g1, sc2, sh2, wfo, bfo, wo, bo, lng, lnb, seq, alpha):
    t, d = x2.shape
    d_f = yf.shape[1]
    tm = 512
    tpb = seq // tm
    bsz = t // seq
    row = lambda i: (i, 0)
    bat = lambda i: (i // tpb, 0, 0)
    return pl.pallas_call(
        functools.partial(_merge_kernel, alpha=alpha),
        grid=(t // tm,),
        in_specs=[pl.BlockSpec((tm, d), row), pl.BlockSpec((tm, d), row), pl.BlockSpec((tm, d_f), row),
                  pl.BlockSpec((tm, 2 * d), row),
                  pl.BlockSpec((1, 1, d), bat), pl.BlockSpec((1, 1, d), bat), pl.BlockSpec((1, 1, d), bat),
                  _const_spec((d_f, d)), _const_spec((1, d)), _const_spec((d, d)), _const_spec((1, d)),
                  _const_spec((1, d)), _const_spec((1, d))],
        out_specs=[pl.BlockSpec((tm, d), row), pl.BlockSpec((tm, d), row), pl.BlockSpec((tm, d), row),
                   pl.BlockSpec((tm * PACK_SUB, LANE), row)],
        out_shape=[jax.ShapeDtypeStruct((t, d), F32), jax.ShapeDtypeStruct((t, d), BF16),
                   jax.ShapeDtypeStruct((t, d), BF16), jax.ShapeDtypeStruct((t * PACK_SUB, LANE), U32)],
        compiler_params=_cparams(("parallel",)),
        name="merge",
    )(x2, brs, yf, gates, g1.reshape(bsz, 1, d), sc2.reshape(bsz, 1, d), sh2.reshape(bsz, 1, d),
      wfo, bfo, wo, bo, lng, lnb)


def _router_kernel(hhi_ref, hlo_ref, whi_ref, wlo_ref, bias_ref, tri_ref,
                   eidx_ref, gate_ref, rank_ref, cnt_ref, base_ref):
    n_e = whi_ref.shape[0]
    tm = hhi_ref.shape[0]
    gsz = n_e // N_EXPERT_GROUPS
    nt = (((1,), (1,)), ((), ()))

    @pl.when(pl.program_id(0) == 0)
    def _():
        base_ref[...] = jnp.zeros_like(base_ref)

    hhi = hhi_ref[...]
    logits = lax.dot_general(whi_ref[...], hhi, nt, preferred_element_type=F32)
    logits = logits + lax.dot_general(wlo_ref[...], hhi, nt, preferred_element_type=F32)
    logits = logits + lax.dot_general(whi_ref[...], hlo_ref[...], nt, preferred_element_type=F32)
    scores = jax.nn.sigmoid(logits)
    sel = scores + bias_ref[:, 0:1]

    g3 = sel.reshape(N_EXPERT_GROUPS, gsz, tm)
    i3 = lax.broadcasted_iota(I32, g3.shape, 1).astype(F32)
    m1 = jnp.max(g3, axis=1, keepdims=True)
    first = jnp.min(jnp.where(g3 == m1, i3, float(gsz)), axis=1, keepdims=True)
    m2 = jnp.max(jnp.where(i3 == first, NEG_INF, g3), axis=1, keepdims=True)
    gs = (m1 + m2).reshape(N_EXPERT_GROUPS, tm)

    gi = lax.broadcasted_iota(I32, gs.shape, 0).astype(F32)
    gsel = jnp.zeros(gs.shape, F32)
    cur = gs
    for _ in range(TOPK_GROUPS):
        m = jnp.max(cur, axis=0, keepdims=True)
        f = jnp.min(jnp.where(cur == m, gi, float(N_EXPERT_GROUPS)), axis=0, keepdims=True)
        pick = gi == f
        gsel = jnp.where(pick, 1.0, gsel)
        cur = jnp.where(pick, NEG_INF, cur)
    gmask = jnp.broadcast_to(gsel.reshape(N_EXPERT_GROUPS, 1, tm), g3.shape).reshape(n_e, tm)
    masked = jnp.where(gmask > 0.5, sel, NEG_INF)

    ri = lax.broadcasted_iota(I32, (n_e, tm), 0).astype(F32)
    picks = []
    gates = []
    multihot = jnp.zeros((n_e, tm), F32)
    for _ in range(TOP_K):
        m = jnp.max(masked, axis=0, keepdims=True)
        f = jnp.min(jnp.where(masked == m, ri, float(n_e)), axis=0, keepdims=True)
        pick = ri == f
        picks.append(f)
        gates.append(jnp.sum(jnp.where(pick, scores, 0.0), axis=0, keepdims=True))
        multihot = jnp.where(pick, 1.0, multihot)
        masked = jnp.where(pick, NEG_INF, masked)
    gsum = gates[0]
    for g in gates[1:]:
        gsum = gsum + g

    rankmat = jnp.dot(multihot.astype(BF16), tri_ref[...], preferred_element_type=F32) + base_ref[:, 0:1]
    for k in range(TOP_K):
        pick = ri == picks[k]
        eidx_ref[k:k + 1, :] = picks[k].astype(I32)
        gate_ref[k:k + 1, :] = gates[k] / gsum * ROUTED_SCALE
        rank_ref[k:k + 1, :] = jnp.sum(jnp.where(pick, rankmat, 0.0), axis=0, keepdims=True).astype(I32)
    base_ref[...] = base_ref[...] + jnp.sum(multihot, axis=1, keepdims=True)
    cnt_ref[...] = base_ref[...].astype(I32)


def _router(hhi, hlo, wrt_hi, wrt_lo, bias):
    t, d = hhi.shape
    n_e = wrt_hi.shape[0]
    tm = 512
    tri = (jnp.arange(tm)[:, None] < jnp.arange(tm)[None, :]).astype(BF16)
    kt = lambda i: (0, i)
    return pl.pallas_call(
        _router_kernel,
        grid=(t // tm,),
        in_specs=[pl.BlockSpec((tm, d), lambda i: (i, 0)), pl.BlockSpec((tm, d), lambda i: (i, 0)),
                  _const_spec((n_e, d)), _const_spec((n_e, d)), _const_spec((n_e, LANE)),
                  _const_spec((tm, tm))],
        out_specs=[pl.BlockSpec((TOP_K, tm), kt), pl.BlockSpec((TOP_K, tm), kt), pl.BlockSpec((TOP_K, tm), kt),
                   pl.BlockSpec((n_e, LANE), lambda i: (0, 0))],
        out_shape=[jax.ShapeDtypeStruct((TOP_K, t), I32), jax.ShapeDtypeStruct((TOP_K, t), F32),
                   jax.ShapeDtypeStruct((TOP_K, t), I32), jax.ShapeDtypeStruct((n_e, LANE), I32)],
        scratch_shapes=[pltpu.VMEM((n_e, LANE), F32)],
        compiler_params=_cparams(("arbitrary",)),
        name="router",
    )(hhi, hlo, wrt_hi, wrt_lo, jnp.broadcast_to(bias.astype(F32).reshape(n_e, 1), (n_e, LANE)), tri)


def _dest_kernel(eidx_ref, rank_ref, pstart_ref, dest_ref):
    n_e = pstart_ref.shape[0]
    tm = eidx_ref.shape[1]
    ri = lax.broadcasted_iota(I32, (n_e, tm), 0)
    ps = pstart_ref[:, 0:1].astype(F32)
    for k in range(TOP_K):
        hit = ri == eidx_ref[k:k + 1, :]
        base = jnp.sum(jnp.where(hit, ps, 0.0), axis=0, keepdims=True)
        dest_ref[k:k + 1, :] = base.astype(I32) + rank_ref[k:k + 1, :]


def _dest(eidx, rank, pstart):
    k, t = eidx.shape
    n_e = pstart.shape[0]
    tm = 2048
    kt = lambda i: (0, i)
    return pl.pallas_call(
        _dest_kernel,
        grid=(t // tm,),
        in_specs=[pl.BlockSpec((k, tm), kt), pl.BlockSpec((k, tm), kt), _const_spec((n_e, LANE))],
        out_specs=pl.BlockSpec((k, tm), kt),
        out_shape=jax.ShapeDtypeStruct((k, t), I32),
        compiler_params=_cparams(("parallel",)),
        name="dest",
    )(eidx, rank, jnp.broadcast_to(pstart.astype(I32).reshape(n_e, 1), (n_e, LANE)))


def _zero_fill(pad_lo_ref, pad_n_ref, zeros_ref, rows_ref, sem):
    def each_copy(act):
        def per_entry(e, _):
            lo = pad_lo_ref[e]
            n = pad_n_ref[e]
            n_full = n // FFN_BLOCK

            def full(j, _):
                act(pltpu.make_async_copy(
                    zeros_ref, rows_ref.at[pl.ds(pl.multiple_of((lo + j * FFN_BLOCK) * PACK_SUB, PACK_SUB),
                                                 FFN_BLOCK * PACK_SUB), :], sem))
                return 0

            lax.fori_loop(0, n_full, full, 0)
            off = lo + n_full * FFN_BLOCK
            rem = n - n_full * FFN_BLOCK
            bit = FFN_BLOCK // 2
            while bit >= 1:
                take = rem & bit

                @pl.when(take != 0)
                def _(off=off, bit=bit):
                    act(pltpu.make_async_copy(
                        zeros_ref.at[pl.ds(0, bit * PACK_SUB), :],
                        rows_ref.at[pl.ds(pl.multiple_of(off * PACK_SUB, PACK_SUB), bit * PACK_SUB), :], sem))

                off = off + take
                bit //= 2
            return 0

        lax.fori_loop(0, pad_lo_ref.shape[0], per_entry, 0)

    each_copy(lambda cp: cp.start())
    each_copy(lambda cp: cp.wait())


def _dispatch_kernel(pad_lo_ref, pad_n_ref, dest_ref, hp_ref, rows_ref, zeros_ref, sem, zsem):
    tm = hp_ref.shape[0] // PACK_SUB

    @pl.when(pl.program_id(0) == 0)
    def _():
        zeros_ref[...] = jnp.zeros_like(zeros_ref)
        _zero_fill(pad_lo_ref, pad_n_ref, zeros_ref, rows_ref, zsem)

    def body(tt, _):
        for k in range(TOP_K):
            pltpu.make_async_copy(_row_slab(hp_ref, tt), _row_slab(rows_ref, dest_ref[0, 0, k * tm + tt]),
                                  sem).start(priority=k % 2)
        return 0

    lax.fori_loop(0, tm, body, 0, unroll=ISSUE_UNROLL)
    all_rows = rows_ref.at[pl.ds(0, TOP_K * tm * PACK_SUB), :]
    pltpu.make_async_copy(all_rows, all_rows, sem).wait()


def _dispatch(hp, dest_tiles, pad_lo, pad_n, n_rows):
    n_tile, _, per_tile = dest_tiles.shape
    tm = per_tile // TOP_K
    return pl.pallas_call(
        _dispatch_kernel,
        grid_spec=pltpu.PrefetchScalarGridSpec(
            num_scalar_prefetch=2,
            grid=(n_tile,),
            in_specs=[pl.BlockSpec((1, 1, per_tile), lambda i, lo, n: (i, 0, 0), memory_space=pltpu.SMEM),
                      pl.BlockSpec((tm * PACK_SUB, LANE), lambda i, lo, n: (i, 0))],
            out_specs=pl.BlockSpec(memory_space=pl.ANY),
            scratch_shapes=[pltpu.VMEM((FFN_BLOCK * PACK_SUB, LANE), U32),
                            pltpu.SemaphoreType.DMA, pltpu.SemaphoreType.DMA],
        ),
        out_shape=jax.ShapeDtypeStruct((n_rows * PACK_SUB, LANE), U32),
        compiler_params=_cparams(("arbitrary",)),
        name="dispatch",
    )(pad_lo, pad_n, dest_tiles, hp)


def _ffn_kernel(blk_e_ref, n_used_ref, rows_ref, wg_ref, wu_ref, wd_ref, y_ref, wgu_bf_ref, wd_bf_ref):
    b = pl.program_id(0)
    live = b < n_used_ref[0]
    d_e = wd_ref.shape[1]

    @pl.when(jnp.logical_or(b == 0, blk_e_ref[b] != blk_e_ref[jnp.maximum(b - 1, 0)]))
    def _():
        wgu_bf_ref[:, :d_e] = wg_ref[0].astype(BF16)
        wgu_bf_ref[:, d_e:] = wu_ref[0].astype(BF16)
        wd_bf_ref[...] = wd_ref[0].astype(BF16)

    @pl.when(live)
    def _():
        x = _unpack_rows(rows_ref, FFN_BLOCK).astype(BF16)
        au = jnp.dot(x, wgu_bf_ref[...], preferred_element_type=F32)
        hid = (_silu(au[:, :d_e]) * au[:, d_e:]).astype(BF16)
        _pack_rows(y_ref, jnp.dot(hid, wd_bf_ref[...], preferred_element_type=F32))

    @pl.when(jnp.logical_not(live))
    def _():
        y_ref[...] = jnp.zeros_like(y_ref)


def _ffn(rows, blk_e, n_used, wg, wu, wd):
    n_rows = rows.shape[0] // PACK_SUB
    n_e, d, d_e = wg.shape
    n_blk = n_rows // FFN_BLOCK
    blk = lambda b, e, n: (b, 0)
    exp = lambda b, e, n: (e[b], 0, 0)
    return pl.pallas_call(
        _ffn_kernel,
        grid_spec=pltpu.PrefetchScalarGridSpec(
            num_scalar_prefetch=2,
            grid=(n_blk,),
            in_specs=[pl.BlockSpec((FFN_BLOCK * PACK_SUB, LANE), blk),
                      pl.BlockSpec((1, d, d_e), exp), pl.BlockSpec((1, d, d_e), exp),
                      pl.BlockSpec((1, d_e, d), exp)],
            out_specs=pl.BlockSpec((FFN_BLOCK * PACK_SUB, LANE), blk),
            scratch_shapes=[pltpu.VMEM((d, 2 * d_e), BF16), pltpu.VMEM((d_e, d), BF16)],
        ),
        out_shape=jax.ShapeDtypeStruct((n_rows * PACK_SUB, LANE), U32),
        compiler_params=_cparams(("arbitrary",)),
        name="ffn",
    )(blk_e, n_used, rows, wg, wu, wd)


def _final_kernel(dest_ref, dnext_ref, x1_ref, h_ref, gate_ref, g2_ref, wsg_ref, wsu_ref, wsd_ref, lng_ref, lnb_ref,
                  y_ref, o_ref, ybuf_ref, sems, *, alpha):
    i = pl.program_id(0)
    tm = x1_ref.shape[0]
    per_tile = TOP_K * tm

    def gather(d_ref, slot):
        def body(tt, _):
            for k in range(TOP_K):
                pltpu.make_async_copy(_row_slab(y_ref, d_ref[0, 0, k * tm + tt]),
                                      _row_slab(ybuf_ref, slot * per_tile + k * tm + tt),
                                      sems.at[slot]).start(priority=k % 2)
            return 0

        lax.fori_loop(0, tm, body, 0, unroll=ISSUE_UNROLL)

    @pl.when(i == 0)
    def _():
        gather(dest_ref, 0)

    @pl.when(i + 1 < pl.num_programs(0))
    def _():
        gather(dnext_ref, (i + 1) % 2)

    h = h_ref[...]
    a = jnp.dot(h, wsg_ref[...], preferred_element_type=F32)
    u = jnp.dot(h, wsu_ref[...], preferred_element_type=F32)
    shared = jnp.dot((_silu(a) * u).astype(BF16), wsd_ref[...], preferred_element_type=F32)

    slot = i % 2
    half = ybuf_ref.at[pl.ds(pl.multiple_of(slot * per_tile * PACK_SUB, per_tile * PACK_SUB), per_tile * PACK_SUB), :]
    pltpu.make_async_copy(half, half, sems.at[slot]).wait()
    first = slot * per_tile
    routed = gate_ref[:, 0:1] * _unpack_rows(ybuf_ref, tm, first=first)
    for k in range(1, TOP_K):
        routed = routed + gate_ref[:, k:k + 1] * _unpack_rows(ybuf_ref, tm, first=first + k * tm)

    v = alpha * x1_ref[...] + g2_ref[0] * (shared + routed)
    o_ref[...] = _standardize(v) * lng_ref[...] + lnb_ref[...]


def _final(x1, hhi, y_rows, dest_tiles, gate_t, g2, wsg, wsu, wsd, lng, lnb, seq, alpha):
    t, d = x1.shape
    d_sh = wsg.shape[1]
    n_tile, _, per_tile = dest_tiles.shape
    tm = per_tile // TOP_K
    tpb = seq // tm
    bsz = t // seq
    row = lambda i: (i, 0)
    return pl.pallas_call(
        functools.partial(_final_kernel, alpha=alpha),
        grid=(n_tile,),
        in_specs=[pl.BlockSpec((1, 1, per_tile), lambda i: (i, 0, 0), memory_space=pltpu.SMEM),
                  pl.BlockSpec((1, 1, per_tile), lambda i: (jnp.minimum(i + 1, n_tile - 1), 0, 0),
                               memory_space=pltpu.SMEM),
                  pl.BlockSpec((tm, d), row), pl.BlockSpec((tm, d), row), pl.BlockSpec((tm, TOP_K), row),
                  pl.BlockSpec((1, 1, d), lambda i: (i // tpb, 0, 0)),
                  _const_spec((d, d_sh)), _const_spec((d, d_sh)), _const_spec((d_sh, d)),
                  _const_spec((1, d)), _const_spec((1, d)),
                  pl.BlockSpec(memory_space=pl.ANY)],
        out_specs=pl.BlockSpec((tm, d), row),
        out_shape=jax.ShapeDtypeStruct((t, d), F32),
        scratch_shapes=[pltpu.VMEM((2 * TOP_K * tm * PACK_SUB, LANE), U32), pltpu.SemaphoreType.DMA((2,))],
        compiler_params=_cparams(("arbitrary",)),
        name="final",
    )(dest_tiles, dest_tiles, x1, hhi, gate_t, g2.reshape(bsz, 1, d), wsg, wsu, wsd, lng, lnb, y_rows)


def _split_hi_lo(w):
    hi = w.astype(BF16)
    return hi, (w - hi.astype(F32)).astype(BF16)


def kernel(x, c, w_ada, b_ada, w_in, b_in, s5_lambda_re, s5_lambda_im, s5_log_dt, s5_b_re, s5_b_im, s5_c_re, s5_c_im, s5_d, w_s5_glu, b_s5_glu, w_fourier, b_fourier, w_out, b_out, ln1_g, ln1_b, w_router, router_bias, w_exp_gate, w_exp_up, w_exp_down, w_sh_gate, w_sh_up, w_sh_down, ln2_g, ln2_b):
    bsz, seq, d = x.shape
    depth = w_ada.shape[0]
    alpha = (2 * depth) ** 0.25
    t = bsz * seq
    d_s5 = s5_d.shape[1]
    d_f = w_fourier.shape[1]
    fw = d_f // FOURIER_GROUPS
    n_e = w_router.shape[2]
    n_chunk = seq // LANE
    seq_tabs, cmat = _dft_tables(seq, fw)
    row = lambda v: v.astype(F32).reshape(1, -1)

    x2 = x.reshape(t, d)
    for l in range(depth):
        mod = _adaln(c, w_ada[l], b_ada[l])
        sh1, sc1, g1, sh2, sc2, g2 = jnp.split(mod, 6, axis=-1)

        wi = w_in[l]
        bi = b_in[l].astype(F32)
        ws5t = wi[:, :d_s5].T.astype(BF16)
        bs5 = jnp.broadcast_to(bi[:d_s5].reshape(d_s5, 1), (d_s5, LANE))
        us5, z, gates = _proj(x2, sc1, sh1, ws5t, bs5,
                              wi[:, d_s5:d_s5 + d_f].astype(BF16), row(bi[d_s5:d_s5 + d_f]), cmat,
                              wi[:, d_s5 + d_f:].astype(BF16), row(bi[d_s5 + d_f:]), seq)

        tables = _s5_tables(s5_lambda_re[l], s5_lambda_im[l], s5_log_dt[l], s5_b_re[l], s5_b_im[l],
                            s5_c_re[l], s5_c_im[l], s5_d[l])
        ys5 = _s5(us5, tables, n_chunk)
        brs = _glu(ys5, w_s5_glu[l].astype(BF16), row(b_s5_glu[l]))
        yf = _seqdft(seq_tabs, z, seq, fw)

        x1, hhi, hlo, hp = _merge(x2, brs, yf, gates, g1, sc2, sh2,
                              w_fourier[l].astype(BF16), row(b_fourier[l]),
                              w_out[l].astype(BF16), row(b_out[l]), row(ln1_g[l]), row(ln1_b[l]), seq, alpha)

        wrt_hi, wrt_lo = _split_hi_lo(w_router[l].astype(F32).T)
        eidx, gate, rank, cnt = _router(hhi, hlo, wrt_hi, wrt_lo, router_bias[l])

        counts = cnt[:, 0]
        padded = ((counts + FFN_BLOCK - 1) // FFN_BLOCK) * FFN_BLOCK
        pend = jnp.cumsum(padded)
        pstart = (pend - padded).astype(I32)
        dest = _dest(eidx, rank, pstart)
        n_blk = (t * TOP_K + n_e * (FFN_BLOCK - 1) + FFN_BLOCK - 1) // FFN_BLOCK
        n_rows = n_blk * FFN_BLOCK
        blk_start = jnp.arange(n_blk, dtype=I32) * FFN_BLOCK
        blk_e = jnp.minimum(jnp.sum((pend[None, :] <= blk_start[:, None]).astype(I32), axis=1), n_e - 1)
        n_used = (pend[-1:] // FFN_BLOCK).astype(I32)
        pad_lo = jnp.concatenate([pstart + counts, pend[-1:]]).astype(I32)
        pad_n = jnp.concatenate([padded - counts, n_rows - pend[-1:]]).astype(I32)

        def tiles(tm):
            return dest.reshape(TOP_K, t // tm, tm).transpose(1, 0, 2).reshape(t // tm, 1, TOP_K * tm)

        rows = _dispatch(hp, tiles(DISPATCH_TILE), pad_lo, pad_n, n_rows)
        y_rows = _ffn(rows, blk_e, n_used, w_exp_gate[l], w_exp_up[l], w_exp_down[l])
        x2 = _final(x1, hhi, y_rows, tiles(COMBINE_TILE), gate.T, g2,
                    w_sh_gate[l].astype(BF16), w_sh_up[l].astype(BF16), w_sh_down[l].astype(BF16),
                    row(ln2_g[l]), row(ln2_b[l]), seq, alpha)
    return x2.reshape(bsz, seq, d)
```

```python
import functools
import math

import jax
import jax.numpy as jnp
from jax import lax
from jax.experimental import pallas as pl
from jax.experimental.pallas import tpu as pltpu

F32 = jnp.float32
BF16 = jnp.bfloat16
I32 = jnp.int32

TOP_K = 8
N_EXPERT_GROUPS = 8
TOPK_GROUPS = 4
ROUTED_SCALE = 2.5
FOURIER_GROUPS = 4
LN_EPS = 1e-5

LANE = 128
VMEM_LIMIT = 56 * 1024 * 1024

HIGHEST = lax.Precision.HIGHEST
NEG_INF = float("-inf")

FFN_BLOCK = 512
DISPATCH_TILE = 512
COMBINE_TILE = 256
ISSUE_UNROLL = 4


def _cparams(sem):
    return pltpu.CompilerParams(dimension_semantics=sem, vmem_limit_bytes=VMEM_LIMIT)


def _const_spec(shape):
    nd = len(shape)
    return pl.BlockSpec(shape, lambda *_: (0,) * nd, pipeline_mode=pl.Buffered(1))


def _standardize(x):
    mu = jnp.mean(x, axis=-1, keepdims=True)
    xc = x - mu
    var = jnp.mean(xc * xc, axis=-1, keepdims=True)
    return xc * lax.rsqrt(var + LN_EPS)


def _silu(x):
    return x * jax.nn.sigmoid(x)


def _gelu_tanh(x):
    return 0.5 * x * (1.0 + jnp.tanh(math.sqrt(2.0 / math.pi) * (x + 0.044715 * (x * x * x))))


def _adaln_kernel(c_ref, w_ref, b_ref, o_ref):
    a = _silu(c_ref[...])
    o_ref[...] = jnp.dot(a, w_ref[...], precision=HIGHEST, preferred_element_type=F32) + b_ref[...]


def _adaln(c, w, b):
    bsz, d = c.shape
    n = w.shape[1]
    tn = 512
    return pl.pallas_call(
        _adaln_kernel,
        grid=(n // tn,),
        in_specs=[pl.BlockSpec((bsz, d), lambda j: (0, 0)),
                  pl.BlockSpec((d, tn), lambda j: (0, j)),
                  pl.BlockSpec((1, tn), lambda j: (0, j))],
        out_specs=pl.BlockSpec((bsz, tn), lambda j: (0, j)),
        out_shape=jax.ShapeDtypeStruct((bsz, n), F32),
        compiler_params=_cparams(("parallel",)),
        name="adaln",
    )(c, w, b.reshape(1, n))


def _proj_kernel(x_ref, sc_ref, sh_ref, ws5t_ref, bs5_ref, wf_ref, bf_ref, cs_ref, wg_ref, bg_ref,
                 us5_ref, z_ref, gates_ref):
    tm = x_ref.shape[0]
    u = (_standardize(x_ref[...]) * (1.0 + sc_ref[0]) + sh_ref[0]).astype(BF16)
    p = lax.dot_general(ws5t_ref[...], u, (((1,), (1,)), ((), ())), preferred_element_type=F32)
    p = p + bs5_ref[:, 0:1]
    d_s5, n_j, _ = us5_ref.shape
    us5_2d = us5_ref.reshape(d_s5 * n_j, LANE)
    for j in range(n_j):
        us5_2d[pl.ds(j, d_s5, stride=n_j), :] = p[:, j * LANE:(j + 1) * LANE]
    uf = (jnp.dot(u, wf_ref[...], preferred_element_type=F32) + bf_ref[...]).astype(BF16)
    d_f = uf.shape[1]
    fw = d_f // FOURIER_GROUPS
    for q in range(FOURIER_GROUPS):
        zq = jnp.dot(uf[:, q * fw:(q + 1) * fw], cs_ref[...], preferred_element_type=F32)
        z_ref[:, q * fw:(q + 1) * fw] = zq[:, :fw].astype(BF16)
        z_ref[:, d_f + q * fw:d_f + (q + 1) * fw] = zq[:, fw:].astype(BF16)
    n_g = wg_ref.shape[1]
    half = n_g // 2
    for q in range(2):
        gp = jnp.dot(u, wg_ref[:, q * half:(q + 1) * half], preferred_element_type=F32)
        gp = gp + bg_ref[:, q * half:(q + 1) * half]
        gates_ref[:, q * half:(q + 1) * half] = jax.nn.sigmoid(gp).astype(BF16)


def _proj(x2, sc, sh, ws5t, bs5, wf, bf, cs, wg, bg, seq):
    t, d = x2.shape
    d_s5 = ws5t.shape[0]
    d_f = wf.shape[1]
    n_g = wg.shape[1]
    tm = 1024
    tpb = seq // tm
    bsz = t // seq
    return pl.pallas_call(
        _proj_kernel,
        grid=(t // tm,),
        in_specs=[pl.BlockSpec((tm, d), lambda i: (i, 0)),
                  pl.BlockSpec((1, 1, d), lambda i: (i // tpb, 0, 0)),
                  pl.BlockSpec((1, 1, d), lambda i: (i // tpb, 0, 0)),
                  _const_spec((d_s5, d)), _const_spec((d_s5, LANE)),
                  _const_spec((d, d_f)), _const_spec((1, d_f)),
                  _const_spec(cs.shape),
                  _const_spec((d, n_g)), _const_spec((1, n_g))],
        out_specs=[pl.BlockSpec((d_s5, tm // LANE, LANE), lambda i: (0, i, 0)),
                   pl.BlockSpec((tm, 2 * d_f), lambda i: (i, 0)),
                   pl.BlockSpec((tm, n_g), lambda i: (i, 0))],
        out_shape=[jax.ShapeDtypeStruct((d_s5, t // LANE, LANE), F32),
                   jax.ShapeDtypeStruct((t, 2 * d_f), BF16),
                   jax.ShapeDtypeStruct((t, n_g), BF16)],
        compiler_params=_cparams(("parallel",)),
        name="proj",
    )(x2, sc.reshape(bsz, 1, d), sh.reshape(bsz, 1, d), ws5t, bs5, wf, bf, cs, wg, bg)


def _s5_tables(lam_re, lam_im, log_dt, b_re, b_im, c_re, c_im, d_skip):
    L = LANE
    hp = HIGHEST
    lr, li = lam_re.astype(F32), lam_im.astype(F32)
    dt = jnp.exp(log_dt.astype(F32))[:, :, None]
    mag = jnp.exp(lr * dt)
    ang = li * dt
    ab_re, ab_im = mag * jnp.cos(ang), mag * jnp.sin(ang)
    den = lr * lr + li * li
    nr = ab_re - 1.0
    coef_re = (nr * lr + ab_im * li) / den
    coef_im = (ab_im * lr - nr * li) / den
    br, bi = b_re.astype(F32), b_im.astype(F32)
    bb_re = coef_re[..., None] * br - coef_im[..., None] * bi
    bb_im = coef_re[..., None] * bi + coef_im[..., None] * br
    cr, ci = c_re.astype(F32), c_im.astype(F32)
    n_g, n_p, n_h = br.shape[1], br.shape[2], br.shape[3]

    k = jnp.arange(L + 1, dtype=F32)[None, None, :, None]
    pmag = jnp.exp(k * (lr * dt)[:, :, None, :])
    pang = k * (li * dt)[:, :, None, :]
    pw_re, pw_im = pmag * jnp.cos(pang), pmag * jnp.sin(pang)

    m_re = cr[:, :, :, None, :] * jnp.swapaxes(bb_re, 2, 3)[:, :, None, :, :] \
        - ci[:, :, :, None, :] * jnp.swapaxes(bb_im, 2, 3)[:, :, None, :, :]
    m_im = cr[:, :, :, None, :] * jnp.swapaxes(bb_im, 2, 3)[:, :, None, :, :] \
        + ci[:, :, :, None, :] * jnp.swapaxes(bb_re, 2, 3)[:, :, None, :, :]
    kap = jnp.einsum("dgohp,dgkp->dgohk", m_re, pw_re, precision=hp) \
        - jnp.einsum("dgohp,dgkp->dgohk", m_im, pw_im, precision=hp)
    kb = kap[1]
    kpos = kap[0, ..., :L].at[..., 0].add(kb[..., 0]).reshape(n_g, n_h * n_h, L)
    kneg = jnp.concatenate([jnp.zeros_like(kb[..., 0:1]), kb[..., L - 1:0:-1]], axis=-1)
    kneg = kneg.reshape(n_g, n_h * n_h, L)

    cat = lambda *parts: jnp.concatenate(parts, axis=-1)
    pf_re, pf_im = pw_re[0, :, L - 1::-1][:, :L], pw_im[0, :, L - 1::-1][:, :L]
    pb_re, pb_im = pw_re[1, :, :L], pw_im[1, :, :L]
    sp, spsw = cat(pf_re, pf_im, pb_re, pb_im), cat(pf_im, pf_re, pb_im, pb_re)
    bt_re, bt_im = jnp.swapaxes(bb_re, 2, 3), jnp.swapaxes(bb_im, 2, 3)
    bx, by = cat(bt_re[0], bt_re[0], bt_re[1], bt_re[1]), cat(-bt_im[0], bt_im[0], -bt_im[1], bt_im[1])

    qf_re, qf_im = pw_re[0, :, 1:L + 1], pw_im[0, :, 1:L + 1]
    qb_re, qb_im = pw_re[1, :, L:0:-1], pw_im[1, :, L:0:-1]
    qt, qtsw = cat(qf_re, qf_im, qb_re, qb_im), cat(qf_im, qf_re, qb_im, qb_re)
    cx, cy = cat(cr[0], -cr[0], cr[1], -cr[1]), cat(-ci[0], -ci[0], -ci[1], -ci[1])

    al_re, al_im = pw_re[:, :, L], pw_im[:, :, L]
    al = jnp.stack([cat(al_re[0], al_re[0], al_re[1], al_re[1]),
                    cat(-al_im[0], al_im[0], -al_im[1], al_im[1])], axis=1)

    dsk = jnp.broadcast_to(d_skip.astype(F32).reshape(n_g, n_h, 1), (n_g, n_h, L))
    return kpos, kneg, sp, spsw, bx, by, qt, qtsw, cx, cy, al, dsk


def _s5_kernel(a_ref, kpos_ref, kneg_ref, sp_ref, spsw_ref, bx_ref, by_ref, qt_ref, qtsw_ref, cx_ref, cy_ref,
               al_ref, d_ref, y_ref, abf_ref, tp_ref, ws_ref, sf_ref, sb_ref, xf_ref, xb_ref, *, n_chunk):
    n_h, r, L = a_ref.shape
    bsz = r // n_chunk
    half = sp_ref.shape[2] // 2
    nt = (((1,), (1,)), ((), ()))

    sp, spsw = sp_ref[0], spsw_ref[0]
    for h in range(n_h):
        abf_ref[:, h * L:(h + 1) * L] = a_ref[h].astype(BF16)
        ws_ref[h * L:(h + 1) * L, :] = (sp * bx_ref[0, h:h + 1, :] + spsw * by_ref[0, h:h + 1, :]).astype(BF16)
    abf = abf_ref[...]

    s_all = jnp.dot(abf, ws_ref[...], preferred_element_type=F32)
    sf_ref[...] = s_all[:, :half]
    sb_ref[...] = s_all[:, half:]

    alx = al_ref[0, 0:1, :]
    aly = al_ref[0, 1:2, :]

    def carry(e, lo):
        ex = e * alx[:, lo:lo + half]
        ey = pltpu.roll(e, half // 2, 1) * aly[:, lo:lo + half]
        return ex + ey

    e = jnp.zeros((bsz, half), F32)
    for c in range(n_chunk):
        xf_ref[pl.ds(c, bsz, stride=n_chunk), :] = e
        e = carry(e, 0) + sf_ref[pl.ds(c, bsz, stride=n_chunk), :]
    e = jnp.zeros((bsz, half), F32)
    for c in range(n_chunk - 1, -1, -1):
        xb_ref[pl.ds(c, bsz, stride=n_chunk), :] = e
        e = carry(e, half) + sb_ref[pl.ds(c, bsz, stride=n_chunk), :]
    xin = jnp.concatenate([xf_ref[...], xb_ref[...]], axis=1).astype(BF16)

    s_idx = lax.broadcasted_iota(I32, (L, L), 0)
    j_idx = lax.broadcasted_iota(I32, (L, L), 1)
    fwd_part = j_idx + s_idx < L
    qt, qtsw = qt_ref[0], qtsw_ref[0]

    def pair(op, _):
        for oo in range(2):
            o = 2 * op + oo
            for h in range(n_h):
                row = o * n_h + h
                kp = jnp.broadcast_to(kpos_ref[0, pl.ds(row, 1), :], (L, L))
                kn = jnp.broadcast_to(kneg_ref[0, pl.ds(row, 1), :], (L, L))
                tile = pltpu.roll(jnp.where(fwd_part, kp, kn), 0, 1, stride=1, stride_axis=0)
                tp_ref[h * L:(h + 1) * L, oo * L:(oo + 1) * L] = tile.astype(BF16)
        wo_t = jnp.concatenate(
            [qt * cx_ref[0, pl.ds(2 * op + oo, 1), :] + qtsw * cy_ref[0, pl.ds(2 * op + oo, 1), :]
             for oo in range(2)], axis=0).astype(BF16)
        yp = jnp.dot(abf, tp_ref[...], preferred_element_type=F32)
        yp = yp + lax.dot_general(xin, wo_t, nt, preferred_element_type=F32)
        for oo in range(2):
            o = 2 * op + oo
            y_ref[o] = yp[:, oo * L:(oo + 1) * L] + a_ref[o] * d_ref[0, pl.ds(o, 1), :]
        return 0

    lax.fori_loop(0, n_h // 2, pair, 0)


def _s5(us5, tables, n_chunk):
    kpos, kneg, sp, spsw, bx, by, qt, qtsw, cx, cy, al, dsk = tables
    d_s5, r, L = us5.shape
    n_g = kpos.shape[0]
    n_h = d_s5 // n_g
    n_st = sp.shape[2]
    g3 = lambda g: (g, 0, 0)
    pw_spec = pl.BlockSpec((1, L, n_st), g3)
    hv_spec = pl.BlockSpec((1, n_h, n_st), g3)
    return pl.pallas_call(
        functools.partial(_s5_kernel, n_chunk=n_chunk),
        grid=(n_g,),
        in_specs=[pl.BlockSpec((n_h, r, L), g3),
                  pl.BlockSpec((1, n_h * n_h, L), g3), pl.BlockSpec((1, n_h * n_h, L), g3),
                  pw_spec, pw_spec, hv_spec, hv_spec, pw_spec, pw_spec, hv_spec, hv_spec,
                  pl.BlockSpec((1, 2, n_st), g3), pl.BlockSpec((1, n_h, L), g3)],
        out_specs=pl.BlockSpec((n_h, r, L), g3),
        out_shape=jax.ShapeDtypeStruct((d_s5, r, L), F32),
        scratch_shapes=[pltpu.VMEM((r, n_h * L), BF16), pltpu.VMEM((n_h * L, 2 * L), BF16),
                        pltpu.VMEM((n_h * L, n_st), BF16),
                        pltpu.VMEM((r, n_st // 2), F32), pltpu.VMEM((r, n_st // 2), F32),
                        pltpu.VMEM((r, n_st // 2), F32), pltpu.VMEM((r, n_st // 2), F32)],
        compiler_params=_cparams(("parallel",)),
        name="s5",
    )(us5, kpos, kneg, sp, spsw, bx, by, qt, qtsw, cx, cy, al, dsk)


DFT_SPLIT = 64


def _dft_tables(seq, fw):
    def angles(mult, n_rows, n):
        s = jnp.arange(n, dtype=I32)[None, :]
        q = jnp.arange(n_rows, dtype=I32)[:, None]
        return (2.0 * math.pi / n) * ((mult * q * s) % n).astype(F32)

    ang_a = angles(DFT_SPLIT, seq // DFT_SPLIT, seq)
    ang_b = angles(1, DFT_SPLIT, seq)
    seq_tabs = (jnp.cos(ang_a), jnp.sin(ang_a), jnp.cos(ang_b), jnp.sin(ang_b))
    ang_c = angles(1, fw, fw)
    return seq_tabs, jnp.concatenate([jnp.cos(ang_c), jnp.sin(ang_c)], axis=1).astype(BF16)


def _seqdft_kernel(ca_ref, sa_ref, cb_ref, sb_ref, z_ref, o_ref, f_ref, *, scale):
    seq = z_ref.shape[0]
    d_f = o_ref.shape[1]
    tk = o_ref.shape[0]

    @pl.when(pl.program_id(1) == 0)
    def _():
        a0 = pl.program_id(0) * (tk // DFT_SPLIT)
        cb, sb = cb_ref[...], sb_ref[...]
        for j in range(tk // DFT_SPLIT):
            ca = ca_ref[pl.ds(a0 + j, 1), :]
            sa = sa_ref[pl.ds(a0 + j, 1), :]
            rows = slice(j * DFT_SPLIT, (j + 1) * DFT_SPLIT)
            f_ref[rows, :seq] = (ca * cb - sa * sb).astype(BF16)
            f_ref[rows, seq:] = (-(sa * cb + ca * sb)).astype(BF16)

    acc = jnp.dot(f_ref[:, :seq], z_ref[:, :d_f], preferred_element_type=F32)
    acc = acc + jnp.dot(f_ref[:, seq:], z_ref[:, d_f:], preferred_element_type=F32)
    o_ref[...] = (acc * scale).astype(o_ref.dtype)


def _seqdft(seq_tabs, z, seq, fw):
    t, two_df = z.shape
    d_f = two_df // 2
    bsz = t // seq
    tk = 512
    nk = seq // tk
    scale = 1.0 / math.sqrt(seq * fw)
    return pl.pallas_call(
        functools.partial(_seqdft_kernel, scale=scale),
        grid=(nk, bsz),
        in_specs=[_const_spec(tab.shape) for tab in seq_tabs]
        + [pl.BlockSpec((seq, two_df), lambda k, b: (b, 0))],
        out_specs=pl.BlockSpec((tk, d_f), lambda k, b: (b * nk + k, 0)),
        out_shape=jax.ShapeDtypeStruct((t, d_f), BF16),
        scratch_shapes=[pltpu.VMEM((tk, 2 * seq), BF16)],
        compiler_params=_cparams(("parallel", "arbitrary")),
        name="seqdft",
    )(*seq_tabs, z)


def _glu_kernel(y_ref, w_ref, b_ref, o_ref, zt_ref):
    d_s5, n_j, _ = y_ref.shape
    d = o_ref.shape[1]
    y2 = y_ref.reshape(d_s5 * n_j, LANE)
    for j in range(n_j):
        zt_ref[j * LANE:(j + 1) * LANE, :] = _gelu_tanh(y2[pl.ds(j, d_s5, stride=n_j), :]).T.astype(BF16)
    zt = zt_ref[...]
    a = jnp.dot(zt, w_ref[:, :d], preferred_element_type=F32) + b_ref[:, :d]
    g = jnp.dot(zt, w_ref[:, d:], preferred_element_type=F32) + b_ref[:, d:]
    o_ref[...] = (a * jax.nn.sigmoid(g)).astype(BF16)


def _glu(ys5, w, b):
    d_s5, r, L = ys5.shape
    t = r * L
    n = w.shape[1]
    tm = 1024
    return pl.pallas_call(
        _glu_kernel,
        grid=(t // tm,),
        in_specs=[pl.BlockSpec((d_s5, tm // L, L), lambda i: (0, i, 0)),
                  _const_spec((d_s5, n)), _const_spec((1, n))],
        out_specs=pl.BlockSpec((tm, n // 2), lambda i: (i, 0)),
        out_shape=jax.ShapeDtypeStruct((t, n // 2), BF16),
        scratch_shapes=[pltpu.VMEM((tm, d_s5), BF16)],
        compiler_params=_cparams(("parallel",)),
        name="glu",
    )(ys5, w, b)


U32 = jnp.uint32
PACK_SUB = 4


def _row_slab(ref, r):
    return ref.at[pl.ds(pl.multiple_of(r * PACK_SUB, PACK_SUB), PACK_SUB), :]


def _pack_rows(ref, v, first=0):
    rows, d = v.shape
    half = d // 2
    bits = lax.bitcast_convert_type(v.astype(BF16).astype(F32), U32)
    for c in range(PACK_SUB):
        lo = bits[:, c * LANE:(c + 1) * LANE] >> 16
        hi = bits[:, half + c * LANE:half + (c + 1) * LANE] & jnp.uint32(0xFFFF0000)
        ref[pl.ds(first * PACK_SUB + c, rows, stride=PACK_SUB), :] = hi | lo


def _unpack_rows(ref, rows, first=0):
    los, his = [], []
    for c in range(PACK_SUB):
        w = ref[pl.ds(first * PACK_SUB + c, rows, stride=PACK_SUB), :]
        los.append(lax.bitcast_convert_type(w << 16, F32))
        his.append(lax.bitcast_convert_type(w & jnp.uint32(0xFFFF0000), F32))
    return jnp.concatenate(los + his, axis=1)


def _merge_kernel(x_ref, brs_ref, yf_ref, gates_ref, g1_ref, sc_ref, sh_ref, wfo_ref, bfo_ref,
                  wo_ref, bo_ref, lng_ref, lnb_ref, x1_ref, hhi_ref, hlo_ref, hp_ref, *, alpha):
    d = x_ref.shape[1]
    br_f = jnp.dot(yf_ref[...], wfo_ref[...], preferred_element_type=F32) + bfo_ref[...]
    merged = gates_ref[:, :d].astype(F32) * brs_ref[...].astype(F32) + gates_ref[:, d:].astype(F32) * br_f
    mix = jnp.dot(merged.astype(BF16), wo_ref[...], preferred_element_type=F32) + bo_ref[...]
    v = alpha * x_ref[...] + g1_ref[0] * mix
    x1 = _standardize(v) * lng_ref[...] + lnb_ref[...]
    x1_ref[...] = x1
    h = _standardize(x1) * (1.0 + sc_ref[0]) + sh_ref[0]
    hhi = h.astype(BF16)
    hhi_ref[...] = hhi
    hlo_ref[...] = (h - hhi.astype(F32)).astype(BF16)
    _pack_rows(hp_ref, h)


def _merge(x2, brs, yf, gates, g1, sc2, sh2, wfo, bfo, wo, bo, lng, lnb, seq, alpha):
    t, d = x2.shape
    d_f = yf.shape[1]
    tm = 512
    tpb = seq // tm
    bsz = t // seq
    row = lambda i: (i, 0)
    bat = lambda i: (i // tpb, 0, 0)
    return pl.pallas_call(
        functools.partial(_merge_kernel, alpha=alpha),
        grid=(t // tm,),
        in_specs=[pl.BlockSpec((tm, d), row), pl.BlockSpec((tm, d), row), pl.BlockSpec((tm, d_f), row),
                  pl.BlockSpec((tm, 2 * d), row),
                  pl.BlockSpec((1, 1, d), bat), pl.BlockSpec((1, 1, d), bat), pl.BlockSpec((1, 1, d), bat),
                  _const_spec((d_f, d)), _const_spec((1, d)), _const_spec((d, d)), _const_spec((1, d)),
                  _const_spec((1, d)), _const_spec((1, d))],
        out_specs=[pl.BlockSpec((tm, d), row), pl.BlockSpec((tm, d), row), pl.BlockSpec((tm, d), row),
                   pl.BlockSpec((tm * PACK_SUB, LANE), row)],
        out_shape=[jax.ShapeDtypeStruct((t, d), F32), jax.ShapeDtypeStruct((t, d), BF16),
                   jax.ShapeDtypeStruct((t, d), BF16), jax.ShapeDtypeStruct((t * PACK_SUB, LANE), U32)],
        compiler_params=_cparams(("parallel",)),
        name="merge",
    )(x2, brs, yf, gates, g1.reshape(bsz, 1, d), sc2.reshape(bsz, 1, d), sh2.reshape(bsz, 1, d),
      wfo, bfo, wo, bo, lng, lnb)


def _router_kernel(hhi_ref, hlo_ref, whi_ref, wlo_ref, bias_ref, tri_ref,
                   eidx_ref, gate_ref, rank_ref, cnt_ref, base_ref):
    n_e = whi_ref.shape[0]
    tm = hhi_ref.shape[0]
    gsz = n_e // N_EXPERT_GROUPS
    nt = (((1,), (1,)), ((), ()))

    @pl.when(pl.program_id(0) == 0)
    def _():
        base_ref[...] = jnp.zeros_like(base_ref)

    hhi = hhi_ref[...]
    logits = lax.dot_general(whi_ref[...], hhi, nt, preferred_element_type=F32)
    logits = logits + lax.dot_general(wlo_ref[...], hhi, nt, preferred_element_type=F32)
    logits = logits + lax.dot_general(whi_ref[...], hlo_ref[...], nt, preferred_element_type=F32)
    scores = jax.nn.sigmoid(logits)
    sel = scores + bias_ref[:, 0:1]

    g3 = sel.reshape(N_EXPERT_GROUPS, gsz, tm)
    i3 = lax.broadcasted_iota(I32, g3.shape, 1).astype(F32)
    m1 = jnp.max(g3, axis=1, keepdims=True)
    first = jnp.min(jnp.where(g3 == m1, i3, float(gsz)), axis=1, keepdims=True)
    m2 = jnp.max(jnp.where(i3 == first, NEG_INF, g3), axis=1, keepdims=True)
    gs = (m1 + m2).reshape(N_EXPERT_GROUPS, tm)

    gi = lax.broadcasted_iota(I32, gs.shape, 0).astype(F32)
    gsel = jnp.zeros(gs.shape, F32)
    cur = gs
    for _ in range(TOPK_GROUPS):
        m = jnp.max(cur, axis=0, keepdims=True)
        f = jnp.min(jnp.where(cur == m, gi, float(N_EXPERT_GROUPS)), axis=0, keepdims=True)
        pick = gi == f
        gsel = jnp.where(pick, 1.0, gsel)
        cur = jnp.where(pick, NEG_INF, cur)
    gmask = jnp.broadcast_to(gsel.reshape(N_EXPERT_GROUPS, 1, tm), g3.shape).reshape(n_e, tm)
    masked = jnp.where(gmask > 0.5, sel, NEG_INF)

    ri = lax.broadcasted_iota(I32, (n_e, tm), 0).astype(F32)
    picks = []
    gates = []
    multihot = jnp.zeros((n_e, tm), F32)
    for _ in range(TOP_K):
        m = jnp.max(masked, axis=0, keepdims=True)
        f = jnp.min(jnp.where(masked == m, ri, float(n_e)), axis=0, keepdims=True)
        pick = ri == f
        picks.append(f)
        gates.append(jnp.sum(jnp.where(pick, scores, 0.0), axis=0, keepdims=True))
        multihot = jnp.where(pick, 1.0, multihot)
        masked = jnp.where(pick, NEG_INF, masked)
    gsum = gates[0]
    for g in gates[1:]:
        gsum = gsum + g

    rankmat = jnp.dot(multihot.astype(BF16), tri_ref[...], preferred_element_type=F32) + base_ref[:, 0:1]
    for k in range(TOP_K):
        pick = ri == picks[k]
        eidx_ref[k:k + 1, :] = picks[k].astype(I32)
        gate_ref[k:k + 1, :] = gates[k] / gsum * ROUTED_SCALE
        rank_ref[k:k + 1, :] = jnp.sum(jnp.where(pick, rankmat, 0.0), axis=0, keepdims=True).astype(I32)
    base_ref[...] = base_ref[...] + jnp.sum(multihot, axis=1, keepdims=True)
    cnt_ref[...] = base_ref[...].astype(I32)


def _router(hhi, hlo, wrt_hi, wrt_lo, bias):
    t, d = hhi.shape
    n_e = wrt_hi.shape[0]
    tm = 512
    tri = (jnp.arange(tm)[:, None] < jnp.arange(tm)[None, :]).astype(BF16)
    kt = lambda i: (0, i)
    return pl.pallas_call(
        _router_kernel,
        grid=(t // tm,),
        in_specs=[pl.BlockSpec((tm, d), lambda i: (i, 0)), pl.BlockSpec((tm, d), lambda i: (i, 0)),
                  _const_spec((n_e, d)), _const_spec((n_e, d)), _const_spec((n_e, LANE)),
                  _const_spec((tm, tm))],
        out_specs=[pl.BlockSpec((TOP_K, tm), kt), pl.BlockSpec((TOP_K, tm), kt), pl.BlockSpec((TOP_K, tm), kt),
                   pl.BlockSpec((n_e, LANE), lambda i: (0, 0))],
        out_shape=[jax.ShapeDtypeStruct((TOP_K, t), I32), jax.ShapeDtypeStruct((TOP_K, t), F32),
                   jax.ShapeDtypeStruct((TOP_K, t), I32), jax.ShapeDtypeStruct((n_e, LANE), I32)],
        scratch_shapes=[pltpu.VMEM((n_e, LANE), F32)],
        compiler_params=_cparams(("arbitrary",)),
        name="router",
    )(hhi, hlo, wrt_hi, wrt_lo, jnp.broadcast_to(bias.astype(F32).reshape(n_e, 1), (n_e, LANE)), tri)


def _dest_kernel(eidx_ref, rank_ref, pstart_ref, dest_ref):
    n_e = pstart_ref.shape[0]
    tm = eidx_ref.shape[1]
    ri = lax.broadcasted_iota(I32, (n_e, tm), 0)
    ps = pstart_ref[:, 0:1].astype(F32)
    for k in range(TOP_K):
        hit = ri == eidx_ref[k:k + 1, :]
        base = jnp.sum(jnp.where(hit, ps, 0.0), axis=0, keepdims=True)
        dest_ref[k:k + 1, :] = base.astype(I32) + rank_ref[k:k + 1, :]


def _dest(eidx, rank, pstart):
    k, t = eidx.shape
    n_e = pstart.shape[0]
    tm = 2048
    kt = lambda i: (0, i)
    return pl.pallas_call(
        _dest_kernel,
        grid=(t // tm,),
        in_specs=[pl.BlockSpec((k, tm), kt), pl.BlockSpec((k, tm), kt), _const_spec((n_e, LANE))],
        out_specs=pl.BlockSpec((k, tm), kt),
        out_shape=jax.ShapeDtypeStruct((k, t), I32),
        compiler_params=_cparams(("parallel",)),
        name="dest",
    )(eidx, rank, jnp.broadcast_to(pstart.astype(I32).reshape(n_e, 1), (n_e, LANE)))


def _zero_fill(pad_lo_ref, pad_n_ref, zeros_ref, rows_ref, sem):
    def each_copy(act):
        def per_entry(e, _):
            lo = pad_lo_ref[e]
            n = pad_n_ref[e]
            n_full = n // FFN_BLOCK

            def full(j, _):
                act(pltpu.make_async_copy(
                    zeros_ref, rows_ref.at[pl.ds(pl.multiple_of((lo + j * FFN_BLOCK) * PACK_SUB, PACK_SUB),
                                                 FFN_BLOCK * PACK_SUB), :], sem))
                return 0

            lax.fori_loop(0, n_full, full, 0)
            off = lo + n_full * FFN_BLOCK
            rem = n - n_full * FFN_BLOCK
            bit = FFN_BLOCK // 2
            while bit >= 1:
                take = rem & bit

                @pl.when(take != 0)
                def _(off=off, bit=bit):
                    act(pltpu.make_async_copy(
                        zeros_ref.at[pl.ds(0, bit * PACK_SUB), :],
                        rows_ref.at[pl.ds(pl.multiple_of(off * PACK_SUB, PACK_SUB), bit * PACK_SUB), :], sem))

                off = off + take
                bit //= 2
            return 0

        lax.fori_loop(0, pad_lo_ref.shape[0], per_entry, 0)

    each_copy(lambda cp: cp.start())
    each_copy(lambda cp: cp.wait())


def _dispatch_kernel(pad_lo_ref, pad_n_ref, dest_ref, hp_ref, rows_ref, zeros_ref, sem, zsem):
    tm = hp_ref.shape[0] // PACK_SUB

    @pl.when(pl.program_id(0) == 0)
    def _():
        zeros_ref[...] = jnp.zeros_like(zeros_ref)
        _zero_fill(pad_lo_ref, pad_n_ref, zeros_ref, rows_ref, zsem)

    def body(tt, _):
        for k in range(TOP_K):
            pltpu.make_async_copy(_row_slab(hp_ref, tt), _row_slab(rows_ref, dest_ref[0, 0, k * tm + tt]),
                                  sem).start(priority=k % 2)
        return 0

    lax.fori_loop(0, tm, body, 0, unroll=ISSUE_UNROLL)
    all_rows = rows_ref.at[pl.ds(0, TOP_K * tm * PACK_SUB), :]
    pltpu.make_async_copy(all_rows, all_rows, sem).wait()


def _dispatch(hp, dest_tiles, pad_lo, pad_n, n_rows):
    n_tile, _, per_tile = dest_tiles.shape
    tm = per_tile // TOP_K
    return pl.pallas_call(
        _dispatch_kernel,
        grid_spec=pltpu.PrefetchScalarGridSpec(
            num_scalar_prefetch=2,
            grid=(n_tile,),
            in_specs=[pl.BlockSpec((1, 1, per_tile), lambda i, lo, n: (i, 0, 0), memory_space=pltpu.SMEM),
                      pl.BlockSpec((tm * PACK_SUB, LANE), lambda i, lo, n: (i, 0))],
            out_specs=pl.BlockSpec(memory_space=pl.ANY),
            scratch_shapes=[pltpu.VMEM((FFN_BLOCK * PACK_SUB, LANE), U32),
                            pltpu.SemaphoreType.DMA, pltpu.SemaphoreType.DMA],
        ),
        out_shape=jax.ShapeDtypeStruct((n_rows * PACK_SUB, LANE), U32),
        compiler_params=_cparams(("arbitrary",)),
        name="dispatch",
    )(pad_lo, pad_n, dest_tiles, hp)


def _ffn_kernel(blk_e_ref, n_used_ref, rows_ref, wg_ref, wu_ref, wd_ref, y_ref, wgu_bf_ref, wd_bf_ref):
    b = pl.program_id(0)
    live = b < n_used_ref[0]
    d_e = wd_ref.shape[1]

    @pl.when(jnp.logical_or(b == 0, blk_e_ref[b] != blk_e_ref[jnp.maximum(b - 1, 0)]))
    def _():
        wgu_bf_ref[:, :d_e] = wg_ref[0].astype(BF16)
        wgu_bf_ref[:, d_e:] = wu_ref[0].astype(BF16)
        wd_bf_ref[...] = wd_ref[0].astype(BF16)

    @pl.when(live)
    def _():
        x = _unpack_rows(rows_ref, FFN_BLOCK).astype(BF16)
        au = jnp.dot(x, wgu_bf_ref[...], preferred_element_type=F32)
        hid = (_silu(au[:, :d_e]) * au[:, d_e:]).astype(BF16)
        _pack_rows(y_ref, jnp.dot(hid, wd_bf_ref[...], preferred_element_type=F32))

    @pl.when(jnp.logical_not(live))
    def _():
        y_ref[...] = jnp.zeros_like(y_ref)


def _ffn(rows, blk_e, n_used, wg, wu, wd):
    n_rows = rows.shape[0] // PACK_SUB
    n_e, d, d_e = wg.shape
    n_blk = n_rows // FFN_BLOCK
    blk = lambda b, e, n: (b, 0)
    exp = lambda b, e, n: (e[b], 0, 0)
    return pl.pallas_call(
        _ffn_kernel,
        grid_spec=pltpu.PrefetchScalarGridSpec(
            num_scalar_prefetch=2,
            grid=(n_blk,),
            in_specs=[pl.BlockSpec((FFN_BLOCK * PACK_SUB, LANE), blk),
                      pl.BlockSpec((1, d, d_e), exp), pl.BlockSpec((1, d, d_e), exp),
                      pl.BlockSpec((1, d_e, d), exp)],
            out_specs=pl.BlockSpec((FFN_BLOCK * PACK_SUB, LANE), blk),
            scratch_shapes=[pltpu.VMEM((d, 2 * d_e), BF16), pltpu.VMEM((d_e, d), BF16)],
        ),
        out_shape=jax.ShapeDtypeStruct((n_rows * PACK_SUB, LANE), U32),
        compiler_params=_cparams(("arbitrary",)),
        name="ffn",
    )(blk_e, n_used, rows, wg, wu, wd)


def _final_kernel(dest_ref, dnext_ref, x1_ref, h_ref, gate_ref, g2_ref, wsg_ref, wsu_ref, wsd_ref, lng_ref, lnb_ref,
                  y_ref, o_ref, ybuf_ref, sems, *, alpha):
    i = pl.program_id(0)
    tm = x1_ref.shape[0]
    per_tile = TOP_K * tm

    def gather(d_ref, slot):
        def body(tt, _):
            for k in range(TOP_K):
                pltpu.make_async_copy(_row_slab(y_ref, d_ref[0, 0, k * tm + tt]),
                                      _row_slab(ybuf_ref, slot * per_tile + k * tm + tt),
                                      sems.at[slot]).start(priority=k % 2)
            return 0

        lax.fori_loop(0, tm, body, 0, unroll=ISSUE_UNROLL)

    @pl.when(i == 0)
    def _():
        gather(dest_ref, 0)

    @pl.when(i + 1 < pl.num_programs(0))
    def _():
        gather(dnext_ref, (i + 1) % 2)

    h = h_ref[...]
    a = jnp.dot(h, wsg_ref[...], preferred_element_type=F32)
    u = jnp.dot(h, wsu_ref[...], preferred_element_type=F32)
    shared = jnp.dot((_silu(a) * u).astype(BF16), wsd_ref[...], preferred_element_type=F32)

    slot = i % 2
    half = ybuf_ref.at[pl.ds(pl.multiple_of(slot * per_tile * PACK_SUB, per_tile * PACK_SUB), per_tile * PACK_SUB), :]
    pltpu.make_async_copy(half, half, sems.at[slot]).wait()
    first = slot * per_tile
    routed = gate_ref[:, 0:1] * _unpack_rows(ybuf_ref, tm, first=first)
    for k in range(1, TOP_K):
        routed = routed + gate_ref[:, k:k + 1] * _unpack_rows(ybuf_ref, tm, first=first + k * tm)

    v = alpha * x1_ref[...] + g2_ref[0] * (shared + routed)
    o_ref[...] = _standardize(v) * lng_ref[...] + lnb_ref[...]


def _final(x1, hhi, y_rows, dest_tiles, gate_t, g2, wsg, wsu, wsd, lng, lnb, seq, alpha):
    t, d = x1.shape
    d_sh = wsg.shape[1]
    n_tile, _, per_tile = dest_tiles.shape
    tm = per_tile // TOP_K
    tpb = seq // tm
    bsz = t // seq
    row = lambda i: (i, 0)
    return pl.pallas_call(
        functools.partial(_final_kernel, alpha=alpha),
        grid=(n_tile,),
        in_specs=[pl.BlockSpec((1, 1, per_tile), lambda i: (i, 0, 0), memory_space=pltpu.SMEM),
                  pl.BlockSpec((1, 1, per_tile), lambda i: (jnp.minimum(i + 1, n_tile - 1), 0, 0),
                               memory_space=pltpu.SMEM),
                  pl.BlockSpec((tm, d), row), pl.BlockSpec((tm, d), row), pl.BlockSpec((tm, TOP_K), row),
                  pl.BlockSpec((1, 1, d), lambda i: (i // tpb, 0, 0)),
                  _const_spec((d, d_sh)), _const_spec((d, d_sh)), _const_spec((d_sh, d)),
                  _const_spec((1, d)), _const_spec((1, d)),
                  pl.BlockSpec(memory_space=pl.ANY)],
        out_specs=pl.BlockSpec((tm, d), row),
        out_shape=jax.ShapeDtypeStruct((t, d), F32),
        scratch_shapes=[pltpu.VMEM((2 * TOP_K * tm * PACK_SUB, LANE), U32), pltpu.SemaphoreType.DMA((2,))],
        compiler_params=_cparams(("arbitrary",)),
        name="final",
    )(dest_tiles, dest_tiles, x1, hhi, gate_t, g2.reshape(bsz, 1, d), wsg, wsu, wsd, lng, lnb, y_rows)


def _split_hi_lo(w):
    hi = w.astype(BF16)
    return hi, (w - hi.astype(F32)).astype(BF16)


def kernel(x, c, w_ada, b_ada, w_in, b_in, s5_lambda_re, s5_lambda_im, s5_log_dt, s5_b_re, s5_b_im, s5_c_re, s5_c_im, s5_d, w_s5_glu, b_s5_glu, w_fourier, b_fourier, w_out, b_out, ln1_g, ln1_b, w_router, router_bias, w_exp_gate, w_exp_up, w_exp_down, w_sh_gate, w_sh_up, w_sh_down, ln2_g, ln2_b):
    bsz, seq, d = x.shape
    depth = w_ada.shape[0]
    alpha = (2 * depth) ** 0.25
    t = bsz * seq
    d_s5 = s5_d.shape[1]
    d_f = w_fourier.shape[1]
    fw = d_f // FOURIER_GROUPS
    n_e = w_router.shape[2]
    n_chunk = seq // LANE
    seq_tabs, cmat = _dft_tables(seq, fw)
    row = lambda v: v.astype(F32).reshape(1, -1)

    x2 = x.reshape(t, d)
    for l in range(depth):
        mod = _adaln(c, w_ada[l], b_ada[l])
        sh1, sc1, g1, sh2, sc2, g2 = jnp.split(mod, 6, axis=-1)

        wi = w_in[l]
        bi = b_in[l].astype(F32)
        ws5t = wi[:, :d_s5].T.astype(BF16)
        bs5 = jnp.broadcast_to(bi[:d_s5].reshape(d_s5, 1), (d_s5, LANE))
        us5, z, gates = _proj(x2, sc1, sh1, ws5t, bs5,
                              wi[:, d_s5:d_s5 + d_f].astype(BF16), row(bi[d_s5:d_s5 + d_f]), cmat,
                              wi[:, d_s5 + d_f:].astype(BF16), row(bi[d_s5 + d_f:]), seq)

        tables = _s5_tables(s5_lambda_re[l], s5_lambda_im[l], s5_log_dt[l], s5_b_re[l], s5_b_im[l],
                            s5_c_re[l], s5_c_im[l], s5_d[l])
        ys5 = _s5(us5, tables, n_chunk)
        brs = _glu(ys5, w_s5_glu[l].astype(BF16), row(b_s5_glu[l]))
        yf = _seqdft(seq_tabs, z, seq, fw)

        x1, hhi, hlo, hp = _merge(x2, brs, yf, gates, g1, sc2, sh2,
                              w_fourier[l].astype(BF16), row(b_fourier[l]),
                              w_out[l].astype(BF16), row(b_out[l]), row(ln1_g[l]), row(ln1_b[l]), seq, alpha)

        wrt_hi, wrt_lo = _split_hi_lo(w_router[l].astype(F32).T)
        eidx, gate, rank, cnt = _router(hhi, hlo, wrt_hi, wrt_lo, router_bias[l])

        counts = cnt[:, 0]
        padded = ((counts + FFN_BLOCK - 1) // FFN_BLOCK) * FFN_BLOCK
        pend = jnp.cumsum(padded)
        pstart = (pend - padded).astype(I32)
        dest = _dest(eidx, rank, pstart)
        n_blk = (t * TOP_K + n_e * (FFN_BLOCK - 1) + FFN_BLOCK - 1) // FFN_BLOCK
        n_rows = n_blk * FFN_BLOCK
        blk_start = jnp.arange(n_blk, dtype=I32) * FFN_BLOCK
        blk_e = jnp.minimum(jnp.sum((pend[None, :] <= blk_start[:, None]).astype(I32), axis=1), n_e - 1)
        n_used = (pend[-1:] // FFN_BLOCK).astype(I32)
        pad_lo = jnp.concatenate([pstart + counts, pend[-1:]]).astype(I32)
        pad_n = jnp.concatenate([padded - counts, n_rows - pend[-1:]]).astype(I32)

        def tiles(tm):
            return dest.reshape(TOP_K, t // tm, tm).transpose(1, 0, 2).reshape(t // tm, 1, TOP_K * tm)

        rows = _dispatch(hp, tiles(DISPATCH_TILE), pad_lo, pad_n, n_rows)
        y_rows = _ffn(rows, blk_e, n_used, w_exp_gate[l], w_exp_up[l], w_exp_down[l])
        x2 = _final(x1, hhi, y_rows, tiles(COMBINE_TILE), gate.T, g2,
                    w_sh_gate[l].astype(BF16), w_sh_up[l].astype(BF16), w_sh_down[l].astype(BF16),
                    row(ln2_g[l]), row(ln2_b[l]), seq, alpha)
    return x2.reshape(bsz, seq, d)
```

```python
import functools
import math

import jax
import jax.numpy as jnp
from jax import lax
from jax.experimental import pallas as pl
from jax.experimental.pallas import tpu as pltpu

F32 = jnp.float32
BF16 = jnp.bfloat16
I32 = jnp.int32

TOP_K = 8
N_EXPERT_GROUPS = 8
TOPK_GROUPS = 4
ROUTED_SCALE = 2.5
FOURIER_GROUPS = 4
LN_EPS = 1e-5

LANE = 128
VMEM_LIMIT = 56 * 1024 * 1024

HIGHEST = lax.Precision.HIGHEST
NEG_INF = float("-inf")

FFN_BLOCK = 512
DISPATCH_TILE = 512
COMBINE_TILE = 256
ISSUE_UNROLL = 4


def _cparams(sem):
    return pltpu.CompilerParams(dimension_semantics=sem, vmem_limit_bytes=VMEM_LIMIT)


def _const_spec(shape):
    nd = len(shape)
    return pl.BlockSpec(shape, lambda *_: (0,) * nd, pipeline_mode=pl.Buffered(1))


def _standardize(x):
    mu = jnp.mean(x, axis=-1, keepdims=True)
    xc = x - mu
    var = jnp.mean(xc * xc, axis=-1, keepdims=True)
    return xc * lax.rsqrt(var + LN_EPS)


def _silu(x):
    return x * jax.nn.sigmoid(x)


def _gelu_tanh(x):
    return 0.5 * x * (1.0 + jnp.tanh(math.sqrt(2.0 / math.pi) * (x + 0.044715 * (x * x * x))))


def _adaln_kernel(c_ref, w_ref, b_ref, o_ref):
    a = _silu(c_ref[...])
    o_ref[...] = jnp.dot(a, w_ref[...], precision=HIGHEST, preferred_element_type=F32) + b_ref[...]


def _adaln(c, w, b):
    bsz, d = c.shape
    n = w.shape[1]
    tn = 512
    return pl.pallas_call(
        _adaln_kernel,
        grid=(n // tn,),
        in_specs=[pl.BlockSpec((bsz, d), lambda j: (0, 0)),
                  pl.BlockSpec((d, tn), lambda j: (0, j)),
                  pl.BlockSpec((1, tn), lambda j: (0, j))],
        out_specs=pl.BlockSpec((bsz, tn), lambda j: (0, j)),
        out_shape=jax.ShapeDtypeStruct((bsz, n), F32),
        compiler_params=_cparams(("parallel",)),
        name="adaln",
    )(c, w, b.reshape(1, n))


def _proj_kernel(x_ref, sc_ref, sh_ref, ws5t_ref, bs5_ref, wf_ref, bf_ref, cs_ref, wg_ref, bg_ref,
                 us5_ref, z_ref, gates_ref):
    tm = x_ref.shape[0]
    u = (_standardize(x_ref[...]) * (1.0 + sc_ref[0]) + sh_ref[0]).astype(BF16)
    p = lax.dot_general(ws5t_ref[...], u, (((1,), (1,)), ((), ())), preferred_element_type=F32)
    p = p + bs5_ref[:, 0:1]
    d_s5, n_j, _ = us5_ref.shape
    us5_2d = us5_ref.reshape(d_s5 * n_j, LANE)
    for j in range(n_j):
        us5_2d[pl.ds(j, d_s5, stride=n_j), :] = p[:, j * LANE:(j + 1) * LANE]
    uf = (jnp.dot(u, wf_ref[...], preferred_element_type=F32) + bf_ref[...]).astype(BF16)
    d_f = uf.shape[1]
    fw = d_f // FOURIER_GROUPS
    for q in range(FOURIER_GROUPS):
        zq = jnp.dot(uf[:, q * fw:(q + 1) * fw], cs_ref[...], preferred_element_type=F32)
        z_ref[:, q * fw:(q + 1) * fw] = zq[:, :fw].astype(BF16)
        z_ref[:, d_f + q * fw:d_f + (q + 1) * fw] = zq[:, fw:].astype(BF16)
    n_g = wg_ref.shape[1]
    half = n_g // 2
    for q in range(2):
        gp = jnp.dot(u, wg_ref[:, q * half:(q + 1) * half], preferred_element_type=F32)
        gp = gp + bg_ref[:, q * half:(q + 1) * half]
        gates_ref[:, q * half:(q + 1) * half] = jax.nn.sigmoid(gp).astype(BF16)


def _proj(x2, sc, sh, ws5t, bs5, wf, bf, cs, wg, bg, seq):
    t, d = x2.shape
    d_s5 = ws5t.shape[0]
    d_f = wf.shape[1]
    n_g = wg.shape[1]
    tm = 1024
    tpb = seq // tm
    bsz = t // seq
    return pl.pallas_call(
        _proj_kernel,
        grid=(t // tm,),
        in_specs=[pl.BlockSpec((tm, d), lambda i: (i, 0)),
                  pl.BlockSpec((1, 1, d), lambda i: (i // tpb, 0, 0)),
                  pl.BlockSpec((1, 1, d), lambda i: (i // tpb, 0, 0)),
                  _const_spec((d_s5, d)), _const_spec((d_s5, LANE)),
                  _const_spec((d, d_f)), _const_spec((1, d_f)),
                  _const_spec(cs.shape),
                  _const_spec((d, n_g)), _const_spec((1, n_g))],
        out_specs=[pl.BlockSpec((d_s5, tm // LANE, LANE), lambda i: (0, i, 0)),
                   pl.BlockSpec((tm, 2 * d_f), lambda i: (i, 0)),
                   pl.BlockSpec((tm, n_g), lambda i: (i, 0))],
        out_shape=[jax.ShapeDtypeStruct((d_s5, t // LANE, LANE), F32),
                   jax.ShapeDtypeStruct((t, 2 * d_f), BF16),
                   jax.ShapeDtypeStruct((t, n_g), BF16)],
        compiler_params=_cparams(("parallel",)),
        name="proj",
    )(x2, sc.reshape(bsz, 1, d), sh.reshape(bsz, 1, d), ws5t, bs5, wf, bf, cs, wg, bg)


def _s5_tables(lam_re, lam_im, log_dt, b_re, b_im, c_re, c_im, d_skip):
    L = LANE
    hp = HIGHEST
    lr, li = lam_re.astype(F32), lam_im.astype(F32)
    dt = jnp.exp(log_dt.astype(F32))[:, :, None]
    mag = jnp.exp(lr * dt)
    ang = li * dt
    ab_re, ab_im = mag * jnp.cos(ang), mag * jnp.sin(ang)
    den = lr * lr + li * li
    nr = ab_re - 1.0
    coef_re = (nr * lr + ab_im * li) / den
    coef_im = (ab_im * lr - nr * li) / den
    br, bi = b_re.astype(F32), b_im.astype(F32)
    bb_re = coef_re[..., None] * br - coef_im[..., None] * bi
    bb_im = coef_re[..., None] * bi + coef_im[..., None] * br
    cr, ci = c_re.astype(F32), c_im.astype(F32)
    n_g, n_p, n_h = br.shape[1], br.shape[2], br.shape[3]

    k = jnp.arange(L + 1, dtype=F32)[None, None, :, None]
    pmag = jnp.exp(k * (lr * dt)[:, :, None, :])
    pang = k * (li * dt)[:, :, None, :]
    pw_re, pw_im = pmag * jnp.cos(pang), pmag * jnp.sin(pang)

    m_re = cr[:, :, :, None, :] * jnp.swapaxes(bb_re, 2, 3)[:, :, None, :, :] \
        - ci[:, :, :, None, :] * jnp.swapaxes(bb_im, 2, 3)[:, :, None, :, :]
    m_im = cr[:, :, :, None, :] * jnp.swapaxes(bb_im, 2, 3)[:, :, None, :, :] \
        + ci[:, :, :, None, :] * jnp.swapaxes(bb_re, 2, 3)[:, :, None, :, :]
    kap = jnp.einsum("dgohp,dgkp->dgohk", m_re, pw_re, precision=hp) \
        - jnp.einsum("dgohp,dgkp->dgohk", m_im, pw_im, precision=hp)
    kb = kap[1]
    kpos = kap[0, ..., :L].at[..., 0].add(kb[..., 0]).reshape(n_g, n_h * n_h, L)
    kneg = jnp.concatenate([jnp.zeros_like(kb[..., 0:1]), kb[..., L - 1:0:-1]], axis=-1)
    kneg = kneg.reshape(n_g, n_h * n_h, L)

    cat = lambda *parts: jnp.concatenate(parts, axis=-1)
    pf_re, pf_im = pw_re[0, :, L - 1::-1][:, :L], pw_im[0, :, L - 1::-1][:, :L]
    pb_re, pb_im = pw_re[1, :, :L], pw_im[1, :, :L]
    sp, spsw = cat(pf_re, pf_im, pb_re, pb_im), cat(pf_im, pf_re, pb_im, pb_re)
    bt_re, bt_im = jnp.swapaxes(bb_re, 2, 3), jnp.swapaxes(bb_im, 2, 3)
    bx, by = cat(bt_re[0], bt_re[0], bt_re[1], bt_re[1]), cat(-bt_im[0], bt_im[0], -bt_im[1], bt_im[1])

    qf_re, qf_im = pw_re[0, :, 1:L + 1], pw_im[0, :, 1:L + 1]
    qb_re, qb_im = pw_re[1, :, L:0:-1], pw_im[1, :, L:0:-1]
    qt, qtsw = cat(qf_re, qf_im, qb_re, qb_im), cat(qf_im, qf_re, qb_im, qb_re)
    cx, cy = cat(cr[0], -cr[0], cr[1], -cr[1]), cat(-ci[0], -ci[0], -ci[1], -ci[1])

    al_re, al_im = pw_re[:, :, L], pw_im[:, :, L]
    al = jnp.stack([cat(al_re[0], al_re[0], al_re[1], al_re[1]),
                    cat(-al_im[0], al_im[0], -al_im[1], al_im[1])], axis=1)

    dsk = jnp.broadcast_to(d_skip.astype(F32).reshape(n_g, n_h, 1), (n_g, n_h, L))
    return kpos, kneg, sp, spsw, bx, by, qt, qtsw, cx, cy, al, dsk


def _s5_kernel(a_ref, kpos_ref, kneg_ref, sp_ref, spsw_ref, bx_ref, by_ref, qt_ref, qtsw_ref, cx_ref, cy_ref,
               al_ref, d_ref, y_ref, abf_ref, tp_ref, ws_ref, sf_ref, sb_ref, sfs_ref, sbs_ref, xf_ref, xb_ref,
               *, n_chunk):
    n_h, r, L = a_ref.shape
    bsz = r // n_chunk
    half = sp_ref.shape[2] // 2
    nt = (((1,), (1,)), ((), ()))

    sp, spsw = sp_ref[0], spsw_ref[0]
    for h in range(n_h):
        abf_ref[:, h * L:(h + 1) * L] = a_ref[h].astype(BF16)
        ws_ref[h * L:(h + 1) * L, :] = (sp * bx_ref[0, h:h + 1, :] + spsw * by_ref[0, h:h + 1, :]).astype(BF16)
    abf = abf_ref[...]

    s_all = jnp.dot(abf, ws_ref[...], preferred_element_type=F32)
    sf_ref[...] = s_all[:, :half]
    sb_ref[...] = s_all[:, half:]
    sfs_ref[...] = pltpu.roll(s_all[:, :half], half // 2, 1)
    sbs_ref[...] = pltpu.roll(s_all[:, half:], half // 2, 1)

    alx = al_ref[0, 0:1, :]
    aly = al_ref[0, 1:2, :]

    zero = jnp.zeros((bsz, half), F32)
    ef, efs, eb, ebs = zero, zero, zero, zero
    for c in range(n_chunk):
        cb = n_chunk - 1 - c
        rows_f = pl.ds(c, bsz, stride=n_chunk)
        rows_b = pl.ds(cb, bsz, stride=n_chunk)
        xf_ref[rows_f, :] = ef
        xb_ref[rows_b, :] = eb
        xf, yf = alx[:, :half], aly[:, :half]
        xb, yb = alx[:, half:], aly[:, half:]
        ef, efs = ef * xf + efs * yf + sf_ref[rows_f, :], efs * xf - ef * yf + sfs_ref[rows_f, :]
        eb, ebs = eb * xb + ebs * yb + sb_ref[rows_b, :], ebs * xb - eb * yb + sbs_ref[rows_b, :]
    xin = jnp.concatenate([xf_ref[...], xb_ref[...]], axis=1).astype(BF16)

    s_idx = lax.broadcasted_iota(I32, (L, L), 0)
    j_idx = lax.broadcasted_iota(I32, (L, L), 1)
    fwd_part = j_idx + s_idx < L
    qt, qtsw = qt_ref[0], qtsw_ref[0]

    def pair(op, _):
        for oo in range(2):
            o = 2 * op + oo
            for h in range(n_h):
                row = o * n_h + h
                kp = jnp.broadcast_to(kpos_ref[0, pl.ds(row, 1), :], (L, L))
                kn = jnp.broadcast_to(kneg_ref[0, pl.ds(row, 1), :], (L, L))
                tile = pltpu.roll(jnp.where(fwd_part, kp, kn), 0, 1, stride=1, stride_axis=0)
                tp_ref[h * L:(h + 1) * L, oo * L:(oo + 1) * L] = tile.astype(BF16)
        wo_t = jnp.concatenate(
            [qt * cx_ref[0, pl.ds(2 * op + oo, 1), :] + qtsw * cy_ref[0, pl.ds(2 * op + oo, 1), :]
             for oo in range(2)], axis=0).astype(BF16)
        yp = jnp.dot(abf, tp_ref[...], preferred_element_type=F32)
        yp = yp + lax.dot_general(xin, wo_t, nt, preferred_element_type=F32)
        for oo in range(2):
            o = 2 * op + oo
            y_ref[o] = yp[:, oo * L:(oo + 1) * L] + a_ref[o] * d_ref[0, pl.ds(o, 1), :]
        return 0

    lax.fori_loop(0, n_h // 2, pair, 0)


def _s5(us5, tables, n_chunk):
    kpos, kneg, sp, spsw, bx, by, qt, qtsw, cx, cy, al, dsk = tables
    d_s5, r, L = us5.shape
    n_g = kpos.shape[0]
    n_h = d_s5 // n_g
    n_st = sp.shape[2]
    g3 = lambda g: (g, 0, 0)
    pw_spec = pl.BlockSpec((1, L, n_st), g3)
    hv_spec = pl.BlockSpec((1, n_h, n_st), g3)
    return pl.pallas_call(
        functools.partial(_s5_kernel, n_chunk=n_chunk),
        grid=(n_g,),
        in_specs=[pl.BlockSpec((n_h, r, L), g3),
                  pl.BlockSpec((1, n_h * n_h, L), g3), pl.BlockSpec((1, n_h * n_h, L), g3),
                  pw_spec, pw_spec, hv_spec, hv_spec, pw_spec, pw_spec, hv_spec, hv_spec,
                  pl.BlockSpec((1, 2, n_st), g3), pl.BlockSpec((1, n_h, L), g3)],
        out_specs=pl.BlockSpec((n_h, r, L), g3),
        out_shape=jax.ShapeDtypeStruct((d_s5, r, L), F32),
        scratch_shapes=[pltpu.VMEM((r, n_h * L), BF16), pltpu.VMEM((n_h * L, 2 * L), BF16),
                        pltpu.VMEM((n_h * L, n_st), BF16)]
        + [pltpu.VMEM((r, n_st // 2), F32)] * 6,
        compiler_params=_cparams(("parallel",)),
        name="s5",
    )(us5, kpos, kneg, sp, spsw, bx, by, qt, qtsw, cx, cy, al, dsk)


DFT_SPLIT = 64


def _dft_tables(seq, fw):
    def angles(mult, n_rows, n):
        s = jnp.arange(n, dtype=I32)[None, :]
        q = jnp.arange(n_rows, dtype=I32)[:, None]
        return (2.0 * math.pi / n) * ((mult * q * s) % n).astype(F32)

    ang_a = angles(DFT_SPLIT, seq // DFT_SPLIT, seq)
    ang_b = angles(1, DFT_SPLIT, seq)
    seq_tabs = (jnp.cos(ang_a), jnp.sin(ang_a), jnp.cos(ang_b), jnp.sin(ang_b))
    ang_c = angles(1, fw, fw)
    return seq_tabs, jnp.concatenate([jnp.cos(ang_c), jnp.sin(ang_c)], axis=1).astype(BF16)


def _seqdft_kernel(ca_ref, sa_ref, cb_ref, sb_ref, z_ref, o_ref, f_ref, *, scale):
    seq = z_ref.shape[0]
    d_f = o_ref.shape[1]
    tk = o_ref.shape[0]

    @pl.when(pl.program_id(1) == 0)
    def _():
        a0 = pl.program_id(0) * (tk // DFT_SPLIT)
        cb, sb = cb_ref[...], sb_ref[...]
        for j in range(tk // DFT_SPLIT):
            ca = ca_ref[pl.ds(a0 + j, 1), :]
            sa = sa_ref[pl.ds(a0 + j, 1), :]
            rows = slice(j * DFT_SPLIT, (j + 1) * DFT_SPLIT)
            f_ref[rows, :seq] = (ca * cb - sa * sb).astype(BF16)
            f_ref[rows, seq:] = (-(sa * cb + ca * sb)).astype(BF16)

    acc = jnp.dot(f_ref[:, :seq], z_ref[:, :d_f], preferred_element_type=F32)
    acc = acc + jnp.dot(f_ref[:, seq:], z_ref[:, d_f:], preferred_element_type=F32)
    o_ref[...] = (acc * scale).astype(o_ref.dtype)


def _seqdft(seq_tabs, z, seq, fw):
    t, two_df = z.shape
    d_f = two_df // 2
    bsz = t // seq
    tk = 512
    nk = seq // tk
    scale = 1.0 / math.sqrt(seq * fw)
    return pl.pallas_call(
        functools.partial(_seqdft_kernel, scale=scale),
        grid=(nk, bsz),
        in_specs=[_const_spec(tab.shape) for tab in seq_tabs]
        + [pl.BlockSpec((seq, two_df), lambda k, b: (b, 0))],
        out_specs=pl.BlockSpec((tk, d_f), lambda k, b: (b * nk + k, 0)),
        out_shape=jax.ShapeDtypeStruct((t, d_f), BF16),
        scratch_shapes=[pltpu.VMEM((tk, 2 * seq), BF16)],
        compiler_params=_cparams(("parallel", "arbitrary")),
        name="seqdft",
    )(*seq_tabs, z)


def _glu_kernel(y_ref, w_ref, b_ref, o_ref, zt_ref):
    d_s5, n_j, _ = y_ref.shape
    d = o_ref.shape[1]
    y2 = y_ref.reshape(d_s5 * n_j, LANE)
    for j in range(n_j):
        zt_ref[j * LANE:(j + 1) * LANE, :] = _gelu_tanh(y2[pl.ds(j, d_s5, stride=n_j), :]).T.astype(BF16)
    zt = zt_ref[...]
    a = jnp.dot(zt, w_ref[:, :d], preferred_element_type=F32) + b_ref[:, :d]
    g = jnp.dot(zt, w_ref[:, d:], preferred_element_type=F32) + b_ref[:, d:]
    o_ref[...] = (a * jax.nn.sigmoid(g)).astype(BF16)


def _glu(ys5, w, b):
    d_s5, r, L = ys5.shape
    t = r * L
    n = w.shape[1]
    tm = 1024
    return pl.pallas_call(
        _glu_kernel,
        grid=(t // tm,),
        in_specs=[pl.BlockSpec((d_s5, tm // L, L), lambda i: (0, i, 0)),
                  _const_spec((d_s5, n)), _const_spec((1, n))],
        out_specs=pl.BlockSpec((tm, n // 2), lambda i: (i, 0)),
        out_shape=jax.ShapeDtypeStruct((t, n // 2), BF16),
        scratch_shapes=[pltpu.VMEM((tm, d_s5), BF16)],
        compiler_params=_cparams(("parallel",)),
        name="glu",
    )(ys5, w, b)


U32 = jnp.uint32
PACK_SUB = 4


def _row_slab(ref, r):
    return ref.at[pl.ds(pl.multiple_of(r * PACK_SUB, PACK_SUB), PACK_SUB), :]


def _pack_rows(ref, v, first=0):
    rows, d = v.shape
    half = d // 2
    bits = lax.bitcast_convert_type(v.astype(BF16).astype(F32), U32)
    for c in range(PACK_SUB):
        lo = bits[:, c * LANE:(c + 1) * LANE] >> 16
        hi = bits[:, half + c * LANE:half + (c + 1) * LANE] & jnp.uint32(0xFFFF0000)
        ref[pl.ds(first * PACK_SUB + c, rows, stride=PACK_SUB), :] = hi | lo


def _unpack_rows(ref, rows, first=0):
    los, his = [], []
    for c in range(PACK_SUB):
        w = ref[pl.ds(first * PACK_SUB + c, rows, stride=PACK_SUB), :]
        los.append(lax.bitcast_convert_type(w << 16, F32))
        his.append(lax.bitcast_convert_type(w & jnp.uint32(0xFFFF0000), F32))
    return jnp.concatenate(los + his, axis=1)


def _merge_kernel(x_ref, brs_ref, yf_ref, gates_ref, g1_ref, sc_ref, sh_ref, wfo_ref, bfo_ref,
                  wo_ref, bo_ref, lng_ref, lnb_ref, x1_ref, hhi_ref, hlo_ref, hp_ref, *, alpha):
    d = x_ref.shape[1]
    br_f = jnp.dot(yf_ref[...], wfo_ref[...], preferred_element_type=F32) + bfo_ref[...]
    merged = gates_ref[:, :d].astype(F32) * brs_ref[...].astype(F32) + gates_ref[:, d:].astype(F32) * br_f
    mix = jnp.dot(merged.astype(BF16), wo_ref[...], preferred_element_type=F32) + bo_ref[...]
    v = alpha * x_ref[...] + g1_ref[0] * mix
    x1 = _standardize(v) * lng_ref[...] + lnb_ref[...]
    x1_ref[...] = x1
    h = _standardize(x1) * (1.0 + sc_ref[0]) + sh_ref[0]
    hhi = h.astype(BF16)
    hhi_ref[...] = hhi
    hlo_ref[...] = (h - hhi.astype(F32)).astype(BF16)
    _pack_rows(hp_ref, h)


def _merge(x2, brs, yf, gates, g1, sc2, sh2, wfo, bfo, wo, bo, lng, lnb, seq, alpha):
    t, d = x2.shape
    d_f = yf.shape[1]
    tm = 512
    tpb = seq // tm
    bsz = t // seq
    row = lambda i: (i, 0)
    bat = lambda i: (i // tpb, 0, 0)
    return pl.pallas_call(
        functools.partial(_merge_kernel, alpha=alpha),
        grid=(t // tm,),
        in_specs=[pl.BlockSpec((tm, d), row), pl.BlockSpec((tm, d), row), pl.BlockSpec((tm, d_f), row),
                  pl.BlockSpec((tm, 2 * d), row),
                  pl.BlockSpec((1, 1, d), bat), pl.BlockSpec((1, 1, d), bat), pl.BlockSpec((1, 1, d), bat),
                  _const_spec((d_f, d)), _const_spec((1, d)), _const_spec((d, d)), _const_spec((1, d)),
                  _const_spec((1, d)), _const_spec((1, d))],
        out_specs=[pl.BlockSpec((tm, d), row), pl.BlockSpec((tm, d), row), pl.BlockSpec((tm, d), row),
                   pl.BlockSpec((tm * PACK_SUB, LANE), row)],
        out_shape=[jax.ShapeDtypeStruct((t, d), F32), jax.ShapeDtypeStruct((t, d), BF16),
                   jax.ShapeDtypeStruct((t, d), BF16), jax.ShapeDtypeStruct((t * PACK_SUB, LANE), U32)],
        compiler_params=_cparams(("parallel",)),
        name="merge",
    )(x2, brs, yf, gates, g1.reshape(bsz, 1, d), sc2.reshape(bsz, 1, d), sh2.reshape(bsz, 1, d),
      wfo, bfo, wo, bo, lng, lnb)


def _router_kernel(hhi_ref, hlo_ref, whi_ref, wlo_ref, bias_ref, tri_ref,
                   eidx_ref, gate_ref, rank_ref, cnt_ref, base_ref):
    n_e = whi_ref.shape[0]
    tm = hhi_ref.shape[0]
    gsz = n_e // N_EXPERT_GROUPS
    nt = (((1,), (1,)), ((), ()))

    @pl.when(pl.program_id(0) == 0)
    def _():
        base_ref[...] = jnp.zeros_like(base_ref)

    hhi = hhi_ref[...]
    logits = lax.dot_general(whi_ref[...], hhi, nt, preferred_element_type=F32)
    logits = logits + lax.dot_general(wlo_ref[...], hhi, nt, preferred_element_type=F32)
    logits = logits + lax.dot_general(whi_ref[...], hlo_ref[...], nt, preferred_element_type=F32)
    scores = jax.nn.sigmoid(logits)
    sel = scores + bias_ref[:, 0:1]

    g3 = sel.reshape(N_EXPERT_GROUPS, gsz, tm)
    i3 = lax.broadcasted_iota(I32, g3.shape, 1).astype(F32)
    m1 = jnp.max(g3, axis=1, keepdims=True)
    first = jnp.min(jnp.where(g3 == m1, i3, float(gsz)), axis=1, keepdims=True)
    m2 = jnp.max(jnp.where(i3 == first, NEG_INF, g3), axis=1, keepdims=True)
    gs = (m1 + m2).reshape(N_EXPERT_GROUPS, tm)

    gi = lax.broadcasted_iota(I32, gs.shape, 0).astype(F32)
    gsel = jnp.zeros(gs.shape, F32)
    cur = gs
    for _ in range(TOPK_GROUPS):
        m = jnp.max(cur, axis=0, keepdims=True)
        f = jnp.min(jnp.where(cur == m, gi, float(N_EXPERT_GROUPS)), axis=0, keepdims=True)
        pick = gi == f
        gsel = jnp.where(pick, 1.0, gsel)
        cur = jnp.where(pick, NEG_INF, cur)
    gmask = jnp.broadcast_to(gsel.reshape(N_EXPERT_GROUPS, 1, tm), g3.shape).reshape(n_e, tm)
    masked = jnp.where(gmask > 0.5, sel, NEG_INF)

    ri = lax.broadcasted_iota(I32, (n_e, tm), 0).astype(F32)
    picks = []
    gates = []
    multihot = jnp.zeros((n_e, tm), F32)
    for _ in range(TOP_K):
        m = jnp.max(masked, axis=0, keepdims=True)
        f = jnp.min(jnp.where(masked == m, ri, float(n_e)), axis=0, keepdims=True)
        pick = ri == f
        picks.append(f)
        gates.append(jnp.sum(jnp.where(pick, scores, 0.0), axis=0, keepdims=True))
        multihot = jnp.where(pick, 1.0, multihot)
        masked = jnp.where(pick, NEG_INF, masked)
    gsum = gates[0]
    for g in gates[1:]:
        gsum = gsum + g

    rankmat = jnp.dot(multihot.astype(BF16), tri_ref[...], preferred_element_type=F32) + base_ref[:, 0:1]
    for k in range(TOP_K):
        pick = ri == picks[k]
        eidx_ref[k:k + 1, :] = picks[k].astype(I32)
        gate_ref[k:k + 1, :] = gates[k] / gsum * ROUTED_SCALE
        rank_ref[k:k + 1, :] = jnp.sum(jnp.where(pick, rankmat, 0.0), axis=0, keepdims=True).astype(I32)
    base_ref[...] = base_ref[...] + jnp.sum(multihot, axis=1, keepdims=True)
    cnt_ref[...] = base_ref[...].astype(I32)


def _router(hhi, hlo, wrt_hi, wrt_lo, bias):
    t, d = hhi.shape
    n_e = wrt_hi.shape[0]
    tm = 512
    tri = (jnp.arange(tm)[:, None] < jnp.arange(tm)[None, :]).astype(BF16)
    kt = lambda i: (0, i)
    return pl.pallas_call(
        _router_kernel,
        grid=(t // tm,),
        in_specs=[pl.BlockSpec((tm, d), lambda i: (i, 0)), pl.BlockSpec((tm, d), lambda i: (i, 0)),
                  _const_spec((n_e, d)), _const_spec((n_e, d)), _const_spec((n_e, LANE)),
                  _const_spec((tm, tm))],
        out_specs=[pl.BlockSpec((TOP_K, tm), kt), pl.BlockSpec((TOP_K, tm), kt), pl.BlockSpec((TOP_K, tm), kt),
                   pl.BlockSpec((n_e, LANE), lambda i: (0, 0))],
        out_shape=[jax.ShapeDtypeStruct((TOP_K, t), I32), jax.ShapeDtypeStruct((TOP_K, t), F32),
                   jax.ShapeDtypeStruct((TOP_K, t), I32), jax.ShapeDtypeStruct((n_e, LANE), I32)],
        scratch_shapes=[pltpu.VMEM((n_e, LANE), F32)],
        compiler_params=_cparams(("arbitrary",)),
        name="router",
    )(hhi, hlo, wrt_hi, wrt_lo, jnp.broadcast_to(bias.astype(F32).reshape(n_e, 1), (n_e, LANE)), tri)


def _dest_kernel(eidx_ref, rank_ref, pstart_ref, dest_ref):
    n_e = pstart_ref.shape[0]
    tm = eidx_ref.shape[1]
    ri = lax.broadcasted_iota(I32, (n_e, tm), 0)
    ps = pstart_ref[:, 0:1].astype(F32)
    for k in range(TOP_K):
        hit = ri == eidx_ref[k:k + 1, :]
        base = jnp.sum(jnp.where(hit, ps, 0.0), axis=0, keepdims=True)
        dest_ref[k:k + 1, :] = base.astype(I32) + rank_ref[k:k + 1, :]


def _dest(eidx, rank, pstart):
    k, t = eidx.shape
    n_e = pstart.shape[0]
    tm = 2048
    kt = lambda i: (0, i)
    return pl.pallas_call(
        _dest_kernel,
        grid=(t // tm,),
        in_specs=[pl.BlockSpec((k, tm), kt), pl.BlockSpec((k, tm), kt), _const_spec((n_e, LANE))],
        out_specs=pl.BlockSpec((k, tm), kt),
        out_shape=jax.ShapeDtypeStruct((k, t), I32),
        compiler_params=_cparams(("parallel",)),
        name="dest",
    )(eidx, rank, jnp.broadcast_to(pstart.astype(I32).reshape(n_e, 1), (n_e, LANE)))


def _zero_fill(pad_lo_ref, pad_n_ref, zeros_ref, rows_ref, sem):
    def each_copy(act):
        def per_entry(e, _):
            lo = pad_lo_ref[e]
            n = pad_n_ref[e]
            n_full = n // FFN_BLOCK

            def full(j, _):
                act(pltpu.make_async_copy(
                    zeros_ref, rows_ref.at[pl.ds(pl.multiple_of((lo + j * FFN_BLOCK) * PACK_SUB, PACK_SUB),
                                                 FFN_BLOCK * PACK_SUB), :], sem))
                return 0

            lax.fori_loop(0, n_full, full, 0)
            off = lo + n_full * FFN_BLOCK
            rem = n - n_full * FFN_BLOCK
            bit = FFN_BLOCK // 2
            while bit >= 1:
                take = rem & bit

                @pl.when(take != 0)
                def _(off=off, bit=bit):
                    act(pltpu.make_async_copy(
                        zeros_ref.at[pl.ds(0, bit * PACK_SUB), :],
                        rows_ref.at[pl.ds(pl.multiple_of(off * PACK_SUB, PACK_SUB), bit * PACK_SUB), :], sem))

                off = off + take
                bit //= 2
            return 0

        lax.fori_loop(0, pad_lo_ref.shape[0], per_entry, 0)

    each_copy(lambda cp: cp.start())
    each_copy(lambda cp: cp.wait())


def _dispatch_kernel(pad_lo_ref, pad_n_ref, dest_ref, hp_ref, rows_ref, zeros_ref, sem, zsem):
    tm = hp_ref.shape[0] // PACK_SUB

    @pl.when(pl.program_id(0) == 0)
    def _():
        zeros_ref[...] = jnp.zeros_like(zeros_ref)
        _zero_fill(pad_lo_ref, pad_n_ref, zeros_ref, rows_ref, zsem)

    def body(tt, _):
        for k in range(TOP_K):
            pltpu.make_async_copy(_row_slab(hp_ref, tt), _row_slab(rows_ref, dest_ref[0, 0, k * tm + tt]),
                                  sem).start(priority=k % 2)
        return 0

    lax.fori_loop(0, tm, body, 0, unroll=ISSUE_UNROLL)
    all_rows = rows_ref.at[pl.ds(0, TOP_K * tm * PACK_SUB), :]
    pltpu.make_async_copy(all_rows, all_rows, sem).wait()


def _dispatch(hp, dest_tiles, pad_lo, pad_n, n_rows):
    n_tile, _, per_tile = dest_tiles.shape
    tm = per_tile // TOP_K
    return pl.pallas_call(
        _dispatch_kernel,
        grid_spec=pltpu.PrefetchScalarGridSpec(
            num_scalar_prefetch=2,
            grid=(n_tile,),
            in_specs=[pl.BlockSpec((1, 1, per_tile), lambda i, lo, n: (i, 0, 0), memory_space=pltpu.SMEM),
                      pl.BlockSpec((tm * PACK_SUB, LANE), lambda i, lo, n: (i, 0))],
            out_specs=pl.BlockSpec(memory_space=pl.ANY),
            scratch_shapes=[pltpu.VMEM((FFN_BLOCK * PACK_SUB, LANE), U32),
                            pltpu.SemaphoreType.DMA, pltpu.SemaphoreType.DMA],
        ),
        out_shape=jax.ShapeDtypeStruct((n_rows * PACK_SUB, LANE), U32),
        compiler_params=_cparams(("arbitrary",)),
        name="dispatch",
    )(pad_lo, pad_n, dest_tiles, hp)


def _ffn_kernel(blk_e_ref, n_used_ref, next_blk_ref, rows_ref, wg_ref, wu_ref, wd_ref, y_ref,
                wg_st_ref, wu_st_ref, wd_st_ref, wgu_bf_ref, wd_bf_ref, sems):
    b = pl.program_id(0)
    n_used = n_used_ref[0]
    live = b < n_used
    d_e = wd_st_ref.shape[0]
    e = blk_e_ref[b]

    def fetch(expert):
        return (pltpu.make_async_copy(wg_ref.at[expert], wg_st_ref, sems.at[0]),
                pltpu.make_async_copy(wu_ref.at[expert], wu_st_ref, sems.at[1]),
                pltpu.make_async_copy(wd_ref.at[expert], wd_st_ref, sems.at[2]))

    @pl.when(b == 0)
    def _():
        for cp in fetch(e):
            cp.start()

    first_block = jnp.logical_and(live, jnp.logical_or(b == 0, e != blk_e_ref[jnp.maximum(b - 1, 0)]))

    @pl.when(first_block)
    def _():
        for cp in fetch(e):
            cp.wait()
        wgu_bf_ref[:, :d_e] = wg_st_ref[...].astype(BF16)
        wgu_bf_ref[:, d_e:] = wu_st_ref[...].astype(BF16)
        wd_bf_ref[...] = wd_st_ref[...].astype(BF16)
        nb = next_blk_ref[e]

        @pl.when(nb < n_used)
        def _():
            for cp in fetch(blk_e_ref[nb]):
                cp.start()

    @pl.when(live)
    def _():
        x = _unpack_rows(rows_ref, FFN_BLOCK).astype(BF16)
        au = jnp.dot(x, wgu_bf_ref[...], preferred_element_type=F32)
        hid = (_silu(au[:, :d_e]) * au[:, d_e:]).astype(BF16)
        _pack_rows(y_ref, jnp.dot(hid, wd_bf_ref[...], preferred_element_type=F32))

    @pl.when(jnp.logical_not(live))
    def _():
        y_ref[...] = jnp.zeros_like(y_ref)


def _ffn(rows, blk_e, n_used, next_blk, wg, wu, wd):
    n_rows = rows.shape[0] // PACK_SUB
    n_e, d, d_e = wg.shape
    n_blk = n_rows // FFN_BLOCK
    blk = lambda b, *_: (b, 0)
    hbm = pl.BlockSpec(memory_space=pl.ANY)
    return pl.pallas_call(
        _ffn_kernel,
        grid_spec=pltpu.PrefetchScalarGridSpec(
            num_scalar_prefetch=3,
            grid=(n_blk,),
            in_specs=[pl.BlockSpec((FFN_BLOCK * PACK_SUB, LANE), blk), hbm, hbm, hbm],
            out_specs=pl.BlockSpec((FFN_BLOCK * PACK_SUB, LANE), blk),
            scratch_shapes=[pltpu.VMEM((d, d_e), F32), pltpu.VMEM((d, d_e), F32), pltpu.VMEM((d_e, d), F32),
                            pltpu.VMEM((d, 2 * d_e), BF16), pltpu.VMEM((d_e, d), BF16),
                            pltpu.SemaphoreType.DMA((3,))],
        ),
        out_shape=jax.ShapeDtypeStruct((n_rows * PACK_SUB, LANE), U32),
        compiler_params=_cparams(("arbitrary",)),
        name="ffn",
    )(blk_e, n_used, next_blk, rows, wg, wu, wd)


def _final_kernel(dest_ref, dnext_ref, x1_ref, h_ref, gate_ref, g2_ref, wsg_ref, wsu_ref, wsd_ref, lng_ref, lnb_ref,
                  y_ref, o_ref, ybuf_ref, sems, *, alpha):
    i = pl.program_id(0)
    tm = x1_ref.shape[0]
    per_tile = TOP_K * tm

    def gather(d_ref, slot):
        def body(tt, _):
            for k in range(TOP_K):
                pltpu.make_async_copy(_row_slab(y_ref, d_ref[0, 0, k * tm + tt]),
                                      _row_slab(ybuf_ref, slot * per_tile + k * tm + tt),
                                      sems.at[slot]).start(priority=k % 2)
            return 0

        lax.fori_loop(0, tm, body, 0, unroll=ISSUE_UNROLL)

    @pl.when(i == 0)
    def _():
        gather(dest_ref, 0)

    @pl.when(i + 1 < pl.num_programs(0))
    def _():
        gather(dnext_ref, (i + 1) % 2)

    h = h_ref[...]
    a = jnp.dot(h, wsg_ref[...], preferred_element_type=F32)
    u = jnp.dot(h, wsu_ref[...], preferred_element_type=F32)
    shared = jnp.dot((_silu(a) * u).astype(BF16), wsd_ref[...], preferred_element_type=F32)

    slot = i % 2
    half = ybuf_ref.at[pl.ds(pl.multiple_of(slot * per_tile * PACK_SUB, per_tile * PACK_SUB), per_tile * PACK_SUB), :]
    pltpu.make_async_copy(half, half, sems.at[slot]).wait()
    first = slot * per_tile
    routed = gate_ref[:, 0:1] * _unpack_rows(ybuf_ref, tm, first=first)
    for k in range(1, TOP_K):
        routed = routed + gate_ref[:, k:k + 1] * _unpack_rows(ybuf_ref, tm, first=first + k * tm)

    v = alpha * x1_ref[...] + g2_ref[0] * (shared + routed)
    o_ref[...] = _standardize(v) * lng_ref[...] + lnb_ref[...]


def _final(x1, hhi, y_rows, dest_tiles, gate_t, g2, wsg, wsu, wsd, lng, lnb, seq, alpha):
    t, d = x1.shape
    d_sh = wsg.shape[1]
    n_tile, _, per_tile = dest_tiles.shape
    tm = per_tile // TOP_K
    tpb = seq // tm
    bsz = t // seq
    row = lambda i: (i, 0)
    return pl.pallas_call(
        functools.partial(_final_kernel, alpha=alpha),
        grid=(n_tile,),
        in_specs=[pl.BlockSpec((1, 1, per_tile), lambda i: (i, 0, 0), memory_space=pltpu.SMEM),
                  pl.BlockSpec((1, 1, per_tile), lambda i: (jnp.minimum(i + 1, n_tile - 1), 0, 0),
                               memory_space=pltpu.SMEM),
                  pl.BlockSpec((tm, d), row), pl.BlockSpec((tm, d), row), pl.BlockSpec((tm, TOP_K), row),
                  pl.BlockSpec((1, 1, d), lambda i: (i // tpb, 0, 0)),
                  _const_spec((d, d_sh)), _const_spec((d, d_sh)), _const_spec((d_sh, d)),
                  _const_spec((1, d)), _const_spec((1, d)),
                  pl.BlockSpec(memory_space=pl.ANY)],
        out_specs=pl.BlockSpec((tm, d), row),
        out_shape=jax.ShapeDtypeStruct((t, d), F32),
        scratch_shapes=[pltpu.VMEM((2 * TOP_K * tm * PACK_SUB, LANE), U32), pltpu.SemaphoreType.DMA((2,))],
        compiler_params=_cparams(("arbitrary",)),
        name="final",
    )(dest_tiles, dest_tiles, x1, hhi, gate_t, g2.reshape(bsz, 1, d), wsg, wsu, wsd, lng, lnb, y_rows)


def _split_hi_lo(w):
    hi = w.astype(BF16)
    return hi, (w - hi.astype(F32)).astype(BF16)


def kernel(x, c, w_ada, b_ada, w_in, b_in, s5_lambda_re, s5_lambda_im, s5_log_dt, s5_b_re, s5_b_im, s5_c_re, s5_c_im, s5_d, w_s5_glu, b_s5_glu, w_fourier, b_fourier, w_out, b_out, ln1_g, ln1_b, w_router, router_bias, w_exp_gate, w_exp_up, w_exp_down, w_sh_gate, w_sh_up, w_sh_down, ln2_g, ln2_b):
    bsz, seq, d = x.shape
    depth = w_ada.shape[0]
    alpha = (2 * depth) ** 0.25
    t = bsz * seq
    d_s5 = s5_d.shape[1]
    d_f = w_fourier.shape[1]
    fw = d_f // FOURIER_GROUPS
    n_e = w_router.shape[2]
    n_chunk = seq // LANE
    seq_tabs, cmat = _dft_tables(seq, fw)
    row = lambda v: v.astype(F32).reshape(1, -1)

    x2 = x.reshape(t, d)
    for l in range(depth):
        mod = _adaln(c, w_ada[l], b_ada[l])
        sh1, sc1, g1, sh2, sc2, g2 = jnp.split(mod, 6, axis=-1)

        wi = w_in[l]
        bi = b_in[l].astype(F32)
        ws5t = wi[:, :d_s5].T.astype(BF16)
        bs5 = jnp.broadcast_to(bi[:d_s5].reshape(d_s5, 1), (d_s5, LANE))
        us5, z, gates = _proj(x2, sc1, sh1, ws5t, bs5,
                              wi[:, d_s5:d_s5 + d_f].astype(BF16), row(bi[d_s5:d_s5 + d_f]), cmat,
                              wi[:, d_s5 + d_f:].astype(BF16), row(bi[d_s5 + d_f:]), seq)

        tables = _s5_tables(s5_lambda_re[l], s5_lambda_im[l], s5_log_dt[l], s5_b_re[l], s5_b_im[l],
                            s5_c_re[l], s5_c_im[l], s5_d[l])
        ys5 = _s5(us5, tables, n_chunk)
        brs = _glu(ys5, w_s5_glu[l].astype(BF16), row(b_s5_glu[l]))
        yf = _seqdft(seq_tabs, z, seq, fw)

        x1, hhi, hlo, hp = _merge(x2, brs, yf, gates, g1, sc2, sh2,
                              w_fourier[l].astype(BF16), row(b_fourier[l]),
                              w_out[l].astype(BF16), row(b_out[l]), row(ln1_g[l]), row(ln1_b[l]), seq, alpha)

        wrt_hi, wrt_lo = _split_hi_lo(w_router[l].astype(F32).T)
        eidx, gate, rank, cnt = _router(hhi, hlo, wrt_hi, wrt_lo, router_bias[l])

        counts = cnt[:, 0]
        padded = ((counts + FFN_BLOCK - 1) // FFN_BLOCK) * FFN_BLOCK
        pend = jnp.cumsum(padded)
        pstart = (pend - padded).astype(I32)
        dest = _dest(eidx, rank, pstart)
        n_blk = (t * TOP_K + n_e * (FFN_BLOCK - 1) + FFN_BLOCK - 1) // FFN_BLOCK
        n_rows = n_blk * FFN_BLOCK
        blk_start = jnp.arange(n_blk, dtype=I32) * FFN_BLOCK
        blk_e = jnp.minimum(jnp.sum((pend[None, :] <= blk_start[:, None]).astype(I32), axis=1), n_e - 1)
        n_used = (pend[-1:] // FFN_BLOCK).astype(I32)
        pad_lo = jnp.concatenate([pstart + counts, pend[-1:]]).astype(I32)
        pad_n = jnp.concatenate([padded - counts, n_rows - pend[-1:]]).astype(I32)

        def tiles(tm):
            return dest.reshape(TOP_K, t // tm, tm).transpose(1, 0, 2).reshape(t // tm, 1, TOP_K * tm)

        rows = _dispatch(hp, tiles(DISPATCH_TILE), pad_lo, pad_n, n_rows)
        next_blk = (pend // FFN_BLOCK).astype(I32)
        y_rows = _ffn(rows, blk_e, n_used, next_blk, w_exp_gate[l], w_exp_up[l], w_exp_down[l])
        x2 = _final(x1, hhi, y_rows, tiles(COMBINE_TILE), gate.T, g2,
                    w_sh_gate[l].astype(BF16), w_sh_up[l].astype(BF16), w_sh_down[l].astype(BF16),
                    row(ln2_g[l]), row(ln2_b[l]), seq, alpha)
    return x2.reshape(bsz, seq, d)
```

```python
import functools
import math

import jax
import jax.numpy as jnp
from jax import lax
from jax.experimental import pallas as pl
from jax.experimental.pallas import tpu as pltpu

F32 = jnp.float32
BF16 = jnp.bfloat16
I32 = jnp.int32

TOP_K = 8
N_EXPERT_GROUPS = 8
TOPK_GROUPS = 4
ROUTED_SCALE = 2.5
FOURIER_GROUPS = 4
LN_EPS = 1e-5

LANE = 128
VMEM_LIMIT = 56 * 1024 * 1024

HIGHEST = lax.Precision.HIGHEST
NEG_INF = float("-inf")

FFN_BLOCK = 512
DISPATCH_TILE = 512
COMBINE_TILE = 256
ISSUE_UNROLL = 4


def _cparams(sem):
    return pltpu.CompilerParams(dimension_semantics=sem, vmem_limit_bytes=VMEM_LIMIT)


def _const_spec(shape):
    nd = len(shape)
    return pl.BlockSpec(shape, lambda *_: (0,) * nd, pipeline_mode=pl.Buffered(1))


def _standardize(x):
    mu = jnp.mean(x, axis=-1, keepdims=True)
    xc = x - mu
    var = jnp.mean(xc * xc, axis=-1, keepdims=True)
    return xc * lax.rsqrt(var + LN_EPS)


def _silu(x):
    return x * jax.nn.sigmoid(x)


def _gelu_tanh(x):
    return 0.5 * x * (1.0 + jnp.tanh(math.sqrt(2.0 / math.pi) * (x + 0.044715 * (x * x * x))))


def _adaln_kernel(c_ref, w_ref, b_ref, o_ref):
    a = _silu(c_ref[...])
    o_ref[...] = jnp.dot(a, w_ref[...], precision=HIGHEST, preferred_element_type=F32) + b_ref[...]


def _adaln(c, w, b):
    bsz, d = c.shape
    n = w.shape[1]
    tn = 512
    return pl.pallas_call(
        _adaln_kernel,
        grid=(n // tn,),
        in_specs=[pl.BlockSpec((bsz, d), lambda j: (0, 0)),
                  pl.BlockSpec((d, tn), lambda j: (0, j)),
                  pl.BlockSpec((1, tn), lambda j: (0, j))],
        out_specs=pl.BlockSpec((bsz, tn), lambda j: (0, j)),
        out_shape=jax.ShapeDtypeStruct((bsz, n), F32),
        compiler_params=_cparams(("parallel",)),
        name="adaln",
    )(c, w, b.reshape(1, n))


def _proj_kernel(x_ref, sc_ref, sh_ref, ws5t_ref, bs5_ref, wf_ref, bf_ref, cs_ref, wg_ref, bg_ref,
                 us5_ref, z_ref, gates_ref):
    tm = x_ref.shape[0]
    u = (_standardize(x_ref[...]) * (1.0 + sc_ref[0]) + sh_ref[0]).astype(BF16)
    p = lax.dot_general(ws5t_ref[...], u, (((1,), (1,)), ((), ())), preferred_element_type=F32)
    p = p + bs5_ref[:, 0:1]
    d_s5, n_j, _ = us5_ref.shape
    us5_2d = us5_ref.reshape(d_s5 * n_j, LANE)
    for j in range(n_j):
        us5_2d[pl.ds(j, d_s5, stride=n_j), :] = p[:, j * LANE:(j + 1) * LANE]
    uf = (jnp.dot(u, wf_ref[...], preferred_element_type=F32) + bf_ref[...]).astype(BF16)
    d_f = uf.shape[1]
    fw = d_f // FOURIER_GROUPS
    for q in range(FOURIER_GROUPS):
        zq = jnp.dot(uf[:, q * fw:(q + 1) * fw], cs_ref[...], preferred_element_type=F32)
        z_ref[:, q * fw:(q + 1) * fw] = zq[:, :fw].astype(BF16)
        z_ref[:, d_f + q * fw:d_f + (q + 1) * fw] = zq[:, fw:].astype(BF16)
    n_g = wg_ref.shape[1]
    half = n_g // 2
    for q in range(2):
        gp = jnp.dot(u, wg_ref[:, q * half:(q + 1) * half], preferred_element_type=F32)
        gp = gp + bg_ref[:, q * half:(q + 1) * half]
        gates_ref[:, q * half:(q + 1) * half] = jax.nn.sigmoid(gp).astype(BF16)


def _proj(x2, sc, sh, ws5t, bs5, wf, bf, cs, wg, bg, seq):
    t, d = x2.shape
    d_s5 = ws5t.shape[0]
    d_f = wf.shape[1]
    n_g = wg.shape[1]
    tm = 1024
    tpb = seq // tm
    bsz = t // seq
    return pl.pallas_call(
        _proj_kernel,
        grid=(t // tm,),
        in_specs=[pl.BlockSpec((tm, d), lambda i: (i, 0)),
                  pl.BlockSpec((1, 1, d), lambda i: (i // tpb, 0, 0)),
                  pl.BlockSpec((1, 1, d), lambda i: (i // tpb, 0, 0)),
                  _const_spec((d_s5, d)), _const_spec((d_s5, LANE)),
                  _const_spec((d, d_f)), _const_spec((1, d_f)),
                  _const_spec(cs.shape),
                  _const_spec((d, n_g)), _const_spec((1, n_g))],
        out_specs=[pl.BlockSpec((d_s5, tm // LANE, LANE), lambda i: (0, i, 0)),
                   pl.BlockSpec((tm, 2 * d_f), lambda i: (i, 0)),
                   pl.BlockSpec((tm, n_g), lambda i: (i, 0))],
        out_shape=[jax.ShapeDtypeStruct((d_s5, t // LANE, LANE), F32),
                   jax.ShapeDtypeStruct((t, 2 * d_f), BF16),
                   jax.ShapeDtypeStruct((t, n_g), BF16)],
        compiler_params=_cparams(("parallel",)),
        name="proj",
    )(x2, sc.reshape(bsz, 1, d), sh.reshape(bsz, 1, d), ws5t, bs5, wf, bf, cs, wg, bg)


def _s5_tables(lam_re, lam_im, log_dt, b_re, b_im, c_re, c_im, d_skip):
    L = LANE
    hp = HIGHEST
    lr, li = lam_re.astype(F32), lam_im.astype(F32)
    dt = jnp.exp(log_dt.astype(F32))[:, :, None]
    mag = jnp.exp(lr * dt)
    ang = li * dt
    ab_re, ab_im = mag * jnp.cos(ang), mag * jnp.sin(ang)
    den = lr * lr + li * li
    nr = ab_re - 1.0
    coef_re = (nr * lr + ab_im * li) / den
    coef_im = (ab_im * lr - nr * li) / den
    br, bi = b_re.astype(F32), b_im.astype(F32)
    bb_re = coef_re[..., None] * br - coef_im[..., None] * bi
    bb_im = coef_re[..., None] * bi + coef_im[..., None] * br
    cr, ci = c_re.astype(F32), c_im.astype(F32)
    n_g, n_p, n_h = br.shape[1], br.shape[2], br.shape[3]

    k = jnp.arange(L + 1, dtype=F32)[None, None, :, None]
    pmag = jnp.exp(k * (lr * dt)[:, :, None, :])
    pang = k * (li * dt)[:, :, None, :]
    pw_re, pw_im = pmag * jnp.cos(pang), pmag * jnp.sin(pang)

    m_re = cr[:, :, :, None, :] * jnp.swapaxes(bb_re, 2, 3)[:, :, None, :, :] \
        - ci[:, :, :, None, :] * jnp.swapaxes(bb_im, 2, 3)[:, :, None, :, :]
    m_im = cr[:, :, :, None, :] * jnp.swapaxes(bb_im, 2, 3)[:, :, None, :, :] \
        + ci[:, :, :, None, :] * jnp.swapaxes(bb_re, 2, 3)[:, :, None, :, :]
    kap = jnp.einsum("dgohp,dgkp->dgohk", m_re, pw_re, precision=hp) \
        - jnp.einsum("dgohp,dgkp->dgohk", m_im, pw_im, precision=hp)
    kb = kap[1]
    kpos = kap[0, ..., :L].at[..., 0].add(kb[..., 0]).reshape(n_g, n_h * n_h, L)
    kneg = jnp.concatenate([jnp.zeros_like(kb[..., 0:1]), kb[..., L - 1:0:-1]], axis=-1)
    kneg = kneg.reshape(n_g, n_h * n_h, L)

    cat = lambda *parts: jnp.concatenate(parts, axis=-1)
    pf_re, pf_im = pw_re[0, :, L - 1::-1][:, :L], pw_im[0, :, L - 1::-1][:, :L]
    pb_re, pb_im = pw_re[1, :, :L], pw_im[1, :, :L]
    sp, spsw = cat(pf_re, pf_im, pb_re, pb_im), cat(pf_im, pf_re, pb_im, pb_re)
    bt_re, bt_im = jnp.swapaxes(bb_re, 2, 3), jnp.swapaxes(bb_im, 2, 3)
    bx, by = cat(bt_re[0], bt_re[0], bt_re[1], bt_re[1]), cat(-bt_im[0], bt_im[0], -bt_im[1], bt_im[1])

    qf_re, qf_im = pw_re[0, :, 1:L + 1], pw_im[0, :, 1:L + 1]
    qb_re, qb_im = pw_re[1, :, L:0:-1], pw_im[1, :, L:0:-1]
    qt, qtsw = cat(qf_re, qf_im, qb_re, qb_im), cat(qf_im, qf_re, qb_im, qb_re)
    cx, cy = cat(cr[0], -cr[0], cr[1], -cr[1]), cat(-ci[0], -ci[0], -ci[1], -ci[1])

    al_re, al_im = pw_re[:, :, L], pw_im[:, :, L]
    al = jnp.stack([cat(al_re[0], al_re[0], al_re[1], al_re[1]),
                    cat(-al_im[0], al_im[0], -al_im[1], al_im[1])], axis=1)

    dsk = jnp.broadcast_to(d_skip.astype(F32).reshape(n_g, n_h, 1), (n_g, n_h, L))
    return kpos, kneg, sp, spsw, bx, by, qt, qtsw, cx, cy, al, dsk


def _s5_kernel(a_ref, kpos_ref, kneg_ref, sp_ref, spsw_ref, bx_ref, by_ref, qt_ref, qtsw_ref, cx_ref, cy_ref,
               al_ref, d_ref, y_ref, abf_ref, tp_ref, ws_ref, sf_ref, sb_ref, sfs_ref, sbs_ref, xf_ref, xb_ref,
               *, n_chunk):
    n_h, r, L = a_ref.shape
    bsz = r // n_chunk
    half = sp_ref.shape[2] // 2
    nt = (((1,), (1,)), ((), ()))

    sp, spsw = sp_ref[0], spsw_ref[0]
    for h in range(n_h):
        abf_ref[:, h * L:(h + 1) * L] = a_ref[h].astype(BF16)
        ws_ref[h * L:(h + 1) * L, :] = (sp * bx_ref[0, h:h + 1, :] + spsw * by_ref[0, h:h + 1, :]).astype(BF16)
    abf = abf_ref[...]

    s_all = jnp.dot(abf, ws_ref[...], preferred_element_type=F32)
    sf_ref[...] = s_all[:, :half]
    sb_ref[...] = s_all[:, half:]
    sfs_ref[...] = pltpu.roll(s_all[:, :half], half // 2, 1)
    sbs_ref[...] = pltpu.roll(s_all[:, half:], half // 2, 1)

    alx = al_ref[0, 0:1, :]
    aly = al_ref[0, 1:2, :]

    zero = jnp.zeros((bsz, half), F32)
    ef, efs, eb, ebs = zero, zero, zero, zero
    for c in range(n_chunk):
        cb = n_chunk - 1 - c
        rows_f = pl.ds(c, bsz, stride=n_chunk)
        rows_b = pl.ds(cb, bsz, stride=n_chunk)
        xf_ref[rows_f, :] = ef
        xb_ref[rows_b, :] = eb
        xf, yf = alx[:, :half], aly[:, :half]
        xb, yb = alx[:, half:], aly[:, half:]
        ef, efs = ef * xf + efs * yf + sf_ref[rows_f, :], efs * xf - ef * yf + sfs_ref[rows_f, :]
        eb, ebs = eb * xb + ebs * yb + sb_ref[rows_b, :], ebs * xb - eb * yb + sbs_ref[rows_b, :]
    xin = jnp.concatenate([xf_ref[...], xb_ref[...]], axis=1).astype(BF16)

    s_idx = lax.broadcasted_iota(I32, (L, L), 0)
    j_idx = lax.broadcasted_iota(I32, (L, L), 1)
    fwd_part = j_idx + s_idx < L
    qt, qtsw = qt_ref[0], qtsw_ref[0]

    def pair(op, _):
        for oo in range(2):
            o = 2 * op + oo
            for h in range(n_h):
                row = o * n_h + h
                kp = jnp.broadcast_to(kpos_ref[0, pl.ds(row, 1), :], (L, L))
                kn = jnp.broadcast_to(kneg_ref[0, pl.ds(row, 1), :], (L, L))
                tile = pltpu.roll(jnp.where(fwd_part, kp, kn), 0, 1, stride=1, stride_axis=0)
                tp_ref[h * L:(h + 1) * L, oo * L:(oo + 1) * L] = tile.astype(BF16)
        wo_t = jnp.concatenate(
            [qt * cx_ref[0, pl.ds(2 * op + oo, 1), :] + qtsw * cy_ref[0, pl.ds(2 * op + oo, 1), :]
             for oo in range(2)], axis=0).astype(BF16)
        yp = jnp.dot(abf, tp_ref[...], preferred_element_type=F32)
        yp = yp + lax.dot_general(xin, wo_t, nt, preferred_element_type=F32)
        for oo in range(2):
            o = 2 * op + oo
            y_ref[o] = yp[:, oo * L:(oo + 1) * L] + a_ref[o] * d_ref[0, pl.ds(o, 1), :]
        return 0

    lax.fori_loop(0, n_h // 2, pair, 0)


def _s5(us5, tables, n_chunk):
    kpos, kneg, sp, spsw, bx, by, qt, qtsw, cx, cy, al, dsk = tables
    d_s5, r, L = us5.shape
    n_g = kpos.shape[0]
    n_h = d_s5 // n_g
    n_st = sp.shape[2]
    g3 = lambda g: (g, 0, 0)
    pw_spec = pl.BlockSpec((1, L, n_st), g3)
    hv_spec = pl.BlockSpec((1, n_h, n_st), g3)
    return pl.pallas_call(
        functools.partial(_s5_kernel, n_chunk=n_chunk),
        grid=(n_g,),
        in_specs=[pl.BlockSpec((n_h, r, L), g3),
                  pl.BlockSpec((1, n_h * n_h, L), g3), pl.BlockSpec((1, n_h * n_h, L), g3),
                  pw_spec, pw_spec, hv_spec, hv_spec, pw_spec, pw_spec, hv_spec, hv_spec,
                  pl.BlockSpec((1, 2, n_st), g3), pl.BlockSpec((1, n_h, L), g3)],
        out_specs=pl.BlockSpec((n_h, r, L), g3),
        out_shape=jax.ShapeDtypeStruct((d_s5, r, L), F32),
        scratch_shapes=[pltpu.VMEM((r, n_h * L), BF16), pltpu.VMEM((n_h * L, 2 * L), BF16),
                        pltpu.VMEM((n_h * L, n_st), BF16)]
        + [pltpu.VMEM((r, n_st // 2), F32)] * 6,
        compiler_params=_cparams(("parallel",)),
        name="s5",
    )(us5, kpos, kneg, sp, spsw, bx, by, qt, qtsw, cx, cy, al, dsk)


DFT_SPLIT = 64


def _dft_tables(seq, fw):
    def angles(mult, n_rows, n):
        s = jnp.arange(n, dtype=I32)[None, :]
        q = jnp.arange(n_rows, dtype=I32)[:, None]
        return (2.0 * math.pi / n) * ((mult * q * s) % n).astype(F32)

    ang_a = angles(DFT_SPLIT, seq // DFT_SPLIT, seq)
    ang_b = angles(1, DFT_SPLIT, seq)
    seq_tabs = (jnp.cos(ang_a), jnp.sin(ang_a), jnp.cos(ang_b), jnp.sin(ang_b))
    ang_c = angles(1, fw, fw)
    return seq_tabs, jnp.concatenate([jnp.cos(ang_c), jnp.sin(ang_c)], axis=1).astype(BF16)


DFT_EXTRA = 16


def _seqdft_kernel(ca_ref, sa_ref, cb_ref, sb_ref, flip_ref, z_ref, lo_ref, hi_ref, f_ref, *, scale):
    seq = z_ref.shape[0]
    tk, d_f = lo_ref.shape

    @pl.when(pl.program_id(1) == 0)
    def _():
        a0 = pl.program_id(0) * (tk // DFT_SPLIT)
        cb, sb = cb_ref[...], sb_ref[...]

        def put(rows, ca, sa, n_b):
            f_ref[rows, :seq] = (ca * cb[:n_b] - sa * sb[:n_b]).astype(BF16)
            f_ref[rows, seq:] = (-(sa * cb[:n_b] + ca * sb[:n_b])).astype(BF16)

        for j in range(tk // DFT_SPLIT):
            put(slice(j * DFT_SPLIT, (j + 1) * DFT_SPLIT),
                ca_ref[pl.ds(a0 + j, 1), :], sa_ref[pl.ds(a0 + j, 1), :], DFT_SPLIT)
        a_next = a0 + tk // DFT_SPLIT
        put(slice(tk, tk + DFT_EXTRA), ca_ref[pl.ds(a_next, 1), :], sa_ref[pl.ds(a_next, 1), :], DFT_EXTRA)

    p = jnp.dot(f_ref[:, :seq], z_ref[:, :d_f], preferred_element_type=F32)
    q = jnp.dot(f_ref[:, seq:], z_ref[:, d_f:], preferred_element_type=F32)
    lo_ref[...] = ((p[:tk] + q[:tk]) * scale).astype(lo_ref.dtype)
    mirror = ((p - q) * scale).astype(BF16)
    hi_ref[...] = jnp.dot(flip_ref[...], mirror, preferred_element_type=F32).astype(hi_ref.dtype)


def _seqdft(seq_tabs, z, seq, fw):
    t, two_df = z.shape
    d_f = two_df // 2
    bsz = t // seq
    tk = 512
    nk2 = seq // (2 * tk)
    scale = 1.0 / math.sqrt(seq * fw)
    flip = (jnp.arange(tk)[:, None] + jnp.arange(tk + DFT_EXTRA)[None, :] == tk).astype(BF16)
    half = jax.ShapeDtypeStruct((t // 2, d_f), BF16)
    return pl.pallas_call(
        functools.partial(_seqdft_kernel, scale=scale),
        grid=(nk2, bsz),
        in_specs=[_const_spec(tab.shape) for tab in seq_tabs] + [_const_spec(flip.shape)]
        + [pl.BlockSpec((seq, two_df), lambda k, b: (b, 0))],
        out_specs=[pl.BlockSpec((tk, d_f), lambda k, b: (b * nk2 + k, 0)),
                   pl.BlockSpec((tk, d_f), lambda k, b: (b * nk2 + nk2 - 1 - k, 0))],
        out_shape=[half, half],
        scratch_shapes=[pltpu.VMEM((tk + DFT_EXTRA, 2 * seq), BF16)],
        compiler_params=_cparams(("parallel", "arbitrary")),
        name="seqdft",
    )(*seq_tabs, flip, z)


def _glu_kernel(y_ref, w_ref, b_ref, o_ref, zt_ref):
    d_s5, n_j, _ = y_ref.shape
    d = o_ref.shape[1]
    y2 = y_ref.reshape(d_s5 * n_j, LANE)
    for j in range(n_j):
        zt_ref[j * LANE:(j + 1) * LANE, :] = _gelu_tanh(y2[pl.ds(j, d_s5, stride=n_j), :]).T.astype(BF16)
    zt = zt_ref[...]
    a = jnp.dot(zt, w_ref[:, :d], preferred_element_type=F32) + b_ref[:, :d]
    g = jnp.dot(zt, w_ref[:, d:], preferred_element_type=F32) + b_ref[:, d:]
    o_ref[...] = (a * jax.nn.sigmoid(g)).astype(BF16)


def _glu(ys5, w, b):
    d_s5, r, L = ys5.shape
    t = r * L
    n = w.shape[1]
    tm = 1024
    return pl.pallas_call(
        _glu_kernel,
        grid=(t // tm,),
        in_specs=[pl.BlockSpec((d_s5, tm // L, L), lambda i: (0, i, 0)),
                  _const_spec((d_s5, n)), _const_spec((1, n))],
        out_specs=pl.BlockSpec((tm, n // 2), lambda i: (i, 0)),
        out_shape=jax.ShapeDtypeStruct((t, n // 2), BF16),
        scratch_shapes=[pltpu.VMEM((tm, d_s5), BF16)],
        compiler_params=_cparams(("parallel",)),
        name="glu",
    )(ys5, w, b)


U32 = jnp.uint32
PACK_SUB = 4


def _row_slab(ref, r):
    return ref.at[pl.ds(pl.multiple_of(r * PACK_SUB, PACK_SUB), PACK_SUB), :]


def _pack_rows(ref, v, first=0):
    rows, d = v.shape
    half = d // 2
    bits = lax.bitcast_convert_type(v.astype(BF16).astype(F32), U32)
    for c in range(PACK_SUB):
        lo = bits[:, c * LANE:(c + 1) * LANE] >> 16
        hi = bits[:, half + c * LANE:half + (c + 1) * LANE] & jnp.uint32(0xFFFF0000)
        ref[pl.ds(first * PACK_SUB + c, rows, stride=PACK_SUB), :] = hi | lo


def _unpack_rows(ref, rows, first=0):
    los, his = [], []
    for c in range(PACK_SUB):
        w = ref[pl.ds(first * PACK_SUB + c, rows, stride=PACK_SUB), :]
        los.append(lax.bitcast_convert_type(w << 16, F32))
        his.append(lax.bitcast_convert_type(w & jnp.uint32(0xFFFF0000), F32))
    return jnp.concatenate(los + his, axis=1)


def _merge_kernel(x_ref, brs_ref, yf_lo_ref, yf_hi_ref, gates_ref, g1_ref, sc_ref, sh_ref, wfo_ref, bfo_ref,
                  wo_ref, bo_ref, lng_ref, lnb_ref, x1_ref, hhi_ref, hlo_ref, hp_ref, *, alpha, tiles_per_seq):
    d = x_ref.shape[1]
    in_first_half = (pl.program_id(0) % tiles_per_seq) < tiles_per_seq // 2
    yf = jnp.where(in_first_half, yf_lo_ref[...], yf_hi_ref[...])
    br_f = jnp.dot(yf, wfo_ref[...], preferred_element_type=F32) + bfo_ref[...]
    merged = gates_ref[:, :d].astype(F32) * brs_ref[...].astype(F32) + gates_ref[:, d:].astype(F32) * br_f
    mix = jnp.dot(merged.astype(BF16), wo_ref[...], preferred_element_type=F32) + bo_ref[...]
    v = alpha * x_ref[...] + g1_ref[0] * mix
    x1 = _standardize(v) * lng_ref[...] + lnb_ref[...]
    x1_ref[...] = x1
    h = _standardize(x1) * (1.0 + sc_ref[0]) + sh_ref[0]
    hhi = h.astype(BF16)
    hhi_ref[...] = hhi
    hlo_ref[...] = (h - hhi.astype(F32)).astype(BF16)
    _pack_rows(hp_ref, h)


def _merge(x2, brs, yf_lo, yf_hi, gates, g1, sc2, sh2, wfo, bfo, wo, bo, lng, lnb, seq, alpha):
    t, d = x2.shape
    d_f = yf_lo.shape[1]
    tm = 512
    tpb = seq // tm
    tph = tpb // 2
    bsz = t // seq
    row = lambda i: (i, 0)
    bat = lambda i: (i // tpb, 0, 0)
    lo_row = lambda i: ((i // tpb) * tph + jnp.minimum(i % tpb, tph - 1), 0)
    hi_row = lambda i: ((i // tpb) * tph + jnp.maximum(i % tpb - tph, 0), 0)
    return pl.pallas_call(
        functools.partial(_merge_kernel, alpha=alpha, tiles_per_seq=tpb),
        grid=(t // tm,),
        in_specs=[pl.BlockSpec((tm, d), row), pl.BlockSpec((tm, d), row),
                  pl.BlockSpec((tm, d_f), lo_row), pl.BlockSpec((tm, d_f), hi_row),
                  pl.BlockSpec((tm, 2 * d), row),
                  pl.BlockSpec((1, 1, d), bat), pl.BlockSpec((1, 1, d), bat), pl.BlockSpec((1, 1, d), bat),
                  _const_spec((d_f, d)), _const_spec((1, d)), _const_spec((d, d)), _const_spec((1, d)),
                  _const_spec((1, d)), _const_spec((1, d))],
        out_specs=[pl.BlockSpec((tm, d), row), pl.BlockSpec((tm, d), row), pl.BlockSpec((tm, d), row),
                   pl.BlockSpec((tm * PACK_SUB, LANE), row)],
        out_shape=[jax.ShapeDtypeStruct((t, d), F32), jax.ShapeDtypeStruct((t, d), BF16),
                   jax.ShapeDtypeStruct((t, d), BF16), jax.ShapeDtypeStruct((t * PACK_SUB, LANE), U32)],
        compiler_params=_cparams(("parallel",)),
        name="merge",
    )(x2, brs, yf_lo, yf_hi, gates, g1.reshape(bsz, 1, d), sc2.reshape(bsz, 1, d), sh2.reshape(bsz, 1, d),
      wfo, bfo, wo, bo, lng, lnb)


def _router_kernel(hhi_ref, hlo_ref, whi_ref, wlo_ref, bias_ref, tri_ref,
                   eidx_ref, gate_ref, rank_ref, cnt_ref, base_ref):
    n_e = whi_ref.shape[0]
    tm = hhi_ref.shape[0]
    gsz = n_e // N_EXPERT_GROUPS
    nt = (((1,), (1,)), ((), ()))

    @pl.when(pl.program_id(0) == 0)
    def _():
        base_ref[...] = jnp.zeros_like(base_ref)

    hhi = hhi_ref[...]
    logits = lax.dot_general(whi_ref[...], hhi, nt, preferred_element_type=F32)
    logits = logits + lax.dot_general(wlo_ref[...], hhi, nt, preferred_element_type=F32)
    logits = logits + lax.dot_general(whi_ref[...], hlo_ref[...], nt, preferred_element_type=F32)
    scores = jax.nn.sigmoid(logits)
    sel = scores + bias_ref[:, 0:1]

    g3 = sel.reshape(N_EXPERT_GROUPS, gsz, tm)
    i3 = lax.broadcasted_iota(I32, g3.shape, 1).astype(F32)
    m1 = jnp.max(g3, axis=1, keepdims=True)
    first = jnp.min(jnp.where(g3 == m1, i3, float(gsz)), axis=1, keepdims=True)
    m2 = jnp.max(jnp.where(i3 == first, NEG_INF, g3), axis=1, keepdims=True)
    gs = (m1 + m2).reshape(N_EXPERT_GROUPS, tm)

    gi = lax.broadcasted_iota(I32, gs.shape, 0).astype(F32)
    gsel = jnp.zeros(gs.shape, F32)
    cur = gs
    for _ in range(TOPK_GROUPS):
        m = jnp.max(cur, axis=0, keepdims=True)
        f = jnp.min(jnp.where(cur == m, gi, float(N_EXPERT_GROUPS)), axis=0, keepdims=True)
        pick = gi == f
        gsel = jnp.where(pick, 1.0, gsel)
        cur = jnp.where(pick, NEG_INF, cur)
    gmask = jnp.broadcast_to(gsel.reshape(N_EXPERT_GROUPS, 1, tm), g3.shape).reshape(n_e, tm)
    masked = jnp.where(gmask > 0.5, sel, NEG_INF)

    ri = lax.broadcasted_iota(I32, (n_e, tm), 0).astype(F32)
    picks = []
    gates = []
    multihot = jnp.zeros((n_e, tm), F32)
    for _ in range(TOP_K):
        m = jnp.max(masked, axis=0, keepdims=True)
        f = jnp.min(jnp.where(masked == m, ri, float(n_e)), axis=0, keepdims=True)
        pick = ri == f
        picks.append(f)
        gates.append(jnp.sum(jnp.where(pick, scores, 0.0), axis=0, keepdims=True))
        multihot = jnp.where(pick, 1.0, multihot)
        masked = jnp.where(pick, NEG_INF, masked)
    gsum = gates[0]
    for g in gates[1:]:
        gsum = gsum + g

    rankmat = jnp.dot(multihot.astype(BF16), tri_ref[...], preferred_element_type=F32) + base_ref[:, 0:1]
    for k in range(TOP_K):
        pick = ri == picks[k]
        eidx_ref[k:k + 1, :] = picks[k].astype(I32)
        gate_ref[k:k + 1, :] = gates[k] / gsum * ROUTED_SCALE
        rank_ref[k:k + 1, :] = jnp.sum(jnp.where(pick, rankmat, 0.0), axis=0, keepdims=True).astype(I32)
    base_ref[...] = base_ref[...] + jnp.sum(multihot, axis=1, keepdims=True)
    cnt_ref[...] = base_ref[...].astype(I32)


def _router(hhi, hlo, wrt_hi, wrt_lo, bias):
    t, d = hhi.shape
    n_e = wrt_hi.shape[0]
    tm = 512
    tri = (jnp.arange(tm)[:, None] < jnp.arange(tm)[None, :]).astype(BF16)
    kt = lambda i: (0, i)
    return pl.pallas_call(
        _router_kernel,
        grid=(t // tm,),
        in_specs=[pl.BlockSpec((tm, d), lambda i: (i, 0)), pl.BlockSpec((tm, d), lambda i: (i, 0)),
                  _const_spec((n_e, d)), _const_spec((n_e, d)), _const_spec((n_e, LANE)),
                  _const_spec((tm, tm))],
        out_specs=[pl.BlockSpec((TOP_K, tm), kt), pl.BlockSpec((TOP_K, tm), kt), pl.BlockSpec((TOP_K, tm), kt),
                   pl.BlockSpec((n_e, LANE), lambda i: (0, 0))],
        out_shape=[jax.ShapeDtypeStruct((TOP_K, t), I32), jax.ShapeDtypeStruct((TOP_K, t), F32),
                   jax.ShapeDtypeStruct((TOP_K, t), I32), jax.ShapeDtypeStruct((n_e, LANE), I32)],
        scratch_shapes=[pltpu.VMEM((n_e, LANE), F32)],
        compiler_params=_cparams(("arbitrary",)),
        name="router",
    )(hhi, hlo, wrt_hi, wrt_lo, jnp.broadcast_to(bias.astype(F32).reshape(n_e, 1), (n_e, LANE)), tri)


def _dest_kernel(eidx_ref, rank_ref, pstart_ref, dest_ref):
    n_e = pstart_ref.shape[0]
    tm = eidx_ref.shape[1]
    ri = lax.broadcasted_iota(I32, (n_e, tm), 0)
    ps = pstart_ref[:, 0:1].astype(F32)
    for k in range(TOP_K):
        hit = ri == eidx_ref[k:k + 1, :]
        base = jnp.sum(jnp.where(hit, ps, 0.0), axis=0, keepdims=True)
        dest_ref[k:k + 1, :] = base.astype(I32) + rank_ref[k:k + 1, :]


def _dest(eidx, rank, pstart):
    k, t = eidx.shape
    n_e = pstart.shape[0]
    tm = 2048
    kt = lambda i: (0, i)
    return pl.pallas_call(
        _dest_kernel,
        grid=(t // tm,),
        in_specs=[pl.BlockSpec((k, tm), kt), pl.BlockSpec((k, tm), kt), _const_spec((n_e, LANE))],
        out_specs=pl.BlockSpec((k, tm), kt),
        out_shape=jax.ShapeDtypeStruct((k, t), I32),
        compiler_params=_cparams(("parallel",)),
        name="dest",
    )(eidx, rank, jnp.broadcast_to(pstart.astype(I32).reshape(n_e, 1), (n_e, LANE)))


def _zero_fill(pad_lo_ref, pad_n_ref, zeros_ref, rows_ref, sem):
    def each_copy(act):
        def per_entry(e, _):
            lo = pad_lo_ref[e]
            n = pad_n_ref[e]
            n_full = n // FFN_BLOCK

            def full(j, _):
                act(pltpu.make_async_copy(
                    zeros_ref, rows_ref.at[pl.ds(pl.multiple_of((lo + j * FFN_BLOCK) * PACK_SUB, PACK_SUB),
                                                 FFN_BLOCK * PACK_SUB), :], sem))
                return 0

            lax.fori_loop(0, n_full, full, 0)
            off = lo + n_full * FFN_BLOCK
            rem = n - n_full * FFN_BLOCK
            bit = FFN_BLOCK // 2
            while bit >= 1:
                take = rem & bit

                @pl.when(take != 0)
                def _(off=off, bit=bit):
                    act(pltpu.make_async_copy(
                        zeros_ref.at[pl.ds(0, bit * PACK_SUB), :],
                        rows_ref.at[pl.ds(pl.multiple_of(off * PACK_SUB, PACK_SUB), bit * PACK_SUB), :], sem))

                off = off + take
                bit //= 2
            return 0

        lax.fori_loop(0, pad_lo_ref.shape[0], per_entry, 0)

    each_copy(lambda cp: cp.start())
    each_copy(lambda cp: cp.wait())


def _dispatch_kernel(pad_lo_ref, pad_n_ref, dest_ref, hp_ref, rows_ref, zeros_ref, sem, zsem):
    tm = hp_ref.shape[0] // PACK_SUB

    @pl.when(pl.program_id(0) == 0)
    def _():
        zeros_ref[...] = jnp.zeros_like(zeros_ref)
        _zero_fill(pad_lo_ref, pad_n_ref, zeros_ref, rows_ref, zsem)

    def body(tt, _):
        for k in range(TOP_K):
            pltpu.make_async_copy(_row_slab(hp_ref, tt), _row_slab(rows_ref, dest_ref[0, 0, k * tm + tt]),
                                  sem).start(priority=k % 2)
        return 0

    lax.fori_loop(0, tm, body, 0, unroll=ISSUE_UNROLL)
    all_rows = rows_ref.at[pl.ds(0, TOP_K * tm * PACK_SUB), :]
    pltpu.make_async_copy(all_rows, all_rows, sem).wait()


def _dispatch(hp, dest_tiles, pad_lo, pad_n, n_rows):
    n_tile, _, per_tile = dest_tiles.shape
    tm = per_tile // TOP_K
    return pl.pallas_call(
        _dispatch_kernel,
        grid_spec=pltpu.PrefetchScalarGridSpec(
            num_scalar_prefetch=2,
            grid=(n_tile,),
            in_specs=[pl.BlockSpec((1, 1, per_tile), lambda i, lo, n: (i, 0, 0), memory_space=pltpu.SMEM),
                      pl.BlockSpec((tm * PACK_SUB, LANE), lambda i, lo, n: (i, 0))],
            out_specs=pl.BlockSpec(memory_space=pl.ANY),
            scratch_shapes=[pltpu.VMEM((FFN_BLOCK * PACK_SUB, LANE), U32),
                            pltpu.SemaphoreType.DMA, pltpu.SemaphoreType.DMA],
        ),
        out_shape=jax.ShapeDtypeStruct((n_rows * PACK_SUB, LANE), U32),
        compiler_params=_cparams(("arbitrary",)),
        name="dispatch",
    )(pad_lo, pad_n, dest_tiles, hp)


def _ffn_kernel(blk_e_ref, n_used_ref, next_blk_ref, rows_ref, wg_ref, wu_ref, wd_ref, y_ref,
                wg_st_ref, wu_st_ref, wd_st_ref, wgu_bf_ref, wd_bf_ref, sems):
    b = pl.program_id(0)
    n_used = n_used_ref[0]
    live = b < n_used
    d_e = wd_st_ref.shape[0]
    e = blk_e_ref[b]

    def fetch(expert):
        return (pltpu.make_async_copy(wg_ref.at[expert], wg_st_ref, sems.at[0]),
                pltpu.make_async_copy(wu_ref.at[expert], wu_st_ref, sems.at[1]),
                pltpu.make_async_copy(wd_ref.at[expert], wd_st_ref, sems.at[2]))

    @pl.when(b == 0)
    def _():
        for cp in fetch(e):
            cp.start()

    first_block = jnp.logical_and(live, jnp.logical_or(b == 0, e != blk_e_ref[jnp.maximum(b - 1, 0)]))

    @pl.when(first_block)
    def _():
        for cp in fetch(e):
            cp.wait()
        wgu_bf_ref[:, :d_e] = wg_st_ref[...].astype(BF16)
        wgu_bf_ref[:, d_e:] = wu_st_ref[...].astype(BF16)
        wd_bf_ref[...] = wd_st_ref[...].astype(BF16)
        nb = next_blk_ref[e]

        @pl.when(nb < n_used)
        def _():
            for cp in fetch(blk_e_ref[nb]):
                cp.start()

    @pl.when(live)
    def _():
        x = _unpack_rows(rows_ref, FFN_BLOCK).astype(BF16)
        au = jnp.dot(x, wgu_bf_ref[...], preferred_element_type=F32)
        hid = (_silu(au[:, :d_e]) * au[:, d_e:]).astype(BF16)
        _pack_rows(y_ref, jnp.dot(hid, wd_bf_ref[...], preferred_element_type=F32))

    @pl.when(jnp.logical_not(live))
    def _():
        y_ref[...] = jnp.zeros_like(y_ref)


def _ffn(rows, blk_e, n_used, next_blk, wg, wu, wd):
    n_rows = rows.shape[0] // PACK_SUB
    n_e, d, d_e = wg.shape
    n_blk = n_rows // FFN_BLOCK
    blk = lambda b, *_: (b, 0)
    hbm = pl.BlockSpec(memory_space=pl.ANY)
    return pl.pallas_call(
        _ffn_kernel,
        grid_spec=pltpu.PrefetchScalarGridSpec(
            num_scalar_prefetch=3,
            grid=(n_blk,),
            in_specs=[pl.BlockSpec((FFN_BLOCK * PACK_SUB, LANE), blk), hbm, hbm, hbm],
            out_specs=pl.BlockSpec((FFN_BLOCK * PACK_SUB, LANE), blk),
            scratch_shapes=[pltpu.VMEM((d, d_e), F32), pltpu.VMEM((d, d_e), F32), pltpu.VMEM((d_e, d), F32),
                            pltpu.VMEM((d, 2 * d_e), BF16), pltpu.VMEM((d_e, d), BF16),
                            pltpu.SemaphoreType.DMA((3,))],
        ),
        out_shape=jax.ShapeDtypeStruct((n_rows * PACK_SUB, LANE), U32),
        compiler_params=_cparams(("arbitrary",)),
        name="ffn",
    )(blk_e, n_used, next_blk, rows, wg, wu, wd)


def _final_kernel(dest_ref, dnext_ref, x1_ref, h_ref, gate_ref, g2_ref, wsg_ref, wsu_ref, wsd_ref, lng_ref, lnb_ref,
                  y_ref, o_ref, ybuf_ref, sems, *, alpha):
    i = pl.program_id(0)
    tm = x1_ref.shape[0]
    per_tile = TOP_K * tm

    def gather(d_ref, slot):
        def body(tt, _):
            for k in range(TOP_K):
                pltpu.make_async_copy(_row_slab(y_ref, d_ref[0, 0, k * tm + tt]),
                                      _row_slab(ybuf_ref, slot * per_tile + k * tm + tt),
                                      sems.at[slot]).start(priority=k % 2)
            return 0

        lax.fori_loop(0, tm, body, 0, unroll=ISSUE_UNROLL)

    @pl.when(i == 0)
    def _():
        gather(dest_ref, 0)

    @pl.when(i + 1 < pl.num_programs(0))
    def _():
        gather(dnext_ref, (i + 1) % 2)

    h = h_ref[...]
    a = jnp.dot(h, wsg_ref[...], preferred_element_type=F32)
    u = jnp.dot(h, wsu_ref[...], preferred_element_type=F32)
    shared = jnp.dot((_silu(a) * u).astype(BF16), wsd_ref[...], preferred_element_type=F32)

    slot = i % 2
    half = ybuf_ref.at[pl.ds(pl.multiple_of(slot * per_tile * PACK_SUB, per_tile * PACK_SUB), per_tile * PACK_SUB), :]
    pltpu.make_async_copy(half, half, sems.at[slot]).wait()
    first = slot * per_tile
    routed = gate_ref[:, 0:1] * _unpack_rows(ybuf_ref, tm, first=first)
    for k in range(1, TOP_K):
        routed = routed + gate_ref[:, k:k + 1] * _unpack_rows(ybuf_ref, tm, first=first + k * tm)

    v = alpha * x1_ref[...] + g2_ref[0] * (shared + routed)
    o_ref[...] = _standardize(v) * lng_ref[...] + lnb_ref[...]


def _final(x1, hhi, y_rows, dest_tiles, gate_t, g2, wsg, wsu, wsd, lng, lnb, seq, alpha):
    t, d = x1.shape
    d_sh = wsg.shape[1]
    n_tile, _, per_tile = dest_tiles.shape
    tm = per_tile // TOP_K
    tpb = seq // tm
    bsz = t // seq
    row = lambda i: (i, 0)
    return pl.pallas_call(
        functools.partial(_final_kernel, alpha=alpha),
        grid=(n_tile,),
        in_specs=[pl.BlockSpec((1, 1, per_tile), lambda i: (i, 0, 0), memory_space=pltpu.SMEM),
                  pl.BlockSpec((1, 1, per_tile), lambda i: (jnp.minimum(i + 1, n_tile - 1), 0, 0),
                               memory_space=pltpu.SMEM),
                  pl.BlockSpec((tm, d), row), pl.BlockSpec((tm, d), row), pl.BlockSpec((tm, TOP_K), row),
                  pl.BlockSpec((1, 1, d), lambda i: (i // tpb, 0, 0)),
                  _const_spec((d, d_sh)), _const_spec((d, d_sh)), _const_spec((d_sh, d)),
                  _const_spec((1, d)), _const_spec((1, d)),
                  pl.BlockSpec(memory_space=pl.ANY)],
        out_specs=pl.BlockSpec((tm, d), row),
        out_shape=jax.ShapeDtypeStruct((t, d), F32),
        scratch_shapes=[pltpu.VMEM((2 * TOP_K * tm * PACK_SUB, LANE), U32), pltpu.SemaphoreType.DMA((2,))],
        compiler_params=_cparams(("arbitrary",)),
        name="final",
    )(dest_tiles, dest_tiles, x1, hhi, gate_t, g2.reshape(bsz, 1, d), wsg, wsu, wsd, lng, lnb, y_rows)


def _split_hi_lo(w):
    hi = w.astype(BF16)
    return hi, (w - hi.astype(F32)).astype(BF16)


def kernel(x, c, w_ada, b_ada, w_in, b_in, s5_lambda_re, s5_lambda_im, s5_log_dt, s5_b_re, s5_b_im, s5_c_re, s5_c_im, s5_d, w_s5_glu, b_s5_glu, w_fourier, b_fourier, w_out, b_out, ln1_g, ln1_b, w_router, router_bias, w_exp_gate, w_exp_up, w_exp_down, w_sh_gate, w_sh_up, w_sh_down, ln2_g, ln2_b):
    bsz, seq, d = x.shape
    depth = w_ada.shape[0]
    alpha = (2 * depth) ** 0.25
    t = bsz * seq
    d_s5 = s5_d.shape[1]
    d_f = w_fourier.shape[1]
    fw = d_f // FOURIER_GROUPS
    n_e = w_router.shape[2]
    n_chunk = seq // LANE
    seq_tabs, cmat = _dft_tables(seq, fw)
    row = lambda v: v.astype(F32).reshape(1, -1)

    x2 = x.reshape(t, d)
    for l in range(depth):
        mod = _adaln(c, w_ada[l], b_ada[l])
        sh1, sc1, g1, sh2, sc2, g2 = jnp.split(mod, 6, axis=-1)

        wi = w_in[l]
        bi = b_in[l].astype(F32)
        ws5t = wi[:, :d_s5].T.astype(BF16)
        bs5 = jnp.broadcast_to(bi[:d_s5].reshape(d_s5, 1), (d_s5, LANE))
        us5, z, gates = _proj(x2, sc1, sh1, ws5t, bs5,
                              wi[:, d_s5:d_s5 + d_f].astype(BF16), row(bi[d_s5:d_s5 + d_f]), cmat,
                              wi[:, d_s5 + d_f:].astype(BF16), row(bi[d_s5 + d_f:]), seq)

        tables = _s5_tables(s5_lambda_re[l], s5_lambda_im[l], s5_log_dt[l], s5_b_re[l], s5_b_im[l],
                            s5_c_re[l], s5_c_im[l], s5_d[l])
        ys5 = _s5(us5, tables, n_chunk)
        brs = _glu(ys5, w_s5_glu[l].astype(BF16), row(b_s5_glu[l]))
        yf_lo, yf_hi = _seqdft(seq_tabs, z, seq, fw)

        x1, hhi, hlo, hp = _merge(x2, brs, yf_lo, yf_hi, gates, g1, sc2, sh2,
                              w_fourier[l].astype(BF16), row(b_fourier[l]),
                              w_out[l].astype(BF16), row(b_out[l]), row(ln1_g[l]), row(ln1_b[l]), seq, alpha)

        wrt_hi, wrt_lo = _split_hi_lo(w_router[l].astype(F32).T)
        eidx, gate, rank, cnt = _router(hhi, hlo, wrt_hi, wrt_lo, router_bias[l])

        counts = cnt[:, 0]
        padded = ((counts + FFN_BLOCK - 1) // FFN_BLOCK) * FFN_BLOCK
        pend = jnp.cumsum(padded)
        pstart = (pend - padded).astype(I32)
        dest = _dest(eidx, rank, pstart)
        n_blk = (t * TOP_K + n_e * (FFN_BLOCK - 1) + FFN_BLOCK - 1) // FFN_BLOCK
        n_rows = n_blk * FFN_BLOCK
        blk_start = jnp.arange(n_blk, dtype=I32) * FFN_BLOCK
        blk_e = jnp.minimum(jnp.sum((pend[None, :] <= blk_start[:, None]).astype(I32), axis=1), n_e - 1)
        n_used = (pend[-1:] // FFN_BLOCK).astype(I32)
        pad_lo = jnp.concatenate([pstart + counts, pend[-1:]]).astype(I32)
        pad_n = jnp.concatenate([padded - counts, n_rows - pend[-1:]]).astype(I32)

        def tiles(tm):
            return dest.reshape(TOP_K, t // tm, tm).transpose(1, 0, 2).reshape(t // tm, 1, TOP_K * tm)

        rows = _dispatch(hp, tiles(DISPATCH_TILE), pad_lo, pad_n, n_rows)
        next_blk = (pend // FFN_BLOCK).astype(I32)
        y_rows = _ffn(rows, blk_e, n_used, next_blk, w_exp_gate[l], w_exp_up[l], w_exp_down[l])
        x2 = _final(x1, hhi, y_rows, tiles(COMBINE_TILE), gate.T, g2,
                    w_sh_gate[l].astype(BF16), w_sh_up[l].astype(BF16), w_sh_down[l].astype(BF16),
                    row(ln2_g[l]), row(ln2_b[l]), seq, alpha)
    return x2.reshape(bsz, seq, d)
```

```python
import functools
import math

import jax
import jax.numpy as jnp
from jax import lax
from jax.experimental import pallas as pl
from jax.experimental.pallas import tpu as pltpu

F32 = jnp.float32
BF16 = jnp.bfloat16
I32 = jnp.int32

TOP_K = 8
N_EXPERT_GROUPS = 8
TOPK_GROUPS = 4
ROUTED_SCALE = 2.5
FOURIER_GROUPS = 4
LN_EPS = 1e-5

LANE = 128
VMEM_LIMIT = 56 * 1024 * 1024

HIGHEST = lax.Precision.HIGHEST
NEG_INF = float("-inf")

FFN_BLOCK = 512
DISPATCH_TILE = 512
COMBINE_TILE = 256
ISSUE_UNROLL = 4


def _cparams(sem):
    return pltpu.CompilerParams(dimension_semantics=sem, vmem_limit_bytes=VMEM_LIMIT)


def _const_spec(shape):
    nd = len(shape)
    return pl.BlockSpec(shape, lambda *_: (0,) * nd, pipeline_mode=pl.Buffered(1))


def _standardize(x):
    mu = jnp.mean(x, axis=-1, keepdims=True)
    xc = x - mu
    var = jnp.mean(xc * xc, axis=-1, keepdims=True)
    return xc * lax.rsqrt(var + LN_EPS)


def _silu(x):
    return x * jax.nn.sigmoid(x)


def _gelu_tanh(x):
    return 0.5 * x * (1.0 + jnp.tanh(math.sqrt(2.0 / math.pi) * (x + 0.044715 * (x * x * x))))


def _adaln_kernel(c_ref, w_ref, b_ref, o_ref):
    a = _silu(c_ref[...])
    o_ref[...] = jnp.dot(a, w_ref[...], precision=HIGHEST, preferred_element_type=F32) + b_ref[...]


def _adaln(c, w, b):
    bsz, d = c.shape
    n = w.shape[1]
    tn = 512
    return pl.pallas_call(
        _adaln_kernel,
        grid=(n // tn,),
        in_specs=[pl.BlockSpec((bsz, d), lambda j: (0, 0)),
                  pl.BlockSpec((d, tn), lambda j: (0, j)),
                  pl.BlockSpec((1, tn), lambda j: (0, j))],
        out_specs=pl.BlockSpec((bsz, tn), lambda j: (0, j)),
        out_shape=jax.ShapeDtypeStruct((bsz, n), F32),
        compiler_params=_cparams(("parallel",)),
        name="adaln",
    )(c, w, b.reshape(1, n))


def _proj_kernel(x_ref, sc_ref, sh_ref, ws5t_ref, bs5_ref, wf_ref, bf_ref, cs_ref, wg_ref, bg_ref,
                 us5_ref, z_ref, gates_ref):
    tm = x_ref.shape[0]
    u = (_standardize(x_ref[...]) * (1.0 + sc_ref[0]) + sh_ref[0]).astype(BF16)
    p = lax.dot_general(ws5t_ref[...], u, (((1,), (1,)), ((), ())), preferred_element_type=F32)
    p = p + bs5_ref[:, 0:1]
    d_s5, n_j, _ = us5_ref.shape
    us5_2d = us5_ref.reshape(d_s5 * n_j, LANE)
    for j in range(n_j):
        us5_2d[pl.ds(j, d_s5, stride=n_j), :] = p[:, j * LANE:(j + 1) * LANE]
    uf = (jnp.dot(u, wf_ref[...], preferred_element_type=F32) + bf_ref[...]).astype(BF16)
    d_f = uf.shape[1]
    fw = d_f // FOURIER_GROUPS
    for q in range(FOURIER_GROUPS):
        zq = jnp.dot(uf[:, q * fw:(q + 1) * fw], cs_ref[...], preferred_element_type=F32)
        z_ref[:, q * fw:(q + 1) * fw] = zq[:, :fw].astype(BF16)
        z_ref[:, d_f + q * fw:d_f + (q + 1) * fw] = zq[:, fw:].astype(BF16)
    n_g = wg_ref.shape[1]
    half = n_g // 2
    for q in range(2):
        gp = jnp.dot(u, wg_ref[:, q * half:(q + 1) * half], preferred_element_type=F32)
        gp = gp + bg_ref[:, q * half:(q + 1) * half]
        gates_ref[:, q * half:(q + 1) * half] = jax.nn.sigmoid(gp).astype(BF16)


def _proj(x2, sc, sh, ws5t, bs5, wf, bf, cs, wg, bg, seq):
    t, d = x2.shape
    d_s5 = ws5t.shape[0]
    d_f = wf.shape[1]
    n_g = wg.shape[1]
    tm = 1024
    tpb = seq // tm
    bsz = t // seq
    return pl.pallas_call(
        _proj_kernel,
        grid=(t // tm,),
        in_specs=[pl.BlockSpec((tm, d), lambda i: (i, 0)),
                  pl.BlockSpec((1, 1, d), lambda i: (i // tpb, 0, 0)),
                  pl.BlockSpec((1, 1, d), lambda i: (i // tpb, 0, 0)),
                  _const_spec((d_s5, d)), _const_spec((d_s5, LANE)),
                  _const_spec((d, d_f)), _const_spec((1, d_f)),
                  _const_spec(cs.shape),
                  _const_spec((d, n_g)), _const_spec((1, n_g))],
        out_specs=[pl.BlockSpec((d_s5, tm // LANE, LANE), lambda i: (0, i, 0)),
                   pl.BlockSpec((tm, 2 * d_f), lambda i: (i, 0)),
                   pl.BlockSpec((tm, n_g), lambda i: (i, 0))],
        out_shape=[jax.ShapeDtypeStruct((d_s5, t // LANE, LANE), F32),
                   jax.ShapeDtypeStruct((t, 2 * d_f), BF16),
                   jax.ShapeDtypeStruct((t, n_g), BF16)],
        compiler_params=_cparams(("parallel",)),
        name="proj",
    )(x2, sc.reshape(bsz, 1, d), sh.reshape(bsz, 1, d), ws5t, bs5, wf, bf, cs, wg, bg)


def _s5_tables(lam_re, lam_im, log_dt, b_re, b_im, c_re, c_im, d_skip):
    L = LANE
    hp = HIGHEST
    lr, li = lam_re.astype(F32), lam_im.astype(F32)
    dt = jnp.exp(log_dt.astype(F32))[:, :, None]
    mag = jnp.exp(lr * dt)
    ang = li * dt
    ab_re, ab_im = mag * jnp.cos(ang), mag * jnp.sin(ang)
    den = lr * lr + li * li
    nr = ab_re - 1.0
    coef_re = (nr * lr + ab_im * li) / den
    coef_im = (ab_im * lr - nr * li) / den
    br, bi = b_re.astype(F32), b_im.astype(F32)
    bb_re = coef_re[..., None] * br - coef_im[..., None] * bi
    bb_im = coef_re[..., None] * bi + coef_im[..., None] * br
    cr, ci = c_re.astype(F32), c_im.astype(F32)
    n_g, n_p, n_h = br.shape[1], br.shape[2], br.shape[3]

    k = jnp.arange(L + 1, dtype=F32)[None, None, :, None]
    pmag = jnp.exp(k * (lr * dt)[:, :, None, :])
    pang = k * (li * dt)[:, :, None, :]
    pw_re, pw_im = pmag * jnp.cos(pang), pmag * jnp.sin(pang)

    m_re = cr[:, :, :, None, :] * jnp.swapaxes(bb_re, 2, 3)[:, :, None, :, :] \
        - ci[:, :, :, None, :] * jnp.swapaxes(bb_im, 2, 3)[:, :, None, :, :]
    m_im = cr[:, :, :, None, :] * jnp.swapaxes(bb_im, 2, 3)[:, :, None, :, :] \
        + ci[:, :, :, None, :] * jnp.swapaxes(bb_re, 2, 3)[:, :, None, :, :]
    kap = jnp.einsum("dgohp,dgkp->dgohk", m_re, pw_re, precision=hp) \
        - jnp.einsum("dgohp,dgkp->dgohk", m_im, pw_im, precision=hp)
    kb = kap[1]
    kpos = kap[0, ..., :L].at[..., 0].add(kb[..., 0]).reshape(n_g, n_h * n_h, L)
    kneg = jnp.concatenate([jnp.zeros_like(kb[..., 0:1]), kb[..., L - 1:0:-1]], axis=-1)
    kneg = kneg.reshape(n_g, n_h * n_h, L)

    cat = lambda *parts: jnp.concatenate(parts, axis=-1)
    pf_re, pf_im = pw_re[0, :, L - 1::-1][:, :L], pw_im[0, :, L - 1::-1][:, :L]
    pb_re, pb_im = pw_re[1, :, :L], pw_im[1, :, :L]
    sp, spsw = cat(pf_re, pf_im, pb_re, pb_im), cat(pf_im, pf_re, pb_im, pb_re)
    bt_re, bt_im = jnp.swapaxes(bb_re, 2, 3), jnp.swapaxes(bb_im, 2, 3)
    bx, by = cat(bt_re[0], bt_re[0], bt_re[1], bt_re[1]), cat(-bt_im[0], bt_im[0], -bt_im[1], bt_im[1])

    qf_re, qf_im = pw_re[0, :, 1:L + 1], pw_im[0, :, 1:L + 1]
    qb_re, qb_im = pw_re[1, :, L:0:-1], pw_im[1, :, L:0:-1]
    qt, qtsw = cat(qf_re, qf_im, qb_re, qb_im), cat(qf_im, qf_re, qb_im, qb_re)
    cx, cy = cat(cr[0], -cr[0], cr[1], -cr[1]), cat(-ci[0], -ci[0], -ci[1], -ci[1])

    al_re, al_im = pw_re[:, :, L], pw_im[:, :, L]
    al = jnp.stack([cat(al_re[0], al_re[0], al_re[1], al_re[1]),
                    cat(-al_im[0], al_im[0], -al_im[1], al_im[1])], axis=1)

    dsk = jnp.broadcast_to(d_skip.astype(F32).reshape(n_g, n_h, 1), (n_g, n_h, L))
    return kpos, kneg, sp, spsw, bx, by, qt, qtsw, cx, cy, al, dsk


def _s5_kernel(a_ref, kpos_ref, kneg_ref, sp_ref, spsw_ref, bx_ref, by_ref, qt_ref, qtsw_ref, cx_ref, cy_ref,
               al_ref, d_ref, y_ref, abf_ref, tp_ref, ws_ref, sf_ref, sb_ref, sfs_ref, sbs_ref, xf_ref, xb_ref,
               *, n_chunk):
    n_h, r, L = a_ref.shape
    bsz = r // n_chunk
    half = sp_ref.shape[2] // 2
    nt = (((1,), (1,)), ((), ()))

    sp, spsw = sp_ref[0], spsw_ref[0]
    for h in range(n_h):
        abf_ref[:, h * L:(h + 1) * L] = a_ref[h].astype(BF16)
        ws_ref[h * L:(h + 1) * L, :] = (sp * bx_ref[0, h:h + 1, :] + spsw * by_ref[0, h:h + 1, :]).astype(BF16)
    abf = abf_ref[...]

    s_all = jnp.dot(abf, ws_ref[...], preferred_element_type=F32)
    sf_ref[...] = s_all[:, :half]
    sb_ref[...] = s_all[:, half:]
    sfs_ref[...] = pltpu.roll(s_all[:, :half], half // 2, 1)
    sbs_ref[...] = pltpu.roll(s_all[:, half:], half // 2, 1)

    alx = al_ref[0, 0:1, :]
    aly = al_ref[0, 1:2, :]

    zero = jnp.zeros((bsz, half), F32)
    ef, efs, eb, ebs = zero, zero, zero, zero
    for c in range(n_chunk):
        cb = n_chunk - 1 - c
        rows_f = pl.ds(c, bsz, stride=n_chunk)
        rows_b = pl.ds(cb, bsz, stride=n_chunk)
        xf_ref[rows_f, :] = ef
        xb_ref[rows_b, :] = eb
        xf, yf = alx[:, :half], aly[:, :half]
        xb, yb = alx[:, half:], aly[:, half:]
        ef, efs = ef * xf + efs * yf + sf_ref[rows_f, :], efs * xf - ef * yf + sfs_ref[rows_f, :]
        eb, ebs = eb * xb + ebs * yb + sb_ref[rows_b, :], ebs * xb - eb * yb + sbs_ref[rows_b, :]
    xin = jnp.concatenate([xf_ref[...], xb_ref[...]], axis=1).astype(BF16)

    s_idx = lax.broadcasted_iota(I32, (L, L), 0)
    j_idx = lax.broadcasted_iota(I32, (L, L), 1)
    fwd_part = j_idx + s_idx < L
    qt, qtsw = qt_ref[0], qtsw_ref[0]

    def pair(op, _):
        for oo in range(2):
            o = 2 * op + oo
            for h in range(n_h):
                row = o * n_h + h
                kp = jnp.broadcast_to(kpos_ref[0, pl.ds(row, 1), :], (L, L))
                kn = jnp.broadcast_to(kneg_ref[0, pl.ds(row, 1), :], (L, L))
                tile = pltpu.roll(jnp.where(fwd_part, kp, kn), 0, 1, stride=1, stride_axis=0)
                tp_ref[h * L:(h + 1) * L, oo * L:(oo + 1) * L] = tile.astype(BF16)
        wo_t = jnp.concatenate(
            [qt * cx_ref[0, pl.ds(2 * op + oo, 1), :] + qtsw * cy_ref[0, pl.ds(2 * op + oo, 1), :]
             for oo in range(2)], axis=0).astype(BF16)
        yp = jnp.dot(abf, tp_ref[...], preferred_element_type=F32)
        yp = yp + lax.dot_general(xin, wo_t, nt, preferred_element_type=F32)
        for oo in range(2):
            o = 2 * op + oo
            y_ref[o] = yp[:, oo * L:(oo + 1) * L] + a_ref[o] * d_ref[0, pl.ds(o, 1), :]
        return 0

    lax.fori_loop(0, n_h // 2, pair, 0)


def _s5(us5, tables, n_chunk):
    kpos, kneg, sp, spsw, bx, by, qt, qtsw, cx, cy, al, dsk = tables
    d_s5, r, L = us5.shape
    n_g = kpos.shape[0]
    n_h = d_s5 // n_g
    n_st = sp.shape[2]
    g3 = lambda g: (g, 0, 0)
    pw_spec = pl.BlockSpec((1, L, n_st), g3)
    hv_spec = pl.BlockSpec((1, n_h, n_st), g3)
    return pl.pallas_call(
        functools.partial(_s5_kernel, n_chunk=n_chunk),
        grid=(n_g,),
        in_specs=[pl.BlockSpec((n_h, r, L), g3),
                  pl.BlockSpec((1, n_h * n_h, L), g3), pl.BlockSpec((1, n_h * n_h, L), g3),
                  pw_spec, pw_spec, hv_spec, hv_spec, pw_spec, pw_spec, hv_spec, hv_spec,
                  pl.BlockSpec((1, 2, n_st), g3), pl.BlockSpec((1, n_h, L), g3)],
        out_specs=pl.BlockSpec((n_h, r, L), g3),
        out_shape=jax.ShapeDtypeStruct((d_s5, r, L), F32),
        scratch_shapes=[pltpu.VMEM((r, n_h * L), BF16), pltpu.VMEM((n_h * L, 2 * L), BF16),
                        pltpu.VMEM((n_h * L, n_st), BF16)]
        + [pltpu.VMEM((r, n_st // 2), F32)] * 6,
        compiler_params=_cparams(("parallel",)),
        name="s5",
    )(us5, kpos, kneg, sp, spsw, bx, by, qt, qtsw, cx, cy, al, dsk)


DFT_SPLIT = 64


def _dft_tables(seq, fw):
    def angles(mult, n_rows, n):
        s = jnp.arange(n, dtype=I32)[None, :]
        q = jnp.arange(n_rows, dtype=I32)[:, None]
        return (2.0 * math.pi / n) * ((mult * q * s) % n).astype(F32)

    ang_a = angles(DFT_SPLIT, seq // DFT_SPLIT, seq)
    ang_b = angles(1, DFT_SPLIT, seq)
    seq_tabs = (jnp.cos(ang_a), jnp.sin(ang_a), jnp.cos(ang_b), jnp.sin(ang_b))
    ang_c = angles(1, fw, fw)
    return seq_tabs, jnp.concatenate([jnp.cos(ang_c), jnp.sin(ang_c)], axis=1).astype(BF16)


DFT_EXTRA = 16


def _seqdft_kernel(ca_ref, sa_ref, cb_ref, sb_ref, flip_ref, z_ref, lo_ref, hi_ref, f_ref, *, scale):
    seq = z_ref.shape[0]
    tk, d_f = lo_ref.shape

    @pl.when(pl.program_id(1) == 0)
    def _():
        a0 = pl.program_id(0) * (tk // DFT_SPLIT)
        cb, sb = cb_ref[...], sb_ref[...]

        def put(rows, ca, sa, n_b):
            f_ref[rows, :seq] = (ca * cb[:n_b] - sa * sb[:n_b]).astype(BF16)
            f_ref[rows, seq:] = (-(sa * cb[:n_b] + ca * sb[:n_b])).astype(BF16)

        for j in range(tk // DFT_SPLIT):
            put(slice(j * DFT_SPLIT, (j + 1) * DFT_SPLIT),
                ca_ref[pl.ds(a0 + j, 1), :], sa_ref[pl.ds(a0 + j, 1), :], DFT_SPLIT)
        a_next = a0 + tk // DFT_SPLIT
        put(slice(tk, tk + DFT_EXTRA), ca_ref[pl.ds(a_next, 1), :], sa_ref[pl.ds(a_next, 1), :], DFT_EXTRA)

    p = jnp.dot(f_ref[:, :seq], z_ref[:, :d_f], preferred_element_type=F32)
    q = jnp.dot(f_ref[:, seq:], z_ref[:, d_f:], preferred_element_type=F32)
    lo_ref[...] = ((p[:tk] + q[:tk]) * scale).astype(lo_ref.dtype)
    mirror = ((p - q) * scale).astype(BF16)
    hi_ref[...] = jnp.dot(flip_ref[...], mirror, preferred_element_type=F32).astype(hi_ref.dtype)


def _seqdft(seq_tabs, z, seq, fw):
    t, two_df = z.shape
    d_f = two_df // 2
    bsz = t // seq
    tk = 512
    nk2 = seq // (2 * tk)
    scale = 1.0 / math.sqrt(seq * fw)
    flip = (jnp.arange(tk)[:, None] + jnp.arange(tk + DFT_EXTRA)[None, :] == tk).astype(BF16)
    half = jax.ShapeDtypeStruct((t // 2, d_f), BF16)
    return pl.pallas_call(
        functools.partial(_seqdft_kernel, scale=scale),
        grid=(nk2, bsz),
        in_specs=[_const_spec(tab.shape) for tab in seq_tabs] + [_const_spec(flip.shape)]
        + [pl.BlockSpec((seq, two_df), lambda k, b: (b, 0))],
        out_specs=[pl.BlockSpec((tk, d_f), lambda k, b: (b * nk2 + k, 0)),
                   pl.BlockSpec((tk, d_f), lambda k, b: (b * nk2 + nk2 - 1 - k, 0))],
        out_shape=[half, half],
        scratch_shapes=[pltpu.VMEM((tk + DFT_EXTRA, 2 * seq), BF16)],
        compiler_params=_cparams(("parallel", "arbitrary")),
        name="seqdft",
    )(*seq_tabs, flip, z)


def _glu_kernel(y_ref, w_ref, b_ref, o_ref, zt_ref):
    d_s5, n_j, _ = y_ref.shape
    d = o_ref.shape[1]
    y2 = y_ref.reshape(d_s5 * n_j, LANE)
    for j in range(n_j):
        zt_ref[j * LANE:(j + 1) * LANE, :] = _gelu_tanh(y2[pl.ds(j, d_s5, stride=n_j), :]).T.astype(BF16)
    zt = zt_ref[...]
    a = jnp.dot(zt, w_ref[:, :d], preferred_element_type=F32) + b_ref[:, :d]
    g = jnp.dot(zt, w_ref[:, d:], preferred_element_type=F32) + b_ref[:, d:]
    o_ref[...] = (a * jax.nn.sigmoid(g)).astype(BF16)


def _glu(ys5, w, b):
    d_s5, r, L = ys5.shape
    t = r * L
    n = w.shape[1]
    tm = 1024
    return pl.pallas_call(
        _glu_kernel,
        grid=(t // tm,),
        in_specs=[pl.BlockSpec((d_s5, tm // L, L), lambda i: (0, i, 0)),
                  _const_spec((d_s5, n)), _const_spec((1, n))],
        out_specs=pl.BlockSpec((tm, n // 2), lambda i: (i, 0)),
        out_shape=jax.ShapeDtypeStruct((t, n // 2), BF16),
        scratch_shapes=[pltpu.VMEM((tm, d_s5), BF16)],
        compiler_params=_cparams(("parallel",)),
        name="glu",
    )(ys5, w, b)


U32 = jnp.uint32
PACK_SUB = 4


def _row_slab(ref, r):
    return ref.at[pl.ds(pl.multiple_of(r * PACK_SUB, PACK_SUB), PACK_SUB), :]


def _pack_rows(ref, v, first=0):
    rows, d = v.shape
    half = d // 2
    bits = lax.bitcast_convert_type(v.astype(BF16).astype(F32), U32)
    for c in range(PACK_SUB):
        lo = bits[:, c * LANE:(c + 1) * LANE] >> 16
        hi = bits[:, half + c * LANE:half + (c + 1) * LANE] & jnp.uint32(0xFFFF0000)
        ref[pl.ds(first * PACK_SUB + c, rows, stride=PACK_SUB), :] = hi | lo


def _unpack_rows(ref, rows, first=0):
    los, his = [], []
    for c in range(PACK_SUB):
        w = ref[pl.ds(first * PACK_SUB + c, rows, stride=PACK_SUB), :]
        los.append(lax.bitcast_convert_type(w << 16, F32))
        his.append(lax.bitcast_convert_type(w & jnp.uint32(0xFFFF0000), F32))
    return jnp.concatenate(los + his, axis=1)


def _merge_kernel(x_ref, brs_ref, yf_lo_ref, yf_hi_ref, gates_ref, g1_ref, sc_ref, sh_ref, wfo_ref, bfo_ref,
                  wo_ref, bo_ref, lng_ref, lnb_ref, whi_ref, wlo_ref, bias_ref, tri_ref,
                  x1_ref, hhi_ref, hp_ref, eidx_ref, gate_ref, rank_ref, cnt_ref, base_ref,
                  *, alpha, tiles_per_seq):
    d = x_ref.shape[1]
    in_first_half = (pl.program_id(0) % tiles_per_seq) < tiles_per_seq // 2
    yf = jnp.where(in_first_half, yf_lo_ref[...], yf_hi_ref[...])
    br_f = jnp.dot(yf, wfo_ref[...], preferred_element_type=F32) + bfo_ref[...]
    merged = gates_ref[:, :d].astype(F32) * brs_ref[...].astype(F32) + gates_ref[:, d:].astype(F32) * br_f
    mix = jnp.dot(merged.astype(BF16), wo_ref[...], preferred_element_type=F32) + bo_ref[...]
    v = alpha * x_ref[...] + g1_ref[0] * mix
    x1 = _standardize(v) * lng_ref[...] + lnb_ref[...]
    x1_ref[...] = x1
    h = _standardize(x1) * (1.0 + sc_ref[0]) + sh_ref[0]
    hhi = h.astype(BF16)
    hhi_ref[...] = hhi
    _pack_rows(hp_ref, h)
    _route(hhi, (h - hhi.astype(F32)).astype(BF16), whi_ref, wlo_ref, bias_ref, tri_ref,
           eidx_ref, gate_ref, rank_ref, cnt_ref, base_ref)


def _merge(x2, brs, yf_lo, yf_hi, gates, g1, sc2, sh2, wfo, bfo, wo, bo, lng, lnb, wrt_hi, wrt_lo, bias,
           seq, alpha):
    t, d = x2.shape
    d_f = yf_lo.shape[1]
    n_e = wrt_hi.shape[0]
    tm = 512
    tri = (jnp.arange(tm)[:, None] < jnp.arange(tm)[None, :]).astype(BF16)
    kt = lambda i: (0, i)
    tpb = seq // tm
    tph = tpb // 2
    bsz = t // seq
    row = lambda i: (i, 0)
    bat = lambda i: (i // tpb, 0, 0)
    lo_row = lambda i: ((i // tpb) * tph + jnp.minimum(i % tpb, tph - 1), 0)
    hi_row = lambda i: ((i // tpb) * tph + jnp.maximum(i % tpb - tph, 0), 0)
    return pl.pallas_call(
        functools.partial(_merge_kernel, alpha=alpha, tiles_per_seq=tpb),
        grid=(t // tm,),
        in_specs=[pl.BlockSpec((tm, d), row), pl.BlockSpec((tm, d), row),
                  pl.BlockSpec((tm, d_f), lo_row), pl.BlockSpec((tm, d_f), hi_row),
                  pl.BlockSpec((tm, 2 * d), row),
                  pl.BlockSpec((1, 1, d), bat), pl.BlockSpec((1, 1, d), bat), pl.BlockSpec((1, 1, d), bat),
                  _const_spec((d_f, d)), _const_spec((1, d)), _const_spec((d, d)), _const_spec((1, d)),
                  _const_spec((1, d)), _const_spec((1, d)),
                  _const_spec((n_e, d)), _const_spec((n_e, d)), _const_spec((n_e, LANE)), _const_spec((tm, tm))],
        out_specs=[pl.BlockSpec((tm, d), row), pl.BlockSpec((tm, d), row),
                   pl.BlockSpec((tm * PACK_SUB, LANE), row),
                   pl.BlockSpec((TOP_K, tm), kt), pl.BlockSpec((TOP_K, tm), kt), pl.BlockSpec((TOP_K, tm), kt),
                   pl.BlockSpec((n_e, LANE), lambda i: (0, 0))],
        out_shape=[jax.ShapeDtypeStruct((t, d), F32), jax.ShapeDtypeStruct((t, d), BF16),
                   jax.ShapeDtypeStruct((t * PACK_SUB, LANE), U32),
                   jax.ShapeDtypeStruct((TOP_K, t), I32), jax.ShapeDtypeStruct((TOP_K, t), F32),
                   jax.ShapeDtypeStruct((TOP_K, t), I32), jax.ShapeDtypeStruct((n_e, LANE), I32)],
        scratch_shapes=[pltpu.VMEM((n_e, LANE), F32)],
        compiler_params=_cparams(("arbitrary",)),
        name="merge_route",
    )(x2, brs, yf_lo, yf_hi, gates, g1.reshape(bsz, 1, d), sc2.reshape(bsz, 1, d), sh2.reshape(bsz, 1, d),
      wfo, bfo, wo, bo, lng, lnb, wrt_hi, wrt_lo,
      jnp.broadcast_to(bias.astype(F32).reshape(n_e, 1), (n_e, LANE)), tri)


def _route(hhi, hlo, whi_ref, wlo_ref, bias_ref, tri_ref, eidx_ref, gate_ref, rank_ref, cnt_ref, base_ref):
    n_e = whi_ref.shape[0]
    tm = hhi.shape[0]
    gsz = n_e // N_EXPERT_GROUPS
    nt = (((1,), (1,)), ((), ()))

    @pl.when(pl.program_id(0) == 0)
    def _():
        base_ref[...] = jnp.zeros_like(base_ref)

    logits = lax.dot_general(whi_ref[...], hhi, nt, preferred_element_type=F32)
    logits = logits + lax.dot_general(wlo_ref[...], hhi, nt, preferred_element_type=F32)
    logits = logits + lax.dot_general(whi_ref[...], hlo, nt, preferred_element_type=F32)
    scores = jax.nn.sigmoid(logits)
    sel = scores + bias_ref[:, 0:1]

    g3 = sel.reshape(N_EXPERT_GROUPS, gsz, tm)
    i3 = lax.broadcasted_iota(I32, g3.shape, 1).astype(F32)
    m1 = jnp.max(g3, axis=1, keepdims=True)
    first = jnp.min(jnp.where(g3 == m1, i3, float(gsz)), axis=1, keepdims=True)
    m2 = jnp.max(jnp.where(i3 == first, NEG_INF, g3), axis=1, keepdims=True)
    gs = (m1 + m2).reshape(N_EXPERT_GROUPS, tm)

    gi = lax.broadcasted_iota(I32, gs.shape, 0).astype(F32)
    gsel = jnp.zeros(gs.shape, F32)
    cur = gs
    for _ in range(TOPK_GROUPS):
        m = jnp.max(cur, axis=0, keepdims=True)
        f = jnp.min(jnp.where(cur == m, gi, float(N_EXPERT_GROUPS)), axis=0, keepdims=True)
        pick = gi == f
        gsel = jnp.where(pick, 1.0, gsel)
        cur = jnp.where(pick, NEG_INF, cur)
    gmask = jnp.broadcast_to(gsel.reshape(N_EXPERT_GROUPS, 1, tm), g3.shape).reshape(n_e, tm)
    masked = jnp.where(gmask > 0.5, sel, NEG_INF)

    ri = lax.broadcasted_iota(I32, (n_e, tm), 0).astype(F32)
    picks = []
    gates = []
    multihot = jnp.zeros((n_e, tm), F32)
    for _ in range(TOP_K):
        m = jnp.max(masked, axis=0, keepdims=True)
        f = jnp.min(jnp.where(masked == m, ri, float(n_e)), axis=0, keepdims=True)
        pick = ri == f
        picks.append(f)
        gates.append(jnp.sum(jnp.where(pick, scores, 0.0), axis=0, keepdims=True))
        multihot = jnp.where(pick, 1.0, multihot)
        masked = jnp.where(pick, NEG_INF, masked)
    gsum = gates[0]
    for g in gates[1:]:
        gsum = gsum + g

    rankmat = jnp.dot(multihot.astype(BF16), tri_ref[...], preferred_element_type=F32) + base_ref[:, 0:1]
    for k in range(TOP_K):
        pick = ri == picks[k]
        eidx_ref[k:k + 1, :] = picks[k].astype(I32)
        gate_ref[k:k + 1, :] = gates[k] / gsum * ROUTED_SCALE
        rank_ref[k:k + 1, :] = jnp.sum(jnp.where(pick, rankmat, 0.0), axis=0, keepdims=True).astype(I32)
    base_ref[...] = base_ref[...] + jnp.sum(multihot, axis=1, keepdims=True)
    cnt_ref[...] = base_ref[...].astype(I32)


def _dest_kernel(eidx_ref, rank_ref, pstart_ref, dest_ref):
    n_e = pstart_ref.shape[0]
    tm = eidx_ref.shape[1]
    ri = lax.broadcasted_iota(I32, (n_e, tm), 0)
    ps = pstart_ref[:, 0:1].astype(F32)
    for k in range(TOP_K):
        hit = ri == eidx_ref[k:k + 1, :]
        base = jnp.sum(jnp.where(hit, ps, 0.0), axis=0, keepdims=True)
        dest_ref[k:k + 1, :] = base.astype(I32) + rank_ref[k:k + 1, :]


def _dest(eidx, rank, pstart):
    k, t = eidx.shape
    n_e = pstart.shape[0]
    tm = 2048
    kt = lambda i: (0, i)
    return pl.pallas_call(
        _dest_kernel,
        grid=(t // tm,),
        in_specs=[pl.BlockSpec((k, tm), kt), pl.BlockSpec((k, tm), kt), _const_spec((n_e, LANE))],
        out_specs=pl.BlockSpec((k, tm), kt),
        out_shape=jax.ShapeDtypeStruct((k, t), I32),
        compiler_params=_cparams(("parallel",)),
        name="dest",
    )(eidx, rank, jnp.broadcast_to(pstart.astype(I32).reshape(n_e, 1), (n_e, LANE)))


def _zero_fill(pad_lo_ref, pad_n_ref, zeros_ref, rows_ref, sem):
    def each_copy(act):
        def per_entry(e, _):
            lo = pad_lo_ref[e]
            n = pad_n_ref[e]
            n_full = n // FFN_BLOCK

            def full(j, _):
                act(pltpu.make_async_copy(
                    zeros_ref, rows_ref.at[pl.ds(pl.multiple_of((lo + j * FFN_BLOCK) * PACK_SUB, PACK_SUB),
                                                 FFN_BLOCK * PACK_SUB), :], sem))
                return 0

            lax.fori_loop(0, n_full, full, 0)
            off = lo + n_full * FFN_BLOCK
            rem = n - n_full * FFN_BLOCK
            bit = FFN_BLOCK // 2
            while bit >= 1:
                take = rem & bit

                @pl.when(take != 0)
                def _(off=off, bit=bit):
                    act(pltpu.make_async_copy(
                        zeros_ref.at[pl.ds(0, bit * PACK_SUB), :],
                        rows_ref.at[pl.ds(pl.multiple_of(off * PACK_SUB, PACK_SUB), bit * PACK_SUB), :], sem))

                off = off + take
                bit //= 2
            return 0

        lax.fori_loop(0, pad_lo_ref.shape[0], per_entry, 0)

    each_copy(lambda cp: cp.start())
    each_copy(lambda cp: cp.wait())


def _dispatch_kernel(pad_lo_ref, pad_n_ref, dest_ref, hp_ref, rows_ref, zeros_ref, sem, zsem):
    tm = hp_ref.shape[0] // PACK_SUB

    @pl.when(pl.program_id(0) == 0)
    def _():
        zeros_ref[...] = jnp.zeros_like(zeros_ref)
        _zero_fill(pad_lo_ref, pad_n_ref, zeros_ref, rows_ref, zsem)

    def body(tt, _):
        for k in range(TOP_K):
            pltpu.make_async_copy(_row_slab(hp_ref, tt), _row_slab(rows_ref, dest_ref[0, 0, k * tm + tt]),
                                  sem).start(priority=k % 2)
        return 0

    lax.fori_loop(0, tm, body, 0, unroll=ISSUE_UNROLL)
    all_rows = rows_ref.at[pl.ds(0, TOP_K * tm * PACK_SUB), :]
    pltpu.make_async_copy(all_rows, all_rows, sem).wait()


def _dispatch(hp, dest_tiles, pad_lo, pad_n, n_rows):
    n_tile, _, per_tile = dest_tiles.shape
    tm = per_tile // TOP_K
    return pl.pallas_call(
        _dispatch_kernel,
        grid_spec=pltpu.PrefetchScalarGridSpec(
            num_scalar_prefetch=2,
            grid=(n_tile,),
            in_specs=[pl.BlockSpec((1, 1, per_tile), lambda i, lo, n: (i, 0, 0), memory_space=pltpu.SMEM),
                      pl.BlockSpec((tm * PACK_SUB, LANE), lambda i, lo, n: (i, 0))],
            out_specs=pl.BlockSpec(memory_space=pl.ANY),
            scratch_shapes=[pltpu.VMEM((FFN_BLOCK * PACK_SUB, LANE), U32),
                            pltpu.SemaphoreType.DMA, pltpu.SemaphoreType.DMA],
        ),
        out_shape=jax.ShapeDtypeStruct((n_rows * PACK_SUB, LANE), U32),
        compiler_params=_cparams(("arbitrary",)),
        name="dispatch",
    )(pad_lo, pad_n, dest_tiles, hp)


def _ffn_kernel(blk_e_ref, n_used_ref, next_blk_ref, rows_ref, wg_ref, wu_ref, wd_ref, y_ref,
                wg_st_ref, wu_st_ref, wd_st_ref, wgu_bf_ref, wd_bf_ref, sems):
    b = pl.program_id(0)
    n_used = n_used_ref[0]
    live = b < n_used
    d_e = wd_st_ref.shape[0]
    e = blk_e_ref[b]

    def fetch(expert):
        return (pltpu.make_async_copy(wg_ref.at[expert], wg_st_ref, sems.at[0]),
                pltpu.make_async_copy(wu_ref.at[expert], wu_st_ref, sems.at[1]),
                pltpu.make_async_copy(wd_ref.at[expert], wd_st_ref, sems.at[2]))

    @pl.when(b == 0)
    def _():
        for cp in fetch(e):
            cp.start()

    first_block = jnp.logical_and(live, jnp.logical_or(b == 0, e != blk_e_ref[jnp.maximum(b - 1, 0)]))

    @pl.when(first_block)
    def _():
        for cp in fetch(e):
            cp.wait()
        wgu_bf_ref[:, :d_e] = wg_st_ref[...].astype(BF16)
        wgu_bf_ref[:, d_e:] = wu_st_ref[...].astype(BF16)
        wd_bf_ref[...] = wd_st_ref[...].astype(BF16)
        nb = next_blk_ref[e]

        @pl.when(nb < n_used)
        def _():
            for cp in fetch(blk_e_ref[nb]):
                cp.start()

    @pl.when(live)
    def _():
        x = _unpack_rows(rows_ref, FFN_BLOCK).astype(BF16)
        au = jnp.dot(x, wgu_bf_ref[...], preferred_element_type=F32)
        hid = (_silu(au[:, :d_e]) * au[:, d_e:]).astype(BF16)
        _pack_rows(y_ref, jnp.dot(hid, wd_bf_ref[...], preferred_element_type=F32))

    @pl.when(jnp.logical_not(live))
    def _():
        y_ref[...] = jnp.zeros_like(y_ref)


def _ffn(rows, blk_e, n_used, next_blk, wg, wu, wd):
    n_rows = rows.shape[0] // PACK_SUB
    n_e, d, d_e = wg.shape
    n_blk = n_rows // FFN_BLOCK
    blk = lambda b, *_: (b, 0)
    hbm = pl.BlockSpec(memory_space=pl.ANY)
    return pl.pallas_call(
        _ffn_kernel,
        grid_spec=pltpu.PrefetchScalarGridSpec(
            num_scalar_prefetch=3,
            grid=(n_blk,),
            in_specs=[pl.BlockSpec((FFN_BLOCK * PACK_SUB, LANE), blk), hbm, hbm, hbm],
            out_specs=pl.BlockSpec((FFN_BLOCK * PACK_SUB, LANE), blk),
            scratch_shapes=[pltpu.VMEM((d, d_e), F32), pltpu.VMEM((d, d_e), F32), pltpu.VMEM((d_e, d), F32),
                            pltpu.VMEM((d, 2 * d_e), BF16), pltpu.VMEM((d_e, d), BF16),
                            pltpu.SemaphoreType.DMA((3,))],
        ),
        out_shape=jax.ShapeDtypeStruct((n_rows * PACK_SUB, LANE), U32),
        compiler_params=_cparams(("arbitrary",)),
        name="ffn",
    )(blk_e, n_used, next_blk, rows, wg, wu, wd)


def _final_kernel(dest_ref, dnext_ref, x1_ref, h_ref, gate_ref, g2_ref, wsg_ref, wsu_ref, wsd_ref, lng_ref, lnb_ref,
                  y_ref, o_ref, ybuf_ref, sems, *, alpha):
    i = pl.program_id(0)
    tm = x1_ref.shape[0]
    per_tile = TOP_K * tm

    def gather(d_ref, slot):
        def body(tt, _):
            for k in range(TOP_K):
                pltpu.make_async_copy(_row_slab(y_ref, d_ref[0, 0, k * tm + tt]),
                                      _row_slab(ybuf_ref, slot * per_tile + k * tm + tt),
                                      sems.at[slot]).start(priority=k % 2)
            return 0

        lax.fori_loop(0, tm, body, 0, unroll=ISSUE_UNROLL)

    @pl.when(i == 0)
    def _():
        gather(dest_ref, 0)

    @pl.when(i + 1 < pl.num_programs(0))
    def _():
        gather(dnext_ref, (i + 1) % 2)

    h = h_ref[...]
    a = jnp.dot(h, wsg_ref[...], preferred_element_type=F32)
    u = jnp.dot(h, wsu_ref[...], preferred_element_type=F32)
    shared = jnp.dot((_silu(a) * u).astype(BF16), wsd_ref[...], preferred_element_type=F32)

    slot = i % 2
    half = ybuf_ref.at[pl.ds(pl.multiple_of(slot * per_tile * PACK_SUB, per_tile * PACK_SUB), per_tile * PACK_SUB), :]
    pltpu.make_async_copy(half, half, sems.at[slot]).wait()
    first = slot * per_tile
    routed = gate_ref[:, 0:1] * _unpack_rows(ybuf_ref, tm, first=first)
    for k in range(1, TOP_K):
        routed = routed + gate_ref[:, k:k + 1] * _unpack_rows(ybuf_ref, tm, first=first + k * tm)

    v = alpha * x1_ref[...] + g2_ref[0] * (shared + routed)
    o_ref[...] = _standardize(v) * lng_ref[...] + lnb_ref[...]


def _final(x1, hhi, y_rows, dest_tiles, gate_t, g2, wsg, wsu, wsd, lng, lnb, seq, alpha):
    t, d = x1.shape
    d_sh = wsg.shape[1]
    n_tile, _, per_tile = dest_tiles.shape
    tm = per_tile // TOP_K
    tpb = seq // tm
    bsz = t // seq
    row = lambda i: (i, 0)
    return pl.pallas_call(
        functools.partial(_final_kernel, alpha=alpha),
        grid=(n_tile,),
        in_specs=[pl.BlockSpec((1, 1, per_tile), lambda i: (i, 0, 0), memory_space=pltpu.SMEM),
                  pl.BlockSpec((1, 1, per_tile), lambda i: (jnp.minimum(i + 1, n_tile - 1), 0, 0),
                               memory_space=pltpu.SMEM),
                  pl.BlockSpec((tm, d), row), pl.BlockSpec((tm, d), row), pl.BlockSpec((tm, TOP_K), row),
                  pl.BlockSpec((1, 1, d), lambda i: (i // tpb, 0, 0)),
                  _const_spec((d, d_sh)), _const_spec((d, d_sh)), _const_spec((d_sh, d)),
                  _const_spec((1, d)), _const_spec((1, d)),
                  pl.BlockSpec(memory_space=pl.ANY)],
        out_specs=pl.BlockSpec((tm, d), row),
        out_shape=jax.ShapeDtypeStruct((t, d), F32),
        scratch_shapes=[pltpu.VMEM((2 * TOP_K * tm * PACK_SUB, LANE), U32), pltpu.SemaphoreType.DMA((2,))],
        compiler_params=_cparams(("arbitrary",)),
        name="final",
    )(dest_tiles, dest_tiles, x1, hhi, gate_t, g2.reshape(bsz, 1, d), wsg, wsu, wsd, lng, lnb, y_rows)


def _split_hi_lo(w):
    hi = w.astype(BF16)
    return hi, (w - hi.astype(F32)).astype(BF16)


def kernel(x, c, w_ada, b_ada, w_in, b_in, s5_lambda_re, s5_lambda_im, s5_log_dt, s5_b_re, s5_b_im, s5_c_re, s5_c_im, s5_d, w_s5_glu, b_s5_glu, w_fourier, b_fourier, w_out, b_out, ln1_g, ln1_b, w_router, router_bias, w_exp_gate, w_exp_up, w_exp_down, w_sh_gate, w_sh_up, w_sh_down, ln2_g, ln2_b):
    bsz, seq, d = x.shape
    depth = w_ada.shape[0]
    alpha = (2 * depth) ** 0.25
    t = bsz * seq
    d_s5 = s5_d.shape[1]
    d_f = w_fourier.shape[1]
    fw = d_f // FOURIER_GROUPS
    n_e = w_router.shape[2]
    n_chunk = seq // LANE
    seq_tabs, cmat = _dft_tables(seq, fw)
    row = lambda v: v.astype(F32).reshape(1, -1)

    x2 = x.reshape(t, d)
    for l in range(depth):
        mod = _adaln(c, w_ada[l], b_ada[l])
        sh1, sc1, g1, sh2, sc2, g2 = jnp.split(mod, 6, axis=-1)

        wi = w_in[l]
        bi = b_in[l].astype(F32)
        ws5t = wi[:, :d_s5].T.astype(BF16)
        bs5 = jnp.broadcast_to(bi[:d_s5].reshape(d_s5, 1), (d_s5, LANE))
        us5, z, gates = _proj(x2, sc1, sh1, ws5t, bs5,
                              wi[:, d_s5:d_s5 + d_f].astype(BF16), row(bi[d_s5:d_s5 + d_f]), cmat,
                              wi[:, d_s5 + d_f:].astype(BF16), row(bi[d_s5 + d_f:]), seq)

        tables = _s5_tables(s5_lambda_re[l], s5_lambda_im[l], s5_log_dt[l], s5_b_re[l], s5_b_im[l],
                            s5_c_re[l], s5_c_im[l], s5_d[l])
        ys5 = _s5(us5, tables, n_chunk)
        brs = _glu(ys5, w_s5_glu[l].astype(BF16), row(b_s5_glu[l]))
        yf_lo, yf_hi = _seqdft(seq_tabs, z, seq, fw)

        wrt_hi, wrt_lo = _split_hi_lo(w_router[l].astype(F32).T)
        x1, hhi, hp, eidx, gate, rank, cnt = _merge(
            x2, brs, yf_lo, yf_hi, gates, g1, sc2, sh2, w_fourier[l].astype(BF16), row(b_fourier[l]),
            w_out[l].astype(BF16), row(b_out[l]), row(ln1_g[l]), row(ln1_b[l]),
            wrt_hi, wrt_lo, router_bias[l], seq, alpha)

        counts = cnt[:, 0]
        padded = ((counts + FFN_BLOCK - 1) // FFN_BLOCK) * FFN_BLOCK
        pend = jnp.cumsum(padded)
        pstart = (pend - padded).astype(I32)
        dest = _dest(eidx, rank, pstart)
        n_blk = (t * TOP_K + n_e * (FFN_BLOCK - 1) + FFN_BLOCK - 1) // FFN_BLOCK
        n_rows = n_blk * FFN_BLOCK
        blk_start = jnp.arange(n_blk, dtype=I32) * FFN_BLOCK
        blk_e = jnp.minimum(jnp.sum((pend[None, :] <= blk_start[:, None]).astype(I32), axis=1), n_e - 1)
        n_used = (pend[-1:] // FFN_BLOCK).astype(I32)
        pad_lo = jnp.concatenate([pstart + counts, pend[-1:]]).astype(I32)
        pad_n = jnp.concatenate([padded - counts, n_rows - pend[-1:]]).astype(I32)

        def tiles(tm):
            return dest.reshape(TOP_K, t // tm, tm).transpose(1, 0, 2).reshape(t // tm, 1, TOP_K * tm)

        rows = _dispatch(hp, tiles(DISPATCH_TILE), pad_lo, pad_n, n_rows)
        next_blk = (pend // FFN_BLOCK).astype(I32)
        y_rows = _ffn(rows, blk_e, n_used, next_blk, w_exp_gate[l], w_exp_up[l], w_exp_down[l])
        x2 = _final(x1, hhi, y_rows, tiles(COMBINE_TILE), gate.T, g2,
                    w_sh_gate[l].astype(BF16), w_sh_up[l].astype(BF16), w_sh_down[l].astype(BF16),
                    row(ln2_g[l]), row(ln2_b[l]), seq, alpha)
    return x2.reshape(bsz, seq, d)
```

```python
import functools
import math

import jax
import jax.numpy as jnp
from jax import lax
from jax.experimental import pallas as pl
from jax.experimental.pallas import tpu as pltpu

F32 = jnp.float32
BF16 = jnp.bfloat16
I32 = jnp.int32

TOP_K = 8
N_EXPERT_GROUPS = 8
TOPK_GROUPS = 4
ROUTED_SCALE = 2.5
FOURIER_GROUPS = 4
LN_EPS = 1e-5

LANE = 128
VMEM_LIMIT = 56 * 1024 * 1024

HIGHEST = lax.Precision.HIGHEST
NEG_INF = float("-inf")

FFN_BLOCK = 512
DISPATCH_TILE = 1024
COMBINE_TILE = 512
ISSUE_UNROLL = 4


def _cparams(sem):
    return pltpu.CompilerParams(dimension_semantics=sem, vmem_limit_bytes=VMEM_LIMIT)


def _const_spec(shape):
    nd = len(shape)
    return pl.BlockSpec(shape, lambda *_: (0,) * nd, pipeline_mode=pl.Buffered(1))


def _standardize(x):
    mu = jnp.mean(x, axis=-1, keepdims=True)
    xc = x - mu
    var = jnp.mean(xc * xc, axis=-1, keepdims=True)
    return xc * lax.rsqrt(var + LN_EPS)


def _silu(x):
    return x * jax.nn.sigmoid(x)


def _gelu_tanh(x):
    return 0.5 * x * (1.0 + jnp.tanh(math.sqrt(2.0 / math.pi) * (x + 0.044715 * (x * x * x))))


def _adaln_kernel(c_ref, w_ref, b_ref, o_ref):
    a = _silu(c_ref[...])
    o_ref[...] = jnp.dot(a, w_ref[...], precision=HIGHEST, preferred_element_type=F32) + b_ref[...]


def _adaln(c, w, b):
    bsz, d = c.shape
    n = w.shape[1]
    tn = 512
    return pl.pallas_call(
        _adaln_kernel,
        grid=(n // tn,),
        in_specs=[pl.BlockSpec((bsz, d), lambda j: (0, 0)),
                  pl.BlockSpec((d, tn), lambda j: (0, j)),
                  pl.BlockSpec((1, tn), lambda j: (0, j))],
        out_specs=pl.BlockSpec((bsz, tn), lambda j: (0, j)),
        out_shape=jax.ShapeDtypeStruct((bsz, n), F32),
        compiler_params=_cparams(("parallel",)),
        name="adaln",
    )(c, w, b.reshape(1, n))


def _proj_kernel(x_ref, sc_ref, sh_ref, ws5t_ref, bs5_ref, wf_ref, bf_ref, cs_ref, wg_ref, bg_ref,
                 us5_ref, z_ref, gates_ref):
    tm = x_ref.shape[0]
    u = (_standardize(x_ref[...]) * (1.0 + sc_ref[0]) + sh_ref[0]).astype(BF16)
    p = lax.dot_general(ws5t_ref[...], u, (((1,), (1,)), ((), ())), preferred_element_type=F32)
    p = p + bs5_ref[:, 0:1]
    d_s5, n_j, _ = us5_ref.shape
    us5_2d = us5_ref.reshape(d_s5 * n_j, LANE)
    for j in range(n_j):
        us5_2d[pl.ds(j, d_s5, stride=n_j), :] = p[:, j * LANE:(j + 1) * LANE]
    uf = (jnp.dot(u, wf_ref[...], preferred_element_type=F32) + bf_ref[...]).astype(BF16)
    d_f = uf.shape[1]
    fw = d_f // FOURIER_GROUPS
    for q in range(FOURIER_GROUPS):
        zq = jnp.dot(uf[:, q * fw:(q + 1) * fw], cs_ref[...], preferred_element_type=F32)
        z_ref[:, q * fw:(q + 1) * fw] = zq[:, :fw].astype(BF16)
        z_ref[:, d_f + q * fw:d_f + (q + 1) * fw] = zq[:, fw:].astype(BF16)
    n_g = wg_ref.shape[1]
    half = n_g // 2
    for q in range(2):
        gp = jnp.dot(u, wg_ref[:, q * half:(q + 1) * half], preferred_element_type=F32)
        gp = gp + bg_ref[:, q * half:(q + 1) * half]
        gates_ref[:, q * half:(q + 1) * half] = jax.nn.sigmoid(gp).astype(BF16)


def _proj(x2, sc, sh, ws5t, bs5, wf, bf, cs, wg, bg, seq):
    t, d = x2.shape
    d_s5 = ws5t.shape[0]
    d_f = wf.shape[1]
    n_g = wg.shape[1]
    tm = 1024
    tpb = seq // tm
    bsz = t // seq
    return pl.pallas_call(
        _proj_kernel,
        grid=(t // tm,),
        in_specs=[pl.BlockSpec((tm, d), lambda i: (i, 0)),
                  pl.BlockSpec((1, 1, d), lambda i: (i // tpb, 0, 0)),
                  pl.BlockSpec((1, 1, d), lambda i: (i // tpb, 0, 0)),
                  _const_spec((d_s5, d)), _const_spec((d_s5, LANE)),
                  _const_spec((d, d_f)), _const_spec((1, d_f)),
                  _const_spec(cs.shape),
                  _const_spec((d, n_g)), _const_spec((1, n_g))],
        out_specs=[pl.BlockSpec((d_s5, tm // LANE, LANE), lambda i: (0, i, 0)),
                   pl.BlockSpec((tm, 2 * d_f), lambda i: (i, 0)),
                   pl.BlockSpec((tm, n_g), lambda i: (i, 0))],
        out_shape=[jax.ShapeDtypeStruct((d_s5, t // LANE, LANE), F32),
                   jax.ShapeDtypeStruct((t, 2 * d_f), BF16),
                   jax.ShapeDtypeStruct((t, n_g), BF16)],
        compiler_params=_cparams(("parallel",)),
        name="proj",
    )(x2, sc.reshape(bsz, 1, d), sh.reshape(bsz, 1, d), ws5t, bs5, wf, bf, cs, wg, bg)


def _s5_tables(lam_re, lam_im, log_dt, b_re, b_im, c_re, c_im, d_skip):
    L = LANE
    hp = HIGHEST
    lr, li = lam_re.astype(F32), lam_im.astype(F32)
    dt = jnp.exp(log_dt.astype(F32))[:, :, None]
    mag = jnp.exp(lr * dt)
    ang = li * dt
    ab_re, ab_im = mag * jnp.cos(ang), mag * jnp.sin(ang)
    den = lr * lr + li * li
    nr = ab_re - 1.0
    coef_re = (nr * lr + ab_im * li) / den
    coef_im = (ab_im * lr - nr * li) / den
    br, bi = b_re.astype(F32), b_im.astype(F32)
    bb_re = coef_re[..., None] * br - coef_im[..., None] * bi
    bb_im = coef_re[..., None] * bi + coef_im[..., None] * br
    cr, ci = c_re.astype(F32), c_im.astype(F32)
    n_g, n_p, n_h = br.shape[1], br.shape[2], br.shape[3]

    k = jnp.arange(L + 1, dtype=F32)[None, None, :, None]
    pmag = jnp.exp(k * (lr * dt)[:, :, None, :])
    pang = k * (li * dt)[:, :, None, :]
    pw_re, pw_im = pmag * jnp.cos(pang), pmag * jnp.sin(pang)

    m_re = cr[:, :, :, None, :] * jnp.swapaxes(bb_re, 2, 3)[:, :, None, :, :] \
        - ci[:, :, :, None, :] * jnp.swapaxes(bb_im, 2, 3)[:, :, None, :, :]
    m_im = cr[:, :, :, None, :] * jnp.swapaxes(bb_im, 2, 3)[:, :, None, :, :] \
        + ci[:, :, :, None, :] * jnp.swapaxes(bb_re, 2, 3)[:, :, None, :, :]
    kap = jnp.einsum("dgohp,dgkp->dgohk", m_re, pw_re, precision=hp) \
        - jnp.einsum("dgohp,dgkp->dgohk", m_im, pw_im, precision=hp)
    kb = kap[1]
    kpos = kap[0, ..., :L].at[..., 0].add(kb[..., 0]).reshape(n_g, n_h * n_h, L)
    kneg = jnp.concatenate([jnp.zeros_like(kb[..., 0:1]), kb[..., L - 1:0:-1]], axis=-1)
    kneg = kneg.reshape(n_g, n_h * n_h, L)

    cat = lambda *parts: jnp.concatenate(parts, axis=-1)
    pf_re, pf_im = pw_re[0, :, L - 1::-1][:, :L], pw_im[0, :, L - 1::-1][:, :L]
    pb_re, pb_im = pw_re[1, :, :L], pw_im[1, :, :L]
    sp, spsw = cat(pf_re, pf_im, pb_re, pb_im), cat(pf_im, pf_re, pb_im, pb_re)
    bt_re, bt_im = jnp.swapaxes(bb_re, 2, 3), jnp.swapaxes(bb_im, 2, 3)
    bx, by = cat(bt_re[0], bt_re[0], bt_re[1], bt_re[1]), cat(-bt_im[0], bt_im[0], -bt_im[1], bt_im[1])

    qf_re, qf_im = pw_re[0, :, 1:L + 1], pw_im[0, :, 1:L + 1]
    qb_re, qb_im = pw_re[1, :, L:0:-1], pw_im[1, :, L:0:-1]
    qt, qtsw = cat(qf_re, qf_im, qb_re, qb_im), cat(qf_im, qf_re, qb_im, qb_re)
    cx, cy = cat(cr[0], -cr[0], cr[1], -cr[1]), cat(-ci[0], -ci[0], -ci[1], -ci[1])

    al_re, al_im = pw_re[:, :, L], pw_im[:, :, L]
    al = jnp.stack([cat(al_re[0], al_re[0], al_re[1], al_re[1]),
                    cat(-al_im[0], al_im[0], -al_im[1], al_im[1])], axis=1)

    dsk = jnp.broadcast_to(d_skip.astype(F32).reshape(n_g, n_h, 1), (n_g, n_h, L))
    return kpos, kneg, sp, spsw, bx, by, qt, qtsw, cx, cy, al, dsk


def _s5_kernel(a_ref, kpos_ref, kneg_ref, sp_ref, spsw_ref, bx_ref, by_ref, qt_ref, qtsw_ref, cx_ref, cy_ref,
               al_ref, d_ref, y_ref, abf_ref, tp_ref, ws_ref, sf_ref, sb_ref, sfs_ref, sbs_ref, xf_ref, xb_ref,
               *, n_chunk):
    n_h, r, L = a_ref.shape
    bsz = r // n_chunk
    half = sp_ref.shape[2] // 2
    nt = (((1,), (1,)), ((), ()))

    sp, spsw = sp_ref[0], spsw_ref[0]
    for h in range(n_h):
        abf_ref[:, h * L:(h + 1) * L] = a_ref[h].astype(BF16)
        ws_ref[h * L:(h + 1) * L, :] = (sp * bx_ref[0, h:h + 1, :] + spsw * by_ref[0, h:h + 1, :]).astype(BF16)
    abf = abf_ref[...]

    s_all = jnp.dot(abf, ws_ref[...], preferred_element_type=F32)
    sf_ref[...] = s_all[:, :half]
    sb_ref[...] = s_all[:, half:]
    sfs_ref[...] = pltpu.roll(s_all[:, :half], half // 2, 1)
    sbs_ref[...] = pltpu.roll(s_all[:, half:], half // 2, 1)

    alx = al_ref[0, 0:1, :]
    aly = al_ref[0, 1:2, :]

    zero = jnp.zeros((bsz, half), F32)
    ef, efs, eb, ebs = zero, zero, zero, zero
    for c in range(n_chunk):
        cb = n_chunk - 1 - c
        rows_f = pl.ds(c, bsz, stride=n_chunk)
        rows_b = pl.ds(cb, bsz, stride=n_chunk)
        xf_ref[rows_f, :] = ef
        xb_ref[rows_b, :] = eb
        xf, yf = alx[:, :half], aly[:, :half]
        xb, yb = alx[:, half:], aly[:, half:]
        ef, efs = ef * xf + efs * yf + sf_ref[rows_f, :], efs * xf - ef * yf + sfs_ref[rows_f, :]
        eb, ebs = eb * xb + ebs * yb + sb_ref[rows_b, :], ebs * xb - eb * yb + sbs_ref[rows_b, :]
    xin = jnp.concatenate([xf_ref[...], xb_ref[...]], axis=1).astype(BF16)

    s_idx = lax.broadcasted_iota(I32, (L, L), 0)
    j_idx = lax.broadcasted_iota(I32, (L, L), 1)
    fwd_part = j_idx + s_idx < L
    qt, qtsw = qt_ref[0], qtsw_ref[0]

    def pair(op, _):
        for oo in range(2):
            o = 2 * op + oo
            for h in range(n_h):
                row = o * n_h + h
                kp = jnp.broadcast_to(kpos_ref[0, pl.ds(row, 1), :], (L, L))
                kn = jnp.broadcast_to(kneg_ref[0, pl.ds(row, 1), :], (L, L))
                tile = pltpu.roll(jnp.where(fwd_part, kp, kn), 0, 1, stride=1, stride_axis=0)
                tp_ref[h * L:(h + 1) * L, oo * L:(oo + 1) * L] = tile.astype(BF16)
        wo_t = jnp.concatenate(
            [qt * cx_ref[0, pl.ds(2 * op + oo, 1), :] + qtsw * cy_ref[0, pl.ds(2 * op + oo, 1), :]
             for oo in range(2)], axis=0).astype(BF16)
        yp = jnp.dot(abf, tp_ref[...], preferred_element_type=F32)
        yp = yp + lax.dot_general(xin, wo_t, nt, preferred_element_type=F32)
        for oo in range(2):
            o = 2 * op + oo
            y_ref[o] = yp[:, oo * L:(oo + 1) * L] + a_ref[o] * d_ref[0, pl.ds(o, 1), :]
        return 0

    lax.fori_loop(0, n_h // 2, pair, 0)


def _s5(us5, tables, n_chunk):
    kpos, kneg, sp, spsw, bx, by, qt, qtsw, cx, cy, al, dsk = tables
    d_s5, r, L = us5.shape
    n_g = kpos.shape[0]
    n_h = d_s5 // n_g
    n_st = sp.shape[2]
    g3 = lambda g: (g, 0, 0)
    pw_spec = pl.BlockSpec((1, L, n_st), g3)
    hv_spec = pl.BlockSpec((1, n_h, n_st), g3)
    return pl.pallas_call(
        functools.partial(_s5_kernel, n_chunk=n_chunk),
        grid=(n_g,),
        in_specs=[pl.BlockSpec((n_h, r, L), g3),
                  pl.BlockSpec((1, n_h * n_h, L), g3), pl.BlockSpec((1, n_h * n_h, L), g3),
                  pw_spec, pw_spec, hv_spec, hv_spec, pw_spec, pw_spec, hv_spec, hv_spec,
                  pl.BlockSpec((1, 2, n_st), g3), pl.BlockSpec((1, n_h, L), g3)],
        out_specs=pl.BlockSpec((n_h, r, L), g3),
        out_shape=jax.ShapeDtypeStruct((d_s5, r, L), F32),
        scratch_shapes=[pltpu.VMEM((r, n_h * L), BF16), pltpu.VMEM((n_h * L, 2 * L), BF16),
                        pltpu.VMEM((n_h * L, n_st), BF16)]
        + [pltpu.VMEM((r, n_st // 2), F32)] * 6,
        compiler_params=_cparams(("parallel",)),
        name="s5",
    )(us5, kpos, kneg, sp, spsw, bx, by, qt, qtsw, cx, cy, al, dsk)


DFT_SPLIT = 64


def _dft_tables(seq, fw):
    def angles(mult, n_rows, n):
        s = jnp.arange(n, dtype=I32)[None, :]
        q = jnp.arange(n_rows, dtype=I32)[:, None]
        return (2.0 * math.pi / n) * ((mult * q * s) % n).astype(F32)

    ang_a = angles(DFT_SPLIT, seq // DFT_SPLIT, seq)
    ang_b = angles(1, DFT_SPLIT, seq)
    seq_tabs = (jnp.cos(ang_a), jnp.sin(ang_a), jnp.cos(ang_b), jnp.sin(ang_b))
    ang_c = angles(1, fw, fw)
    return seq_tabs, jnp.concatenate([jnp.cos(ang_c), jnp.sin(ang_c)], axis=1).astype(BF16)


DFT_EXTRA = 16


def _seqdft_kernel(ca_ref, sa_ref, cb_ref, sb_ref, flip_ref, z_ref, lo_ref, hi_ref, f_ref, *, scale):
    seq = z_ref.shape[0]
    tk, d_f = lo_ref.shape

    @pl.when(pl.program_id(1) == 0)
    def _():
        a0 = pl.program_id(0) * (tk // DFT_SPLIT)
        cb, sb = cb_ref[...], sb_ref[...]

        def put(rows, ca, sa, n_b):
            f_ref[rows, :seq] = (ca * cb[:n_b] - sa * sb[:n_b]).astype(BF16)
            f_ref[rows, seq:] = (-(sa * cb[:n_b] + ca * sb[:n_b])).astype(BF16)

        for j in range(tk // DFT_SPLIT):
            put(slice(j * DFT_SPLIT, (j + 1) * DFT_SPLIT),
                ca_ref[pl.ds(a0 + j, 1), :], sa_ref[pl.ds(a0 + j, 1), :], DFT_SPLIT)
        a_next = a0 + tk // DFT_SPLIT
        put(slice(tk, tk + DFT_EXTRA), ca_ref[pl.ds(a_next, 1), :], sa_ref[pl.ds(a_next, 1), :], DFT_EXTRA)

    p = jnp.dot(f_ref[:, :seq], z_ref[:, :d_f], preferred_element_type=F32)
    q = jnp.dot(f_ref[:, seq:], z_ref[:, d_f:], preferred_element_type=F32)
    lo_ref[...] = ((p[:tk] + q[:tk]) * scale).astype(lo_ref.dtype)
    mirror = ((p - q) * scale).astype(BF16)
    hi_ref[...] = jnp.dot(flip_ref[...], mirror, preferred_element_type=F32).astype(hi_ref.dtype)


def _seqdft(seq_tabs, z, seq, fw):
    t, two_df = z.shape
    d_f = two_df // 2
    bsz = t // seq
    tk = 512
    nk2 = seq // (2 * tk)
    scale = 1.0 / math.sqrt(seq * fw)
    flip = (jnp.arange(tk)[:, None] + jnp.arange(tk + DFT_EXTRA)[None, :] == tk).astype(BF16)
    half = jax.ShapeDtypeStruct((t // 2, d_f), BF16)
    return pl.pallas_call(
        functools.partial(_seqdft_kernel, scale=scale),
        grid=(nk2, bsz),
        in_specs=[_const_spec(tab.shape) for tab in seq_tabs] + [_const_spec(flip.shape)]
        + [pl.BlockSpec((seq, two_df), lambda k, b: (b, 0))],
        out_specs=[pl.BlockSpec((tk, d_f), lambda k, b: (b * nk2 + k, 0)),
                   pl.BlockSpec((tk, d_f), lambda k, b: (b * nk2 + nk2 - 1 - k, 0))],
        out_shape=[half, half],
        scratch_shapes=[pltpu.VMEM((tk + DFT_EXTRA, 2 * seq), BF16)],
        compiler_params=_cparams(("parallel", "arbitrary")),
        name="seqdft",
    )(*seq_tabs, flip, z)


def _glu_kernel(y_ref, w_ref, b_ref, o_ref, zt_ref):
    d_s5, n_j, _ = y_ref.shape
    d = o_ref.shape[1]
    y2 = y_ref.reshape(d_s5 * n_j, LANE)
    for j in range(n_j):
        zt_ref[j * LANE:(j + 1) * LANE, :] = _gelu_tanh(y2[pl.ds(j, d_s5, stride=n_j), :]).T.astype(BF16)
    zt = zt_ref[...]
    a = jnp.dot(zt, w_ref[:, :d], preferred_element_type=F32) + b_ref[:, :d]
    g = jnp.dot(zt, w_ref[:, d:], preferred_element_type=F32) + b_ref[:, d:]
    o_ref[...] = (a * jax.nn.sigmoid(g)).astype(BF16)


def _glu(ys5, w, b):
    d_s5, r, L = ys5.shape
    t = r * L
    n = w.shape[1]
    tm = 1024
    return pl.pallas_call(
        _glu_kernel,
        grid=(t // tm,),
        in_specs=[pl.BlockSpec((d_s5, tm // L, L), lambda i: (0, i, 0)),
                  _const_spec((d_s5, n)), _const_spec((1, n))],
        out_specs=pl.BlockSpec((tm, n // 2), lambda i: (i, 0)),
        out_shape=jax.ShapeDtypeStruct((t, n // 2), BF16),
        scratch_shapes=[pltpu.VMEM((tm, d_s5), BF16)],
        compiler_params=_cparams(("parallel",)),
        name="glu",
    )(ys5, w, b)


U32 = jnp.uint32
PACK_SUB = 4


def _row_slab(ref, r):
    return ref.at[pl.ds(pl.multiple_of(r * PACK_SUB, PACK_SUB), PACK_SUB), :]


def _pack_rows(ref, v, first=0):
    rows, d = v.shape
    half = d // 2
    bits = lax.bitcast_convert_type(v.astype(BF16).astype(F32), U32)
    for c in range(PACK_SUB):
        lo = bits[:, c * LANE:(c + 1) * LANE] >> 16
        hi = bits[:, half + c * LANE:half + (c + 1) * LANE] & jnp.uint32(0xFFFF0000)
        ref[pl.ds(first * PACK_SUB + c, rows, stride=PACK_SUB), :] = hi | lo


def _unpack_rows(ref, rows, first=0):
    los, his = [], []
    for c in range(PACK_SUB):
        w = ref[pl.ds(first * PACK_SUB + c, rows, stride=PACK_SUB), :]
        los.append(lax.bitcast_convert_type(w << 16, F32))
        his.append(lax.bitcast_convert_type(w & jnp.uint32(0xFFFF0000), F32))
    return jnp.concatenate(los + his, axis=1)


def _merge_kernel(x_ref, brs_ref, yf_lo_ref, yf_hi_ref, gates_ref, g1_ref, sc_ref, sh_ref, wfo_ref, bfo_ref,
                  wo_ref, bo_ref, lng_ref, lnb_ref, whi_ref, wlo_ref, bias_ref, tri_ref,
                  x1_ref, hhi_ref, hp_ref, eidx_ref, gate_ref, rank_ref, cnt_ref, base_ref,
                  *, alpha, tiles_per_seq):
    d = x_ref.shape[1]
    in_first_half = (pl.program_id(0) % tiles_per_seq) < tiles_per_seq // 2
    yf = jnp.where(in_first_half, yf_lo_ref[...], yf_hi_ref[...])
    br_f = jnp.dot(yf, wfo_ref[...], preferred_element_type=F32) + bfo_ref[...]
    merged = gates_ref[:, :d].astype(F32) * brs_ref[...].astype(F32) + gates_ref[:, d:].astype(F32) * br_f
    mix = jnp.dot(merged.astype(BF16), wo_ref[...], preferred_element_type=F32) + bo_ref[...]
    v = alpha * x_ref[...] + g1_ref[0] * mix
    x1 = _standardize(v) * lng_ref[...] + lnb_ref[...]
    x1_ref[...] = x1
    h = _standardize(x1) * (1.0 + sc_ref[0]) + sh_ref[0]
    hhi = h.astype(BF16)
    hhi_ref[...] = hhi
    _pack_rows(hp_ref, h)
    _route(hhi, (h - hhi.astype(F32)).astype(BF16), whi_ref, wlo_ref, bias_ref, tri_ref,
           eidx_ref, gate_ref, rank_ref, cnt_ref, base_ref)


def _merge(x2, brs, yf_lo, yf_hi, gates, g1, sc2, sh2, wfo, bfo, wo, bo, lng, lnb, wrt_hi, wrt_lo, bias,
           seq, alpha):
    t, d = x2.shape
    d_f = yf_lo.shape[1]
    n_e = wrt_hi.shape[0]
    tm = 512
    tri = (jnp.arange(tm)[:, None] < jnp.arange(tm)[None, :]).astype(BF16)
    kt = lambda i: (0, i)
    tpb = seq // tm
    tph = tpb // 2
    bsz = t // seq
    row = lambda i: (i, 0)
    bat = lambda i: (i // tpb, 0, 0)
    lo_row = lambda i: ((i // tpb) * tph + jnp.minimum(i % tpb, tph - 1), 0)
    hi_row = lambda i: ((i // tpb) * tph + jnp.maximum(i % tpb - tph, 0), 0)
    return pl.pallas_call(
        functools.partial(_merge_kernel, alpha=alpha, tiles_per_seq=tpb),
        grid=(t // tm,),
        in_specs=[pl.BlockSpec((tm, d), row), pl.BlockSpec((tm, d), row),
                  pl.BlockSpec((tm, d_f), lo_row), pl.BlockSpec((tm, d_f), hi_row),
                  pl.BlockSpec((tm, 2 * d), row),
                  pl.BlockSpec((1, 1, d), bat), pl.BlockSpec((1, 1, d), bat), pl.BlockSpec((1, 1, d), bat),
                  _const_spec((d_f, d)), _const_spec((1, d)), _const_spec((d, d)), _const_spec((1, d)),
                  _const_spec((1, d)), _const_spec((1, d)),
                  _const_spec((n_e, d)), _const_spec((n_e, d)), _const_spec((n_e, LANE)), _const_spec((tm, tm))],
        out_specs=[pl.BlockSpec((tm, d), row), pl.BlockSpec((tm, d), row),
                   pl.BlockSpec((tm * PACK_SUB, LANE), row),
                   pl.BlockSpec((TOP_K, tm), kt), pl.BlockSpec((TOP_K, tm), kt), pl.BlockSpec((TOP_K, tm), kt),
                   pl.BlockSpec((n_e, LANE), lambda i: (0, 0))],
        out_shape=[jax.ShapeDtypeStruct((t, d), F32), jax.ShapeDtypeStruct((t, d), BF16),
                   jax.ShapeDtypeStruct((t * PACK_SUB, LANE), U32),
                   jax.ShapeDtypeStruct((TOP_K, t), I32), jax.ShapeDtypeStruct((TOP_K, t), F32),
                   jax.ShapeDtypeStruct((TOP_K, t), I32), jax.ShapeDtypeStruct((n_e, LANE), I32)],
        scratch_shapes=[pltpu.VMEM((n_e, LANE), F32)],
        compiler_params=_cparams(("arbitrary",)),
        name="merge_route",
    )(x2, brs, yf_lo, yf_hi, gates, g1.reshape(bsz, 1, d), sc2.reshape(bsz, 1, d), sh2.reshape(bsz, 1, d),
      wfo, bfo, wo, bo, lng, lnb, wrt_hi, wrt_lo,
      jnp.broadcast_to(bias.astype(F32).reshape(n_e, 1), (n_e, LANE)), tri)


def _route(hhi, hlo, whi_ref, wlo_ref, bias_ref, tri_ref, eidx_ref, gate_ref, rank_ref, cnt_ref, base_ref):
    n_e = whi_ref.shape[0]
    tm = hhi.shape[0]
    gsz = n_e // N_EXPERT_GROUPS
    nt = (((1,), (1,)), ((), ()))

    @pl.when(pl.program_id(0) == 0)
    def _():
        base_ref[...] = jnp.zeros_like(base_ref)

    logits = lax.dot_general(whi_ref[...], hhi, nt, preferred_element_type=F32)
    logits = logits + lax.dot_general(wlo_ref[...], hhi, nt, preferred_element_type=F32)
    logits = logits + lax.dot_general(whi_ref[...], hlo, nt, preferred_element_type=F32)
    scores = jax.nn.sigmoid(logits)
    sel = scores + bias_ref[:, 0:1]

    g3 = sel.reshape(N_EXPERT_GROUPS, gsz, tm)
    i3 = lax.broadcasted_iota(I32, g3.shape, 1).astype(F32)
    m1 = jnp.max(g3, axis=1, keepdims=True)
    first = jnp.min(jnp.where(g3 == m1, i3, float(gsz)), axis=1, keepdims=True)
    m2 = jnp.max(jnp.where(i3 == first, NEG_INF, g3), axis=1, keepdims=True)
    gs = (m1 + m2).reshape(N_EXPERT_GROUPS, tm)

    gi = lax.broadcasted_iota(I32, gs.shape, 0).astype(F32)
    gsel = jnp.zeros(gs.shape, F32)
    cur = gs
    for _ in range(TOPK_GROUPS):
        m = jnp.max(cur, axis=0, keepdims=True)
        f = jnp.min(jnp.where(cur == m, gi, float(N_EXPERT_GROUPS)), axis=0, keepdims=True)
        pick = gi == f
        gsel = jnp.where(pick, 1.0, gsel)
        cur = jnp.where(pick, NEG_INF, cur)
    gmask = jnp.broadcast_to(gsel.reshape(N_EXPERT_GROUPS, 1, tm), g3.shape).reshape(n_e, tm)
    masked = jnp.where(gmask > 0.5, sel, NEG_INF)

    ri = lax.broadcasted_iota(I32, (n_e, tm), 0).astype(F32)
    picks = []
    gates = []
    multihot = jnp.zeros((n_e, tm), F32)
    for _ in range(TOP_K):
        m = jnp.max(masked, axis=0, keepdims=True)
        f = jnp.min(jnp.where(masked == m, ri, float(n_e)), axis=0, keepdims=True)
        pick = ri == f
        picks.append(f)
        gates.append(jnp.sum(jnp.where(pick, scores, 0.0), axis=0, keepdims=True))
        multihot = jnp.where(pick, 1.0, multihot)
        masked = jnp.where(pick, NEG_INF, masked)
    gsum = gates[0]
    for g in gates[1:]:
        gsum = gsum + g

    rankmat = jnp.dot(multihot.astype(BF16), tri_ref[...], preferred_element_type=F32) + base_ref[:, 0:1]
    for k in range(TOP_K):
        pick = ri == picks[k]
        eidx_ref[k:k + 1, :] = picks[k].astype(I32)
        gate_ref[k:k + 1, :] = gates[k] / gsum * ROUTED_SCALE
        rank_ref[k:k + 1, :] = jnp.sum(jnp.where(pick, rankmat, 0.0), axis=0, keepdims=True).astype(I32)
    base_ref[...] = base_ref[...] + jnp.sum(multihot, axis=1, keepdims=True)
    cnt_ref[...] = base_ref[...].astype(I32)


def _dest_kernel(eidx_ref, rank_ref, pstart_ref, dest_ref):
    n_e = pstart_ref.shape[0]
    tm = eidx_ref.shape[1]
    ri = lax.broadcasted_iota(I32, (n_e, tm), 0)
    ps = pstart_ref[:, 0:1].astype(F32)
    for k in range(TOP_K):
        hit = ri == eidx_ref[k:k + 1, :]
        base = jnp.sum(jnp.where(hit, ps, 0.0), axis=0, keepdims=True)
        dest_ref[k:k + 1, :] = base.astype(I32) + rank_ref[k:k + 1, :]


def _dest(eidx, rank, pstart):
    k, t = eidx.shape
    n_e = pstart.shape[0]
    tm = 2048
    kt = lambda i: (0, i)
    return pl.pallas_call(
        _dest_kernel,
        grid=(t // tm,),
        in_specs=[pl.BlockSpec((k, tm), kt), pl.BlockSpec((k, tm), kt), _const_spec((n_e, LANE))],
        out_specs=pl.BlockSpec((k, tm), kt),
        out_shape=jax.ShapeDtypeStruct((k, t), I32),
        compiler_params=_cparams(("parallel",)),
        name="dest",
    )(eidx, rank, jnp.broadcast_to(pstart.astype(I32).reshape(n_e, 1), (n_e, LANE)))


def _zero_fill(pad_lo_ref, pad_n_ref, zeros_ref, rows_ref, sem):
    def each_copy(act):
        def per_entry(e, _):
            lo = pad_lo_ref[e]
            n = pad_n_ref[e]
            n_full = n // FFN_BLOCK

            def full(j, _):
                act(pltpu.make_async_copy(
                    zeros_ref, rows_ref.at[pl.ds(pl.multiple_of((lo + j * FFN_BLOCK) * PACK_SUB, PACK_SUB),
                                                 FFN_BLOCK * PACK_SUB), :], sem))
                return 0

            lax.fori_loop(0, n_full, full, 0)
            off = lo + n_full * FFN_BLOCK
            rem = n - n_full * FFN_BLOCK
            bit = FFN_BLOCK // 2
            while bit >= 1:
                take = rem & bit

                @pl.when(take != 0)
                def _(off=off, bit=bit):
                    act(pltpu.make_async_copy(
                        zeros_ref.at[pl.ds(0, bit * PACK_SUB), :],
                        rows_ref.at[pl.ds(pl.multiple_of(off * PACK_SUB, PACK_SUB), bit * PACK_SUB), :], sem))

                off = off + take
                bit //= 2
            return 0

        lax.fori_loop(0, pad_lo_ref.shape[0], per_entry, 0)

    each_copy(lambda cp: cp.start())
    each_copy(lambda cp: cp.wait())


def _dispatch_kernel(pad_lo_ref, pad_n_ref, dest_ref, hp_ref, rows_ref, zeros_ref, sem, zsem):
    tm = hp_ref.shape[0] // PACK_SUB

    @pl.when(pl.program_id(0) == 0)
    def _():
        zeros_ref[...] = jnp.zeros_like(zeros_ref)
        _zero_fill(pad_lo_ref, pad_n_ref, zeros_ref, rows_ref, zsem)

    def body(tt, _):
        for k in range(TOP_K):
            pltpu.make_async_copy(_row_slab(hp_ref, tt), _row_slab(rows_ref, dest_ref[0, 0, k * tm + tt]),
                                  sem).start(priority=k % 2)
        return 0

    lax.fori_loop(0, tm, body, 0, unroll=ISSUE_UNROLL)
    all_rows = rows_ref.at[pl.ds(0, TOP_K * tm * PACK_SUB), :]
    pltpu.make_async_copy(all_rows, all_rows, sem).wait()


def _dispatch(hp, dest_tiles, pad_lo, pad_n, n_rows):
    n_tile, _, per_tile = dest_tiles.shape
    tm = per_tile // TOP_K
    return pl.pallas_call(
        _dispatch_kernel,
        grid_spec=pltpu.PrefetchScalarGridSpec(
            num_scalar_prefetch=2,
            grid=(n_tile,),
            in_specs=[pl.BlockSpec((1, 1, per_tile), lambda i, lo, n: (i, 0, 0), memory_space=pltpu.SMEM),
                      pl.BlockSpec((tm * PACK_SUB, LANE), lambda i, lo, n: (i, 0))],
            out_specs=pl.BlockSpec(memory_space=pl.ANY),
            scratch_shapes=[pltpu.VMEM((FFN_BLOCK * PACK_SUB, LANE), U32),
                            pltpu.SemaphoreType.DMA, pltpu.SemaphoreType.DMA],
        ),
        out_shape=jax.ShapeDtypeStruct((n_rows * PACK_SUB, LANE), U32),
        compiler_params=_cparams(("arbitrary",)),
        name="dispatch",
    )(pad_lo, pad_n, dest_tiles, hp)


def _ffn_kernel(blk_e_ref, n_used_ref, next_blk_ref, rows_ref, wg_ref, wu_ref, wd_ref, y_ref,
                wg_st_ref, wu_st_ref, wd_st_ref, wgu_bf_ref, wd_bf_ref, sems):
    b = pl.program_id(0)
    n_used = n_used_ref[0]
    live = b < n_used
    d_e = wd_st_ref.shape[0]
    e = blk_e_ref[b]

    def fetch(expert):
        return (pltpu.make_async_copy(wg_ref.at[expert], wg_st_ref, sems.at[0]),
                pltpu.make_async_copy(wu_ref.at[expert], wu_st_ref, sems.at[1]),
                pltpu.make_async_copy(wd_ref.at[expert], wd_st_ref, sems.at[2]))

    @pl.when(b == 0)
    def _():
        for cp in fetch(e):
            cp.start()

    first_block = jnp.logical_and(live, jnp.logical_or(b == 0, e != blk_e_ref[jnp.maximum(b - 1, 0)]))

    @pl.when(first_block)
    def _():
        for cp in fetch(e):
            cp.wait()
        wgu_bf_ref[:, :d_e] = wg_st_ref[...].astype(BF16)
        wgu_bf_ref[:, d_e:] = wu_st_ref[...].astype(BF16)
        wd_bf_ref[...] = wd_st_ref[...].astype(BF16)
        nb = next_blk_ref[e]

        @pl.when(nb < n_used)
        def _():
            for cp in fetch(blk_e_ref[nb]):
                cp.start()

    @pl.when(live)
    def _():
        x = _unpack_rows(rows_ref, FFN_BLOCK).astype(BF16)
        au = jnp.dot(x, wgu_bf_ref[...], preferred_element_type=F32)
        hid = (_silu(au[:, :d_e]) * au[:, d_e:]).astype(BF16)
        _pack_rows(y_ref, jnp.dot(hid, wd_bf_ref[...], preferred_element_type=F32))

    @pl.when(jnp.logical_not(live))
    def _():
        y_ref[...] = jnp.zeros_like(y_ref)


def _ffn(rows, blk_e, n_used, next_blk, wg, wu, wd):
    n_rows = rows.shape[0] // PACK_SUB
    n_e, d, d_e = wg.shape
    n_blk = n_rows // FFN_BLOCK
    blk = lambda b, *_: (b, 0)
    hbm = pl.BlockSpec(memory_space=pl.ANY)
    return pl.pallas_call(
        _ffn_kernel,
        grid_spec=pltpu.PrefetchScalarGridSpec(
            num_scalar_prefetch=3,
            grid=(n_blk,),
            in_specs=[pl.BlockSpec((FFN_BLOCK * PACK_SUB, LANE), blk), hbm, hbm, hbm],
            out_specs=pl.BlockSpec((FFN_BLOCK * PACK_SUB, LANE), blk),
            scratch_shapes=[pltpu.VMEM((d, d_e), F32), pltpu.VMEM((d, d_e), F32), pltpu.VMEM((d_e, d), F32),
                            pltpu.VMEM((d, 2 * d_e), BF16), pltpu.VMEM((d_e, d), BF16),
                            pltpu.SemaphoreType.DMA((3,))],
        ),
        out_shape=jax.ShapeDtypeStruct((n_rows * PACK_SUB, LANE), U32),
        compiler_params=_cparams(("arbitrary",)),
        name="ffn",
    )(blk_e, n_used, next_blk, rows, wg, wu, wd)


def _final_kernel(dest_ref, dnext_ref, x1_ref, h_ref, gate_ref, g2_ref, wsg_ref, wsu_ref, wsd_ref, lng_ref, lnb_ref,
                  y_ref, o_ref, ybuf_ref, sems, *, alpha):
    i = pl.program_id(0)
    tm = x1_ref.shape[0]
    per_tile = TOP_K * tm

    def gather(d_ref, slot):
        def body(tt, _):
            for k in range(TOP_K):
                pltpu.make_async_copy(_row_slab(y_ref, d_ref[0, 0, k * tm + tt]),
                                      _row_slab(ybuf_ref, slot * per_tile + k * tm + tt),
                                      sems.at[slot]).start(priority=k % 2)
            return 0

        lax.fori_loop(0, tm, body, 0, unroll=ISSUE_UNROLL)

    @pl.when(i == 0)
    def _():
        gather(dest_ref, 0)

    @pl.when(i + 1 < pl.num_programs(0))
    def _():
        gather(dnext_ref, (i + 1) % 2)

    h = h_ref[...]
    a = jnp.dot(h, wsg_ref[...], preferred_element_type=F32)
    u = jnp.dot(h, wsu_ref[...], preferred_element_type=F32)
    shared = jnp.dot((_silu(a) * u).astype(BF16), wsd_ref[...], preferred_element_type=F32)

    slot = i % 2
    half = ybuf_ref.at[pl.ds(pl.multiple_of(slot * per_tile * PACK_SUB, per_tile * PACK_SUB), per_tile * PACK_SUB), :]
    pltpu.make_async_copy(half, half, sems.at[slot]).wait()
    first = slot * per_tile
    routed = gate_ref[:, 0:1] * _unpack_rows(ybuf_ref, tm, first=first)
    for k in range(1, TOP_K):
        routed = routed + gate_ref[:, k:k + 1] * _unpack_rows(ybuf_ref, tm, first=first + k * tm)

    v = alpha * x1_ref[...] + g2_ref[0] * (shared + routed)
    o_ref[...] = _standardize(v) * lng_ref[...] + lnb_ref[...]


def _final(x1, hhi, y_rows, dest_tiles, gate_t, g2, wsg, wsu, wsd, lng, lnb, seq, alpha):
    t, d = x1.shape
    d_sh = wsg.shape[1]
    n_tile, _, per_tile = dest_tiles.shape
    tm = per_tile // TOP_K
    tpb = seq // tm
    bsz = t // seq
    row = lambda i: (i, 0)
    return pl.pallas_call(
        functools.partial(_final_kernel, alpha=alpha),
        grid=(n_tile,),
        in_specs=[pl.BlockSpec((1, 1, per_tile), lambda i: (i, 0, 0), memory_space=pltpu.SMEM),
                  pl.BlockSpec((1, 1, per_tile), lambda i: (jnp.minimum(i + 1, n_tile - 1), 0, 0),
                               memory_space=pltpu.SMEM),
                  pl.BlockSpec((tm, d), row), pl.BlockSpec((tm, d), row), pl.BlockSpec((tm, TOP_K), row),
                  pl.BlockSpec((1, 1, d), lambda i: (i // tpb, 0, 0)),
                  _const_spec((d, d_sh)), _const_spec((d, d_sh)), _const_spec((d_sh, d)),
                  _const_spec((1, d)), _const_spec((1, d)),
                  pl.BlockSpec(memory_space=pl.ANY)],
        out_specs=pl.BlockSpec((tm, d), row),
        out_shape=jax.ShapeDtypeStruct((t, d), F32),
        scratch_shapes=[pltpu.VMEM((2 * TOP_K * tm * PACK_SUB, LANE), U32), pltpu.SemaphoreType.DMA((2,))],
        compiler_params=_cparams(("arbitrary",)),
        name="final",
    )(dest_tiles, dest_tiles, x1, hhi, gate_t, g2.reshape(bsz, 1, d), wsg, wsu, wsd, lng, lnb, y_rows)


def _split_hi_lo(w):
    hi = w.astype(BF16)
    return hi, (w - hi.astype(F32)).astype(BF16)


def kernel(x, c, w_ada, b_ada, w_in, b_in, s5_lambda_re, s5_lambda_im, s5_log_dt, s5_b_re, s5_b_im, s5_c_re, s5_c_im, s5_d, w_s5_glu, b_s5_glu, w_fourier, b_fourier, w_out, b_out, ln1_g, ln1_b, w_router, router_bias, w_exp_gate, w_exp_up, w_exp_down, w_sh_gate, w_sh_up, w_sh_down, ln2_g, ln2_b):
    bsz, seq, d = x.shape
    depth = w_ada.shape[0]
    alpha = (2 * depth) ** 0.25
    t = bsz * seq
    d_s5 = s5_d.shape[1]
    d_f = w_fourier.shape[1]
    fw = d_f // FOURIER_GROUPS
    n_e = w_router.shape[2]
    n_chunk = seq // LANE
    seq_tabs, cmat = _dft_tables(seq, fw)
    row = lambda v: v.astype(F32).reshape(1, -1)

    x2 = x.reshape(t, d)
    for l in range(depth):
        mod = _adaln(c, w_ada[l], b_ada[l])
        sh1, sc1, g1, sh2, sc2, g2 = jnp.split(mod, 6, axis=-1)

        wi = w_in[l]
        bi = b_in[l].astype(F32)
        ws5t = wi[:, :d_s5].T.astype(BF16)
        bs5 = jnp.broadcast_to(bi[:d_s5].reshape(d_s5, 1), (d_s5, LANE))
        us5, z, gates = _proj(x2, sc1, sh1, ws5t, bs5,
                              wi[:, d_s5:d_s5 + d_f].astype(BF16), row(bi[d_s5:d_s5 + d_f]), cmat,
                              wi[:, d_s5 + d_f:].astype(BF16), row(bi[d_s5 + d_f:]), seq)

        tables = _s5_tables(s5_lambda_re[l], s5_lambda_im[l], s5_log_dt[l], s5_b_re[l], s5_b_im[l],
                            s5_c_re[l], s5_c_im[l], s5_d[l])
        ys5 = _s5(us5, tables, n_chunk)
        brs = _glu(ys5, w_s5_glu[l].astype(BF16), row(b_s5_glu[l]))
        yf_lo, yf_hi = _seqdft(seq_tabs, z, seq, fw)

        wrt_hi, wrt_lo = _split_hi_lo(w_router[l].astype(F32).T)
        x1, hhi, hp, eidx, gate, rank, cnt = _merge(
            x2, brs, yf_lo, yf_hi, gates, g1, sc2, sh2, w_fourier[l].astype(BF16), row(b_fourier[l]),
            w_out[l].astype(BF16), row(b_out[l]), row(ln1_g[l]), row(ln1_b[l]),
            wrt_hi, wrt_lo, router_bias[l], seq, alpha)

        counts = cnt[:, 0]
        padded = ((counts + FFN_BLOCK - 1) // FFN_BLOCK) * FFN_BLOCK
        pend = jnp.cumsum(padded)
        pstart = (pend - padded).astype(I32)
        dest = _dest(eidx, rank, pstart)
        n_blk = (t * TOP_K + n_e * (FFN_BLOCK - 1) + FFN_BLOCK - 1) // FFN_BLOCK
        n_rows = n_blk * FFN_BLOCK
        blk_start = jnp.arange(n_blk, dtype=I32) * FFN_BLOCK
        blk_e = jnp.minimum(jnp.sum((pend[None, :] <= blk_start[:, None]).astype(I32), axis=1), n_e - 1)
        n_used = (pend[-1:] // FFN_BLOCK).astype(I32)
        pad_lo = jnp.concatenate([pstart + counts, pend[-1:]]).astype(I32)
        pad_n = jnp.concatenate([padded - counts, n_rows - pend[-1:]]).astype(I32)

        def tiles(tm):
            return dest.reshape(TOP_K, t // tm, tm).transpose(1, 0, 2).reshape(t // tm, 1, TOP_K * tm)

        rows = _dispatch(hp, tiles(DISPATCH_TILE), pad_lo, pad_n, n_rows)
        next_blk = (pend // FFN_BLOCK).astype(I32)
        y_rows = _ffn(rows, blk_e, n_used, next_blk, w_exp_gate[l], w_exp_up[l], w_exp_down[l])
        x2 = _final(x1, hhi, y_rows, tiles(COMBINE_TILE), gate.T, g2,
                    w_sh_gate[l].astype(BF16), w_sh_up[l].astype(BF16), w_sh_down[l].astype(BF16),
                    row(ln2_g[l]), row(ln2_b[l]), seq, alpha)
    return x2.reshape(bsz, seq, d)
```

```python
import functools
import math

import jax
import jax.numpy as jnp
from jax import lax
from jax.experimental import pallas as pl
from jax.experimental.pallas import tpu as pltpu

F32 = jnp.float32
BF16 = jnp.bfloat16
I32 = jnp.int32

TOP_K = 8
N_EXPERT_GROUPS = 8
TOPK_GROUPS = 4
ROUTED_SCALE = 2.5
FOURIER_GROUPS = 4
LN_EPS = 1e-5

LANE = 128
VMEM_LIMIT = 56 * 1024 * 1024

HIGHEST = lax.Precision.HIGHEST
NEG_INF = float("-inf")

FFN_BLOCK = 512
DISPATCH_TILE = 1024
COMBINE_TILE = 256
ISSUE_UNROLL = 4


def _cparams(sem):
    return pltpu.CompilerParams(dimension_semantics=sem, vmem_limit_bytes=VMEM_LIMIT)


def _const_spec(shape):
    nd = len(shape)
    return pl.BlockSpec(shape, lambda *_: (0,) * nd, pipeline_mode=pl.Buffered(1))


def _standardize(x):
    mu = jnp.mean(x, axis=-1, keepdims=True)
    xc = x - mu
    var = jnp.mean(xc * xc, axis=-1, keepdims=True)
    return xc * lax.rsqrt(var + LN_EPS)


def _silu(x):
    return x * jax.nn.sigmoid(x)


def _gelu_tanh(x):
    return 0.5 * x * (1.0 + jnp.tanh(math.sqrt(2.0 / math.pi) * (x + 0.044715 * (x * x * x))))


def _adaln_kernel(c_ref, w_ref, b_ref, o_ref):
    a = _silu(c_ref[...])
    o_ref[...] = jnp.dot(a, w_ref[...], precision=HIGHEST, preferred_element_type=F32) + b_ref[...]


def _adaln(c, w, b):
    bsz, d = c.shape
    n = w.shape[1]
    tn = 512
    return pl.pallas_call(
        _adaln_kernel,
        grid=(n // tn,),
        in_specs=[pl.BlockSpec((bsz, d), lambda j: (0, 0)),
                  pl.BlockSpec((d, tn), lambda j: (0, j)),
                  pl.BlockSpec((1, tn), lambda j: (0, j))],
        out_specs=pl.BlockSpec((bsz, tn), lambda j: (0, j)),
        out_shape=jax.ShapeDtypeStruct((bsz, n), F32),
        compiler_params=_cparams(("parallel",)),
        name="adaln",
    )(c, w, b.reshape(1, n))


def _proj_kernel(x_ref, sc_ref, sh_ref, ws5t_ref, bs5_ref, wf_ref, bf_ref, cs_ref, wg_ref, bg_ref,
                 us5_ref, z_ref, gates_ref):
    tm = x_ref.shape[0]
    u = (_standardize(x_ref[...]) * (1.0 + sc_ref[0]) + sh_ref[0]).astype(BF16)
    p = lax.dot_general(ws5t_ref[...], u, (((1,), (1,)), ((), ())), preferred_element_type=F32)
    p = p + bs5_ref[:, 0:1]
    d_s5, n_j, _ = us5_ref.shape
    us5_2d = us5_ref.reshape(d_s5 * n_j, LANE)
    for j in range(n_j):
        us5_2d[pl.ds(j, d_s5, stride=n_j), :] = p[:, j * LANE:(j + 1) * LANE]
    uf = (jnp.dot(u, wf_ref[...], preferred_element_type=F32) + bf_ref[...]).astype(BF16)
    d_f = uf.shape[1]
    fw = d_f // FOURIER_GROUPS
    for q in range(FOURIER_GROUPS):
        zq = jnp.dot(uf[:, q * fw:(q + 1) * fw], cs_ref[...], preferred_element_type=F32)
        z_ref[:, q * fw:(q + 1) * fw] = zq[:, :fw].astype(BF16)
        z_ref[:, d_f + q * fw:d_f + (q + 1) * fw] = zq[:, fw:].astype(BF16)
    n_g = wg_ref.shape[1]
    half = n_g // 2
    for q in range(2):
        gp = jnp.dot(u, wg_ref[:, q * half:(q + 1) * half], preferred_element_type=F32)
        gp = gp + bg_ref[:, q * half:(q + 1) * half]
        gates_ref[:, q * half:(q + 1) * half] = jax.nn.sigmoid(gp).astype(BF16)


def _proj(x2, sc, sh, ws5t, bs5, wf, bf, cs, wg, bg, seq):
    t, d = x2.shape
    d_s5 = ws5t.shape[0]
    d_f = wf.shape[1]
    n_g = wg.shape[1]
    tm = 1024
    tpb = seq // tm
    bsz = t // seq
    return pl.pallas_call(
        _proj_kernel,
        grid=(t // tm,),
        in_specs=[pl.BlockSpec((tm, d), lambda i: (i, 0)),
                  pl.BlockSpec((1, 1, d), lambda i: (i // tpb, 0, 0)),
                  pl.BlockSpec((1, 1, d), lambda i: (i // tpb, 0, 0)),
                  _const_spec((d_s5, d)), _const_spec((d_s5, LANE)),
                  _const_spec((d, d_f)), _const_spec((1, d_f)),
                  _const_spec(cs.shape),
                  _const_spec((d, n_g)), _const_spec((1, n_g))],
        out_specs=[pl.BlockSpec((d_s5, tm // LANE, LANE), lambda i: (0, i, 0)),
                   pl.BlockSpec((tm, 2 * d_f), lambda i: (i, 0)),
                   pl.BlockSpec((tm, n_g), lambda i: (i, 0))],
        out_shape=[jax.ShapeDtypeStruct((d_s5, t // LANE, LANE), F32),
                   jax.ShapeDtypeStruct((t, 2 * d_f), BF16),
                   jax.ShapeDtypeStruct((t, n_g), BF16)],
        compiler_params=_cparams(("parallel",)),
        name="proj",
    )(x2, sc.reshape(bsz, 1, d), sh.reshape(bsz, 1, d), ws5t, bs5, wf, bf, cs, wg, bg)


def _s5_tables(lam_re, lam_im, log_dt, b_re, b_im, c_re, c_im, d_skip):
    L = LANE
    hp = HIGHEST
    lr, li = lam_re.astype(F32), lam_im.astype(F32)
    dt = jnp.exp(log_dt.astype(F32))[:, :, None]
    mag = jnp.exp(lr * dt)
    ang = li * dt
    ab_re, ab_im = mag * jnp.cos(ang), mag * jnp.sin(ang)
    den = lr * lr + li * li
    nr = ab_re - 1.0
    coef_re = (nr * lr + ab_im * li) / den
    coef_im = (ab_im * lr - nr * li) / den
    br, bi = b_re.astype(F32), b_im.astype(F32)
    bb_re = coef_re[..., None] * br - coef_im[..., None] * bi
    bb_im = coef_re[..., None] * bi + coef_im[..., None] * br
    cr, ci = c_re.astype(F32), c_im.astype(F32)
    n_g, n_p, n_h = br.shape[1], br.shape[2], br.shape[3]

    k = jnp.arange(L + 1, dtype=F32)[None, None, :, None]
    pmag = jnp.exp(k * (lr * dt)[:, :, None, :])
    pang = k * (li * dt)[:, :, None, :]
    pw_re, pw_im = pmag * jnp.cos(pang), pmag * jnp.sin(pang)

    m_re = cr[:, :, :, None, :] * jnp.swapaxes(bb_re, 2, 3)[:, :, None, :, :] \
        - ci[:, :, :, None, :] * jnp.swapaxes(bb_im, 2, 3)[:, :, None, :, :]
    m_im = cr[:, :, :, None, :] * jnp.swapaxes(bb_im, 2, 3)[:, :, None, :, :] \
        + ci[:, :, :, None, :] * jnp.swapaxes(bb_re, 2, 3)[:, :, None, :, :]
    kap = jnp.einsum("dgohp,dgkp->dgohk", m_re, pw_re, precision=hp) \
        - jnp.einsum("dgohp,dgkp->dgohk", m_im, pw_im, precision=hp)
    kb = kap[1]
    kpos = kap[0, ..., :L].at[..., 0].add(kb[..., 0]).reshape(n_g, n_h * n_h, L)
    kneg = jnp.concatenate([jnp.zeros_like(kb[..., 0:1]), kb[..., L - 1:0:-1]], axis=-1)
    kneg = kneg.reshape(n_g, n_h * n_h, L)

    cat = lambda *parts: jnp.concatenate(parts, axis=-1)
    pf_re, pf_im = pw_re[0, :, L - 1::-1][:, :L], pw_im[0, :, L - 1::-1][:, :L]
    pb_re, pb_im = pw_re[1, :, :L], pw_im[1, :, :L]
    sp, spsw = cat(pf_re, pf_im, pb_re, pb_im), cat(pf_im, pf_re, pb_im, pb_re)
    bt_re, bt_im = jnp.swapaxes(bb_re, 2, 3), jnp.swapaxes(bb_im, 2, 3)
    bx, by = cat(bt_re[0], bt_re[0], bt_re[1], bt_re[1]), cat(-bt_im[0], bt_im[0], -bt_im[1], bt_im[1])

    qf_re, qf_im = pw_re[0, :, 1:L + 1], pw_im[0, :, 1:L + 1]
    qb_re, qb_im = pw_re[1, :, L:0:-1], pw_im[1, :, L:0:-1]
    qt, qtsw = cat(qf_re, qf_im, qb_re, qb_im), cat(qf_im, qf_re, qb_im, qb_re)
    cx, cy = cat(cr[0], -cr[0], cr[1], -cr[1]), cat(-ci[0], -ci[0], -ci[1], -ci[1])

    al_re, al_im = pw_re[:, :, L], pw_im[:, :, L]
    al = jnp.stack([cat(al_re[0], al_re[0], al_re[1], al_re[1]),
                    cat(-al_im[0], al_im[0], -al_im[1], al_im[1])], axis=1)

    dsk = jnp.broadcast_to(d_skip.astype(F32).reshape(n_g, n_h, 1), (n_g, n_h, L))
    return kpos, kneg, sp, spsw, bx, by, qt, qtsw, cx, cy, al, dsk


def _s5_kernel(a_ref, kpos_ref, kneg_ref, sp_ref, spsw_ref, bx_ref, by_ref, qt_ref, qtsw_ref, cx_ref, cy_ref,
               al_ref, d_ref, y_ref, abf_ref, tp_ref, ws_ref, sf_ref, sb_ref, sfs_ref, sbs_ref, xf_ref, xb_ref,
               *, n_chunk):
    n_h, r, L = a_ref.shape
    bsz = r // n_chunk
    half = sp_ref.shape[2] // 2
    nt = (((1,), (1,)), ((), ()))

    sp, spsw = sp_ref[0], spsw_ref[0]
    for h in range(n_h):
        abf_ref[:, h * L:(h + 1) * L] = a_ref[h].astype(BF16)
        ws_ref[h * L:(h + 1) * L, :] = (sp * bx_ref[0, h:h + 1, :] + spsw * by_ref[0, h:h + 1, :]).astype(BF16)
    abf = abf_ref[...]

    s_all = jnp.dot(abf, ws_ref[...], preferred_element_type=F32)
    sf_ref[...] = s_all[:, :half]
    sb_ref[...] = s_all[:, half:]
    sfs_ref[...] = pltpu.roll(s_all[:, :half], half // 2, 1)
    sbs_ref[...] = pltpu.roll(s_all[:, half:], half // 2, 1)

    alx = al_ref[0, 0:1, :]
    aly = al_ref[0, 1:2, :]

    zero = jnp.zeros((bsz, half), F32)
    ef, efs, eb, ebs = zero, zero, zero, zero
    for c in range(n_chunk):
        cb = n_chunk - 1 - c
        rows_f = pl.ds(c, bsz, stride=n_chunk)
        rows_b = pl.ds(cb, bsz, stride=n_chunk)
        xf_ref[rows_f, :] = ef
        xb_ref[rows_b, :] = eb
        xf, yf = alx[:, :half], aly[:, :half]
        xb, yb = alx[:, half:], aly[:, half:]
        ef, efs = ef * xf + efs * yf + sf_ref[rows_f, :], efs * xf - ef * yf + sfs_ref[rows_f, :]
        eb, ebs = eb * xb + ebs * yb + sb_ref[rows_b, :], ebs * xb - eb * yb + sbs_ref[rows_b, :]
    xin = jnp.concatenate([xf_ref[...], xb_ref[...]], axis=1).astype(BF16)

    s_idx = lax.broadcasted_iota(I32, (L, L), 0)
    j_idx = lax.broadcasted_iota(I32, (L, L), 1)
    fwd_part = j_idx + s_idx < L
    qt, qtsw = qt_ref[0], qtsw_ref[0]

    def pair(op, _):
        for oo in range(2):
            o = 2 * op + oo
            for h in range(n_h):
                row = o * n_h + h
                kp = jnp.broadcast_to(kpos_ref[0, pl.ds(row, 1), :], (L, L))
                kn = jnp.broadcast_to(kneg_ref[0, pl.ds(row, 1), :], (L, L))
                tile = pltpu.roll(jnp.where(fwd_part, kp, kn), 0, 1, stride=1, stride_axis=0)
                tp_ref[h * L:(h + 1) * L, oo * L:(oo + 1) * L] = tile.astype(BF16)
        wo_t = jnp.concatenate(
            [qt * cx_ref[0, pl.ds(2 * op + oo, 1), :] + qtsw * cy_ref[0, pl.ds(2 * op + oo, 1), :]
             for oo in range(2)], axis=0).astype(BF16)
        yp = jnp.dot(abf, tp_ref[...], preferred_element_type=F32)
        yp = yp + lax.dot_general(xin, wo_t, nt, preferred_element_type=F32)
        for oo in range(2):
            o = 2 * op + oo
            y_ref[o] = yp[:, oo * L:(oo + 1) * L] + a_ref[o] * d_ref[0, pl.ds(o, 1), :]
        return 0

    lax.fori_loop(0, n_h // 2, pair, 0)


def _s5(us5, tables, n_chunk):
    kpos, kneg, sp, spsw, bx, by, qt, qtsw, cx, cy, al, dsk = tables
    d_s5, r, L = us5.shape
    n_g = kpos.shape[0]
    n_h = d_s5 // n_g
    n_st = sp.shape[2]
    g3 = lambda g: (g, 0, 0)
    pw_spec = pl.BlockSpec((1, L, n_st), g3)
    hv_spec = pl.BlockSpec((1, n_h, n_st), g3)
    return pl.pallas_call(
        functools.partial(_s5_kernel, n_chunk=n_chunk),
        grid=(n_g,),
        in_specs=[pl.BlockSpec((n_h, r, L), g3),
                  pl.BlockSpec((1, n_h * n_h, L), g3), pl.BlockSpec((1, n_h * n_h, L), g3),
                  pw_spec, pw_spec, hv_spec, hv_spec, pw_spec, pw_spec, hv_spec, hv_spec,
                  pl.BlockSpec((1, 2, n_st), g3), pl.BlockSpec((1, n_h, L), g3)],
        out_specs=pl.BlockSpec((n_h, r, L), g3),
        out_shape=jax.ShapeDtypeStruct((d_s5, r, L), F32),
        scratch_shapes=[pltpu.VMEM((r, n_h * L), BF16), pltpu.VMEM((n_h * L, 2 * L), BF16),
                        pltpu.VMEM((n_h * L, n_st), BF16)]
        + [pltpu.VMEM((r, n_st // 2), F32)] * 6,
        compiler_params=_cparams(("parallel",)),
        name="s5",
    )(us5, kpos, kneg, sp, spsw, bx, by, qt, qtsw, cx, cy, al, dsk)


DFT_SPLIT = 64


def _dft_tables(seq, fw):
    def angles(mult, n_rows, n):
        s = jnp.arange(n, dtype=I32)[None, :]
        q = jnp.arange(n_rows, dtype=I32)[:, None]
        return (2.0 * math.pi / n) * ((mult * q * s) % n).astype(F32)

    ang_a = angles(DFT_SPLIT, seq // DFT_SPLIT, seq)
    ang_b = angles(1, DFT_SPLIT, seq)
    seq_tabs = (jnp.cos(ang_a), jnp.sin(ang_a), jnp.cos(ang_b), jnp.sin(ang_b))
    ang_c = angles(1, fw, fw)
    return seq_tabs, jnp.concatenate([jnp.cos(ang_c), jnp.sin(ang_c)], axis=1).astype(BF16)


DFT_EXTRA = 16


def _seqdft_kernel(ca_ref, sa_ref, cb_ref, sb_ref, flip_ref, z_ref, lo_ref, hi_ref, f_ref, *, scale):
    seq = z_ref.shape[0]
    tk, d_f = lo_ref.shape

    @pl.when(pl.program_id(1) == 0)
    def _():
        a0 = pl.program_id(0) * (tk // DFT_SPLIT)
        cb, sb = cb_ref[...], sb_ref[...]

        def put(rows, ca, sa, n_b):
            f_ref[rows, :seq] = (ca * cb[:n_b] - sa * sb[:n_b]).astype(BF16)
            f_ref[rows, seq:] = (-(sa * cb[:n_b] + ca * sb[:n_b])).astype(BF16)

        for j in range(tk // DFT_SPLIT):
            put(slice(j * DFT_SPLIT, (j + 1) * DFT_SPLIT),
                ca_ref[pl.ds(a0 + j, 1), :], sa_ref[pl.ds(a0 + j, 1), :], DFT_SPLIT)
        a_next = a0 + tk // DFT_SPLIT
        put(slice(tk, tk + DFT_EXTRA), ca_ref[pl.ds(a_next, 1), :], sa_ref[pl.ds(a_next, 1), :], DFT_EXTRA)

    p = jnp.dot(f_ref[:, :seq], z_ref[:, :d_f], preferred_element_type=F32)
    q = jnp.dot(f_ref[:, seq:], z_ref[:, d_f:], preferred_element_type=F32)
    lo_ref[...] = ((p[:tk] + q[:tk]) * scale).astype(lo_ref.dtype)
    mirror = ((p - q) * scale).astype(BF16)
    hi_ref[...] = jnp.dot(flip_ref[...], mirror, preferred_element_type=F32).astype(hi_ref.dtype)


def _seqdft(seq_tabs, z, seq, fw):
    t, two_df = z.shape
    d_f = two_df // 2
    bsz = t // seq
    tk = 512
    nk2 = seq // (2 * tk)
    scale = 1.0 / math.sqrt(seq * fw)
    flip = (jnp.arange(tk)[:, None] + jnp.arange(tk + DFT_EXTRA)[None, :] == tk).astype(BF16)
    half = jax.ShapeDtypeStruct((t // 2, d_f), BF16)
    return pl.pallas_call(
        functools.partial(_seqdft_kernel, scale=scale),
        grid=(nk2, bsz),
        in_specs=[_const_spec(tab.shape) for tab in seq_tabs] + [_const_spec(flip.shape)]
        + [pl.BlockSpec((seq, two_df), lambda k, b: (b, 0))],
        out_specs=[pl.BlockSpec((tk, d_f), lambda k, b: (b * nk2 + k, 0)),
                   pl.BlockSpec((tk, d_f), lambda k, b: (b * nk2 + nk2 - 1 - k, 0))],
        out_shape=[half, half],
        scratch_shapes=[pltpu.VMEM((tk + DFT_EXTRA, 2 * seq), BF16)],
        compiler_params=_cparams(("parallel", "arbitrary")),
        name="seqdft",
    )(*seq_tabs, flip, z)


def _glu_kernel(y_ref, w_ref, b_ref, o_ref, zt_ref):
    d_s5, n_j, _ = y_ref.shape
    d = o_ref.shape[1]
    y2 = y_ref.reshape(d_s5 * n_j, LANE)
    for j in range(n_j):
        zt_ref[j * LANE:(j + 1) * LANE, :] = _gelu_tanh(y2[pl.ds(j, d_s5, stride=n_j), :]).T.astype(BF16)
    zt = zt_ref[...]
    a = jnp.dot(zt, w_ref[:, :d], preferred_element_type=F32) + b_ref[:, :d]
    g = jnp.dot(zt, w_ref[:, d:], preferred_element_type=F32) + b_ref[:, d:]
    o_ref[...] = (a * jax.nn.sigmoid(g)).astype(BF16)


def _glu(ys5, w, b):
    d_s5, r, L = ys5.shape
    t = r * L
    n = w.shape[1]
    tm = 1024
    return pl.pallas_call(
        _glu_kernel,
        grid=(t // tm,),
        in_specs=[pl.BlockSpec((d_s5, tm // L, L), lambda i: (0, i, 0)),
                  _const_spec((d_s5, n)), _const_spec((1, n))],
        out_specs=pl.BlockSpec((tm, n // 2), lambda i: (i, 0)),
        out_shape=jax.ShapeDtypeStruct((t, n // 2), BF16),
        scratch_shapes=[pltpu.VMEM((tm, d_s5), BF16)],
        compiler_params=_cparams(("parallel",)),
        name="glu",
    )(ys5, w, b)


U32 = jnp.uint32
PACK_SUB = 4


def _row_slab(ref, r):
    return ref.at[pl.ds(pl.multiple_of(r * PACK_SUB, PACK_SUB), PACK_SUB), :]


def _pack_rows(ref, v, first=0):
    rows, d = v.shape
    half = d // 2
    bits = lax.bitcast_convert_type(v.astype(BF16).astype(F32), U32)
    for c in range(PACK_SUB):
        lo = bits[:, c * LANE:(c + 1) * LANE] >> 16
        hi = bits[:, half + c * LANE:half + (c + 1) * LANE] & jnp.uint32(0xFFFF0000)
        ref[pl.ds(first * PACK_SUB + c, rows, stride=PACK_SUB), :] = hi | lo


def _unpack_rows(ref, rows, first=0):
    los, his = [], []
    for c in range(PACK_SUB):
        w = ref[pl.ds(first * PACK_SUB + c, rows, stride=PACK_SUB), :]
        los.append(lax.bitcast_convert_type(w << 16, F32))
        his.append(lax.bitcast_convert_type(w & jnp.uint32(0xFFFF0000), F32))
    return jnp.concatenate(los + his, axis=1)


def _merge_kernel(x_ref, brs_ref, yf_lo_ref, yf_hi_ref, gates_ref, g1_ref, sc_ref, sh_ref, wfo_ref, bfo_ref,
                  wo_ref, bo_ref, lng_ref, lnb_ref, whi_ref, wlo_ref, bias_ref, tri_ref,
                  x1_ref, hhi_ref, hp_ref, eidx_ref, gate_ref, rank_ref, cnt_ref, base_ref,
                  *, alpha, tiles_per_seq):
    d = x_ref.shape[1]
    in_first_half = (pl.program_id(0) % tiles_per_seq) < tiles_per_seq // 2
    yf = jnp.where(in_first_half, yf_lo_ref[...], yf_hi_ref[...])
    br_f = jnp.dot(yf, wfo_ref[...], preferred_element_type=F32) + bfo_ref[...]
    merged = gates_ref[:, :d].astype(F32) * brs_ref[...].astype(F32) + gates_ref[:, d:].astype(F32) * br_f
    mix = jnp.dot(merged.astype(BF16), wo_ref[...], preferred_element_type=F32) + bo_ref[...]
    v = alpha * x_ref[...] + g1_ref[0] * mix
    x1 = _standardize(v) * lng_ref[...] + lnb_ref[...]
    x1_ref[...] = x1
    h = _standardize(x1) * (1.0 + sc_ref[0]) + sh_ref[0]
    hhi = h.astype(BF16)
    hhi_ref[...] = hhi
    _pack_rows(hp_ref, h)
    _route(hhi, (h - hhi.astype(F32)).astype(BF16), whi_ref, wlo_ref, bias_ref, tri_ref,
           eidx_ref, gate_ref, rank_ref, cnt_ref, base_ref)


def _merge(x2, brs, yf_lo, yf_hi, gates, g1, sc2, sh2, wfo, bfo, wo, bo, lng, lnb, wrt_hi, wrt_lo, bias,
           seq, alpha):
    t, d = x2.shape
    d_f = yf_lo.shape[1]
    n_e = wrt_hi.shape[0]
    tm = 512
    tri = (jnp.arange(tm)[:, None] < jnp.arange(tm)[None, :]).astype(BF16)
    kt = lambda i: (0, i)
    tpb = seq // tm
    tph = tpb // 2
    bsz = t // seq
    row = lambda i: (i, 0)
    bat = lambda i: (i // tpb, 0, 0)
    lo_row = lambda i: ((i // tpb) * tph + jnp.minimum(i % tpb, tph - 1), 0)
    hi_row = lambda i: ((i // tpb) * tph + jnp.maximum(i % tpb - tph, 0), 0)
    return pl.pallas_call(
        functools.partial(_merge_kernel, alpha=alpha, tiles_per_seq=tpb),
        grid=(t // tm,),
        in_specs=[pl.BlockSpec((tm, d), row), pl.BlockSpec((tm, d), row),
                  pl.BlockSpec((tm, d_f), lo_row), pl.BlockSpec((tm, d_f), hi_row),
                  pl.BlockSpec((tm, 2 * d), row),
                  pl.BlockSpec((1, 1, d), bat), pl.BlockSpec((1, 1, d), bat), pl.BlockSpec((1, 1, d), bat),
                  _const_spec((d_f, d)), _const_spec((1, d)), _const_spec((d, d)), _const_spec((1, d)),
                  _const_spec((1, d)), _const_spec((1, d)),
                  _const_spec((n_e, d)), _const_spec((n_e, d)), _const_spec((n_e, LANE)), _const_spec((tm, tm))],
        out_specs=[pl.BlockSpec((tm, d), row), pl.BlockSpec((tm, d), row),
                   pl.BlockSpec((tm * PACK_SUB, LANE), row),
                   pl.BlockSpec((TOP_K, tm), kt), pl.BlockSpec((TOP_K, tm), kt), pl.BlockSpec((TOP_K, tm), kt),
                   pl.BlockSpec((n_e, LANE), lambda i: (0, 0))],
        out_shape=[jax.ShapeDtypeStruct((t, d), F32), jax.ShapeDtypeStruct((t, d), BF16),
                   jax.ShapeDtypeStruct((t * PACK_SUB, LANE), U32),
                   jax.ShapeDtypeStruct((TOP_K, t), I32), jax.ShapeDtypeStruct((TOP_K, t), F32),
                   jax.ShapeDtypeStruct((TOP_K, t), I32), jax.ShapeDtypeStruct((n_e, LANE), I32)],
        scratch_shapes=[pltpu.VMEM((n_e, LANE), F32)],
        compiler_params=_cparams(("arbitrary",)),
        name="merge_route",
    )(x2, brs, yf_lo, yf_hi, gates, g1.reshape(bsz, 1, d), sc2.reshape(bsz, 1, d), sh2.reshape(bsz, 1, d),
      wfo, bfo, wo, bo, lng, lnb, wrt_hi, wrt_lo,
      jnp.broadcast_to(bias.astype(F32).reshape(n_e, 1), (n_e, LANE)), tri)


def _route(hhi, hlo, whi_ref, wlo_ref, bias_ref, tri_ref, eidx_ref, gate_ref, rank_ref, cnt_ref, base_ref):
    n_e = whi_ref.shape[0]
    tm = hhi.shape[0]
    gsz = n_e // N_EXPERT_GROUPS
    nt = (((1,), (1,)), ((), ()))

    @pl.when(pl.program_id(0) == 0)
    def _():
        base_ref[...] = jnp.zeros_like(base_ref)

    logits = lax.dot_general(whi_ref[...], hhi, nt, preferred_element_type=F32)
    logits = logits + lax.dot_general(wlo_ref[...], hhi, nt, preferred_element_type=F32)
    logits = logits + lax.dot_general(whi_ref[...], hlo, nt, preferred_element_type=F32)
    scores = jax.nn.sigmoid(logits)
    sel = scores + bias_ref[:, 0:1]

    g3 = sel.reshape(N_EXPERT_GROUPS, gsz, tm)
    i3 = lax.broadcasted_iota(I32, g3.shape, 1).astype(F32)
    m1 = jnp.max(g3, axis=1, keepdims=True)
    first = jnp.min(jnp.where(g3 == m1, i3, float(gsz)), axis=1, keepdims=True)
    m2 = jnp.max(jnp.where(i3 == first, NEG_INF, g3), axis=1, keepdims=True)
    gs = (m1 + m2).reshape(N_EXPERT_GROUPS, tm)

    gi = lax.broadcasted_iota(I32, gs.shape, 0).astype(F32)
    gsel = jnp.zeros(gs.shape, F32)
    cur = gs
    for _ in range(TOPK_GROUPS):
        m = jnp.max(cur, axis=0, keepdims=True)
        f = jnp.min(jnp.where(cur == m, gi, float(N_EXPERT_GROUPS)), axis=0, keepdims=True)
        pick = gi == f
        gsel = jnp.where(pick, 1.0, gsel)
        cur = jnp.where(pick, NEG_INF, cur)
    gmask = jnp.broadcast_to(gsel.reshape(N_EXPERT_GROUPS, 1, tm), g3.shape).reshape(n_e, tm)
    masked = jnp.where(gmask > 0.5, sel, NEG_INF)

    ri = lax.broadcasted_iota(I32, (n_e, tm), 0).astype(F32)
    picks = []
    gates = []
    multihot = jnp.zeros((n_e, tm), F32)
    for _ in range(TOP_K):
        m = jnp.max(masked, axis=0, keepdims=True)
        f = jnp.min(jnp.where(masked == m, ri, float(n_e)), axis=0, keepdims=True)
        pick = ri == f
        picks.append(f)
        gates.append(jnp.sum(jnp.where(pick, scores, 0.0), axis=0, keepdims=True))
        multihot = jnp.where(pick, 1.0, multihot)
        masked = jnp.where(pick, NEG_INF, masked)
    gsum = gates[0]
    for g in gates[1:]:
        gsum = gsum + g

    rankmat = jnp.dot(multihot.astype(BF16), tri_ref[...], preferred_element_type=F32) + base_ref[:, 0:1]
    for k in range(TOP_K):
        pick = ri == picks[k]
        eidx_ref[k:k + 1, :] = picks[k].astype(I32)
        gate_ref[k:k + 1, :] = gates[k] / gsum * ROUTED_SCALE
        rank_ref[k:k + 1, :] = jnp.sum(jnp.where(pick, rankmat, 0.0), axis=0, keepdims=True).astype(I32)
    base_ref[...] = base_ref[...] + jnp.sum(multihot, axis=1, keepdims=True)
    cnt_ref[...] = base_ref[...].astype(I32)


def _dest_kernel(eidx_ref, rank_ref, pstart_ref, dest_ref):
    n_e = pstart_ref.shape[0]
    tm = eidx_ref.shape[1]
    ri = lax.broadcasted_iota(I32, (n_e, tm), 0)
    ps = pstart_ref[:, 0:1].astype(F32)
    for k in range(TOP_K):
        hit = ri == eidx_ref[k:k + 1, :]
        base = jnp.sum(jnp.where(hit, ps, 0.0), axis=0, keepdims=True)
        dest_ref[k:k + 1, :] = base.astype(I32) + rank_ref[k:k + 1, :]


def _dest(eidx, rank, pstart):
    k, t = eidx.shape
    n_e = pstart.shape[0]
    tm = 2048
    kt = lambda i: (0, i)
    return pl.pallas_call(
        _dest_kernel,
        grid=(t // tm,),
        in_specs=[pl.BlockSpec((k, tm), kt), pl.BlockSpec((k, tm), kt), _const_spec((n_e, LANE))],
        out_specs=pl.BlockSpec((k, tm), kt),
        out_shape=jax.ShapeDtypeStruct((k, t), I32),
        compiler_params=_cparams(("parallel",)),
        name="dest",
    )(eidx, rank, jnp.broadcast_to(pstart.astype(I32).reshape(n_e, 1), (n_e, LANE)))


def _zero_fill(pad_lo_ref, pad_n_ref, zeros_ref, rows_ref, sem):
    def each_copy(act):
        def per_entry(e, _):
            lo = pad_lo_ref[e]
            n = pad_n_ref[e]
            n_full = n // FFN_BLOCK

            def full(j, _):
                act(pltpu.make_async_copy(
                    zeros_ref, rows_ref.at[pl.ds(pl.multiple_of((lo + j * FFN_BLOCK) * PACK_SUB, PACK_SUB),
                                                 FFN_BLOCK * PACK_SUB), :], sem))
                return 0

            lax.fori_loop(0, n_full, full, 0)
            off = lo + n_full * FFN_BLOCK
            rem = n - n_full * FFN_BLOCK
            bit = FFN_BLOCK // 2
            while bit >= 1:
                take = rem & bit

                @pl.when(take != 0)
                def _(off=off, bit=bit):
                    act(pltpu.make_async_copy(
                        zeros_ref.at[pl.ds(0, bit * PACK_SUB), :],
                        rows_ref.at[pl.ds(pl.multiple_of(off * PACK_SUB, PACK_SUB), bit * PACK_SUB), :], sem))

                off = off + take
                bit //= 2
            return 0

        lax.fori_loop(0, pad_lo_ref.shape[0], per_entry, 0)

    each_copy(lambda cp: cp.start())
    each_copy(lambda cp: cp.wait())


def _dispatch_kernel(pad_lo_ref, pad_n_ref, dest_ref, hp_ref, rows_ref, zeros_ref, sem, zsem):
    tm = hp_ref.shape[0] // PACK_SUB

    @pl.when(pl.program_id(0) == 0)
    def _():
        zeros_ref[...] = jnp.zeros_like(zeros_ref)
        _zero_fill(pad_lo_ref, pad_n_ref, zeros_ref, rows_ref, zsem)

    def body(tt, _):
        for k in range(TOP_K):
            pltpu.make_async_copy(_row_slab(hp_ref, tt), _row_slab(rows_ref, dest_ref[0, 0, k * tm + tt]),
                                  sem).start(priority=k % 2)
        return 0

    lax.fori_loop(0, tm, body, 0, unroll=ISSUE_UNROLL)
    all_rows = rows_ref.at[pl.ds(0, TOP_K * tm * PACK_SUB), :]
    pltpu.make_async_copy(all_rows, all_rows, sem).wait()


def _dispatch(hp, dest_tiles, pad_lo, pad_n, n_rows):
    n_tile, _, per_tile = dest_tiles.shape
    tm = per_tile // TOP_K
    return pl.pallas_call(
        _dispatch_kernel,
        grid_spec=pltpu.PrefetchScalarGridSpec(
            num_scalar_prefetch=2,
            grid=(n_tile,),
            in_specs=[pl.BlockSpec((1, 1, per_tile), lambda i, lo, n: (i, 0, 0), memory_space=pltpu.SMEM),
                      pl.BlockSpec((tm * PACK_SUB, LANE), lambda i, lo, n: (i, 0))],
            out_specs=pl.BlockSpec(memory_space=pl.ANY),
            scratch_shapes=[pltpu.VMEM((FFN_BLOCK * PACK_SUB, LANE), U32),
                            pltpu.SemaphoreType.DMA, pltpu.SemaphoreType.DMA],
        ),
        out_shape=jax.ShapeDtypeStruct((n_rows * PACK_SUB, LANE), U32),
        compiler_params=_cparams(("arbitrary",)),
        name="dispatch",
    )(pad_lo, pad_n, dest_tiles, hp)


def _ffn_kernel(blk_e_ref, n_used_ref, next_blk_ref, rows_ref, wg_ref, wu_ref, wd_ref, y_ref,
                wg_st_ref, wu_st_ref, wd_st_ref, wgu_bf_ref, wd_bf_ref, sems):
    b = pl.program_id(0)
    n_used = n_used_ref[0]
    live = b < n_used
    d_e = wd_st_ref.shape[0]
    e = blk_e_ref[b]

    def fetch(expert):
        return (pltpu.make_async_copy(wg_ref.at[expert], wg_st_ref, sems.at[0]),
                pltpu.make_async_copy(wu_ref.at[expert], wu_st_ref, sems.at[1]),
                pltpu.make_async_copy(wd_ref.at[expert], wd_st_ref, sems.at[2]))

    @pl.when(b == 0)
    def _():
        for cp in fetch(e):
            cp.start()

    first_block = jnp.logical_and(live, jnp.logical_or(b == 0, e != blk_e_ref[jnp.maximum(b - 1, 0)]))

    @pl.when(first_block)
    def _():
        for cp in fetch(e):
            cp.wait()
        wgu_bf_ref[:, :d_e] = wg_st_ref[...].astype(BF16)
        wgu_bf_ref[:, d_e:] = wu_st_ref[...].astype(BF16)
        wd_bf_ref[...] = wd_st_ref[...].astype(BF16)
        nb = next_blk_ref[e]

        @pl.when(nb < n_used)
        def _():
            for cp in fetch(blk_e_ref[nb]):
                cp.start()

    @pl.when(live)
    def _():
        x = _unpack_rows(rows_ref, FFN_BLOCK).astype(BF16)
        au = jnp.dot(x, wgu_bf_ref[...], preferred_element_type=F32)
        hid = (_silu(au[:, :d_e]) * au[:, d_e:]).astype(BF16)
        _pack_rows(y_ref, jnp.dot(hid, wd_bf_ref[...], preferred_element_type=F32))

    @pl.when(jnp.logical_not(live))
    def _():
        y_ref[...] = jnp.zeros_like(y_ref)


def _ffn(rows, blk_e, n_used, next_blk, wg, wu, wd):
    n_rows = rows.shape[0] // PACK_SUB
    n_e, d, d_e = wg.shape
    n_blk = n_rows // FFN_BLOCK
    blk = lambda b, *_: (b, 0)
    in_blk = lambda b, e, n, nb: (jnp.minimum(b, jnp.maximum(n[0] - 1, 0)), 0)
    hbm = pl.BlockSpec(memory_space=pl.ANY)
    return pl.pallas_call(
        _ffn_kernel,
        grid_spec=pltpu.PrefetchScalarGridSpec(
            num_scalar_prefetch=3,
            grid=(n_blk,),
            in_specs=[pl.BlockSpec((FFN_BLOCK * PACK_SUB, LANE), in_blk), hbm, hbm, hbm],
            out_specs=pl.BlockSpec((FFN_BLOCK * PACK_SUB, LANE), blk),
            scratch_shapes=[pltpu.VMEM((d, d_e), F32), pltpu.VMEM((d, d_e), F32), pltpu.VMEM((d_e, d), F32),
                            pltpu.VMEM((d, 2 * d_e), BF16), pltpu.VMEM((d_e, d), BF16),
                            pltpu.SemaphoreType.DMA((3,))],
        ),
        out_shape=jax.ShapeDtypeStruct((n_rows * PACK_SUB, LANE), U32),
        compiler_params=_cparams(("arbitrary",)),
        name="ffn",
    )(blk_e, n_used, next_blk, rows, wg, wu, wd)


def _final_kernel(dest_ref, dnext_ref, x1_ref, h_ref, gate_ref, g2_ref, wsg_ref, wsu_ref, wsd_ref, lng_ref, lnb_ref,
                  y_ref, o_ref, ybuf_ref, sems, *, alpha):
    i = pl.program_id(0)
    tm = x1_ref.shape[0]
    per_tile = TOP_K * tm

    def gather(d_ref, slot):
        def body(tt, _):
            for k in range(TOP_K):
                pltpu.make_async_copy(_row_slab(y_ref, d_ref[0, 0, k * tm + tt]),
                                      _row_slab(ybuf_ref, slot * per_tile + k * tm + tt),
                                      sems.at[slot]).start(priority=k % 2)
            return 0

        lax.fori_loop(0, tm, body, 0, unroll=ISSUE_UNROLL)

    @pl.when(i == 0)
    def _():
        gather(dest_ref, 0)

    @pl.when(i + 1 < pl.num_programs(0))
    def _():
        gather(dnext_ref, (i + 1) % 2)

    h = h_ref[...]
    a = jnp.dot(h, wsg_ref[...], preferred_element_type=F32)
    u = jnp.dot(h, wsu_ref[...], preferred_element_type=F32)
    shared = jnp.dot((_silu(a) * u).astype(BF16), wsd_ref[...], preferred_element_type=F32)

    slot = i % 2
    half = ybuf_ref.at[pl.ds(pl.multiple_of(slot * per_tile * PACK_SUB, per_tile * PACK_SUB), per_tile * PACK_SUB), :]
    pltpu.make_async_copy(half, half, sems.at[slot]).wait()
    first = slot * per_tile
    routed = gate_ref[:, 0:1] * _unpack_rows(ybuf_ref, tm, first=first)
    for k in range(1, TOP_K):
        routed = routed + gate_ref[:, k:k + 1] * _unpack_rows(ybuf_ref, tm, first=first + k * tm)

    v = alpha * x1_ref[...] + g2_ref[0] * (shared + routed)
    o_ref[...] = _standardize(v) * lng_ref[...] + lnb_ref[...]


def _final(x1, hhi, y_rows, dest_tiles, gate_t, g2, wsg, wsu, wsd, lng, lnb, seq, alpha):
    t, d = x1.shape
    d_sh = wsg.shape[1]
    n_tile, _, per_tile = dest_tiles.shape
    tm = per_tile // TOP_K
    tpb = seq // tm
    bsz = t // seq
    row = lambda i: (i, 0)
    return pl.pallas_call(
        functools.partial(_final_kernel, alpha=alpha),
        grid=(n_tile,),
        in_specs=[pl.BlockSpec((1, 1, per_tile), lambda i: (i, 0, 0), memory_space=pltpu.SMEM),
                  pl.BlockSpec((1, 1, per_tile), lambda i: (jnp.minimum(i + 1, n_tile - 1), 0, 0),
                               memory_space=pltpu.SMEM),
                  pl.BlockSpec((tm, d), row), pl.BlockSpec((tm, d), row), pl.BlockSpec((tm, TOP_K), row),
                  pl.BlockSpec((1, 1, d), lambda i: (i // tpb, 0, 0)),
                  _const_spec((d, d_sh)), _const_spec((d, d_sh)), _const_spec((d_sh, d)),
                  _const_spec((1, d)), _const_spec((1, d)),
                  pl.BlockSpec(memory_space=pl.ANY)],
        out_specs=pl.BlockSpec((tm, d), row),
        out_shape=jax.ShapeDtypeStruct((t, d), F32),
        scratch_shapes=[pltpu.VMEM((2 * TOP_K * tm * PACK_SUB, LANE), U32), pltpu.SemaphoreType.DMA((2,))],
        compiler_params=_cparams(("arbitrary",)),
        name="final",
    )(dest_tiles, dest_tiles, x1, hhi, gate_t, g2.reshape(bsz, 1, d), wsg, wsu, wsd, lng, lnb, y_rows)


def _split_hi_lo(w):
    hi = w.astype(BF16)
    return hi, (w - hi.astype(F32)).astype(BF16)


def kernel(x, c, w_ada, b_ada, w_in, b_in, s5_lambda_re, s5_lambda_im, s5_log_dt, s5_b_re, s5_b_im, s5_c_re, s5_c_im, s5_d, w_s5_glu, b_s5_glu, w_fourier, b_fourier, w_out, b_out, ln1_g, ln1_b, w_router, router_bias, w_exp_gate, w_exp_up, w_exp_down, w_sh_gate, w_sh_up, w_sh_down, ln2_g, ln2_b):
    bsz, seq, d = x.shape
    depth = w_ada.shape[0]
    alpha = (2 * depth) ** 0.25
    t = bsz * seq
    d_s5 = s5_d.shape[1]
    d_f = w_fourier.shape[1]
    fw = d_f // FOURIER_GROUPS
    n_e = w_router.shape[2]
    n_chunk = seq // LANE
    seq_tabs, cmat = _dft_tables(seq, fw)
    row = lambda v: v.astype(F32).reshape(1, -1)

    x2 = x.reshape(t, d)
    for l in range(depth):
        mod = _adaln(c, w_ada[l], b_ada[l])
        sh1, sc1, g1, sh2, sc2, g2 = jnp.split(mod, 6, axis=-1)

        wi = w_in[l]
        bi = b_in[l].astype(F32)
        ws5t = wi[:, :d_s5].T.astype(BF16)
        bs5 = jnp.broadcast_to(bi[:d_s5].reshape(d_s5, 1), (d_s5, LANE))
        us5, z, gates = _proj(x2, sc1, sh1, ws5t, bs5,
                              wi[:, d_s5:d_s5 + d_f].astype(BF16), row(bi[d_s5:d_s5 + d_f]), cmat,
                              wi[:, d_s5 + d_f:].astype(BF16), row(bi[d_s5 + d_f:]), seq)

        tables = _s5_tables(s5_lambda_re[l], s5_lambda_im[l], s5_log_dt[l], s5_b_re[l], s5_b_im[l],
                            s5_c_re[l], s5_c_im[l], s5_d[l])
        ys5 = _s5(us5, tables, n_chunk)
        brs = _glu(ys5, w_s5_glu[l].astype(BF16), row(b_s5_glu[l]))
        yf_lo, yf_hi = _seqdft(seq_tabs, z, seq, fw)

        wrt_hi, wrt_lo = _split_hi_lo(w_router[l].astype(F32).T)
        x1, hhi, hp, eidx, gate, rank, cnt = _merge(
            x2, brs, yf_lo, yf_hi, gates, g1, sc2, sh2, w_fourier[l].astype(BF16), row(b_fourier[l]),
            w_out[l].astype(BF16), row(b_out[l]), row(ln1_g[l]), row(ln1_b[l]),
            wrt_hi, wrt_lo, router_bias[l], seq, alpha)

        counts = cnt[:, 0]
        padded = ((counts + FFN_BLOCK - 1) // FFN_BLOCK) * FFN_BLOCK
        pend = jnp.cumsum(padded)
        pstart = (pend - padded).astype(I32)
        dest = _dest(eidx, rank, pstart)
        n_blk = (t * TOP_K + n_e * (FFN_BLOCK - 1) + FFN_BLOCK - 1) // FFN_BLOCK
        n_rows = n_blk * FFN_BLOCK
        blk_start = jnp.arange(n_blk, dtype=I32) * FFN_BLOCK
        blk_e = jnp.minimum(jnp.sum((pend[None, :] <= blk_start[:, None]).astype(I32), axis=1), n_e - 1)
        n_used = (pend[-1:] // FFN_BLOCK).astype(I32)
        pad_lo = jnp.concatenate([pstart + counts, pend[-1:]]).astype(I32)
        pad_n = jnp.concatenate([padded - counts, n_rows - pend[-1:]]).astype(I32)

        def tiles(tm):
            return dest.reshape(TOP_K, t // tm, tm).transpose(1, 0, 2).reshape(t // tm, 1, TOP_K * tm)

        rows = _dispatch(hp, tiles(DISPATCH_TILE), pad_lo, pad_n, n_rows)
        next_blk = (pend // FFN_BLOCK).astype(I32)
        y_rows = _ffn(rows, blk_e, n_used, next_blk, w_exp_gate[l], w_exp_up[l], w_exp_down[l])
        x2 = _final(x1, hhi, y_rows, tiles(COMBINE_TILE), gate.T, g2,
                    w_sh_gate[l].astype(BF16), w_sh_up[l].astype(BF16), w_sh_down[l].astype(BF16),
                    row(ln2_g[l]), row(ln2_b[l]), seq, alpha)
    return x2.reshape(bsz, seq, d)
```

```python
import functools
import math

import jax
import jax.numpy as jnp
from jax import lax
from jax.experimental import pallas as pl
from jax.experimental.pallas import tpu as pltpu

F32 = jnp.float32
BF16 = jnp.bfloat16
I32 = jnp.int32

TOP_K = 8
N_EXPERT_GROUPS = 8
TOPK_GROUPS = 4
ROUTED_SCALE = 2.5
FOURIER_GROUPS = 4
LN_EPS = 1e-5

LANE = 128
VMEM_LIMIT = 56 * 1024 * 1024

HIGHEST = lax.Precision.HIGHEST
NEG_INF = float("-inf")

FFN_BLOCK = 512
DISPATCH_TILE = 1024
COMBINE_TILE = 256
ISSUE_UNROLL = 4


def _cparams(sem):
    return pltpu.CompilerParams(dimension_semantics=sem, vmem_limit_bytes=VMEM_LIMIT)


def _const_spec(shape):
    nd = len(shape)
    return pl.BlockSpec(shape, lambda *_: (0,) * nd, pipeline_mode=pl.Buffered(1))


def _standardize(x):
    mu = jnp.mean(x, axis=-1, keepdims=True)
    xc = x - mu
    var = jnp.mean(xc * xc, axis=-1, keepdims=True)
    return xc * lax.rsqrt(var + LN_EPS)


def _silu(x):
    return x * jax.nn.sigmoid(x)


def _gelu_tanh(x):
    return 0.5 * x * (1.0 + jnp.tanh(math.sqrt(2.0 / math.pi) * (x + 0.044715 * (x * x * x))))


def _adaln_kernel(c_ref, w_ref, b_ref, o_ref):
    a = _silu(c_ref[...])
    o_ref[...] = jnp.dot(a, w_ref[...], precision=HIGHEST, preferred_element_type=F32) + b_ref[...]


def _adaln(c, w, b):
    bsz, d = c.shape
    n = w.shape[1]
    tn = 512
    return pl.pallas_call(
        _adaln_kernel,
        grid=(n // tn,),
        in_specs=[pl.BlockSpec((bsz, d), lambda j: (0, 0)),
                  pl.BlockSpec((d, tn), lambda j: (0, j)),
                  pl.BlockSpec((1, tn), lambda j: (0, j))],
        out_specs=pl.BlockSpec((bsz, tn), lambda j: (0, j)),
        out_shape=jax.ShapeDtypeStruct((bsz, n), F32),
        compiler_params=_cparams(("parallel",)),
        name="adaln",
    )(c, w, b.reshape(1, n))


def _proj_kernel(x_ref, sc_ref, sh_ref, ws5t_ref, bs5_ref, wf_ref, bf_ref, cs_ref, wg_ref, bg_ref,
                 us5_ref, z_ref, gates_ref):
    tm = x_ref.shape[0]
    u = (_standardize(x_ref[...]) * (1.0 + sc_ref[0]) + sh_ref[0]).astype(BF16)
    p = lax.dot_general(ws5t_ref[...], u, (((1,), (1,)), ((), ())), preferred_element_type=F32)
    p = p + bs5_ref[:, 0:1]
    d_s5, n_j, _ = us5_ref.shape
    us5_2d = us5_ref.reshape(d_s5 * n_j, LANE)
    for j in range(n_j):
        us5_2d[pl.ds(j, d_s5, stride=n_j), :] = p[:, j * LANE:(j + 1) * LANE]
    uf = (jnp.dot(u, wf_ref[...], preferred_element_type=F32) + bf_ref[...]).astype(BF16)
    d_f = uf.shape[1]
    fw = d_f // FOURIER_GROUPS
    for q in range(FOURIER_GROUPS):
        zq = jnp.dot(uf[:, q * fw:(q + 1) * fw], cs_ref[...], preferred_element_type=F32)
        z_ref[:, q * fw:(q + 1) * fw] = zq[:, :fw].astype(BF16)
        z_ref[:, d_f + q * fw:d_f + (q + 1) * fw] = zq[:, fw:].astype(BF16)
    n_g = wg_ref.shape[1]
    half = n_g // 2
    for q in range(2):
        gp = jnp.dot(u, wg_ref[:, q * half:(q + 1) * half], preferred_element_type=F32)
        gp = gp + bg_ref[:, q * half:(q + 1) * half]
        gates_ref[:, q * half:(q + 1) * half] = jax.nn.sigmoid(gp).astype(BF16)


def _proj(x2, sc, sh, ws5t, bs5, wf, bf, cs, wg, bg, seq):
    t, d = x2.shape
    d_s5 = ws5t.shape[0]
    d_f = wf.shape[1]
    n_g = wg.shape[1]
    tm = 1024
    tpb = seq // tm
    bsz = t // seq
    return pl.pallas_call(
        _proj_kernel,
        grid=(t // tm,),
        in_specs=[pl.BlockSpec((tm, d), lambda i: (i, 0)),
                  pl.BlockSpec((1, 1, d), lambda i: (i // tpb, 0, 0)),
                  pl.BlockSpec((1, 1, d), lambda i: (i // tpb, 0, 0)),
                  _const_spec((d_s5, d)), _const_spec((d_s5, LANE)),
                  _const_spec((d, d_f)), _const_spec((1, d_f)),
                  _const_spec(cs.shape),
                  _const_spec((d, n_g)), _const_spec((1, n_g))],
        out_specs=[pl.BlockSpec((d_s5, tm // LANE, LANE), lambda i: (0, i, 0)),
                   pl.BlockSpec((tm, 2 * d_f), lambda i: (i, 0)),
                   pl.BlockSpec((tm, n_g), lambda i: (i, 0))],
        out_shape=[jax.ShapeDtypeStruct((d_s5, t // LANE, LANE), F32),
                   jax.ShapeDtypeStruct((t, 2 * d_f), BF16),
                   jax.ShapeDtypeStruct((t, n_g), BF16)],
        compiler_params=_cparams(("parallel",)),
        name="proj",
    )(x2, sc.reshape(bsz, 1, d), sh.reshape(bsz, 1, d), ws5t, bs5, wf, bf, cs, wg, bg)


def _s5_tables(lam_re, lam_im, log_dt, b_re, b_im, c_re, c_im, d_skip):
    L = LANE
    hp = HIGHEST
    lr, li = lam_re.astype(F32), lam_im.astype(F32)
    dt = jnp.exp(log_dt.astype(F32))[:, :, None]
    mag = jnp.exp(lr * dt)
    ang = li * dt
    ab_re, ab_im = mag * jnp.cos(ang), mag * jnp.sin(ang)
    den = lr * lr + li * li
    nr = ab_re - 1.0
    coef_re = (nr * lr + ab_im * li) / den
    coef_im = (ab_im * lr - nr * li) / den
    br, bi = b_re.astype(F32), b_im.astype(F32)
    bb_re = coef_re[..., None] * br - coef_im[..., None] * bi
    bb_im = coef_re[..., None] * bi + coef_im[..., None] * br
    cr, ci = c_re.astype(F32), c_im.astype(F32)
    n_g, n_p, n_h = br.shape[1], br.shape[2], br.shape[3]

    k = jnp.arange(L + 1, dtype=F32)[None, None, :, None]
    pmag = jnp.exp(k * (lr * dt)[:, :, None, :])
    pang = k * (li * dt)[:, :, None, :]
    pw_re, pw_im = pmag * jnp.cos(pang), pmag * jnp.sin(pang)

    m_re = cr[:, :, :, None, :] * jnp.swapaxes(bb_re, 2, 3)[:, :, None, :, :] \
        - ci[:, :, :, None, :] * jnp.swapaxes(bb_im, 2, 3)[:, :, None, :, :]
    m_im = cr[:, :, :, None, :] * jnp.swapaxes(bb_im, 2, 3)[:, :, None, :, :] \
        + ci[:, :, :, None, :] * jnp.swapaxes(bb_re, 2, 3)[:, :, None, :, :]
    kap = jnp.einsum("dgohp,dgkp->dgohk", m_re, pw_re, precision=hp) \
        - jnp.einsum("dgohp,dgkp->dgohk", m_im, pw_im, precision=hp)
    kb = kap[1]
    kpos = kap[0, ..., :L].at[..., 0].add(kb[..., 0]).reshape(n_g, n_h * n_h, L)
    kneg = jnp.concatenate([jnp.zeros_like(kb[..., 0:1]), kb[..., L - 1:0:-1]], axis=-1)
    kneg = kneg.reshape(n_g, n_h * n_h, L)

    cat = lambda *parts: jnp.concatenate(parts, axis=-1)
    pf_re, pf_im = pw_re[0, :, L - 1::-1][:, :L], pw_im[0, :, L - 1::-1][:, :L]
    pb_re, pb_im = pw_re[1, :, :L], pw_im[1, :, :L]
    sp, spsw = cat(pf_re, pf_im, pb_re, pb_im), cat(pf_im, pf_re, pb_im, pb_re)
    bt_re, bt_im = jnp.swapaxes(bb_re, 2, 3), jnp.swapaxes(bb_im, 2, 3)
    bx, by = cat(bt_re[0], bt_re[0], bt_re[1], bt_re[1]), cat(-bt_im[0], bt_im[0], -bt_im[1], bt_im[1])

    qf_re, qf_im = pw_re[0, :, 1:L + 1], pw_im[0, :, 1:L + 1]
    qb_re, qb_im = pw_re[1, :, L:0:-1], pw_im[1, :, L:0:-1]
    qt, qtsw = cat(qf_re, qf_im, qb_re, qb_im), cat(qf_im, qf_re, qb_im, qb_re)
    cx, cy = cat(cr[0], -cr[0], cr[1], -cr[1]), cat(-ci[0], -ci[0], -ci[1], -ci[1])

    al_re, al_im = pw_re[:, :, L], pw_im[:, :, L]
    al = jnp.stack([cat(al_re[0], al_re[0], al_re[1], al_re[1]),
                    cat(-al_im[0], al_im[0], -al_im[1], al_im[1])], axis=1)

    dsk = jnp.broadcast_to(d_skip.astype(F32).reshape(n_g, n_h, 1), (n_g, n_h, L))
    return kpos, kneg, sp, spsw, bx, by, qt, qtsw, cx, cy, al, dsk


def _s5_kernel(a_ref, kpos_ref, kneg_ref, sp_ref, spsw_ref, bx_ref, by_ref, qt_ref, qtsw_ref, cx_ref, cy_ref,
               al_ref, d_ref, y_ref, abf_ref, tp_ref, ws_ref, sf_ref, sb_ref, sfs_ref, sbs_ref, xf_ref, xb_ref,
               *, n_chunk):
    n_h, r, L = a_ref.shape
    bsz = r // n_chunk
    half = sp_ref.shape[2] // 2
    nt = (((1,), (1,)), ((), ()))

    sp, spsw = sp_ref[0], spsw_ref[0]
    for h in range(n_h):
        abf_ref[:, h * L:(h + 1) * L] = a_ref[h].astype(BF16)
        ws_ref[h * L:(h + 1) * L, :] = (sp * bx_ref[0, h:h + 1, :] + spsw * by_ref[0, h:h + 1, :]).astype(BF16)
    abf = abf_ref[...]

    s_all = jnp.dot(abf, ws_ref[...], preferred_element_type=F32)
    sf_ref[...] = s_all[:, :half]
    sb_ref[...] = s_all[:, half:]
    sfs_ref[...] = pltpu.roll(s_all[:, :half], half // 2, 1)
    sbs_ref[...] = pltpu.roll(s_all[:, half:], half // 2, 1)

    alx = al_ref[0, 0:1, :]
    aly = al_ref[0, 1:2, :]

    zero = jnp.zeros((bsz, half), F32)
    ef, efs, eb, ebs = zero, zero, zero, zero
    for c in range(n_chunk):
        cb = n_chunk - 1 - c
        rows_f = pl.ds(c, bsz, stride=n_chunk)
        rows_b = pl.ds(cb, bsz, stride=n_chunk)
        xf_ref[rows_f, :] = ef
        xb_ref[rows_b, :] = eb
        xf, yf = alx[:, :half], aly[:, :half]
        xb, yb = alx[:, half:], aly[:, half:]
        ef, efs = ef * xf + efs * yf + sf_ref[rows_f, :], efs * xf - ef * yf + sfs_ref[rows_f, :]
        eb, ebs = eb * xb + ebs * yb + sb_ref[rows_b, :], ebs * xb - eb * yb + sbs_ref[rows_b, :]
    xin = jnp.concatenate([xf_ref[...], xb_ref[...]], axis=1).astype(BF16)

    s_idx = lax.broadcasted_iota(I32, (L, L), 0)
    j_idx = lax.broadcasted_iota(I32, (L, L), 1)
    fwd_part = j_idx + s_idx < L
    qt, qtsw = qt_ref[0], qtsw_ref[0]

    def pair(op, _):
        for oo in range(2):
            o = 2 * op + oo
            for h in range(n_h):
                row = o * n_h + h
                kp = jnp.broadcast_to(kpos_ref[0, pl.ds(row, 1), :], (L, L))
                kn = jnp.broadcast_to(kneg_ref[0, pl.ds(row, 1), :], (L, L))
                tile = pltpu.roll(jnp.where(fwd_part, kp, kn), 0, 1, stride=1, stride_axis=0)
                tp_ref[h * L:(h + 1) * L, oo * L:(oo + 1) * L] = tile.astype(BF16)
        wo_t = jnp.concatenate(
            [qt * cx_ref[0, pl.ds(2 * op + oo, 1), :] + qtsw * cy_ref[0, pl.ds(2 * op + oo, 1), :]
             for oo in range(2)], axis=0).astype(BF16)
        yp = jnp.dot(abf, tp_ref[...], preferred_element_type=F32)
        yp = yp + lax.dot_general(xin, wo_t, nt, preferred_element_type=F32)
        for oo in range(2):
            o = 2 * op + oo
            y_ref[o] = yp[:, oo * L:(oo + 1) * L] + a_ref[o] * d_ref[0, pl.ds(o, 1), :]
        return 0

    lax.fori_loop(0, n_h // 2, pair, 0)


def _s5(us5, tables, n_chunk):
    kpos, kneg, sp, spsw, bx, by, qt, qtsw, cx, cy, al, dsk = tables
    d_s5, r, L = us5.shape
    n_g = kpos.shape[0]
    n_h = d_s5 // n_g
    n_st = sp.shape[2]
    g3 = lambda g: (g, 0, 0)
    pw_spec = pl.BlockSpec((1, L, n_st), g3)
    hv_spec = pl.BlockSpec((1, n_h, n_st), g3)
    return pl.pallas_call(
        functools.partial(_s5_kernel, n_chunk=n_chunk),
        grid=(n_g,),
        in_specs=[pl.BlockSpec((n_h, r, L), g3),
                  pl.BlockSpec((1, n_h * n_h, L), g3), pl.BlockSpec((1, n_h * n_h, L), g3),
                  pw_spec, pw_spec, hv_spec, hv_spec, pw_spec, pw_spec, hv_spec, hv_spec,
                  pl.BlockSpec((1, 2, n_st), g3), pl.BlockSpec((1, n_h, L), g3)],
        out_specs=pl.BlockSpec((n_h, r, L), g3),
        out_shape=jax.ShapeDtypeStruct((d_s5, r, L), F32),
        scratch_shapes=[pltpu.VMEM((r, n_h * L), BF16), pltpu.VMEM((n_h * L, 2 * L), BF16),
                        pltpu.VMEM((n_h * L, n_st), BF16)]
        + [pltpu.VMEM((r, n_st // 2), F32)] * 6,
        compiler_params=_cparams(("parallel",)),
        name="s5",
    )(us5, kpos, kneg, sp, spsw, bx, by, qt, qtsw, cx, cy, al, dsk)


DFT_SPLIT = 64


def _dft_tables(seq, fw):
    def angles(mult, n_rows, n):
        s = jnp.arange(n, dtype=I32)[None, :]
        q = jnp.arange(n_rows, dtype=I32)[:, None]
        return (2.0 * math.pi / n) * ((mult * q * s) % n).astype(F32)

    ang_a = angles(DFT_SPLIT, seq // DFT_SPLIT, seq)
    ang_b = angles(1, DFT_SPLIT, seq)
    seq_tabs = (jnp.cos(ang_a), jnp.sin(ang_a), jnp.cos(ang_b), jnp.sin(ang_b))
    ang_c = angles(1, fw, fw)
    return seq_tabs, jnp.concatenate([jnp.cos(ang_c), jnp.sin(ang_c)], axis=1).astype(BF16)


DFT_EXTRA = 16


def _seqdft_kernel(ca_ref, sa_ref, cb_ref, sb_ref, flip_ref, z_ref, lo_ref, hi_ref, f_ref, *, scale):
    seq = z_ref.shape[0]
    tk, d_f = lo_ref.shape

    @pl.when(pl.program_id(1) == 0)
    def _():
        a0 = pl.program_id(0) * (tk // DFT_SPLIT)
        cb, sb = cb_ref[...], sb_ref[...]

        def put(rows, ca, sa, n_b):
            f_ref[rows, :seq] = (ca * cb[:n_b] - sa * sb[:n_b]).astype(BF16)
            f_ref[rows, seq:] = (-(sa * cb[:n_b] + ca * sb[:n_b])).astype(BF16)

        for j in range(tk // DFT_SPLIT):
            put(slice(j * DFT_SPLIT, (j + 1) * DFT_SPLIT),
                ca_ref[pl.ds(a0 + j, 1), :], sa_ref[pl.ds(a0 + j, 1), :], DFT_SPLIT)
        a_next = a0 + tk // DFT_SPLIT
        put(slice(tk, tk + DFT_EXTRA), ca_ref[pl.ds(a_next, 1), :], sa_ref[pl.ds(a_next, 1), :], DFT_EXTRA)

    p = jnp.dot(f_ref[:, :seq], z_ref[:, :d_f], preferred_element_type=F32)
    q = jnp.dot(f_ref[:, seq:], z_ref[:, d_f:], preferred_element_type=F32)
    lo_ref[...] = ((p[:tk] + q[:tk]) * scale).astype(lo_ref.dtype)
    mirror = ((p - q) * scale).astype(BF16)
    hi_ref[...] = jnp.dot(flip_ref[...], mirror, preferred_element_type=F32).astype(hi_ref.dtype)


def _seqdft(seq_tabs, z, seq, fw):
    t, two_df = z.shape
    d_f = two_df // 2
    bsz = t // seq
    tk = 512
    nk2 = seq // (2 * tk)
    scale = 1.0 / math.sqrt(seq * fw)
    flip = (jnp.arange(tk)[:, None] + jnp.arange(tk + DFT_EXTRA)[None, :] == tk).astype(BF16)
    half = jax.ShapeDtypeStruct((t // 2, d_f), BF16)
    return pl.pallas_call(
        functools.partial(_seqdft_kernel, scale=scale),
        grid=(nk2, bsz),
        in_specs=[_const_spec(tab.shape) for tab in seq_tabs] + [_const_spec(flip.shape)]
        + [pl.BlockSpec((seq, two_df), lambda k, b: (b, 0))],
        out_specs=[pl.BlockSpec((tk, d_f), lambda k, b: (b * nk2 + k, 0)),
                   pl.BlockSpec((tk, d_f), lambda k, b: (b * nk2 + nk2 - 1 - k, 0))],
        out_shape=[half, half],
        scratch_shapes=[pltpu.VMEM((tk + DFT_EXTRA, 2 * seq), BF16)],
        compiler_params=_cparams(("parallel", "arbitrary")),
        name="seqdft",
    )(*seq_tabs, flip, z)


def _glu_kernel(y_ref, w_ref, b_ref, o_ref, zt_ref):
    d_s5, n_j, _ = y_ref.shape
    d = o_ref.shape[1]
    y2 = y_ref.reshape(d_s5 * n_j, LANE)
    for j in range(n_j):
        zt_ref[j * LANE:(j + 1) * LANE, :] = _gelu_tanh(y2[pl.ds(j, d_s5, stride=n_j), :]).T.astype(BF16)
    zt = zt_ref[...]
    a = jnp.dot(zt, w_ref[:, :d], preferred_element_type=F32) + b_ref[:, :d]
    g = jnp.dot(zt, w_ref[:, d:], preferred_element_type=F32) + b_ref[:, d:]
    o_ref[...] = (a * jax.nn.sigmoid(g)).astype(BF16)


def _glu(ys5, w, b):
    d_s5, r, L = ys5.shape
    t = r * L
    n = w.shape[1]
    tm = 1024
    return pl.pallas_call(
        _glu_kernel,
        grid=(t // tm,),
        in_specs=[pl.BlockSpec((d_s5, tm // L, L), lambda i: (0, i, 0)),
                  _const_spec((d_s5, n)), _const_spec((1, n))],
        out_specs=pl.BlockSpec((tm, n // 2), lambda i: (i, 0)),
        out_shape=jax.ShapeDtypeStruct((t, n // 2), BF16),
        scratch_shapes=[pltpu.VMEM((tm, d_s5), BF16)],
        compiler_params=_cparams(("parallel",)),
        name="glu",
    )(ys5, w, b)


U32 = jnp.uint32
PACK_SUB = 4


def _row_slab(ref, r):
    return ref.at[pl.ds(pl.multiple_of(r * PACK_SUB, PACK_SUB), PACK_SUB), :]


def _pack_rows(ref, v, first=0):
    rows, d = v.shape
    half = d // 2
    bits = lax.bitcast_convert_type(v.astype(BF16).astype(F32), U32)
    for c in range(PACK_SUB):
        lo = bits[:, c * LANE:(c + 1) * LANE] >> 16
        hi = bits[:, half + c * LANE:half + (c + 1) * LANE] & jnp.uint32(0xFFFF0000)
        ref[pl.ds(first * PACK_SUB + c, rows, stride=PACK_SUB), :] = hi | lo


def _unpack_rows(ref, rows, first=0):
    los, his = [], []
    for c in range(PACK_SUB):
        w = ref[pl.ds(first * PACK_SUB + c, rows, stride=PACK_SUB), :]
        los.append(lax.bitcast_convert_type(w << 16, F32))
        his.append(lax.bitcast_convert_type(w & jnp.uint32(0xFFFF0000), F32))
    return jnp.concatenate(los + his, axis=1)


def _merge_kernel(x_ref, brs_ref, yf_lo_ref, yf_hi_ref, gates_ref, g1_ref, sc_ref, sh_ref, wfo_ref, bfo_ref,
                  wo_ref, bo_ref, lng_ref, lnb_ref, whi_ref, wlo_ref, bias_ref, tri_ref,
                  x1_ref, hhi_ref, hp_ref, eidx_ref, gate_ref, rank_ref, cnt_ref, base_ref,
                  *, alpha, tiles_per_seq):
    d = x_ref.shape[1]
    in_first_half = (pl.program_id(0) % tiles_per_seq) < tiles_per_seq // 2
    yf = jnp.where(in_first_half, yf_lo_ref[...], yf_hi_ref[...])
    br_f = jnp.dot(yf, wfo_ref[...], preferred_element_type=F32) + bfo_ref[...]
    merged = gates_ref[:, :d].astype(F32) * brs_ref[...].astype(F32) + gates_ref[:, d:].astype(F32) * br_f
    mix = jnp.dot(merged.astype(BF16), wo_ref[...], preferred_element_type=F32) + bo_ref[...]
    v = alpha * x_ref[...] + g1_ref[0] * mix
    x1 = _standardize(v) * lng_ref[...] + lnb_ref[...]
    x1_ref[...] = x1
    h = _standardize(x1) * (1.0 + sc_ref[0]) + sh_ref[0]
    hhi = h.astype(BF16)
    hhi_ref[...] = hhi
    _pack_rows(hp_ref, h)
    _route(hhi, (h - hhi.astype(F32)).astype(BF16), whi_ref, wlo_ref, bias_ref, tri_ref,
           eidx_ref, gate_ref, rank_ref, cnt_ref, base_ref)


def _merge(x2, brs, yf_lo, yf_hi, gates, g1, sc2, sh2, wfo, bfo, wo, bo, lng, lnb, wrt_hi, wrt_lo, bias,
           seq, alpha):
    t, d = x2.shape
    d_f = yf_lo.shape[1]
    n_e = wrt_hi.shape[0]
    tm = 512
    tri = (jnp.arange(tm)[:, None] < jnp.arange(tm)[None, :]).astype(BF16)
    kt = lambda i: (0, i)
    tpb = seq // tm
    tph = tpb // 2
    bsz = t // seq
    row = lambda i: (i, 0)
    bat = lambda i: (i // tpb, 0, 0)
    lo_row = lambda i: ((i // tpb) * tph + jnp.minimum(i % tpb, tph - 1), 0)
    hi_row = lambda i: ((i // tpb) * tph + jnp.maximum(i % tpb - tph, 0), 0)
    return pl.pallas_call(
        functools.partial(_merge_kernel, alpha=alpha, tiles_per_seq=tpb),
        grid=(t // tm,),
        in_specs=[pl.BlockSpec((tm, d), row), pl.BlockSpec((tm, d), row),
                  pl.BlockSpec((tm, d_f), lo_row), pl.BlockSpec((tm, d_f), hi_row),
                  pl.BlockSpec((tm, 2 * d), row),
                  pl.BlockSpec((1, 1, d), bat), pl.BlockSpec((1, 1, d), bat), pl.BlockSpec((1, 1, d), bat),
                  _const_spec((d_f, d)), _const_spec((1, d)), _const_spec((d, d)), _const_spec((1, d)),
                  _const_spec((1, d)), _const_spec((1, d)),
                  _const_spec((n_e, d)), _const_spec((n_e, d)), _const_spec((n_e, LANE)), _const_spec((tm, tm))],
        out_specs=[pl.BlockSpec((tm, d), row), pl.BlockSpec((tm, d), row),
                   pl.BlockSpec((tm * PACK_SUB, LANE), row),
                   pl.BlockSpec((TOP_K, tm), kt), pl.BlockSpec((TOP_K, tm), kt), pl.BlockSpec((TOP_K, tm), kt),
                   pl.BlockSpec((n_e, LANE), lambda i: (0, 0))],
        out_shape=[jax.ShapeDtypeStruct((t, d), F32), jax.ShapeDtypeStruct((t, d), BF16),
                   jax.ShapeDtypeStruct((t * PACK_SUB, LANE), U32),
                   jax.ShapeDtypeStruct((TOP_K, t), I32), jax.ShapeDtypeStruct((TOP_K, t), F32),
                   jax.ShapeDtypeStruct((TOP_K, t), I32), jax.ShapeDtypeStruct((n_e, LANE), I32)],
        scratch_shapes=[pltpu.VMEM((n_e, LANE), F32)],
        compiler_params=_cparams(("arbitrary",)),
        name="merge_route",
    )(x2, brs, yf_lo, yf_hi, gates, g1.reshape(bsz, 1, d), sc2.reshape(bsz, 1, d), sh2.reshape(bsz, 1, d),
      wfo, bfo, wo, bo, lng, lnb, wrt_hi, wrt_lo,
      jnp.broadcast_to(bias.astype(F32).reshape(n_e, 1), (n_e, LANE)), tri)


def _route(hhi, hlo, whi_ref, wlo_ref, bias_ref, tri_ref, eidx_ref, gate_ref, rank_ref, cnt_ref, base_ref):
    n_e = whi_ref.shape[0]
    tm = hhi.shape[0]
    gsz = n_e // N_EXPERT_GROUPS
    nt = (((1,), (1,)), ((), ()))

    @pl.when(pl.program_id(0) == 0)
    def _():
        base_ref[...] = jnp.zeros_like(base_ref)

    logits = lax.dot_general(whi_ref[...], hhi, nt, preferred_element_type=F32)
    logits = logits + lax.dot_general(wlo_ref[...], hhi, nt, preferred_element_type=F32)
    logits = logits + lax.dot_general(whi_ref[...], hlo, nt, preferred_element_type=F32)
    scores = jax.nn.sigmoid(logits)
    sel = scores + bias_ref[:, 0:1]

    g3 = sel.reshape(N_EXPERT_GROUPS, gsz, tm)
    i3 = lax.broadcasted_iota(I32, g3.shape, 1).astype(F32)
    m1 = jnp.max(g3, axis=1, keepdims=True)
    first = jnp.min(jnp.where(g3 == m1, i3, float(gsz)), axis=1, keepdims=True)
    m2 = jnp.max(jnp.where(i3 == first, NEG_INF, g3), axis=1, keepdims=True)
    gs = (m1 + m2).reshape(N_EXPERT_GROUPS, tm)

    gi = lax.broadcasted_iota(I32, gs.shape, 0).astype(F32)
    gsel = jnp.zeros(gs.shape, F32)
    cur = gs
    for _ in range(TOPK_GROUPS):
        m = jnp.max(cur, axis=0, keepdims=True)
        f = jnp.min(jnp.where(cur == m, gi, float(N_EXPERT_GROUPS)), axis=0, keepdims=True)
        pick = gi == f
        gsel = jnp.where(pick, 1.0, gsel)
        cur = jnp.where(pick, NEG_INF, cur)
    gmask = jnp.broadcast_to(gsel.reshape(N_EXPERT_GROUPS, 1, tm), g3.shape).reshape(n_e, tm)
    masked = jnp.where(gmask > 0.5, sel, NEG_INF)

    ri = lax.broadcasted_iota(I32, (n_e, tm), 0).astype(F32)
    picks = []
    gates = []
    multihot = jnp.zeros((n_e, tm), F32)
    for _ in range(TOP_K):
        m = jnp.max(masked, axis=0, keepdims=True)
        f = jnp.min(jnp.where(masked == m, ri, float(n_e)), axis=0, keepdims=True)
        pick = ri == f
        picks.append(f)
        gates.append(jnp.sum(jnp.where(pick, scores, 0.0), axis=0, keepdims=True))
        multihot = jnp.where(pick, 1.0, multihot)
        masked = jnp.where(pick, NEG_INF, masked)
    gsum = gates[0]
    for g in gates[1:]:
        gsum = gsum + g

    rankmat = jnp.dot(multihot.astype(BF16), tri_ref[...], preferred_element_type=F32) + base_ref[:, 0:1]
    for k in range(TOP_K):
        pick = ri == picks[k]
        eidx_ref[k:k + 1, :] = picks[k].astype(I32)
        gate_ref[k:k + 1, :] = gates[k] / gsum * ROUTED_SCALE
        rank_ref[k:k + 1, :] = jnp.sum(jnp.where(pick, rankmat, 0.0), axis=0, keepdims=True).astype(I32)
    base_ref[...] = base_ref[...] + jnp.sum(multihot, axis=1, keepdims=True)
    cnt_ref[...] = base_ref[...].astype(I32)


def _dest_kernel(eidx_ref, rank_ref, pstart_ref, dest_ref):
    n_e = pstart_ref.shape[0]
    tm = eidx_ref.shape[1]
    ri = lax.broadcasted_iota(I32, (n_e, tm), 0)
    ps = pstart_ref[:, 0:1].astype(F32)
    for k in range(TOP_K):
        hit = ri == eidx_ref[k:k + 1, :]
        base = jnp.sum(jnp.where(hit, ps, 0.0), axis=0, keepdims=True)
        dest_ref[k:k + 1, :] = base.astype(I32) + rank_ref[k:k + 1, :]


def _dest(eidx, rank, pstart):
    k, t = eidx.shape
    n_e = pstart.shape[0]
    tm = 2048
    kt = lambda i: (0, i)
    return pl.pallas_call(
        _dest_kernel,
        grid=(t // tm,),
        in_specs=[pl.BlockSpec((k, tm), kt), pl.BlockSpec((k, tm), kt), _const_spec((n_e, LANE))],
        out_specs=pl.BlockSpec((k, tm), kt),
        out_shape=jax.ShapeDtypeStruct((k, t), I32),
        compiler_params=_cparams(("parallel",)),
        name="dest",
    )(eidx, rank, jnp.broadcast_to(pstart.astype(I32).reshape(n_e, 1), (n_e, LANE)))


def _zero_fill(pad_lo_ref, pad_n_ref, zeros_ref, rows_ref, sem):
    def each_copy(act):
        def per_entry(e, _):
            lo = pad_lo_ref[e]
            n = pad_n_ref[e]
            n_full = n // FFN_BLOCK

            def full(j, _):
                act(pltpu.make_async_copy(
                    zeros_ref, rows_ref.at[pl.ds(pl.multiple_of((lo + j * FFN_BLOCK) * PACK_SUB, PACK_SUB),
                                                 FFN_BLOCK * PACK_SUB), :], sem))
                return 0

            lax.fori_loop(0, n_full, full, 0)
            off = lo + n_full * FFN_BLOCK
            rem = n - n_full * FFN_BLOCK
            bit = FFN_BLOCK // 2
            while bit >= 1:
                take = rem & bit

                @pl.when(take != 0)
                def _(off=off, bit=bit):
                    act(pltpu.make_async_copy(
                        zeros_ref.at[pl.ds(0, bit * PACK_SUB), :],
                        rows_ref.at[pl.ds(pl.multiple_of(off * PACK_SUB, PACK_SUB), bit * PACK_SUB), :], sem))

                off = off + take
                bit //= 2
            return 0

        lax.fori_loop(0, pad_lo_ref.shape[0], per_entry, 0)

    each_copy(lambda cp: cp.start())
    each_copy(lambda cp: cp.wait())


def _dispatch_kernel(pad_lo_ref, pad_n_ref, dest_ref, hp_ref, rows_ref, zeros_ref, sem, zsem):
    tm = hp_ref.shape[0] // PACK_SUB

    @pl.when(pl.program_id(0) == 0)
    def _():
        zeros_ref[...] = jnp.zeros_like(zeros_ref)
        _zero_fill(pad_lo_ref, pad_n_ref, zeros_ref, rows_ref, zsem)

    def body(tt, _):
        for k in range(TOP_K):
            pltpu.make_async_copy(_row_slab(hp_ref, tt), _row_slab(rows_ref, dest_ref[0, 0, k * tm + tt]),
                                  sem).start(priority=k % 2)
        return 0

    lax.fori_loop(0, tm, body, 0, unroll=ISSUE_UNROLL)
    all_rows = rows_ref.at[pl.ds(0, TOP_K * tm * PACK_SUB), :]
    pltpu.make_async_copy(all_rows, all_rows, sem).wait()


def _dispatch(hp, dest_tiles, pad_lo, pad_n, n_rows):
    n_tile, _, per_tile = dest_tiles.shape
    tm = per_tile // TOP_K
    return pl.pallas_call(
        _dispatch_kernel,
        grid_spec=pltpu.PrefetchScalarGridSpec(
            num_scalar_prefetch=2,
            grid=(n_tile,),
            in_specs=[pl.BlockSpec((1, 1, per_tile), lambda i, lo, n: (i, 0, 0), memory_space=pltpu.SMEM),
                      pl.BlockSpec((tm * PACK_SUB, LANE), lambda i, lo, n: (i, 0))],
            out_specs=pl.BlockSpec(memory_space=pl.ANY),
            scratch_shapes=[pltpu.VMEM((FFN_BLOCK * PACK_SUB, LANE), U32),
                            pltpu.SemaphoreType.DMA, pltpu.SemaphoreType.DMA],
        ),
        out_shape=jax.ShapeDtypeStruct((n_rows * PACK_SUB, LANE), U32),
        compiler_params=_cparams(("arbitrary",)),
        name="dispatch",
    )(pad_lo, pad_n, dest_tiles, hp)


def _ffn_kernel(blk_e_ref, n_used_ref, next_blk_ref, rows_ref, wg_ref, wu_ref, wd_ref, y_ref,
                wg_st_ref, wu_st_ref, wd_st_ref, wgu_bf_ref, wd_bf_ref, sems):
    b = pl.program_id(0)
    n_used = n_used_ref[0]
    live = b < n_used
    d_e = wd_st_ref.shape[0]
    e = blk_e_ref[b]

    def fetch(expert):
        return (pltpu.make_async_copy(wg_ref.at[expert], wg_st_ref, sems.at[0]),
                pltpu.make_async_copy(wu_ref.at[expert], wu_st_ref, sems.at[1]),
                pltpu.make_async_copy(wd_ref.at[expert], wd_st_ref, sems.at[2]))

    @pl.when(b == 0)
    def _():
        for cp in fetch(e):
            cp.start()

    first_block = jnp.logical_and(live, jnp.logical_or(b == 0, e != blk_e_ref[jnp.maximum(b - 1, 0)]))

    @pl.when(first_block)
    def _():
        for cp in fetch(e):
            cp.wait()
        wgu_bf_ref[:, :d_e] = wg_st_ref[...].astype(BF16)
        wgu_bf_ref[:, d_e:] = wu_st_ref[...].astype(BF16)
        wd_bf_ref[...] = wd_st_ref[...].astype(BF16)
        nb = next_blk_ref[e]

        @pl.when(nb < n_used)
        def _():
            for cp in fetch(blk_e_ref[nb]):
                cp.start()

    @pl.when(live)
    def _():
        x = _unpack_rows(rows_ref, FFN_BLOCK).astype(BF16)
        au = jnp.dot(x, wgu_bf_ref[...], preferred_element_type=F32)
        hid = (_silu(au[:, :d_e]) * au[:, d_e:]).astype(BF16)
        _pack_rows(y_ref, jnp.dot(hid, wd_bf_ref[...], preferred_element_type=F32))

    @pl.when(jnp.logical_not(live))
    def _():
        y_ref[...] = jnp.zeros_like(y_ref)


def _ffn(rows, blk_e, n_used, next_blk, wg, wu, wd):
    n_rows = rows.shape[0] // PACK_SUB
    n_e, d, d_e = wg.shape
    n_blk = n_rows // FFN_BLOCK
    blk = lambda b, *_: (b, 0)
    in_blk = lambda b, e, n, nb: (jnp.minimum(b, jnp.maximum(n[0] - 1, 0)), 0)
    hbm = pl.BlockSpec(memory_space=pl.ANY)
    return pl.pallas_call(
        _ffn_kernel,
        grid_spec=pltpu.PrefetchScalarGridSpec(
            num_scalar_prefetch=3,
            grid=(n_blk,),
            in_specs=[pl.BlockSpec((FFN_BLOCK * PACK_SUB, LANE), in_blk), hbm, hbm, hbm],
            out_specs=pl.BlockSpec((FFN_BLOCK * PACK_SUB, LANE), blk),
            scratch_shapes=[pltpu.VMEM((d, d_e), F32), pltpu.VMEM((d, d_e), F32), pltpu.VMEM((d_e, d), F32),
                            pltpu.VMEM((d, 2 * d_e), BF16), pltpu.VMEM((d_e, d), BF16),
                            pltpu.SemaphoreType.DMA((3,))],
        ),
        out_shape=jax.ShapeDtypeStruct((n_rows * PACK_SUB, LANE), U32),
        compiler_params=_cparams(("arbitrary",)),
        name="ffn",
    )(blk_e, n_used, next_blk, rows, wg, wu, wd)


COMBINE_GROUP = 16


def _final_kernel(dest_ref, dnext_ref, x1_ref, h_ref, gate_ref, g2_ref, wsg_ref, wsu_ref, wsd_ref, lng_ref, lnb_ref,
                  y_ref, o_ref, ybuf_ref, routed_ref, sems, *, alpha):
    i = pl.program_id(0)
    last = pl.num_programs(0) - 1
    tm = x1_ref.shape[0]
    per_tile = TOP_K * tm
    slot = i % 2
    first = slot * per_tile

    def start_rows(d_ref, sl, tt):
        for k in range(TOP_K):
            pltpu.make_async_copy(_row_slab(y_ref, d_ref[0, 0, k * tm + tt]),
                                  _row_slab(ybuf_ref, sl * per_tile + k * tm + tt),
                                  sems.at[sl]).start(priority=k % 2)

    def wait_half(sl):
        half = ybuf_ref.at[pl.ds(pl.multiple_of(sl * per_tile * PACK_SUB, per_tile * PACK_SUB),
                                 per_tile * PACK_SUB), :]
        pltpu.make_async_copy(half, half, sems.at[sl]).wait()

    @pl.when(i == 0)
    def _():
        def body(tt, _):
            start_rows(dest_ref, slot, tt)
            return 0

        lax.fori_loop(0, tm, body, 0, unroll=ISSUE_UNROLL)

    wait_half(slot)

    def group(g, _):
        r0 = pl.multiple_of(g * COMBINE_GROUP, COMBINE_GROUP)
        for u in range(COMBINE_GROUP):
            start_rows(dnext_ref, 1 - slot, r0 + u)
        rows = pl.ds(r0, COMBINE_GROUP)
        acc = gate_ref[rows, 0:1] * _unpack_rows(ybuf_ref, COMBINE_GROUP, first=first + r0)
        for k in range(1, TOP_K):
            acc = acc + gate_ref[rows, k:k + 1] * _unpack_rows(ybuf_ref, COMBINE_GROUP, first=first + k * tm + r0)
        routed_ref[rows, :] = acc
        return 0

    lax.fori_loop(0, tm // COMBINE_GROUP, group, 0)

    @pl.when(i == last)
    def _():
        wait_half(1 - slot)

    h = h_ref[...]
    a = jnp.dot(h, wsg_ref[...], preferred_element_type=F32)
    u = jnp.dot(h, wsu_ref[...], preferred_element_type=F32)
    shared = jnp.dot((_silu(a) * u).astype(BF16), wsd_ref[...], preferred_element_type=F32)

    v = alpha * x1_ref[...] + g2_ref[0] * (shared + routed_ref[...])
    o_ref[...] = _standardize(v) * lng_ref[...] + lnb_ref[...]


def _final(x1, hhi, y_rows, dest_tiles, gate_t, g2, wsg, wsu, wsd, lng, lnb, seq, alpha):
    t, d = x1.shape
    d_sh = wsg.shape[1]
    n_tile, _, per_tile = dest_tiles.shape
    tm = per_tile // TOP_K
    tpb = seq // tm
    bsz = t // seq
    row = lambda i: (i, 0)
    return pl.pallas_call(
        functools.partial(_final_kernel, alpha=alpha),
        grid=(n_tile,),
        in_specs=[pl.BlockSpec((1, 1, per_tile), lambda i: (i, 0, 0), memory_space=pltpu.SMEM),
                  pl.BlockSpec((1, 1, per_tile), lambda i: (jnp.minimum(i + 1, n_tile - 1), 0, 0),
                               memory_space=pltpu.SMEM),
                  pl.BlockSpec((tm, d), row), pl.BlockSpec((tm, d), row), pl.BlockSpec((tm, TOP_K), row),
                  pl.BlockSpec((1, 1, d), lambda i: (i // tpb, 0, 0)),
                  _const_spec((d, d_sh)), _const_spec((d, d_sh)), _const_spec((d_sh, d)),
                  _const_spec((1, d)), _const_spec((1, d)),
                  pl.BlockSpec(memory_space=pl.ANY)],
        out_specs=pl.BlockSpec((tm, d), row),
        out_shape=jax.ShapeDtypeStruct((t, d), F32),
        scratch_shapes=[pltpu.VMEM((2 * TOP_K * tm * PACK_SUB, LANE), U32), pltpu.VMEM((tm, d), F32),
                        pltpu.SemaphoreType.DMA((2,))],
        compiler_params=_cparams(("arbitrary",)),
        name="final",
    )(dest_tiles, dest_tiles, x1, hhi, gate_t, g2.reshape(bsz, 1, d), wsg, wsu, wsd, lng, lnb, y_rows)


def _split_hi_lo(w):
    hi = w.astype(BF16)
    return hi, (w - hi.astype(F32)).astype(BF16)


def kernel(x, c, w_ada, b_ada, w_in, b_in, s5_lambda_re, s5_lambda_im, s5_log_dt, s5_b_re, s5_b_im, s5_c_re, s5_c_im, s5_d, w_s5_glu, b_s5_glu, w_fourier, b_fourier, w_out, b_out, ln1_g, ln1_b, w_router, router_bias, w_exp_gate, w_exp_up, w_exp_down, w_sh_gate, w_sh_up, w_sh_down, ln2_g, ln2_b):
    bsz, seq, d = x.shape
    depth = w_ada.shape[0]
    alpha = (2 * depth) ** 0.25
    t = bsz * seq
    d_s5 = s5_d.shape[1]
    d_f = w_fourier.shape[1]
    fw = d_f // FOURIER_GROUPS
    n_e = w_router.shape[2]
    n_chunk = seq // LANE
    seq_tabs, cmat = _dft_tables(seq, fw)
    row = lambda v: v.astype(F32).reshape(1, -1)

    x2 = x.reshape(t, d)
    for l in range(depth):
        mod = _adaln(c, w_ada[l], b_ada[l])
        sh1, sc1, g1, sh2, sc2, g2 = jnp.split(mod, 6, axis=-1)

        wi = w_in[l]
        bi = b_in[l].astype(F32)
        ws5t = wi[:, :d_s5].T.astype(BF16)
        bs5 = jnp.broadcast_to(bi[:d_s5].reshape(d_s5, 1), (d_s5, LANE))
        us5, z, gates = _proj(x2, sc1, sh1, ws5t, bs5,
                              wi[:, d_s5:d_s5 + d_f].astype(BF16), row(bi[d_s5:d_s5 + d_f]), cmat,
                              wi[:, d_s5 + d_f:].astype(BF16), row(bi[d_s5 + d_f:]), seq)

        tables = _s5_tables(s5_lambda_re[l], s5_lambda_im[l], s5_log_dt[l], s5_b_re[l], s5_b_im[l],
                            s5_c_re[l], s5_c_im[l], s5_d[l])
        ys5 = _s5(us5, tables, n_chunk)
        brs = _glu(ys5, w_s5_glu[l].astype(BF16), row(b_s5_glu[l]))
        yf_lo, yf_hi = _seqdft(seq_tabs, z, seq, fw)

        wrt_hi, wrt_lo = _split_hi_lo(w_router[l].astype(F32).T)
        x1, hhi, hp, eidx, gate, rank, cnt = _merge(
            x2, brs, yf_lo, yf_hi, gates, g1, sc2, sh2, w_fourier[l].astype(BF16), row(b_fourier[l]),
            w_out[l].astype(BF16), row(b_out[l]), row(ln1_g[l]), row(ln1_b[l]),
            wrt_hi, wrt_lo, router_bias[l], seq, alpha)

        counts = cnt[:, 0]
        padded = ((counts + FFN_BLOCK - 1) // FFN_BLOCK) * FFN_BLOCK
        pend = jnp.cumsum(padded)
        pstart = (pend - padded).astype(I32)
        dest = _dest(eidx, rank, pstart)
        n_blk = (t * TOP_K + n_e * (FFN_BLOCK - 1) + FFN_BLOCK - 1) // FFN_BLOCK
        n_rows = n_blk * FFN_BLOCK
        blk_start = jnp.arange(n_blk, dtype=I32) * FFN_BLOCK
        blk_e = jnp.minimum(jnp.sum((pend[None, :] <= blk_start[:, None]).astype(I32), axis=1), n_e - 1)
        n_used = (pend[-1:] // FFN_BLOCK).astype(I32)
        pad_lo = jnp.concatenate([pstart + counts, pend[-1:]]).astype(I32)
        pad_n = jnp.concatenate([padded - counts, n_rows - pend[-1:]]).astype(I32)

        def tiles(tm):
            return dest.reshape(TOP_K, t // tm, tm).transpose(1, 0, 2).reshape(t // tm, 1, TOP_K * tm)

        rows = _dispatch(hp, tiles(DISPATCH_TILE), pad_lo, pad_n, n_rows)
        next_blk = (pend // FFN_BLOCK).astype(I32)
        y_rows = _ffn(rows, blk_e, n_used, next_blk, w_exp_gate[l], w_exp_up[l], w_exp_down[l])
        x2 = _final(x1, hhi, y_rows, tiles(COMBINE_TILE), gate.T, g2,
                    w_sh_gate[l].astype(BF16), w_sh_up[l].astype(BF16), w_sh_down[l].astype(BF16),
                    row(ln2_g[l]), row(ln2_b[l]), seq, alpha)
    return x2.reshape(bsz, seq, d)
```
